```python
import math
import jax
import jax.numpy as jnp
from jax import lax
import numpy as np

D_MODEL = 2048
BATCH = 2
SEQ = 16384
DEPTH = 4
DEC_BATCH = 1
DEC_SEQ = 16384
PAST_LEN = 128

N_META = 16
N_MIXERS = 4
GROUP_W = D_MODEL // N_MIXERS
DA_HEADS = 4
DA_HD = GROUP_W // (2 * DA_HEADS)
ROT_DIM = DA_HD // 4
ROPE_THETA = 500000.0
Q_BLOCK = 128
HY_CH = GROUP_W
HY_BANDS = 16
HY_EMB = 1 + 2 * HY_BANDS
HY_FFN = 64
HY_SHIFT = 0.05
HG_HEADS = 4
HG_DK = GROUP_W // HG_HEADS
HG_DV = GROUP_W // HG_HEADS
GD_HEADS = 4
GD_DK = GROUP_W // GD_HEADS
GD_DV = GROUP_W // GD_HEADS
CHUNK = 64
PAD_FRONT = (-N_META) % CHUNK
SHORT_CONV = 3
D_FF = 5632
COLS_A = 3 * GROUP_W
COLS_B = 3 * GROUP_W
COLS_C = 5 * GROUP_W
COLS_D = 4 * GROUP_W + 4 * GD_HEADS
D_IN = COLS_A + COLS_B + COLS_C + COLS_D
ALPHA = (2.0 * DEPTH) ** 0.25
BETA_INIT = (8.0 * DEPTH) ** -0.25
LN_EPS = 1e-5
RMS_EPS = 1e-6
F32 = jnp.float32

kernel_name = 'hybrid_bidir_encoder_trunk'


def layer_norm(x, g, b):
    xf = x.astype(F32)
    mu = jnp.mean(xf, axis=-1, keepdims=True)
    var = jnp.mean(jnp.square(xf - mu), axis=-1, keepdims=True)
    return ((xf - mu) * lax.rsqrt(var + LN_EPS) * g.astype(F32) + b.astype(F32)).astype(x.dtype)


def rms_norm(x, g):
    xf = x.astype(F32)
    return (xf * lax.rsqrt(jnp.mean(xf * xf, axis=-1, keepdims=True) + RMS_EPS) * g.astype(F32)).astype(x.dtype)


def l2norm(x):
    return x * lax.rsqrt(jnp.sum(x * x, axis=-1, keepdims=True) + 1e-6)


def centred_conv(x, w, b=None):
    K = w.shape[0]
    r = K // 2
    L = x.shape[1]
    xp = jnp.pad(x, ((0, 0), (r, r), (0, 0)))
    y = xp[:, 0:L] * w[0]
    for j in range(1, K):
        y = y + xp[:, j:j + L] * w[j]
    if b is not None:
        y = y + b
    return y


def rotary(x, pos):
    half = ROT_DIM // 2
    inv = 1.0 / (ROPE_THETA ** (jnp.arange(half, dtype=F32) / half))
    ang = pos.astype(F32)[:, None] * inv[None]
    cos = jnp.cos(ang)[None, :, None, :]
    sin = jnp.sin(ang)[None, :, None, :]
    xr = x[..., :ROT_DIM].astype(F32)
    x1, x2 = xr[..., :half], xr[..., half:]
    rot = jnp.concatenate([x1 * cos - x2 * sin, x2 * cos + x1 * sin], axis=-1)
    return jnp.concatenate([rot.astype(x.dtype), x[..., ROT_DIM:]], axis=-1)


def pad_front(a):
    return jnp.pad(a, ((0, 0), (PAD_FRONT, 0)) + ((0, 0),) * (a.ndim - 2))


def stack_dirs(a_fwd, a_bwd):
    return jnp.concatenate([pad_front(a_fwd), jnp.flip(pad_front(a_bwd), axis=1)], axis=0)


def merge_dirs(o, B):
    return (o[:B] + jnp.flip(o[B:], axis=1))[:, PAD_FRONT:]


def diff_attn_mixer(pa, pos, l, lam_q1, lam_k1, lam_q2, lam_k2, norm_g):
    B, L, _ = pa.shape
    q, k, v = jnp.split(pa, 3, axis=-1)
    q = rotary(q.reshape(B, L, 2 * DA_HEADS, DA_HD), pos).reshape(B, L, DA_HEADS, 2, DA_HD)
    k = rotary(k.reshape(B, L, 2 * DA_HEADS, DA_HD), pos).reshape(B, L, DA_HEADS, 2, DA_HD)
    v = v.reshape(B, L, DA_HEADS, 2 * DA_HD)
    lam_init = 0.8 - 0.6 * math.exp(-0.3 * l)
    lam = (jnp.exp(jnp.sum(lam_q1.astype(F32) * lam_k1.astype(F32)))
           - jnp.exp(jnp.sum(lam_q2.astype(F32) * lam_k2.astype(F32))) + lam_init)
    scale = DA_HD ** -0.5
    nb = -(-L // Q_BLOCK)
    qp = jnp.pad(q, ((0, 0), (0, nb * Q_BLOCK - L), (0, 0), (0, 0), (0, 0)))
    qb = jnp.moveaxis(qp.reshape(B, nb, Q_BLOCK, DA_HEADS, 2, DA_HD), 1, 0)

    def block(qblk):
        s = jnp.einsum('bqhmd,bkhmd->bhmqk', qblk, k, preferred_element_type=F32) * scale
        prob = jax.nn.softmax(s, axis=-1)
        a = prob[:, :, 0] - lam * prob[:, :, 1]
        return jnp.einsum('bhqk,bkhe->bqhe', a.astype(v.dtype), v, preferred_element_type=F32)

    o = jnp.moveaxis(lax.map(block, qb), 0, 1).reshape(B, nb * Q_BLOCK, DA_HEADS, 2 * DA_HD)[:, :L]
    o = rms_norm(o, norm_g) * (1.0 - lam_init)
    return o.reshape(B, L, GROUP_W)


def hyena_filters(L, w1, b1, f1, w2, b2, f2, w3, decay):
    t = jnp.linspace(0.0, 1.0, L, dtype=F32)[:, None]
    bands = jnp.linspace(1e-4, HY_BANDS - 1, HY_BANDS, dtype=F32)
    ang = (2.0 * math.pi / L) * jnp.arange(L, dtype=F32)[:, None] * bands[None]
    z = jnp.concatenate([t, jnp.cos(ang), -jnp.sin(ang)], axis=-1)
    hid = jnp.sin(f1.astype(F32) * (z @ w1.astype(F32) + b1.astype(F32)))
    hid = jnp.sin(f2.astype(F32) * (hid @ w2.astype(F32) + b2.astype(F32)))
    filt = (hid @ w3.astype(F32)).reshape(L, 2, HY_CH)
    window = jnp.exp(-t[:, :, None] * jnp.abs(decay.astype(F32))[None]) + HY_SHIFT
    filt = filt * window
    filt = filt / (jnp.sum(jnp.abs(filt), axis=(0, 1)) + 1e-6)
    return filt[:, 0], filt[:, 1]


def long_conv(u, hf, hb, d_skip):
    L = u.shape[1]
    kern = jnp.concatenate([hf, jnp.zeros((1, hf.shape[1]), F32), jnp.flip(hb[1:], axis=0)], axis=0)
    uf = jnp.fft.rfft(u, n=2 * L, axis=1)
    kf = jnp.fft.rfft(kern, n=2 * L, axis=0)
    y = jnp.fft.irfft(uf * kf[None], n=2 * L, axis=1)[:, :L]
    return y + u * d_skip.astype(F32)


def hyena_mixer(pb, conv_w, conv_b, w1, b1, f1, w2, b2, f2, w3, decay, d_skip):
    B, L, _ = pb.shape
    u = centred_conv(pb, conv_w, conv_b).astype(F32)
    x0, x1, v = jnp.split(u, 3, axis=-1)
    hf, hb = hyena_filters(L, w1, b1, f1, w2, b2, f2, w3, decay)
    return x0 * long_conv(x1 * v, hf, hb, d_skip)


def gla_chunked(q, k, v, g):
    Bx, T, H, dk = q.shape
    dv = v.shape[-1]
    N = T // CHUNK

    def chunk(a):
        return a.astype(F32).reshape(Bx, N, CHUNK, H, a.shape[-1]).transpose(1, 0, 3, 2, 4)

    qc, kc, vc = chunk(q), chunk(k), chunk(v)
    bc = jnp.cumsum(chunk(g), axis=3)
    causal = jnp.tril(jnp.ones((CHUNK, CHUNK), dtype=bool))

    def step(S, inp):
        qi, ki, vi, bi = inp
        o_inter = jnp.einsum('bhcd,bhde->bhce', qi * jnp.exp(bi), S)
        diff = bi[:, :, :, None, :] - bi[:, :, None, :, :]
        dec = jnp.exp(jnp.where(causal[:, :, None], diff, -jnp.inf))
        A = jnp.einsum('bhid,bhjd,bhijd->bhij', qi, ki, dec)
        o = o_inter + A @ vi
        bl = bi[:, :, -1]
        S = jnp.exp(bl)[..., None] * S + jnp.einsum('bhcd,bhce->bhde', ki * jnp.exp(bl[:, :, None] - bi), vi)
        return S, o

    _, o = lax.scan(step, jnp.zeros((Bx, H, dk, dv), F32), (qc, kc, vc, bc))
    return o.transpose(1, 0, 3, 2, 4).reshape(Bx, T, H, dv)


def hgrn2_mixer(pc, lb, norm_g):
    B, L, _ = pc.shape
    q, ff, fb, i, gate = jnp.split(pc, 5, axis=-1)

    def heads(a):
        return a.reshape(B, L, HG_HEADS, -1)

    q = jax.nn.silu(q.astype(F32)) * HG_DK ** -0.5
    lb = lb.astype(F32)
    g_f = jnp.log(lb[0] + (1.0 - lb[0]) * jax.nn.sigmoid(ff.astype(F32)))
    g_b = jnp.log(lb[1] + (1.0 - lb[1]) * jax.nn.sigmoid(fb.astype(F32)))
    k_f = 1.0 - jnp.exp(g_f)
    k_b = 1.0 - jnp.exp(g_b)
    qh, ih = heads(q), heads(i)
    o = gla_chunked(stack_dirs(qh, qh), stack_dirs(heads(k_f), heads(k_b)),
                    stack_dirs(ih, ih), stack_dirs(heads(g_f), heads(g_b)))
    o = merge_dirs(o, B)
    o = rms_norm(o, norm_g) * jax.nn.silu(heads(gate).astype(F32))
    return o.reshape(B, L, GROUP_W)


def gdn_chunked(q, k, v, beta, g):
    Bx, T, H, dk = q.shape
    dv = v.shape[-1]
    N = T // CHUNK

    def chunk(a):
        return a.astype(F32).reshape(Bx, N, CHUNK, H, a.shape[-1]).transpose(0, 3, 1, 2, 4)

    qc, kc, vc = chunk(q), chunk(k), chunk(v)
    betac = chunk(beta[..., None])[..., 0]
    b = jnp.cumsum(chunk(g[..., None])[..., 0], axis=-1)
    incl = jnp.tril(jnp.ones((CHUNK, CHUNK), dtype=bool))
    strict = jnp.tril(jnp.ones((CHUNK, CHUNK), dtype=bool), -1)
    decay = jnp.exp(jnp.where(incl, b[..., :, None] - b[..., None, :], -jnp.inf))
    kb = kc * betac[..., None]
    M = jnp.where(strict, jnp.einsum('bhnid,bhnjd->bhnij', kb, kc) * decay, 0.0)
    eye = jnp.eye(CHUNK, dtype=F32)
    rhs = jnp.concatenate([vc * betac[..., None], kb * jnp.exp(b)[..., None]], axis=-1)
    sol = lax.linalg.triangular_solve(M + eye, rhs, left_side=True, lower=True, unit_diagonal=True)
    u, w = sol[..., :dv], sol[..., dv:]
    qk = jnp.einsum('bhnid,bhnjd->bhnij', qc, kc) * decay
    qe = qc * jnp.exp(b)[..., None]
    kd = kc * jnp.exp(b[..., -1:] - b)[..., None]
    blast = jnp.exp(b[..., -1])
    xs = tuple(jnp.moveaxis(a, 2, 0) for a in (u, w, qk, qe, kd, blast))

    def step(S, inp):
        u_i, w_i, qk_i, qe_i, kd_i, bl_i = inp
        vnew = u_i - w_i @ S
        o = qe_i @ S + qk_i @ vnew
        S = S * bl_i[..., None, None] + jnp.swapaxes(kd_i, -1, -2) @ vnew
        return S, o

    _, o = lax.scan(step, jnp.zeros((Bx, H, dk, dv), F32), xs)
    return o.transpose(1, 0, 3, 2, 4).reshape(Bx, T, H, dv)


def gdn_mixer(pd, conv_w, a_log, dt_bias, norm_g):
    B, L, _ = pd.shape
    W, H = GROUP_W, GD_HEADS
    qkv, z, bf, bb, af, ab = jnp.split(pd, [3 * W, 4 * W, 4 * W + H, 4 * W + 2 * H, 4 * W + 3 * H], axis=-1)
    qkv = jax.nn.silu(centred_conv(qkv, conv_w).astype(F32))
    q, k, v = jnp.split(qkv, 3, axis=-1)
    q = l2norm(q.reshape(B, L, H, GD_DK)) * GD_DK ** -0.5
    k = l2norm(k.reshape(B, L, H, GD_DK))
    v = v.reshape(B, L, H, GD_DV)
    a_log = a_log.astype(F32)
    dt_bias = dt_bias.astype(F32)
    beta_f = jax.nn.sigmoid(bf.astype(F32))
    beta_b = jax.nn.sigmoid(bb.astype(F32))
    g_f = -jnp.exp(a_log[0]) * jax.nn.softplus(af.astype(F32) + dt_bias[0])
    g_b = -jnp.exp(a_log[1]) * jax.nn.softplus(ab.astype(F32) + dt_bias[1])
    o = gdn_chunked(stack_dirs(q, q), stack_dirs(k, k), stack_dirs(v, v),
                    stack_dirs(beta_f, beta_b), stack_dirs(g_f, g_b))
    o = merge_dirs(o, B)
    o = rms_norm(o, norm_g) * jax.nn.silu(z.reshape(B, L, H, GD_DV).astype(F32))
    return o.reshape(B, L, GROUP_W)


def mixers(h, l, pos, lb, p):
    proj = h @ p['w_in'][l]
    pa, pb, pc, pd = jnp.split(proj, [COLS_A, COLS_A + COLS_B, COLS_A + COLS_B + COLS_C], axis=-1)
    oa = diff_attn_mixer(pa, pos, l, p['lam_q1'][l], p['lam_k1'][l], p['lam_q2'][l], p['lam_k2'][l],
                         p['attn_norm_g'][l])
    ob = hyena_mixer(pb, p['hy_conv_w'][l], p['hy_conv_b'][l], p['hy_w1'][l], p['hy_b1'][l], p['hy_f1'][l],
                     p['hy_w2'][l], p['hy_b2'][l], p['hy_f2'][l], p['hy_w3'][l], p['hy_decay'][l], p['hy_d'][l])
    oc = hgrn2_mixer(pc, lb, p['hg_norm_g'][l])
    od = gdn_mixer(pd, p['gdn_conv_w'][l], p['gdn_a_log'][l], p['gdn_dt_bias'][l], p['gdn_norm_g'][l])
    o = jnp.concatenate([oa, ob, oc, od], axis=-1).astype(h.dtype)
    return o @ p['w_out'][l]


def conv_ffn(x, w_up, conv_w, conv_b, w_down):
    u = centred_conv(x @ w_up, conv_w, conv_b)
    g, up = jnp.split(u, 2, axis=-1)
    return (jax.nn.silu(g) * up) @ w_down


def trunk(x, p):
    B, S, _ = x.shape
    L = S + N_META
    meta = jnp.broadcast_to(p['meta'][None].astype(x.dtype), (B, N_META, D_MODEL))
    h = layer_norm(jnp.concatenate([meta, x], axis=1), p['emb_ln_g'], p['emb_ln_b'])
    pos = jnp.arange(L)
    sm = jax.nn.softmax(p['hg_lb'].astype(F32), axis=0)
    lb_all = jnp.cumsum(sm, axis=0) - sm[0]
    for l in range(DEPTH):
        m = mixers(h, l, pos, lb_all[l], p)
        h = layer_norm(ALPHA * h + m, p['ln1_g'][l], p['ln1_b'][l])
        f = conv_ffn(h, p['w_up'][l], p['ffn_conv_w'][l], p['ffn_conv_b'][l], p['w_down'][l])
        h = layer_norm(ALPHA * h + f, p['ln2_g'][l], p['ln2_b'][l])
    return h[:, N_META:]


def setup_inputs(seed: int = 0) -> dict:
    key = jax.random.key(seed)
    ks = iter(jax.random.split(key, 40))

    def nrm(shape, scale):
        return jax.random.normal(next(ks), shape, F32) * scale

    def gain(shape):
        return 1.0 + nrm(shape, 0.02)

    x_prompt = nrm((BATCH, SEQ, D_MODEL), 1.0)
    x_sample = nrm((DEC_BATCH, DEC_SEQ, D_MODEL), 1.0)
    meta = nrm((N_META, D_MODEL), 1.0)
    emb_ln_g = gain((D_MODEL,))
    emb_ln_b = nrm((D_MODEL,), 0.02)
    w_in = nrm((DEPTH, D_MODEL, D_IN), D_MODEL ** -0.5)
    lam_q1 = nrm((DEPTH, DA_HD), 0.1)
    lam_k1 = nrm((DEPTH, DA_HD), 0.1)
    lam_q2 = nrm((DEPTH, DA_HD), 0.1)
    lam_k2 = nrm((DEPTH, DA_HD), 0.1)
    attn_norm_g = gain((DEPTH, 2 * DA_HD))
    hy_conv_w = nrm((DEPTH, SHORT_CONV, COLS_B), SHORT_CONV ** -0.5)
    hy_conv_b = nrm((DEPTH, COLS_B), 0.02)
    hy_w1 = nrm((DEPTH, HY_EMB, HY_FFN), HY_EMB ** -0.5)
    hy_b1 = nrm((DEPTH, HY_FFN), 0.1)
    hy_f1 = gain((DEPTH, HY_FFN))
    hy_w2 = nrm((DEPTH, HY_FFN, HY_FFN), HY_FFN ** -0.5)
    hy_b2 = nrm((DEPTH, HY_FFN), 0.1)
    hy_f2 = gain((DEPTH, HY_FFN))
    hy_w3 = nrm((DEPTH, HY_FFN, 2 * HY_CH), HY_FFN ** -0.5)
    rates = jnp.abs(jnp.linspace(math.log(1e-2) / 1.5, math.log(1e-2) / 0.3, HY_CH, dtype=F32))
    hy_decay = rates * (1.0 + nrm((DEPTH, 2, HY_CH), 0.05))
    hy_d = nrm((DEPTH, HY_CH), 1.0)
    hg_lb = nrm((DEPTH, 2, GROUP_W), 0.1)
    hg_norm_g = gain((DEPTH, HG_DV))
    gdn_conv_w = nrm((DEPTH, SHORT_CONV, 3 * GROUP_W), SHORT_CONV ** -0.5)
    gdn_a_log = jnp.log(jax.random.uniform(next(ks), (DEPTH, 2, GD_HEADS), F32, 1.0, 16.0))
    dt = jnp.exp(jax.random.uniform(next(ks), (DEPTH, 2, GD_HEADS), F32, math.log(1e-3), math.log(1e-1)))
    gdn_dt_bias = dt + jnp.log(-jnp.expm1(-dt))
    gdn_norm_g = gain((DEPTH, GD_DV))
    w_out = nrm((DEPTH, D_MODEL, D_MODEL), BETA_INIT * D_MODEL ** -0.5)
    ln1_g = gain((DEPTH, D_MODEL))
    ln1_b = nrm((DEPTH, D_MODEL), 0.02)
    w_up = nrm((DEPTH, D_MODEL, 2 * D_FF), D_MODEL ** -0.5)
    ffn_conv_w = nrm((DEPTH, SHORT_CONV, 2 * D_FF), SHORT_CONV ** -0.5)
    ffn_conv_b = nrm((DEPTH, 2 * D_FF), 0.02)
    w_down = nrm((DEPTH, D_FF, D_MODEL), BETA_INIT * D_FF ** -0.5)
    ln2_g = gain((DEPTH, D_MODEL))
    ln2_b = nrm((DEPTH, D_MODEL), 0.02)
    return {'x_prompt': x_prompt, 'x_sample': x_sample, 'meta': meta, 'emb_ln_g': emb_ln_g,
            'emb_ln_b': emb_ln_b, 'w_in': w_in, 'lam_q1': lam_q1, 'lam_k1': lam_k1, 'lam_q2': lam_q2,
            'lam_k2': lam_k2, 'attn_norm_g': attn_norm_g, 'hy_conv_w': hy_conv_w, 'hy_conv_b': hy_conv_b,
            'hy_w1': hy_w1, 'hy_b1': hy_b1, 'hy_f1': hy_f1, 'hy_w2': hy_w2, 'hy_b2': hy_b2, 'hy_f2': hy_f2,
            'hy_w3': hy_w3, 'hy_decay': hy_decay, 'hy_d': hy_d, 'hg_lb': hg_lb, 'hg_norm_g': hg_norm_g,
            'gdn_conv_w': gdn_conv_w, 'gdn_a_log': gdn_a_log, 'gdn_dt_bias': gdn_dt_bias,
            'gdn_norm_g': gdn_norm_g, 'w_out': w_out, 'ln1_g': ln1_g, 'ln1_b': ln1_b, 'w_up': w_up,
            'ffn_conv_w': ffn_conv_w, 'ffn_conv_b': ffn_conv_b, 'w_down': w_down, 'ln2_g': ln2_g,
            'ln2_b': ln2_b}


def reference(x_prompt, x_sample, meta, emb_ln_g, emb_ln_b, w_in, lam_q1, lam_k1, lam_q2, lam_k2,
              attn_norm_g, hy_conv_w, hy_conv_b, hy_w1, hy_b1, hy_f1, hy_w2, hy_b2, hy_f2, hy_w3,
              hy_decay, hy_d, hg_lb, hg_norm_g, gdn_conv_w, gdn_a_log, gdn_dt_bias, gdn_norm_g,
              w_out, ln1_g, ln1_b, w_up, ffn_conv_w, ffn_conv_b, w_down, ln2_g, ln2_b):
    p = dict(meta=meta, emb_ln_g=emb_ln_g, emb_ln_b=emb_ln_b, w_in=w_in, lam_q1=lam_q1, lam_k1=lam_k1,
             lam_q2=lam_q2, lam_k2=lam_k2, attn_norm_g=attn_norm_g, hy_conv_w=hy_conv_w,
             hy_conv_b=hy_conv_b, hy_w1=hy_w1, hy_b1=hy_b1, hy_f1=hy_f1, hy_w2=hy_w2, hy_b2=hy_b2,
             hy_f2=hy_f2, hy_w3=hy_w3, hy_decay=hy_decay, hy_d=hy_d, hg_lb=hg_lb, hg_norm_g=hg_norm_g,
             gdn_conv_w=gdn_conv_w, gdn_a_log=gdn_a_log, gdn_dt_bias=gdn_dt_bias, gdn_norm_g=gdn_norm_g,
             w_out=w_out, ln1_g=ln1_g, ln1_b=ln1_b, w_up=w_up, ffn_conv_w=ffn_conv_w,
             ffn_conv_b=ffn_conv_b, w_down=w_down, ln2_g=ln2_g, ln2_b=ln2_b)
    y_prompt = trunk(x_prompt, p)
    y_sample = trunk(x_sample, p)
    return (y_prompt, y_sample)
```

```python
import functools
import math

import jax
import jax.numpy as jnp
import numpy as np
from jax import lax
from jax.experimental import pallas as pl
from jax.experimental.pallas import tpu as pltpu

D_MODEL = 2048
DEPTH = 4
N_META = 16
GROUP_W = 512
N_HEADS = 4
HEAD_W = 128
DA_HD = 64
ROT_DIM = 16
ROPE_THETA = 500000.0
HY_BANDS = 16
HY_SHIFT = 0.05
CHUNK = 64
D_FF = 5632
ALPHA = (2.0 * DEPTH) ** 0.25
LN_EPS = 1e-5
RMS_EPS = 1e-6
F32 = jnp.float32
BF16 = jnp.bfloat16
HIGHEST = lax.Precision.HIGHEST

ROW_ALIGN = 256
VMEM_LIMIT = 56 * 1024 * 1024


def _cparams(*sem):
    return pltpu.CompilerParams(dimension_semantics=sem, vmem_limit_bytes=VMEM_LIMIT)


def _valid_rows(tile_idx, tiles_per_seq, tm, padf):
    base = (tile_idx % tiles_per_seq) * tm
    return (base + lax.broadcasted_iota(jnp.int32, (tm, 1), 0)) >= padf


def _ln_rows(y, g, b):
    mu = jnp.mean(y, axis=-1, keepdims=True)
    d = y - mu
    var = jnp.mean(d * d, axis=-1, keepdims=True)
    return d * lax.rsqrt(var + LN_EPS) * g + b


def _embed_kernel(x_ref, meta_ref, g_ref, b_ref, o_ref, *, padf):
    t = pl.program_id(1)
    g = g_ref[...]
    b = b_ref[...]

    @pl.when(t == 0)
    def _():
        rows = o_ref.shape[1]
        o_ref[0, :padf, :] = jnp.zeros((padf, o_ref.shape[2]), F32)
        o_ref[0, padf:rows, :] = _ln_rows(meta_ref[...], g, b)

    @pl.when(t > 0)
    def _():
        o_ref[0] = _ln_rows(x_ref[0], g, b)


def embed(x, meta, g, b, padf):
    nb, s, d = x.shape
    tm = ROW_ALIGN
    assert padf + N_META == tm and s % tm == 0
    lp = s + tm
    return pl.pallas_call(
        functools.partial(_embed_kernel, padf=padf),
        out_shape=jax.ShapeDtypeStruct((nb, lp, d), F32),
        grid=(nb, lp // tm),
        in_specs=[pl.BlockSpec((1, tm, d), lambda bi, t: (bi, jnp.maximum(t - 1, 0), 0)),
                  pl.BlockSpec((N_META, d), lambda bi, t: (0, 0)),
                  pl.BlockSpec((1, d), lambda bi, t: (0, 0)),
                  pl.BlockSpec((1, d), lambda bi, t: (0, 0))],
        out_specs=pl.BlockSpec((1, tm, d), lambda bi, t: (bi, t, 0)),
        compiler_params=_cparams("arbitrary", "arbitrary"),
        name="embed_ln",
    )(x, meta, g.reshape(1, d), b.reshape(1, d))


def _mm_kernel(x_ref, w_ref, o_ref):
    o_ref[...] = jnp.dot(x_ref[...].astype(BF16), w_ref[...],
                         preferred_element_type=F32).astype(o_ref.dtype)


def matmul(x, w, tm, tn, out_dtype=F32):
    m, k = x.shape
    n = w.shape[1]
    assert m % tm == 0 and n % tn == 0
    return pl.pallas_call(
        _mm_kernel,
        out_shape=jax.ShapeDtypeStruct((m, n), out_dtype),
        grid=(n // tn, m // tm),
        in_specs=[pl.BlockSpec((tm, k), lambda j, i: (i, 0)),
                  pl.BlockSpec((k, tn), lambda j, i: (0, j))],
        out_specs=pl.BlockSpec((tm, tn), lambda j, i: (i, j)),
        compiler_params=_cparams("arbitrary", "arbitrary"),
        name="dense_matmul",
    )(x, w)


def _proj_ln_kernel(*refs, n_in, tiles_per_seq, padf):
    o_refs = refs[:n_in]
    w_ref, h_ref, g_ref, b_ref, out_ref = refs[n_in:]
    tm = h_ref.shape[0]
    acc = None
    off = 0
    for r in o_refs:
        kw = r.shape[1]
        part = jnp.dot(r[...].astype(BF16), w_ref[off:off + kw, :], preferred_element_type=F32)
        acc = part if acc is None else acc + part
        off += kw
    y = _ln_rows(ALPHA * h_ref[...] + acc, g_ref[...], b_ref[...])
    valid = _valid_rows(pl.program_id(0), tiles_per_seq, tm, padf)
    out_ref[...] = jnp.where(valid, y, 0.0)


def proj_residual_ln(parts, w, h, g, b, lp, padf, tm):
    m, d = h.shape
    assert m % tm == 0 and lp % tm == 0
    n_in = len(parts)
    in_specs = [pl.BlockSpec((tm, p.shape[1]), lambda i: (i, 0)) for p in parts]
    in_specs += [pl.BlockSpec(w.shape, lambda i: (0, 0)),
                 pl.BlockSpec((tm, d), lambda i: (i, 0)),
                 pl.BlockSpec((1, d), lambda i: (0, 0)),
                 pl.BlockSpec((1, d), lambda i: (0, 0))]
    return pl.pallas_call(
        functools.partial(_proj_ln_kernel, n_in=n_in, tiles_per_seq=lp // tm, padf=padf),
        out_shape=jax.ShapeDtypeStruct((m, d), F32),
        grid=(m // tm,),
        in_specs=in_specs,
        out_specs=pl.BlockSpec((tm, d), lambda i: (i, 0)),
        compiler_params=_cparams("arbitrary"),
        name="proj_residual_ln",
    )(*parts, w, h, g.reshape(1, d), b.reshape(1, d))


HALO = 8


def _halo_specs(tm, width, m, col_block=0):
    nblk = m // HALO
    per = tm // HALO

    def prev_map(*idx):
        i = idx[-1]
        return (jnp.maximum(i * per - 1, 0), col_block)

    def next_map(*idx):
        i = idx[-1]
        return (jnp.minimum((i + 1) * per, nblk - 1), col_block)

    return (pl.BlockSpec((HALO, width), prev_map), pl.BlockSpec((HALO, width), next_map))


def _conv3_rows(pm, prev_row, next_row, cw):
    tm = pm.shape[0]
    rid = lax.broadcasted_iota(jnp.int32, (tm, 1), 0)
    down = jnp.where(rid == 0, prev_row, pltpu.roll(pm, 1, 0))
    up = jnp.where(rid == tm - 1, next_row, pltpu.roll(pm, tm - 1, 0))
    return down * cw[0:1, :] + pm * cw[1:2, :] + up * cw[2:3, :]


def _ffn_up_kernel(x_ref, xp_ref, xn_ref, wg_ref, wu_ref, cwg_ref, cwu_ref, cbg_ref, cbu_ref, o_ref):
    i = pl.program_id(1)
    last = pl.num_programs(1) - 1
    x = x_ref[...].astype(BF16)
    halo = jnp.concatenate([xp_ref[...], xn_ref[...]], axis=0).astype(BF16)
    nmask = jnp.where(i == last, 0.0, 1.0)

    def branch(w_ref, cw_ref, cb_ref):
        w = w_ref[...]
        pm = jnp.dot(x, w, preferred_element_type=F32)
        ph = jnp.dot(halo, w, preferred_element_type=F32)
        return _conv3_rows(pm, ph[HALO - 1:HALO, :], ph[HALO:HALO + 1, :] * nmask, cw_ref[...]) + cb_ref[...]

    g = branch(wg_ref, cwg_ref, cbg_ref)
    u = branch(wu_ref, cwu_ref, cbu_ref)
    o_ref[...] = (g * jax.nn.sigmoid(g) * u).astype(o_ref.dtype)


def ffn_up(h, w_up, conv_w, conv_b, tm, tn):
    m, k = h.shape
    f = w_up.shape[1] // 2
    assert m % tm == 0 and f % tn == 0
    nj = f // tn
    xp_spec, xn_spec = _halo_specs(tm, k, m)
    cb = conv_b.reshape(1, 2 * f)
    return pl.pallas_call(
        _ffn_up_kernel,
        out_shape=jax.ShapeDtypeStruct((m, f), BF16),
        grid=(nj, m // tm),
        in_specs=[pl.BlockSpec((tm, k), lambda j, i: (i, 0)), xp_spec, xn_spec,
                  pl.BlockSpec((k, tn), lambda j, i: (0, j)),
                  pl.BlockSpec((k, tn), lambda j, i: (0, j + nj)),
                  pl.BlockSpec((3, tn), lambda j, i: (0, j)),
                  pl.BlockSpec((3, tn), lambda j, i: (0, j + nj)),
                  pl.BlockSpec((1, tn), lambda j, i: (0, j)),
                  pl.BlockSpec((1, tn), lambda j, i: (0, j + nj))],
        out_specs=pl.BlockSpec((tm, tn), lambda j, i: (i, j)),
        compiler_params=_cparams("arbitrary", "arbitrary"),
        name="ffn_up_conv_gate",
    )(h, h, h, w_up, w_up, conv_w, conv_w, cb, cb)


def _ffn_down_kernel(a_ref, w_ref, h_ref, g_ref, b_ref, out_ref, acc_ref, *, tiles_per_seq, padf):
    kk = pl.program_id(1)
    part = jnp.dot(a_ref[...], w_ref[...], preferred_element_type=F32)

    @pl.when(kk == 0)
    def _():
        acc_ref[...] = part

    @pl.when(kk > 0)
    def _():
        acc_ref[...] += part

    @pl.when(kk == pl.num_programs(1) - 1)
    def _():
        tm = h_ref.shape[0]
        y = _ln_rows(ALPHA * h_ref[...] + acc_ref[...], g_ref[...], b_ref[...])
        valid = _valid_rows(pl.program_id(0), tiles_per_seq, tm, padf)
        out_ref[...] = jnp.where(valid, y, 0.0)


def ffn_down_ln(a, w_down, h, g, b, lp, padf, tm, tk):
    m, d = h.shape
    f = a.shape[1]
    assert m % tm == 0 and f % tk == 0 and lp % tm == 0
    return pl.pallas_call(
        functools.partial(_ffn_down_kernel, tiles_per_seq=lp // tm, padf=padf),
        out_shape=jax.ShapeDtypeStruct((m, d), F32),
        grid=(m // tm, f // tk),
        in_specs=[pl.BlockSpec((tm, tk), lambda i, kk: (i, kk)),
                  pl.BlockSpec((tk, d), lambda i, kk: (kk, 0)),
                  pl.BlockSpec((tm, d), lambda i, kk: (i, 0)),
                  pl.BlockSpec((1, d), lambda i, kk: (0, 0)),
                  pl.BlockSpec((1, d), lambda i, kk: (0, 0))],
        out_specs=pl.BlockSpec((tm, d), lambda i, kk: (i, 0)),
        scratch_shapes=[pltpu.VMEM((tm, d), F32)],
        compiler_params=_cparams("arbitrary", "arbitrary"),
        name="ffn_down_residual_ln",
    )(a, w_down, h, g.reshape(1, d), b.reshape(1, d))


def rope_tables(lp, padf):
    half = ROT_DIM // 2
    pos = (jnp.arange(lp) - padf).astype(F32)
    inv = 1.0 / (ROPE_THETA ** (jnp.arange(half, dtype=F32) / half))
    ang = pos[:, None] * inv[None]
    cos, sin = jnp.cos(ang), jnp.sin(ang)
    ones = jnp.ones((lp, DA_HD - ROT_DIM), F32)
    zeros = jnp.zeros((lp, DA_HD - ROT_DIM), F32)
    zh = jnp.zeros((lp, half), F32)
    c = jnp.concatenate([cos, cos, ones], axis=1)
    sa = jnp.concatenate([-sin, zh, zeros], axis=1)
    sb = jnp.concatenate([zh, sin, zeros], axis=1)
    return tuple(jnp.concatenate([t, t], axis=1) for t in (c, sa, sb))


def _attn_prep_kernel(p_ref, c_ref, sa_ref, sb_ref, qlo_ref, qhi_ref, k_ref, v_ref):
    w = GROUP_W
    reps = w // c_ref.shape[1]
    c = jnp.concatenate([c_ref[...]] * reps, axis=1)
    sa = jnp.concatenate([sa_ref[...]] * reps, axis=1)
    sb = jnp.concatenate([sb_ref[...]] * reps, axis=1)
    half = ROT_DIM // 2

    def rot(x):
        return x * c + pltpu.roll(x, w - half, 1) * sa + pltpu.roll(x, half, 1) * sb

    q = rot(p_ref[:, 0:w]) * (DA_HD ** -0.5)
    lane = lax.broadcasted_iota(jnp.int32, (1, w), 1) % (2 * DA_HD)
    qlo_ref[...] = jnp.where(lane < DA_HD, q, 0.0).astype(BF16)
    qhi_ref[...] = jnp.where(lane >= DA_HD, q, 0.0).astype(BF16)
    k_ref[...] = rot(p_ref[:, w:2 * w]).astype(BF16)
    v_ref[...] = p_ref[:, 2 * w:3 * w].astype(BF16)


def attn_prep(proj, tables, lp, tm):
    m = proj.shape[0]
    w = GROUP_W
    tps = lp // tm
    tspec = pl.BlockSpec((tm, 2 * DA_HD), lambda i: (i % tps, 0))
    ospec = pl.BlockSpec((tm, w), lambda i: (i, 0))
    return pl.pallas_call(
        _attn_prep_kernel,
        out_shape=tuple(jax.ShapeDtypeStruct((m, w), BF16) for _ in range(4)),
        grid=(m // tm,),
        in_specs=[pl.BlockSpec((tm, 3 * w), lambda i: (i, 0)), tspec, tspec, tspec],
        out_specs=(ospec, ospec, ospec, ospec),
        compiler_params=_cparams("arbitrary"),
        name="attn_prep_rotary",
    )(proj, *tables)


def _attn_kernel(lam_ref, g_ref, qlo_ref, qhi_ref, k_ref, v_ref, o_ref, qs_ref, m_ref, l_ref, acc_ref,
                 *, tkb, padf, lam_init):
    tq = qlo_ref.shape[0]
    nkb = k_ref.shape[0] // tkb
    qs_ref[:tq, :] = qlo_ref[...]
    qs_ref[tq:, :] = qhi_ref[...]
    m_ref[...] = jnp.full(m_ref.shape, -1e30, F32)
    l_ref[...] = jnp.zeros(l_ref.shape, F32)
    acc_ref[...] = jnp.zeros(acc_ref.shape, F32)

    def step(kb, masked):
        start = kb * tkb
        if not isinstance(start, int):
            start = pl.multiple_of(start, tkb)
        kblk = k_ref[pl.ds(start, tkb), :]
        s = lax.dot_general(qs_ref[...], kblk, (((1,), (1,)), ((), ())), preferred_element_type=F32)
        if masked:
            col = lax.broadcasted_iota(jnp.int32, (1, tkb), 1)
            s = jnp.where(col >= padf, s, -1e30)
        m_prev = m_ref[...]
        m_new = jnp.maximum(m_prev, jnp.max(s, axis=-1, keepdims=True))
        alpha = jnp.exp(m_prev - m_new)
        p = jnp.exp(s - m_new)
        l_ref[...] = alpha * l_ref[...] + jnp.sum(p, axis=-1, keepdims=True)
        pv = jnp.dot(p.astype(BF16), v_ref[pl.ds(start, tkb), :], preferred_element_type=F32)
        acc_ref[...] = alpha * acc_ref[...] + pv
        m_ref[...] = m_new

    step(0, True)

    def body(kb, carry):
        step(kb, False)
        return carry

    lax.fori_loop(1, nkb, body, 0)

    lam = (jnp.exp(jnp.sum(lam_ref[0:1, :] * lam_ref[1:2, :], axis=-1, keepdims=True))
           - jnp.exp(jnp.sum(lam_ref[2:3, :] * lam_ref[3:4, :], axis=-1, keepdims=True)) + lam_init)
    o = acc_ref[...] / l_ref[...]
    o = o[:tq] - lam * o[tq:]
    o = o * lax.rsqrt(jnp.mean(o * o, axis=-1, keepdims=True) + RMS_EPS) * g_ref[...]
    o_ref[...] = (o * (1.0 - lam_init)).astype(o_ref.dtype)


def diff_attention(qlo, qhi, k, v, lam_vecs, norm_g, nb, lp, padf, layer, tq, tkb):
    m = qlo.shape[0]
    assert lp % tq == 0 and lp % tkb == 0 and padf < tkb
    nq = lp // tq
    lam_init = 0.8 - 0.6 * math.exp(-0.3 * layer)
    qspec = pl.BlockSpec((tq, HEAD_W), lambda b, h, qi: (b * nq + qi, h))
    kspec = pl.BlockSpec((lp, HEAD_W), lambda b, h, qi: (b, h))
    return pl.pallas_call(
        functools.partial(_attn_kernel, tkb=tkb, padf=padf, lam_init=lam_init),
        out_shape=jax.ShapeDtypeStruct((m, GROUP_W), BF16),
        grid=(nb, N_HEADS, nq),
        in_specs=[pl.BlockSpec((4, DA_HD), lambda b, h, qi: (0, 0)),
                  pl.BlockSpec((1, HEAD_W), lambda b, h, qi: (0, 0)),
                  qspec, qspec, kspec, kspec],
        out_specs=pl.BlockSpec((tq, HEAD_W), lambda b, h, qi: (b * nq + qi, h)),
        scratch_shapes=[pltpu.VMEM((2 * tq, HEAD_W), BF16), pltpu.VMEM((2 * tq, 1), F32),
                        pltpu.VMEM((2 * tq, 1), F32), pltpu.VMEM((2 * tq, HEAD_W), F32)],
        compiler_params=_cparams("arbitrary", "arbitrary", "arbitrary"),
        name="diff_attention",
    )(lam_vecs, norm_g.reshape(1, HEAD_W), qlo, qhi, k, v)


N_LEVELS = 6


def _scan_constants():
    c = CHUNK
    idx = np.arange(c)
    cs = np.zeros((2, (N_LEVELS + 2) * c, c), np.float32)
    mask = np.zeros((2, N_LEVELS + 2, c, c), np.float32)
    for d in range(2):
        cum = (idx[None, :] <= idx[:, None]) if d == 0 else (idx[None, :] >= idx[:, None])
        cs[d, 0:c] = cum
        for l in range(N_LEVELS):
            s = (c // 2) >> l
            blk = idx // s
            ref = (blk | 1) * s - 1 if d == 0 else (blk | 1) * s
            cs[d, (l + 1) * c:(l + 2) * c] = cum[ref]
            if d == 0:
                mask[d, l] = ((blk[:, None] & 1) == 1) & (blk[None, :] == blk[:, None] - 1)
            else:
                mask[d, l] = ((blk[:, None] & 1) == 0) & (blk[None, :] == blk[:, None] + 1)
        cs[d, (N_LEVELS + 1) * c:] = 1.0
        mask[d, N_LEVELS] = cum
        mask[d, N_LEVELS + 1] = cum & (idx[None, :] != idx[:, None])
    return jnp.asarray(cs), jnp.asarray(mask)


def _nt(a, b):
    return lax.dot_general(a.astype(BF16), b.astype(BF16), (((1,), (1,)), ((), ())),
                           preferred_element_type=F32)


def _tn(a, b):
    return lax.dot_general(a.astype(BF16), b.astype(BF16), (((0,), (0,)), ((), ())),
                           preferred_element_type=F32)


def _nn(a, b):
    return jnp.dot(a.astype(BF16), b.astype(BF16), preferred_element_type=F32)


def _hgrn_prep_kernel(p_ref, lb_ref, q_ref, v_ref, g_ref, sg_ref):
    w = GROUP_W
    q = p_ref[:, 0:w]
    q_ref[...] = (q * jax.nn.sigmoid(q) * (HEAD_W ** -0.5)).astype(BF16)
    for d in range(2):
        lb = lb_ref[d:d + 1, :]
        f = p_ref[:, (1 + d) * w:(2 + d) * w]
        g_ref[d] = jnp.log(lb + (1.0 - lb) * jax.nn.sigmoid(f))
    v_ref[...] = p_ref[:, 3 * w:4 * w].astype(BF16)
    gate = p_ref[:, 4 * w:5 * w]
    sg_ref[...] = (gate * jax.nn.sigmoid(gate)).astype(BF16)


def hgrn_prep(proj, col_block, lb, tm):
    m = proj.shape[0]
    w = GROUP_W
    ospec = pl.BlockSpec((tm, w), lambda i: (i, 0))
    return pl.pallas_call(
        _hgrn_prep_kernel,
        out_shape=(jax.ShapeDtypeStruct((m, w), BF16), jax.ShapeDtypeStruct((m, w), BF16),
                   jax.ShapeDtypeStruct((2, m, w), F32), jax.ShapeDtypeStruct((m, w), BF16)),
        grid=(m // tm,),
        in_specs=[pl.BlockSpec((tm, 5 * w), lambda i: (i, col_block)),
                  pl.BlockSpec((2, w), lambda i: (0, 0))],
        out_specs=(ospec, ospec, pl.BlockSpec((2, tm, w), lambda i: (0, i, 0)), ospec),
        compiler_params=_cparams("arbitrary"),
        name="hgrn_prep",
    )(proj, lb)


def _hgrn_scan_kernel(cs_ref, mask_ref, q_ref, v_ref, g_ref, o_ref, s_ref, *, nch):
    c = CHUNK
    d = pl.program_id(1)

    @pl.when(pl.program_id(3) == 0)
    def _():
        s_ref[...] = jnp.zeros(s_ref.shape, F32)

    eye = mask_ref[0, N_LEVELS] - mask_ref[0, N_LEVELS + 1]
    for ci in range(nch):
        r0 = pl.multiple_of((ci + d * (nch - 1 - 2 * ci)) * c, c)
        g = g_ref[0, pl.ds(r0, c), :]
        q = q_ref[pl.ds(r0, c), :].astype(F32)
        v = v_ref[pl.ds(r0, c), :]
        big = jnp.dot(cs_ref[0], g, precision=HIGHEST, preferred_element_type=F32)
        b = big[0:c]
        tot = big[(N_LEVELS + 1) * c:(N_LEVELS + 2) * c]
        k = 1.0 - jnp.exp(g)
        a = eye * jnp.sum(q * k, axis=-1, keepdims=True)
        for l in range(N_LEVELS):
            e = jnp.exp(-jnp.abs(b - big[(l + 1) * c:(l + 2) * c]))
            a = a + mask_ref[0, l] * _nt(q * e, k * e)
        s_old = s_ref[...]
        o = _nn(a, v) + _nt(q * jnp.exp(b), s_old)
        o_ref[0, pl.ds(r0, c), :] = o
        s_ref[...] = s_old * jnp.exp(tot[0:1, :]) + _tn(v, k * jnp.exp(tot - b))


def hgrn_scan(q, v, g, consts, nb, lp, tb):
    m = q.shape[0]
    cs, mask = consts
    nblk = lp // tb

    def row(b, d, h, t):
        return b * nblk + t + d * (nblk - 1 - 2 * t)

    return pl.pallas_call(
        functools.partial(_hgrn_scan_kernel, nch=tb // CHUNK),
        out_shape=jax.ShapeDtypeStruct((2, m, GROUP_W), F32),
        grid=(nb, 2, N_HEADS, nblk),
        in_specs=[pl.BlockSpec((1,) + cs.shape[1:], lambda b, d, h, t: (d, 0, 0)),
                  pl.BlockSpec((1,) + mask.shape[1:], lambda b, d, h, t: (d, 0, 0, 0)),
                  pl.BlockSpec((tb, HEAD_W), lambda b, d, h, t: (row(b, d, h, t), h)),
                  pl.BlockSpec((tb, HEAD_W), lambda b, d, h, t: (row(b, d, h, t), h)),
                  pl.BlockSpec((1, tb, HEAD_W), lambda b, d, h, t: (d, row(b, d, h, t), h))],
        out_specs=pl.BlockSpec((1, tb, HEAD_W), lambda b, d, h, t: (d, row(b, d, h, t), h)),
        scratch_shapes=[pltpu.VMEM((HEAD_W, HEAD_W), F32)],
        compiler_params=_cparams("arbitrary", "arbitrary", "arbitrary", "arbitrary"),
        name="hgrn_scan",
    )(cs, mask, q, v, g)


def _gated_norm_kernel(o_ref, sg_ref, g_ref, out_ref):
    o = o_ref[0] + o_ref[1]
    gain = g_ref[...]
    for h in range(N_HEADS):
        sl = slice(h * HEAD_W, (h + 1) * HEAD_W)
        oh = o[:, sl]
        y = oh * lax.rsqrt(jnp.mean(oh * oh, axis=-1, keepdims=True) + RMS_EPS) * gain
        out_ref[:, sl] = (y * sg_ref[:, sl].astype(F32)).astype(out_ref.dtype)


def gated_norm(o2, sgate, norm_g, tm):
    m = sgate.shape[0]
    w = GROUP_W
    return pl.pallas_call(
        _gated_norm_kernel,
        out_shape=jax.ShapeDtypeStruct((m, w), BF16),
        grid=(m // tm,),
        in_specs=[pl.BlockSpec((2, tm, w), lambda i: (0, i, 0)),
                  pl.BlockSpec((tm, w), lambda i: (i, 0)),
                  pl.BlockSpec((1, HEAD_W), lambda i: (0, 0))],
        out_specs=pl.BlockSpec((tm, w), lambda i: (i, 0)),
        compiler_params=_cparams("arbitrary"),
        name="gated_rms_norm",
    )(o2, sgate, norm_g.reshape(1, HEAD_W))


GATE_LANES = 128


def _softplus(x):
    return jnp.maximum(x, 0.0) + jnp.log(1.0 + jnp.exp(-jnp.abs(x)))


def _gdn_prep_kernel(p_ref, pp_ref, pn_ref, cw_ref, acoef_ref, dtb_ref,
                     q_ref, k_ref, v_ref, sz_ref, gates_ref, *, tiles_per_seq, padf):
    w = GROUP_W
    i = pl.program_id(0)
    tm = p_ref.shape[0]
    nmask = jnp.where(i == pl.num_programs(0) - 1, 0.0, 1.0)
    x = _conv3_rows(p_ref[:, 0:3 * w], pp_ref[HALO - 1:HALO, :], pn_ref[0:1, :] * nmask, cw_ref[...])
    x = x * jax.nn.sigmoid(x)
    valid = _valid_rows(i, tiles_per_seq, tm, padf)
    for h in range(N_HEADS):
        sl = slice(h * HEAD_W, (h + 1) * HEAD_W)
        qh = x[:, sl]
        q_ref[:, sl] = (qh * lax.rsqrt(jnp.sum(qh * qh, axis=-1, keepdims=True) + 1e-6)
                        * (HEAD_W ** -0.5)).astype(BF16)
        kh = x[:, w + h * HEAD_W:w + (h + 1) * HEAD_W]
        k_ref[:, sl] = (kh * lax.rsqrt(jnp.sum(kh * kh, axis=-1, keepdims=True) + 1e-6)).astype(BF16)
    v_ref[...] = jnp.where(valid, x[:, 2 * w:3 * w], 0.0).astype(BF16)
    z = p_ref[:, 3 * w:4 * w]
    sz_ref[...] = (z * jax.nn.sigmoid(z)).astype(BF16)
    gt = p_ref[:, 4 * w:4 * w + GATE_LANES]
    lane = lax.broadcasted_iota(jnp.int32, (1, GATE_LANES), 1)
    decay = -acoef_ref[...] * _softplus(gt + dtb_ref[...])
    gates_ref[...] = jnp.where(lane < 2 * N_HEADS, jax.nn.sigmoid(gt), decay)


def gdn_prep(pd, conv_w, a_log, dt_bias, lp, padf, tm):
    m = pd.shape[0]
    w = GROUP_W
    zeros8 = jnp.zeros((2 * N_HEADS,), F32)
    padl = jnp.zeros((GATE_LANES - 4 * N_HEADS,), F32)
    acoef = jnp.concatenate([zeros8, jnp.exp(a_log).reshape(-1), padl]).reshape(1, GATE_LANES)
    dtb = jnp.concatenate([zeros8, dt_bias.reshape(-1), padl]).reshape(1, GATE_LANES)
    pspec, nspec = _halo_specs(tm, 3 * w, m)
    ospec = pl.BlockSpec((tm, w), lambda i: (i, 0))
    return pl.pallas_call(
        functools.partial(_gdn_prep_kernel, tiles_per_seq=lp // tm, padf=padf),
        out_shape=tuple(jax.ShapeDtypeStruct((m, w), BF16) for _ in range(4))
        + (jax.ShapeDtypeStruct((m, GATE_LANES), F32),),
        grid=(m // tm,),
        in_specs=[pl.BlockSpec((tm, pd.shape[1]), lambda i: (i, 0)), pspec, nspec,
                  pl.BlockSpec((3, 3 * w), lambda i: (0, 0)),
                  pl.BlockSpec((1, GATE_LANES), lambda i: (0, 0)),
                  pl.BlockSpec((1, GATE_LANES), lambda i: (0, 0))],
        out_specs=(ospec, ospec, ospec, ospec, pl.BlockSpec((tm, GATE_LANES), lambda i: (i, 0))),
        compiler_params=_cparams("arbitrary"),
        name="gdn_prep",
    )(pd, pd, pd, conv_w, acoef, dtb)


def _gate_selectors():
    sel = np.zeros((2, N_HEADS, 2, GATE_LANES, HEAD_W), np.float32)
    for d in range(2):
        for h in range(N_HEADS):
            sel[d, h, 0, N_HEADS * d + h, :] = 1.0
            sel[d, h, 1, 2 * N_HEADS + N_HEADS * d + h, :] = 1.0
    return jnp.asarray(sel)


def _gdn_scan_kernel(cs_ref, mask_ref, sel_ref, q_ref, k_ref, v_ref, gt_ref, o_ref, s_ref, *, nch):
    c = CHUNK
    d = pl.program_id(1)

    @pl.when(pl.program_id(3) == 0)
    def _():
        s_ref[...] = jnp.zeros(s_ref.shape, F32)

    incl = mask_ref[0, N_LEVELS]
    strict = mask_ref[0, N_LEVELS + 1]
    eye = incl - strict
    cum = cs_ref[0, 0:c, :]
    ones = cs_ref[0, (N_LEVELS + 1) * c:(N_LEVELS + 2) * c, :]

    def hdot(a, b):
        return jnp.dot(a, b, precision=HIGHEST, preferred_element_type=F32)

    for ci in range(nch):
        r0 = pl.multiple_of((ci + d * (nch - 1 - 2 * ci)) * c, c)
        gt = gt_ref[pl.ds(r0, c), :]
        q = q_ref[pl.ds(r0, c), :].astype(F32)
        k = k_ref[pl.ds(r0, c), :].astype(F32)
        v = v_ref[pl.ds(r0, c), :].astype(F32)
        beta = hdot(gt, sel_ref[0, 0, 0])
        glog = hdot(gt, sel_ref[0, 0, 1])
        bb = hdot(cum, glog)
        tot = hdot(ones, glog)
        bc = bb[:, 0:c]
        br = hdot(ones, eye * bc)
        dec = incl * jnp.exp(jnp.minimum(bc - br, 0.0))
        kb = k * beta
        n = strict * _nt(kb, k) * dec
        t = eye - n
        pw = _nn(n, n)
        for _ in range(N_LEVELS - 2):
            t = t + _nn(t, pw)
            pw = _nn(pw, pw)
        t = t + _nn(t, pw)
        eb = jnp.exp(bb)
        u = _nn(t, v * beta)
        wm = _nn(t, kb * eb)
        qk = _nt(q, k) * dec
        s_old = s_ref[...]
        vnew = u - _nn(wm, s_old)
        o_ref[0, pl.ds(r0, c), :] = _nn(q * eb, s_old) + _nn(qk, vnew)
        s_ref[...] = s_old * jnp.exp(tot[0:1, :]) + _tn(k * jnp.exp(tot - bb), vnew)


def gdn_scan(q, k, v, gates, consts, sel, nb, lp, tb):
    m = q.shape[0]
    cs, mask = consts
    nblk = lp // tb

    def row(b, d, h, t):
        return b * nblk + t + d * (nblk - 1 - 2 * t)

    hspec = pl.BlockSpec((tb, HEAD_W), lambda b, d, h, t: (row(b, d, h, t), h))
    return pl.pallas_call(
        functools.partial(_gdn_scan_kernel, nch=tb // CHUNK),
        out_shape=jax.ShapeDtypeStruct((2, m, GROUP_W), F32),
        grid=(nb, 2, N_HEADS, nblk),
        in_specs=[pl.BlockSpec((1,) + cs.shape[1:], lambda b, d, h, t: (d, 0, 0)),
                  pl.BlockSpec((1,) + mask.shape[1:], lambda b, d, h, t: (d, 0, 0, 0)),
                  pl.BlockSpec((1, 1) + sel.shape[2:], lambda b, d, h, t: (d, h, 0, 0, 0)),
                  hspec, hspec, hspec,
                  pl.BlockSpec((tb, GATE_LANES), lambda b, d, h, t: (row(b, d, h, t), 0))],
        out_specs=pl.BlockSpec((1, tb, HEAD_W), lambda b, d, h, t: (d, row(b, d, h, t), h)),
        scratch_shapes=[pltpu.VMEM((HEAD_W, HEAD_W), F32)],
        compiler_params=_cparams("arbitrary", "arbitrary", "arbitrary", "arbitrary"),
        name="gdn_scan",
    )(cs, mask, sel, q, k, v, gates)


FFT_N2 = 256
FFT_TC = 8192


def _fft_n1(lp, l_real):
    need = -(-(2 * l_real - 1) // FFT_N2)
    return max(-(-need // 8) * 8, -(-(lp // FFT_N2) // 8) * 8)


def _outer_dft(n1, k_rows, inverse, n_total):
    kp = -(-k_rows // 8) * 8
    a = np.arange(n1)[:, None] * np.arange(kp)[None, :]
    ang = 2.0 * np.pi * (a % n1) / n1
    c, s = np.cos(ang), np.sin(ang)
    live = (np.arange(kp) < k_rows)[None, :]
    c, s = c * live, s * live
    if not inverse:
        return jnp.asarray(np.concatenate([c, -s], axis=0), F32)
    return jnp.asarray(np.concatenate([c.T, -s.T], axis=1) / n_total, F32)


def inner_dft(n1):
    n2 = FFT_N2
    n = n1 * n2
    k = jnp.arange(n1, dtype=jnp.int32)[:, None, None] + n1 * jnp.arange(n2, dtype=jnp.int32)[None, :, None]
    r = (k * jnp.arange(n2, dtype=jnp.int32)[None, None, :]) % n
    ang = r.astype(F32) * (2.0 * math.pi / n)
    gr, gi = jnp.cos(ang), -jnp.sin(ang)
    gg = jnp.concatenate([jnp.concatenate([gr, -gi], axis=2), jnp.concatenate([gi, gr], axis=2)], axis=1)
    return gg.astype(BF16), jnp.swapaxes(gg, 1, 2).astype(BF16)


def _hyena_prep_kernel(p_ref, pp_ref, pn_ref, cw_ref, cb_ref, x0_ref, z_ref, *, tiles_per_seq, padf):
    w = GROUP_W
    i = pl.program_id(0)
    tm = p_ref.shape[0]
    nmask = jnp.where(i == pl.num_programs(0) - 1, 0.0, 1.0)
    u = _conv3_rows(p_ref[...], pp_ref[HALO - 1:HALO, :], pn_ref[0:1, :] * nmask, cw_ref[...]) + cb_ref[...]
    valid = _valid_rows(i, tiles_per_seq, tm, padf)
    x0_ref[...] = u[:, 0:w]
    z_ref[...] = jnp.where(valid, u[:, w:2 * w] * u[:, 2 * w:3 * w], 0.0)


def hyena_prep(pb, conv_w, conv_b, lp, padf, tm):
    m, w3 = pb.shape
    w = GROUP_W
    pspec, nspec = _halo_specs(tm, w3, m)
    ospec = pl.BlockSpec((tm, w), lambda i: (i, 0))
    return pl.pallas_call(
        functools.partial(_hyena_prep_kernel, tiles_per_seq=lp // tm, padf=padf),
        out_shape=(jax.ShapeDtypeStruct((m, w), F32), jax.ShapeDtypeStruct((m, w), F32)),
        grid=(m // tm,),
        in_specs=[pl.BlockSpec((tm, w3), lambda i: (i, 0)), pspec, nspec,
                  pl.BlockSpec((3, w3), lambda i: (0, 0)), pl.BlockSpec((1, w3), lambda i: (0, 0))],
        out_specs=(ospec, ospec),
        compiler_params=_cparams("arbitrary"),
        name="hyena_prep",
    )(pb, pb, pb, conv_w, conv_b.reshape(1, w3))


def _hyena_filter_kernel(w1t_ref, w1c_ref, w1s_ref, b1_ref, f1_ref, w2_ref, b2_ref, f2_ref, w3_ref, dec_ref,
                         filt_ref, asum_ref, *, l_real):
    i = pl.program_id(0)
    tm = filt_ref.shape[0]
    w = GROUP_W
    row = i * tm + lax.broadcasted_iota(jnp.int32, (tm, 1), 0)
    rf = row.astype(F32)
    t = rf * (1.0 / (l_real - 1))
    band = lax.broadcasted_iota(jnp.int32, (1, HY_BANDS), 1).astype(F32)
    bands = 1e-4 + band * ((HY_BANDS - 1 - 1e-4) / (HY_BANDS - 1))
    ang = ((2.0 * math.pi / l_real) * rf) * bands

    def hdot(a, b):
        return jnp.dot(a, b, precision=HIGHEST, preferred_element_type=F32)

    pre = t * w1t_ref[...] + hdot(jnp.cos(ang), w1c_ref[...]) - hdot(jnp.sin(ang), w1s_ref[...])
    hid = jnp.sin(f1_ref[...] * (pre + b1_ref[...]))
    hid = jnp.sin(f2_ref[...] * (hdot(hid, w2_ref[...]) + b2_ref[...]))
    filt = hdot(hid, w3_ref[...])
    window = jnp.exp(-t * jnp.abs(dec_ref[...])) + HY_SHIFT
    filt = jnp.where(row < l_real, filt * window, 0.0)

    @pl.when(i == 0)
    def _():
        asum_ref[...] = jnp.zeros(asum_ref.shape, F32)

    asum_ref[...] += jnp.sum(jnp.abs(filt), axis=0, keepdims=True)
    lane = lax.broadcasted_iota(jnp.int32, (1, 2 * w), 1)
    filt_ref[...] = jnp.where((row == 0) & (lane >= w), 0.0, filt)


def hyena_filter(w1, b1, f1, w2, b2, f2, w3, decay, lp, l_real, tm):
    w = GROUP_W
    nf = w1.shape[1]
    small = lambda a: pl.BlockSpec(a.shape, lambda i: (0,) * a.ndim)
    args = (w1[0:1], w1[1:1 + HY_BANDS], w1[1 + HY_BANDS:], b1.reshape(1, nf), f1.reshape(1, nf),
            w2, b2.reshape(1, nf), f2.reshape(1, nf), w3, decay.reshape(1, 2 * w))
    return pl.pallas_call(
        functools.partial(_hyena_filter_kernel, l_real=l_real),
        out_shape=(jax.ShapeDtypeStruct((lp, 2 * w), F32), jax.ShapeDtypeStruct((1, 2 * w), F32)),
        grid=(lp // tm,),
        in_specs=[small(a) for a in args],
        out_specs=(pl.BlockSpec((tm, 2 * w), lambda i: (i, 0)), pl.BlockSpec((1, 2 * w), lambda i: (0, 0))),
        compiler_params=_cparams("arbitrary"),
        name="hyena_filter",
    )(*args)


def _fft_outer_kernel(f_ref, x_ref, o_ref, pad_ref):
    k = x_ref.shape[1]
    pad_ref[...] = jnp.zeros(pad_ref.shape, F32)
    pad_ref[0:k, :] = x_ref[0]
    o_ref[0] = jnp.dot(f_ref[...], pad_ref[...], precision=HIGHEST, preferred_element_type=F32)


def fft_outer(x, fmat):
    b, k, cols = x.shape
    rows, kp = fmat.shape
    tc = FFT_TC
    return pl.pallas_call(
        _fft_outer_kernel,
        out_shape=jax.ShapeDtypeStruct((b, rows, cols), F32),
        grid=(b, cols // tc),
        in_specs=[pl.BlockSpec((rows, kp), lambda bi, j: (0, 0)),
                  pl.BlockSpec((1, k, tc), lambda bi, j: (bi, 0, j))],
        out_specs=pl.BlockSpec((1, rows, tc), lambda bi, j: (bi, 0, j)),
        scratch_shapes=[pltpu.VMEM((kp, tc), F32)],
        compiler_params=_cparams("arbitrary", "arbitrary"),
        name="fft_outer",
    )(fmat, x)


def _filter_spectrum_kernel(gg_ref, a_ref, asum_ref, o_ref):
    w = GROUP_W
    n2 = FFT_N2
    a = jnp.concatenate([a_ref[0, 0], a_ref[1, 0]], axis=0).astype(BF16)
    x = jnp.dot(gg_ref[0], a, preferred_element_type=F32)
    s = 1.0 / (asum_ref[:, 0:w] + asum_ref[:, w:2 * w] + 1e-6)
    o_ref[0, 0] = (x[0:n2, 0:w] + x[0:n2, w:2 * w]) * s
    o_ref[0, 1] = (x[n2:2 * n2, 0:w] - x[n2:2 * n2, w:2 * w]) * s


def filter_spectrum(a, gg, asum):
    _, n1, n2, w2 = a.shape
    w = w2 // 2
    return pl.pallas_call(
        _filter_spectrum_kernel,
        out_shape=jax.ShapeDtypeStruct((n1, 2, n2, w), F32),
        grid=(n1,),
        in_specs=[pl.BlockSpec((1, 2 * n2, 2 * n2), lambda k: (k, 0, 0)),
                  pl.BlockSpec((2, 1, n2, w2), lambda k: (0, k, 0, 0)),
                  pl.BlockSpec((1, w2), lambda k: (0, 0))],
        out_specs=pl.BlockSpec((1, 2, n2, w), lambda k: (k, 0, 0, 0)),
        compiler_params=_cparams("arbitrary"),
        name="hyena_filter_spectrum",
    )(gg, a, asum)


def _fft_mid_kernel(gg_ref, ggt_ref, kf_ref, a_ref, o_ref):
    n2 = FFT_N2
    a = jnp.concatenate([a_ref[0, 0, 0], a_ref[0, 1, 0]], axis=0).astype(BF16)
    x = jnp.dot(gg_ref[0], a, preferred_element_type=F32)
    xr, xi = x[0:n2], x[n2:2 * n2]
    kr, ki = kf_ref[0, 0], kf_ref[0, 1]
    y = jnp.concatenate([xr * kr - xi * ki, xr * ki + xi * kr], axis=0).astype(BF16)
    bm = jnp.dot(ggt_ref[0], y, preferred_element_type=F32)
    o_ref[0, 0, 0] = bm[0:n2]
    o_ref[0, 1, 0] = bm[n2:2 * n2]


def fft_mid(a, gg, ggt, kf):
    b, _, n1, n2, w = a.shape
    aspec = pl.BlockSpec((1, 2, 1, n2, w), lambda k, bi: (bi, 0, k, 0, 0))
    gspec = pl.BlockSpec((1, 2 * n2, 2 * n2), lambda k, bi: (k, 0, 0))
    return pl.pallas_call(
        _fft_mid_kernel,
        out_shape=jax.ShapeDtypeStruct(a.shape, F32),
        grid=(n1, b),
        in_specs=[gspec, gspec, pl.BlockSpec((1, 2, n2, w), lambda k, bi: (k, 0, 0, 0)), aspec],
        out_specs=aspec,
        compiler_params=_cparams("arbitrary", "arbitrary"),
        name="fft_inner_conv",
    )(gg, ggt, kf, a)


def _fft_final_kernel(f_ref, b_ref, x0_ref, z_ref, d_ref, o_ref):
    k = o_ref.shape[1]
    y = jnp.dot(f_ref[...], b_ref[0], precision=HIGHEST, preferred_element_type=F32)
    o_ref[0] = x0_ref[0] * (y[0:k] + z_ref[0] * d_ref[...])


def fft_final(bm, finv, x0, z, d_tiled):
    b, rows, cols = bm.shape
    k = x0.shape[1]
    kp = finv.shape[0]
    tc = FFT_TC
    xspec = pl.BlockSpec((1, k, tc), lambda bi, j: (bi, 0, j))
    return pl.pallas_call(
        _fft_final_kernel,
        out_shape=jax.ShapeDtypeStruct((b, k, cols), F32),
        grid=(b, cols // tc),
        in_specs=[pl.BlockSpec((kp, rows), lambda bi, j: (0, 0)),
                  pl.BlockSpec((1, rows, tc), lambda bi, j: (bi, 0, j)),
                  xspec, xspec, pl.BlockSpec((1, tc), lambda bi, j: (0, 0))],
        out_specs=xspec,
        compiler_params=_cparams("arbitrary", "arbitrary"),
        name="fft_outer_inverse_gate",
    )(finv, bm, x0, z, d_tiled)


def hyena_mixer(pb, hp, nb, lp, padf, l_real, tm):
    w = GROUP_W
    n2 = FFT_N2
    k1 = lp // n2
    n1 = _fft_n1(lp, l_real)
    n_total = n1 * n2
    ffwd = _outer_dft(n1, k1, False, n_total)
    finv = _outer_dft(n1, k1, True, n_total)
    gg, ggt = hp['gg'], hp['ggt']
    filt, asum = hyena_filter(hp['w1'], hp['b1'], hp['f1'], hp['w2'], hp['b2'], hp['f2'], hp['w3'],
                              hp['decay'], lp, l_real, tm)
    fa = fft_outer(filt.reshape(1, k1, n2 * 2 * w), ffwd)
    kf = filter_spectrum(fa.reshape(2, n1, n2, 2 * w), gg, asum)
    x0, z = hyena_prep(pb, hp['conv_w'], hp['conv_b'], lp, padf, tm)
    a = fft_outer(z.reshape(nb, k1, n2 * w), ffwd)
    bm = fft_mid(a.reshape(nb, 2, n1, n2, w), gg, ggt, kf)
    d_tiled = jnp.tile(hp['d'].reshape(1, w), (1, FFT_TC // w))
    out = fft_final(bm.reshape(nb, 2 * n1, n2 * w), finv, x0.reshape(nb, k1, n2 * w),
                    z.reshape(nb, k1, n2 * w), d_tiled)
    return out.reshape(nb * lp, w)


TM = 640
ATTN_TQ = 640
ATTN_TKB = 640
SCAN_TB = 256
FFN_TK = 1408


def kernel(x_prompt, x_sample, meta, emb_ln_g, emb_ln_b, w_in, lam_q1, lam_k1, lam_q2, lam_k2, attn_norm_g, hy_conv_w, hy_conv_b, hy_w1, hy_b1, hy_f1, hy_w2, hy_b2, hy_f2, hy_w3, hy_decay, hy_d, hg_lb, hg_norm_g, gdn_conv_w, gdn_a_log, gdn_dt_bias, gdn_norm_g, w_out, ln1_g, ln1_b, w_up, ffn_conv_w, ffn_conv_b, w_down, ln2_g, ln2_b):
    n_prompt = x_prompt.shape[0]
    x = jnp.concatenate([x_prompt, x_sample], axis=0)
    nb, seq, d = x.shape
    padf = ROW_ALIGN - N_META
    lp = seq + ROW_ALIGN
    l_real = seq + N_META
    m = nb * lp
    w = GROUP_W

    h = embed(x, meta, emb_ln_g, emb_ln_b, padf).reshape(m, d)
    tables = rope_tables(lp, padf)
    consts = _scan_constants()
    sel = _gate_selectors()
    gg, ggt = inner_dft(_fft_n1(lp, l_real))
    sm = jax.nn.softmax(hg_lb, axis=0)
    lb_all = jnp.cumsum(sm, axis=0) - sm[0]

    for l in range(DEPTH):
        wl = w_in[l].astype(BF16)
        wd = jnp.pad(wl[:, 11 * w:], ((0, 0), (0, 4 * w + GATE_LANES - (wl.shape[1] - 11 * w))))
        pa = matmul(h, wl[:, 0:3 * w], TM, 3 * w)
        pb = matmul(h, wl[:, 3 * w:6 * w], TM, 3 * w)
        pc = matmul(h, wl[:, 6 * w:11 * w], TM, 5 * w // 2)
        pd = matmul(h, wd, TM, wd.shape[1])

        qlo, qhi, ka, va = attn_prep(pa, tables, lp, TM)
        lam_vecs = jnp.stack([lam_q1[l], lam_k1[l], lam_q2[l], lam_k2[l]])
        oa = diff_attention(qlo, qhi, ka, va, lam_vecs, attn_norm_g[l], nb, lp, padf, l, ATTN_TQ, ATTN_TKB)

        hp = dict(gg=gg, ggt=ggt, w1=hy_w1[l], b1=hy_b1[l], f1=hy_f1[l], w2=hy_w2[l], b2=hy_b2[l],
                  f2=hy_f2[l], w3=hy_w3[l], decay=hy_decay[l], d=hy_d[l], conv_w=hy_conv_w[l],
                  conv_b=hy_conv_b[l])
        ob = hyena_mixer(pb, hp, nb, lp, padf, l_real, TM)

        qc, vc, gc, sgc = hgrn_prep(pc, 0, lb_all[l], TM)
        oc = gated_norm(hgrn_scan(qc, vc, gc, consts, nb, lp, SCAN_TB), sgc, hg_norm_g[l], TM)

        qd, kd, vd, szd, gates = gdn_prep(pd, gdn_conv_w[l], gdn_a_log[l], gdn_dt_bias[l], lp, padf, TM)
        od = gated_norm(gdn_scan(qd, kd, vd, gates, consts, sel, nb, lp, SCAN_TB), szd, gdn_norm_g[l], TM)

        h = proj_residual_ln([oa, ob, oc, od], w_out[l].astype(BF16), h, ln1_g[l], ln1_b[l], lp, padf, TM)
        act = ffn_up(h, w_up[l].astype(BF16), ffn_conv_w[l], ffn_conv_b[l], TM, 512)
        f = act.shape[1]
        h = ffn_down_ln(act, w_down[l].astype(BF16), h, ln2_g[l], ln2_b[l], lp, padf, TM,
                        FFN_TK if f % FFN_TK == 0 else f)

    y = h.reshape(nb, lp, d)[:, ROW_ALIGN:]
    return (y[:n_prompt], y[n_prompt:])
```

```python
import functools
import math

import jax
import jax.numpy as jnp
import numpy as np
from jax import lax
from jax.experimental import pallas as pl
from jax.experimental.pallas import tpu as pltpu

D_MODEL = 2048
DEPTH = 4
N_META = 16
GROUP_W = 512
N_HEADS = 4
HEAD_W = 128
DA_HD = 64
ROT_DIM = 16
ROPE_THETA = 500000.0
HY_BANDS = 16
HY_SHIFT = 0.05
CHUNK = 64
D_FF = 5632
ALPHA = (2.0 * DEPTH) ** 0.25
LN_EPS = 1e-5
RMS_EPS = 1e-6
F32 = jnp.float32
BF16 = jnp.bfloat16
HIGHEST = lax.Precision.HIGHEST

ROW_ALIGN = 256
VMEM_LIMIT = 56 * 1024 * 1024


def _cparams(*sem):
    return pltpu.CompilerParams(dimension_semantics=sem, vmem_limit_bytes=VMEM_LIMIT)


def _valid_rows(tile_idx, tiles_per_seq, tm, padf):
    base = (tile_idx % tiles_per_seq) * tm
    return (base + lax.broadcasted_iota(jnp.int32, (tm, 1), 0)) >= padf


def _ln_rows(y, g, b):
    mu = jnp.mean(y, axis=-1, keepdims=True)
    d = y - mu
    var = jnp.mean(d * d, axis=-1, keepdims=True)
    return d * lax.rsqrt(var + LN_EPS) * g + b


def _embed_kernel(x_ref, meta_ref, g_ref, b_ref, o_ref, *, padf):
    t = pl.program_id(1)
    g = g_ref[...]
    b = b_ref[...]

    @pl.when(t == 0)
    def _():
        rows = o_ref.shape[1]
        o_ref[0, :padf, :] = jnp.zeros((padf, o_ref.shape[2]), F32)
        o_ref[0, padf:rows, :] = _ln_rows(meta_ref[...], g, b)

    @pl.when(t > 0)
    def _():
        o_ref[0] = _ln_rows(x_ref[0], g, b)


def embed(x, meta, g, b, padf):
    nb, s, d = x.shape
    tm = ROW_ALIGN
    assert padf + N_META == tm and s % tm == 0
    lp = s + tm
    return pl.pallas_call(
        functools.partial(_embed_kernel, padf=padf),
        out_shape=jax.ShapeDtypeStruct((nb, lp, d), F32),
        grid=(nb, lp // tm),
        in_specs=[pl.BlockSpec((1, tm, d), lambda bi, t: (bi, jnp.maximum(t - 1, 0), 0)),
                  pl.BlockSpec((N_META, d), lambda bi, t: (0, 0)),
                  pl.BlockSpec((1, d), lambda bi, t: (0, 0)),
                  pl.BlockSpec((1, d), lambda bi, t: (0, 0))],
        out_specs=pl.BlockSpec((1, tm, d), lambda bi, t: (bi, t, 0)),
        compiler_params=_cparams("arbitrary", "arbitrary"),
        name="embed_ln",
    )(x, meta, g.reshape(1, d), b.reshape(1, d))


def _mm_kernel(x_ref, w_ref, o_ref):
    o_ref[...] = jnp.dot(x_ref[...].astype(BF16), w_ref[...],
                         preferred_element_type=F32).astype(o_ref.dtype)


def matmul(x, w, tm, tn, out_dtype=F32):
    m, k = x.shape
    n = w.shape[1]
    assert m % tm == 0 and n % tn == 0
    return pl.pallas_call(
        _mm_kernel,
        out_shape=jax.ShapeDtypeStruct((m, n), out_dtype),
        grid=(n // tn, m // tm),
        in_specs=[pl.BlockSpec((tm, k), lambda j, i: (i, 0)),
                  pl.BlockSpec((k, tn), lambda j, i: (0, j))],
        out_specs=pl.BlockSpec((tm, tn), lambda j, i: (i, j)),
        compiler_params=_cparams("arbitrary", "arbitrary"),
        name="dense_matmul",
    )(x, w)


def _proj_ln_kernel(*refs, n_in, tiles_per_seq, padf):
    o_refs = refs[:n_in]
    w_ref, h_ref, g_ref, b_ref, out_ref = refs[n_in:]
    tm = h_ref.shape[0]
    acc = None
    off = 0
    for r in o_refs:
        kw = r.shape[1]
        part = jnp.dot(r[...].astype(BF16), w_ref[off:off + kw, :], preferred_element_type=F32)
        acc = part if acc is None else acc + part
        off += kw
    y = _ln_rows(ALPHA * h_ref[...] + acc, g_ref[...], b_ref[...])
    valid = _valid_rows(pl.program_id(0), tiles_per_seq, tm, padf)
    out_ref[...] = jnp.where(valid, y, 0.0)


def proj_residual_ln(parts, w, h, g, b, lp, padf, tm):
    m, d = h.shape
    assert m % tm == 0 and lp % tm == 0
    n_in = len(parts)
    in_specs = [pl.BlockSpec((tm, p.shape[1]), lambda i: (i, 0)) for p in parts]
    in_specs += [pl.BlockSpec(w.shape, lambda i: (0, 0)),
                 pl.BlockSpec((tm, d), lambda i: (i, 0)),
                 pl.BlockSpec((1, d), lambda i: (0, 0)),
                 pl.BlockSpec((1, d), lambda i: (0, 0))]
    return pl.pallas_call(
        functools.partial(_proj_ln_kernel, n_in=n_in, tiles_per_seq=lp // tm, padf=padf),
        out_shape=jax.ShapeDtypeStruct((m, d), F32),
        grid=(m // tm,),
        in_specs=in_specs,
        out_specs=pl.BlockSpec((tm, d), lambda i: (i, 0)),
        compiler_params=_cparams("arbitrary"),
        name="proj_residual_ln",
    )(*parts, w, h, g.reshape(1, d), b.reshape(1, d))


HALO = 8


def _halo_specs(tm, width, m, col_block=0):
    nblk = m // HALO
    per = tm // HALO

    def prev_map(*idx):
        i = idx[-1]
        return (jnp.maximum(i * per - 1, 0), col_block)

    def next_map(*idx):
        i = idx[-1]
        return (jnp.minimum((i + 1) * per, nblk - 1), col_block)

    return (pl.BlockSpec((HALO, width), prev_map), pl.BlockSpec((HALO, width), next_map))


def _conv3_rows(pm, prev_row, next_row, cw):
    tm = pm.shape[0]
    rid = lax.broadcasted_iota(jnp.int32, (tm, 1), 0)
    down = jnp.where(rid == 0, prev_row, pltpu.roll(pm, 1, 0))
    up = jnp.where(rid == tm - 1, next_row, pltpu.roll(pm, tm - 1, 0))
    return down * cw[0:1, :] + pm * cw[1:2, :] + up * cw[2:3, :]


def _ffn_up_kernel(x_ref, xp_ref, xn_ref, wg_ref, wu_ref, cwg_ref, cwu_ref, cbg_ref, cbu_ref, o_ref):
    i = pl.program_id(1)
    last = pl.num_programs(1) - 1
    x = x_ref[...].astype(BF16)
    halo = jnp.concatenate([xp_ref[...], xn_ref[...]], axis=0).astype(BF16)
    nmask = jnp.where(i == last, 0.0, 1.0)

    def branch(w_ref, cw_ref, cb_ref):
        w = w_ref[...]
        pm = jnp.dot(x, w, preferred_element_type=F32)
        ph = jnp.dot(halo, w, preferred_element_type=F32)
        return _conv3_rows(pm, ph[HALO - 1:HALO, :], ph[HALO:HALO + 1, :] * nmask, cw_ref[...]) + cb_ref[...]

    g = branch(wg_ref, cwg_ref, cbg_ref)
    u = branch(wu_ref, cwu_ref, cbu_ref)
    o_ref[...] = (g * jax.nn.sigmoid(g) * u).astype(o_ref.dtype)


def ffn_up(h, w_up, conv_w, conv_b, tm, tn):
    m, k = h.shape
    f = w_up.shape[1] // 2
    assert m % tm == 0 and f % tn == 0
    nj = f // tn
    xp_spec, xn_spec = _halo_specs(tm, k, m)
    cb = conv_b.reshape(1, 2 * f)
    return pl.pallas_call(
        _ffn_up_kernel,
        out_shape=jax.ShapeDtypeStruct((m, f), BF16),
        grid=(nj, m // tm),
        in_specs=[pl.BlockSpec((tm, k), lambda j, i: (i, 0)), xp_spec, xn_spec,
                  pl.BlockSpec((k, tn), lambda j, i: (0, j)),
                  pl.BlockSpec((k, tn), lambda j, i: (0, j + nj)),
                  pl.BlockSpec((3, tn), lambda j, i: (0, j)),
                  pl.BlockSpec((3, tn), lambda j, i: (0, j + nj)),
                  pl.BlockSpec((1, tn), lambda j, i: (0, j)),
                  pl.BlockSpec((1, tn), lambda j, i: (0, j + nj))],
        out_specs=pl.BlockSpec((tm, tn), lambda j, i: (i, j)),
        compiler_params=_cparams("arbitrary", "arbitrary"),
        name="ffn_up_conv_gate",
    )(h, h, h, w_up, w_up, conv_w, conv_w, cb, cb)


def _ffn_down_kernel(a_ref, w_ref, h_ref, g_ref, b_ref, out_ref, acc_ref, *, tiles_per_seq, padf):
    kk = pl.program_id(1)
    part = jnp.dot(a_ref[...], w_ref[...], preferred_element_type=F32)

    @pl.when(kk == 0)
    def _():
        acc_ref[...] = part

    @pl.when(kk > 0)
    def _():
        acc_ref[...] += part

    @pl.when(kk == pl.num_programs(1) - 1)
    def _():
        tm = h_ref.shape[0]
        y = _ln_rows(ALPHA * h_ref[...] + acc_ref[...], g_ref[...], b_ref[...])
        valid = _valid_rows(pl.program_id(0), tiles_per_seq, tm, padf)
        out_ref[...] = jnp.where(valid, y, 0.0)


def ffn_down_ln(a, w_down, h, g, b, lp, padf, tm, tk):
    m, d = h.shape
    f = a.shape[1]
    assert m % tm == 0 and f % tk == 0 and lp % tm == 0
    return pl.pallas_call(
        functools.partial(_ffn_down_kernel, tiles_per_seq=lp // tm, padf=padf),
        out_shape=jax.ShapeDtypeStruct((m, d), F32),
        grid=(m // tm, f // tk),
        in_specs=[pl.BlockSpec((tm, tk), lambda i, kk: (i, kk)),
                  pl.BlockSpec((tk, d), lambda i, kk: (kk, 0)),
                  pl.BlockSpec((tm, d), lambda i, kk: (i, 0)),
                  pl.BlockSpec((1, d), lambda i, kk: (0, 0)),
                  pl.BlockSpec((1, d), lambda i, kk: (0, 0))],
        out_specs=pl.BlockSpec((tm, d), lambda i, kk: (i, 0)),
        scratch_shapes=[pltpu.VMEM((tm, d), F32)],
        compiler_params=_cparams("arbitrary", "arbitrary"),
        name="ffn_down_residual_ln",
    )(a, w_down, h, g.reshape(1, d), b.reshape(1, d))


def rope_tables(lp, padf):
    half = ROT_DIM // 2
    pos = (jnp.arange(lp) - padf).astype(F32)
    inv = 1.0 / (ROPE_THETA ** (jnp.arange(half, dtype=F32) / half))
    ang = pos[:, None] * inv[None]
    cos, sin = jnp.cos(ang), jnp.sin(ang)
    ones = jnp.ones((lp, DA_HD - ROT_DIM), F32)
    zeros = jnp.zeros((lp, DA_HD - ROT_DIM), F32)
    zh = jnp.zeros((lp, half), F32)
    c = jnp.concatenate([cos, cos, ones], axis=1)
    sa = jnp.concatenate([-sin, zh, zeros], axis=1)
    sb = jnp.concatenate([zh, sin, zeros], axis=1)
    return tuple(jnp.concatenate([t, t], axis=1) for t in (c, sa, sb))


def _attn_prep_kernel(p_ref, c_ref, sa_ref, sb_ref, qlo_ref, qhi_ref, k_ref, vt_ref):
    w = GROUP_W
    reps = w // c_ref.shape[1]
    c = jnp.concatenate([c_ref[...]] * reps, axis=1)
    sa = jnp.concatenate([sa_ref[...]] * reps, axis=1)
    sb = jnp.concatenate([sb_ref[...]] * reps, axis=1)
    half = ROT_DIM // 2

    def rot(x):
        return x * c + pltpu.roll(x, w - half, 1) * sa + pltpu.roll(x, half, 1) * sb

    q = rot(p_ref[:, 0:w]) * (DA_HD ** -0.5)
    lane = lax.broadcasted_iota(jnp.int32, (1, w), 1) % (2 * DA_HD)
    qlo_ref[0] = jnp.where(lane < DA_HD, q, 0.0).T.astype(BF16)
    qhi_ref[0] = jnp.where(lane >= DA_HD, q, 0.0).T.astype(BF16)
    k_ref[...] = rot(p_ref[:, w:2 * w]).astype(BF16)
    vt_ref[0] = p_ref[:, 2 * w:3 * w].T.astype(BF16)


def attn_prep(proj, tables, nb, lp, tm):
    m = proj.shape[0]
    w = GROUP_W
    tps = lp // tm
    tspec = pl.BlockSpec((tm, 2 * DA_HD), lambda i: (i % tps, 0))
    tr_shape = jax.ShapeDtypeStruct((nb, w, lp), BF16)
    tr_spec = pl.BlockSpec((1, w, tm), lambda i: (i // tps, 0, i % tps))
    return pl.pallas_call(
        _attn_prep_kernel,
        out_shape=(tr_shape, tr_shape, jax.ShapeDtypeStruct((m, w), BF16), tr_shape),
        grid=(m // tm,),
        in_specs=[pl.BlockSpec((tm, 3 * w), lambda i: (i, 0)), tspec, tspec, tspec],
        out_specs=(tr_spec, tr_spec, pl.BlockSpec((tm, w), lambda i: (i, 0)), tr_spec),
        compiler_params=_cparams("arbitrary"),
        name="attn_prep_rotary",
    )(proj, *tables)


ATTN_QC = 256
ATTN_ONES = 8


def _attn_kernel(lam_ref, g_ref, qlo_ref, qhi_ref, k_ref, vt_ref, o_ref, qs_ref, m_ref, acc_ref,
                 *, tkb, padf, lam_init, unroll):
    tq = qlo_ref.shape[2]
    nq2 = 2 * tq
    nkb = k_ref.shape[0] // tkb
    qs_ref[:, :tq] = qlo_ref[0]
    qs_ref[:, tq:] = qhi_ref[0]
    row_ok = lax.broadcasted_iota(jnp.int32, (tkb, 1), 0) >= padf
    ones = jnp.ones((ATTN_ONES, tkb), BF16)

    def scores(kb, c0, masked):
        start = kb * tkb
        if not isinstance(start, int):
            start = pl.multiple_of(start, tkb)
        s = jnp.dot(k_ref[pl.ds(start, tkb), :], qs_ref[:, c0:c0 + ATTN_QC], preferred_element_type=F32)
        if masked:
            s = jnp.where(row_ok, s, -1e30)
        return s, start

    def fold(s):
        return jnp.max(s.reshape(tkb // 8, 8, ATTN_QC), axis=0)

    for c0 in range(0, nq2, 2 * ATTN_QC):
        cols = (c0, c0 + ATTN_QC)

        def p1_body(kb, mx):
            return tuple(jnp.maximum(mx[i], fold(scores(kb, cols[i], False)[0])) for i in range(2))

        mx = lax.fori_loop(1, nkb, p1_body, tuple(fold(scores(0, c, True)[0]) for c in cols), unroll=unroll)
        for i in range(2):
            m_ref[:, cols[i]:cols[i] + ATTN_QC] = jnp.max(mx[i], axis=0, keepdims=True)

    for c0 in range(0, nq2, 2 * ATTN_QC):
        cols = (c0, c0 + ATTN_QC)
        mrow = tuple(m_ref[:, c:c + ATTN_QC] for c in cols)

        def consume(kb, s):
            start = kb * tkb
            if not isinstance(start, int):
                start = pl.multiple_of(start, tkb)
            vext = jnp.concatenate([vt_ref[0, :, pl.ds(start, tkb)], ones], axis=0)
            for i in range(2):
                p = jnp.exp(s[i] - mrow[i]).astype(BF16)
                acc_ref[:, cols[i]:cols[i] + ATTN_QC] += jnp.dot(vext, p, preferred_element_type=F32)

        for c in cols:
            acc_ref[:, c:c + ATTN_QC] = jnp.zeros((HEAD_W + ATTN_ONES, ATTN_QC), F32)

        def p2_body(kb, s):
            s_next = tuple(scores(kb + 1, c, False)[0] for c in cols)
            consume(kb, s)
            return s_next

        s_last = lax.fori_loop(0, nkb - 1, p2_body, tuple(scores(0, c, True)[0] for c in cols), unroll=unroll)
        consume(nkb - 1, s_last)

    lam = (jnp.exp(jnp.sum(lam_ref[0:1, :] * lam_ref[1:2, :], axis=-1, keepdims=True))
           - jnp.exp(jnp.sum(lam_ref[2:3, :] * lam_ref[3:4, :], axis=-1, keepdims=True)) + lam_init)
    o = acc_ref[0:HEAD_W, :] / acc_ref[HEAD_W:HEAD_W + 1, :]
    o = (o[:, :tq] - lam * o[:, tq:]).T
    o = o * lax.rsqrt(jnp.mean(o * o, axis=-1, keepdims=True) + RMS_EPS) * g_ref[...]
    o_ref[...] = (o * (1.0 - lam_init)).astype(o_ref.dtype)


def diff_attention(qlo_t, qhi_t, k, v_t, lam_vecs, norm_g, nb, lp, padf, layer, tq, tkb, unroll):
    m = k.shape[0]
    assert lp % tq == 0 and lp % tkb == 0 and padf < tkb and tq % ATTN_QC == 0
    nq = lp // tq
    lam_init = 0.8 - 0.6 * math.exp(-0.3 * layer)
    qspec = pl.BlockSpec((1, HEAD_W, tq), lambda b, h, qi: (b, h, qi))
    return pl.pallas_call(
        functools.partial(_attn_kernel, tkb=tkb, padf=padf, lam_init=lam_init, unroll=unroll),
        out_shape=jax.ShapeDtypeStruct((m, GROUP_W), BF16),
        grid=(nb, N_HEADS, nq),
        in_specs=[pl.BlockSpec((4, DA_HD), lambda b, h, qi: (0, 0)),
                  pl.BlockSpec((1, HEAD_W), lambda b, h, qi: (0, 0)),
                  qspec, qspec,
                  pl.BlockSpec((lp, HEAD_W), lambda b, h, qi: (b, h)),
                  pl.BlockSpec((1, HEAD_W, lp), lambda b, h, qi: (b, h, 0))],
        out_specs=pl.BlockSpec((tq, HEAD_W), lambda b, h, qi: (b * nq + qi, h)),
        scratch_shapes=[pltpu.VMEM((HEAD_W, 2 * tq), BF16), pltpu.VMEM((1, 2 * tq), F32),
                        pltpu.VMEM((HEAD_W + ATTN_ONES, 2 * tq), F32)],
        compiler_params=_cparams("arbitrary", "arbitrary", "arbitrary"),
        name="diff_attention",
    )(lam_vecs, norm_g.reshape(1, HEAD_W), qlo_t, qhi_t, k, v_t)


N_LEVELS = 6


def _scan_constants():
    c = CHUNK
    idx = np.arange(c)
    cs = np.zeros((2, (N_LEVELS + 2) * c, c), np.float32)
    mask = np.zeros((2, N_LEVELS + 2, c, c), np.float32)
    for d in range(2):
        cum = (idx[None, :] <= idx[:, None]) if d == 0 else (idx[None, :] >= idx[:, None])
        cs[d, 0:c] = cum
        for l in range(N_LEVELS):
            s = (c // 2) >> l
            blk = idx // s
            ref = (blk | 1) * s - 1 if d == 0 else (blk | 1) * s
            cs[d, (l + 1) * c:(l + 2) * c] = cum[ref]
            if d == 0:
                mask[d, l] = ((blk[:, None] & 1) == 1) & (blk[None, :] == blk[:, None] - 1)
            else:
                mask[d, l] = ((blk[:, None] & 1) == 0) & (blk[None, :] == blk[:, None] + 1)
        cs[d, (N_LEVELS + 1) * c:] = 1.0
        mask[d, N_LEVELS] = cum
        mask[d, N_LEVELS + 1] = cum & (idx[None, :] != idx[:, None])
    return jnp.asarray(np.concatenate([cs] * 3, axis=2), BF16), jnp.asarray(mask)


def _split3(x):
    hi = x.astype(BF16)
    r = x - hi.astype(F32)
    mid = r.astype(BF16)
    return hi, mid, (r - mid.astype(F32)).astype(BF16)


def _split3_rows(x):
    return jnp.concatenate(_split3(x), axis=0)


def _split3_lanes(x):
    return jnp.concatenate(_split3(x), axis=1)


def _nt(a, b):
    return lax.dot_general(a.astype(BF16), b.astype(BF16), (((1,), (1,)), ((), ())),
                           preferred_element_type=F32)


def _tn(a, b):
    return lax.dot_general(a.astype(BF16), b.astype(BF16), (((0,), (0,)), ((), ())),
                           preferred_element_type=F32)


def _nn(a, b):
    return jnp.dot(a.astype(BF16), b.astype(BF16), preferred_element_type=F32)


def _hgrn_prep_kernel(p_ref, lb_ref, q_ref, v_ref, g_ref, sg_ref):
    w = GROUP_W
    q = p_ref[:, 0:w]
    q_ref[...] = (q * jax.nn.sigmoid(q) * (HEAD_W ** -0.5)).astype(BF16)
    for d in range(2):
        lb = lb_ref[d:d + 1, :]
        f = p_ref[:, (1 + d) * w:(2 + d) * w]
        g_ref[d] = jnp.log(lb + (1.0 - lb) * jax.nn.sigmoid(f))
    v_ref[...] = p_ref[:, 3 * w:4 * w].astype(BF16)
    gate = p_ref[:, 4 * w:5 * w]
    sg_ref[...] = (gate * jax.nn.sigmoid(gate)).astype(BF16)


def hgrn_prep(proj, col_block, lb, tm):
    m = proj.shape[0]
    w = GROUP_W
    ospec = pl.BlockSpec((tm, w), lambda i: (i, 0))
    return pl.pallas_call(
        _hgrn_prep_kernel,
        out_shape=(jax.ShapeDtypeStruct((m, w), BF16), jax.ShapeDtypeStruct((m, w), BF16),
                   jax.ShapeDtypeStruct((2, m, w), F32), jax.ShapeDtypeStruct((m, w), BF16)),
        grid=(m // tm,),
        in_specs=[pl.BlockSpec((tm, 5 * w), lambda i: (i, col_block)),
                  pl.BlockSpec((2, w), lambda i: (0, 0))],
        out_specs=(ospec, ospec, pl.BlockSpec((2, tm, w), lambda i: (0, i, 0)), ospec),
        compiler_params=_cparams("arbitrary"),
        name="hgrn_prep",
    )(proj, lb)


def _head(x, h):
    return x[:, h * HEAD_W:(h + 1) * HEAD_W]


def _chunk_rows(ci, nch, d):
    return pl.multiple_of((ci + d * (nch - 1 - 2 * ci)) * CHUNK, CHUNK)


def _hgrn_scan_kernel(cs_ref, mask_ref, q_ref, v_ref, g_ref, o_ref, s_ref, *, nch):
    c = CHUNK
    d = pl.program_id(1)
    heads = range(N_HEADS)
    chunks = range(nch)

    @pl.when(pl.program_id(2) == 0)
    def _():
        s_ref[...] = jnp.zeros(s_ref.shape, F32)

    eye = mask_ref[0, N_LEVELS] - mask_ref[0, N_LEVELS + 1]
    rows = [_chunk_rows(ci, nch, d) for ci in chunks]
    g = [g_ref[0, pl.ds(r0, c), :] for r0 in rows]
    q = [q_ref[pl.ds(r0, c), :].astype(F32) for r0 in rows]
    v = [v_ref[pl.ds(r0, c), :] for r0 in rows]
    big = [jnp.dot(cs_ref[0], _split3_rows(gi), preferred_element_type=F32) for gi in g]
    b = [x[0:c] for x in big]
    tot = [x[(N_LEVELS + 1) * c:(N_LEVELS + 2) * c] for x in big]
    k = [1.0 - jnp.exp(gi) for gi in g]
    a = [[eye * jnp.sum(_head(q[ci] * k[ci], h), axis=-1, keepdims=True) for h in heads] for ci in chunks]
    for l in range(N_LEVELS):
        e = [jnp.exp(-jnp.abs(b[ci] - big[ci][(l + 1) * c:(l + 2) * c])) for ci in chunks]
        qe = [(q[ci] * e[ci]).astype(BF16) for ci in chunks]
        ke = [(k[ci] * e[ci]).astype(BF16) for ci in chunks]
        lm = mask_ref[0, l]
        a = [[a[ci][h] + lm * _nt(_head(qe[ci], h), _head(ke[ci], h)) for h in heads] for ci in chunks]
    o_intra = [[_nn(a[ci][h], _head(v[ci], h)) for h in heads] for ci in chunks]
    qb = [(q[ci] * jnp.exp(b[ci])).astype(BF16) for ci in chunks]
    kd = [(k[ci] * jnp.exp(tot[ci] - b[ci])).astype(BF16) for ci in chunks]
    dec = [jnp.exp(tot[ci][0:1, :]) for ci in chunks]
    s = [s_ref[h] for h in heads]
    for ci in chunks:
        for h in heads:
            o = o_intra[ci][h] + _nt(_head(qb[ci], h), s[h])
            o_ref[0, pl.ds(rows[ci], c), h * HEAD_W:(h + 1) * HEAD_W] = o
        s = [s[h] * _head(dec[ci], h) + _tn(_head(v[ci], h), _head(kd[ci], h)) for h in heads]
    for h in heads:
        s_ref[h] = s[h]


def _scan_row_block(nblk):
    def row(b, d, t):
        return b * nblk + t + d * (nblk - 1 - 2 * t)
    return row


def hgrn_scan(q, v, g, consts, nb, lp, tb):
    m, w = q.shape
    cs, mask = consts
    row = _scan_row_block(lp // tb)
    return pl.pallas_call(
        functools.partial(_hgrn_scan_kernel, nch=tb // CHUNK),
        out_shape=jax.ShapeDtypeStruct((2, m, w), F32),
        grid=(nb, 2, lp // tb),
        in_specs=[pl.BlockSpec((1,) + cs.shape[1:], lambda b, d, t: (d, 0, 0)),
                  pl.BlockSpec((1,) + mask.shape[1:], lambda b, d, t: (d, 0, 0, 0)),
                  pl.BlockSpec((tb, w), lambda b, d, t: (row(b, d, t), 0)),
                  pl.BlockSpec((tb, w), lambda b, d, t: (row(b, d, t), 0)),
                  pl.BlockSpec((1, tb, w), lambda b, d, t: (d, row(b, d, t), 0))],
        out_specs=pl.BlockSpec((1, tb, w), lambda b, d, t: (d, row(b, d, t), 0)),
        scratch_shapes=[pltpu.VMEM((N_HEADS, HEAD_W, HEAD_W), F32)],
        compiler_params=_cparams("arbitrary", "arbitrary", "arbitrary"),
        name="hgrn_scan",
    )(cs, mask, q, v, g)


def _gated_norm_kernel(o_ref, sg_ref, g_ref, out_ref):
    o = o_ref[0] + o_ref[1]
    gain = g_ref[...]
    for h in range(N_HEADS):
        sl = slice(h * HEAD_W, (h + 1) * HEAD_W)
        oh = o[:, sl]
        y = oh * lax.rsqrt(jnp.mean(oh * oh, axis=-1, keepdims=True) + RMS_EPS) * gain
        out_ref[:, sl] = (y * sg_ref[:, sl].astype(F32)).astype(out_ref.dtype)


def gated_norm(o2, sgate, norm_g, tm):
    m = sgate.shape[0]
    w = GROUP_W
    return pl.pallas_call(
        _gated_norm_kernel,
        out_shape=jax.ShapeDtypeStruct((m, w), BF16),
        grid=(m // tm,),
        in_specs=[pl.BlockSpec((2, tm, w), lambda i: (0, i, 0)),
                  pl.BlockSpec((tm, w), lambda i: (i, 0)),
                  pl.BlockSpec((1, HEAD_W), lambda i: (0, 0))],
        out_specs=pl.BlockSpec((tm, w), lambda i: (i, 0)),
        compiler_params=_cparams("arbitrary"),
        name="gated_rms_norm",
    )(o2, sgate, norm_g.reshape(1, HEAD_W))


GATE_LANES = 128


def _softplus(x):
    return jnp.maximum(x, 0.0) + jnp.log(1.0 + jnp.exp(-jnp.abs(x)))


def _gdn_prep_kernel(p_ref, pp_ref, pn_ref, cw_ref, acoef_ref, dtb_ref,
                     q_ref, k_ref, v_ref, sz_ref, gates_ref, *, tiles_per_seq, padf):
    w = GROUP_W
    i = pl.program_id(0)
    tm = p_ref.shape[0]
    nmask = jnp.where(i == pl.num_programs(0) - 1, 0.0, 1.0)
    x = _conv3_rows(p_ref[:, 0:3 * w], pp_ref[HALO - 1:HALO, :], pn_ref[0:1, :] * nmask, cw_ref[...])
    x = x * jax.nn.sigmoid(x)
    valid = _valid_rows(i, tiles_per_seq, tm, padf)
    for h in range(N_HEADS):
        sl = slice(h * HEAD_W, (h + 1) * HEAD_W)
        qh = x[:, sl]
        q_ref[:, sl] = (qh * lax.rsqrt(jnp.sum(qh * qh, axis=-1, keepdims=True) + 1e-6)
                        * (HEAD_W ** -0.5)).astype(BF16)
        kh = x[:, w + h * HEAD_W:w + (h + 1) * HEAD_W]
        k_ref[:, sl] = (kh * lax.rsqrt(jnp.sum(kh * kh, axis=-1, keepdims=True) + 1e-6)).astype(BF16)
    v_ref[...] = jnp.where(valid, x[:, 2 * w:3 * w], 0.0).astype(BF16)
    z = p_ref[:, 3 * w:4 * w]
    sz_ref[...] = (z * jax.nn.sigmoid(z)).astype(BF16)
    gt = p_ref[:, 4 * w:4 * w + GATE_LANES]
    lane = lax.broadcasted_iota(jnp.int32, (1, GATE_LANES), 1)
    decay = -acoef_ref[...] * _softplus(gt + dtb_ref[...])
    gates_ref[...] = jnp.where(lane < 2 * N_HEADS, jax.nn.sigmoid(gt), decay)


def gdn_prep(pd, conv_w, a_log, dt_bias, lp, padf, tm):
    m = pd.shape[0]
    w = GROUP_W
    zeros8 = jnp.zeros((2 * N_HEADS,), F32)
    padl = jnp.zeros((GATE_LANES - 4 * N_HEADS,), F32)
    acoef = jnp.concatenate([zeros8, jnp.exp(a_log).reshape(-1), padl]).reshape(1, GATE_LANES)
    dtb = jnp.concatenate([zeros8, dt_bias.reshape(-1), padl]).reshape(1, GATE_LANES)
    pspec, nspec = _halo_specs(tm, 3 * w, m)
    ospec = pl.BlockSpec((tm, w), lambda i: (i, 0))
    return pl.pallas_call(
        functools.partial(_gdn_prep_kernel, tiles_per_seq=lp // tm, padf=padf),
        out_shape=tuple(jax.ShapeDtypeStruct((m, w), BF16) for _ in range(4))
        + (jax.ShapeDtypeStruct((m, GATE_LANES), F32),),
        grid=(m // tm,),
        in_specs=[pl.BlockSpec((tm, pd.shape[1]), lambda i: (i, 0)), pspec, nspec,
                  pl.BlockSpec((3, 3 * w), lambda i: (0, 0)),
                  pl.BlockSpec((1, GATE_LANES), lambda i: (0, 0)),
                  pl.BlockSpec((1, GATE_LANES), lambda i: (0, 0))],
        out_specs=(ospec, ospec, ospec, ospec, pl.BlockSpec((tm, GATE_LANES), lambda i: (i, 0))),
        compiler_params=_cparams("arbitrary"),
        name="gdn_prep",
    )(pd, pd, pd, conv_w, acoef, dtb)


def _gate_selectors():
    sel = np.zeros((2, GATE_LANES, 2 * GROUP_W), np.float32)
    for d in range(2):
        for h in range(N_HEADS):
            sel[d, N_HEADS * d + h, h * HEAD_W:(h + 1) * HEAD_W] = 1.0
            sel[d, 2 * N_HEADS + N_HEADS * d + h, GROUP_W + h * HEAD_W:GROUP_W + (h + 1) * HEAD_W] = 1.0
    return jnp.asarray(np.concatenate([sel] * 3, axis=1), BF16)


def _gdn_scan_kernel(cs_ref, mask_ref, sel_ref, q_ref, k_ref, v_ref, gt_ref, o_ref, s_ref, *, nch):
    c = CHUNK
    w = GROUP_W
    d = pl.program_id(1)
    heads = range(N_HEADS)
    chunks = range(nch)
    items = [(ci, h) for ci in chunks for h in heads]

    @pl.when(pl.program_id(2) == 0)
    def _():
        s_ref[...] = jnp.zeros(s_ref.shape, F32)

    incl = mask_ref[0, N_LEVELS]
    strict = mask_ref[0, N_LEVELS + 1]
    eye = incl - strict
    cum_tot = jnp.concatenate([cs_ref[0, 0:c, :], cs_ref[0, (N_LEVELS + 1) * c:(N_LEVELS + 2) * c, :]], axis=0)
    ones3 = cs_ref[0, (N_LEVELS + 1) * c:(N_LEVELS + 2) * c, :]
    rows = [_chunk_rows(ci, nch, d) for ci in chunks]
    q = [q_ref[pl.ds(r0, c), :].astype(F32) for r0 in rows]
    k = [k_ref[pl.ds(r0, c), :].astype(F32) for r0 in rows]
    v = [v_ref[pl.ds(r0, c), :].astype(F32) for r0 in rows]
    bg = [jnp.dot(_split3_lanes(gt_ref[pl.ds(r0, c), :]), sel_ref[0], preferred_element_type=F32)
          for r0 in rows]
    beta = [x[:, 0:w] for x in bg]
    cb = [jnp.dot(cum_tot, _split3_rows(x[:, w:2 * w]), preferred_element_type=F32) for x in bg]
    bb = [x[0:c] for x in cb]
    tot = [x[c:2 * c] for x in cb]
    bc = {(ci, h): bb[ci][:, h * HEAD_W:h * HEAD_W + c] for ci, h in items}
    br = {it: jnp.dot(ones3, _split3_rows(eye * bc[it]), preferred_element_type=F32) for it in items}
    dec = {it: incl * jnp.exp(jnp.minimum(bc[it] - br[it], 0.0)) for it in items}
    kb = [k[ci] * beta[ci] for ci in chunks]
    n = {(ci, h): strict * _nt(_head(kb[ci], h), _head(k[ci], h)) * dec[(ci, h)] for ci, h in items}
    t = {it: eye - n[it] for it in items}
    pw = {it: _nn(n[it], n[it]) for it in items}
    for _ in range(N_LEVELS - 2):
        t = {it: t[it] + _nn(t[it], pw[it]) for it in items}
        pw = {it: _nn(pw[it], pw[it]) for it in items}
    t = {it: t[it] + _nn(t[it], pw[it]) for it in items}
    eb = [jnp.exp(x) for x in bb]
    rhs_u = [v[ci] * beta[ci] for ci in chunks]
    rhs_w = [kb[ci] * eb[ci] for ci in chunks]
    uw = {(ci, h): _nn(t[(ci, h)], jnp.concatenate([_head(rhs_u[ci], h), _head(rhs_w[ci], h)], axis=1))
          for ci, h in items}
    qk = {(ci, h): _nt(_head(q[ci], h), _head(k[ci], h)) * dec[(ci, h)] for ci, h in items}
    qe = [(q[ci] * eb[ci]).astype(BF16) for ci in chunks]
    kd = [(k[ci] * jnp.exp(tot[ci] - bb[ci])).astype(BF16) for ci in chunks]
    sdec = [jnp.exp(tot[ci][0:1, :]) for ci in chunks]
    s = [s_ref[h] for h in heads]
    for ci in chunks:
        wq = {h: _nn(jnp.concatenate([uw[(ci, h)][:, HEAD_W:].astype(BF16), _head(qe[ci], h)], axis=0), s[h])
              for h in heads}
        vnew = {h: uw[(ci, h)][:, 0:HEAD_W] - wq[h][0:c] for h in heads}
        for h in heads:
            o_ref[0, pl.ds(rows[ci], c), h * HEAD_W:(h + 1) * HEAD_W] = wq[h][c:2 * c] + _nn(qk[(ci, h)], vnew[h])
        s = [s[h] * _head(sdec[ci], h) + _tn(_head(kd[ci], h), vnew[h]) for h in heads]
    for h in heads:
        s_ref[h] = s[h]


def gdn_scan(q, k, v, gates, consts, sel, nb, lp, tb):
    m, w = q.shape
    cs, mask = consts
    row = _scan_row_block(lp // tb)
    rspec = pl.BlockSpec((tb, w), lambda b, d, t: (row(b, d, t), 0))
    return pl.pallas_call(
        functools.partial(_gdn_scan_kernel, nch=tb // CHUNK),
        out_shape=jax.ShapeDtypeStruct((2, m, w), F32),
        grid=(nb, 2, lp // tb),
        in_specs=[pl.BlockSpec((1,) + cs.shape[1:], lambda b, d, t: (d, 0, 0)),
                  pl.BlockSpec((1,) + mask.shape[1:], lambda b, d, t: (d, 0, 0, 0)),
                  pl.BlockSpec((1,) + sel.shape[1:], lambda b, d, t: (d, 0, 0)),
                  rspec, rspec, rspec,
                  pl.BlockSpec((tb, GATE_LANES), lambda b, d, t: (row(b, d, t), 0))],
        out_specs=pl.BlockSpec((1, tb, w), lambda b, d, t: (d, row(b, d, t), 0)),
        scratch_shapes=[pltpu.VMEM((N_HEADS, HEAD_W, HEAD_W), F32)],
        compiler_params=_cparams("arbitrary", "arbitrary", "arbitrary"),
        name="gdn_scan",
    )(cs, mask, sel, q, k, v, gates)


FFT_N2 = 256
FFT_TC = 8192


def _fft_n1(lp, l_real):
    need = -(-(2 * l_real - 1) // FFT_N2)
    return max(-(-need // 8) * 8, -(-(lp // FFT_N2) // 8) * 8)


def _outer_dft(n1, k_rows, inverse, n_total):
    kp = -(-k_rows // 8) * 8
    a = np.arange(n1)[:, None] * np.arange(kp)[None, :]
    ang = 2.0 * np.pi * (a % n1) / n1
    c, s = np.cos(ang), np.sin(ang)
    live = (np.arange(kp) < k_rows)[None, :]
    c, s = c * live, s * live
    if not inverse:
        return jnp.asarray(np.concatenate([c, -s], axis=0), F32)
    return jnp.asarray(np.concatenate([c.T, -s.T], axis=1) / n_total, F32)


def inner_dft(n1):
    n2 = FFT_N2
    n = n1 * n2
    k = jnp.arange(n1, dtype=jnp.int32)[:, None, None] + n1 * jnp.arange(n2, dtype=jnp.int32)[None, :, None]
    r = (k * jnp.arange(n2, dtype=jnp.int32)[None, None, :]) % n
    ang = r.astype(F32) * (2.0 * math.pi / n)
    gr, gi = jnp.cos(ang), -jnp.sin(ang)
    gg = jnp.concatenate([jnp.concatenate([gr, -gi], axis=2), jnp.concatenate([gi, gr], axis=2)], axis=1)
    return gg.astype(BF16), jnp.swapaxes(gg, 1, 2).astype(BF16)


def _hyena_prep_kernel(p_ref, pp_ref, pn_ref, cw_ref, cb_ref, x0_ref, z_ref, *, tiles_per_seq, padf):
    w = GROUP_W
    i = pl.program_id(0)
    tm = p_ref.shape[0]
    nmask = jnp.where(i == pl.num_programs(0) - 1, 0.0, 1.0)
    u = _conv3_rows(p_ref[...], pp_ref[HALO - 1:HALO, :], pn_ref[0:1, :] * nmask, cw_ref[...]) + cb_ref[...]
    valid = _valid_rows(i, tiles_per_seq, tm, padf)
    x0_ref[...] = u[:, 0:w]
    z_ref[...] = jnp.where(valid, u[:, w:2 * w] * u[:, 2 * w:3 * w], 0.0)


def hyena_prep(pb, conv_w, conv_b, lp, padf, tm):
    m, w3 = pb.shape
    w = GROUP_W
    pspec, nspec = _halo_specs(tm, w3, m)
    ospec = pl.BlockSpec((tm, w), lambda i: (i, 0))
    return pl.pallas_call(
        functools.partial(_hyena_prep_kernel, tiles_per_seq=lp // tm, padf=padf),
        out_shape=(jax.ShapeDtypeStruct((m, w), F32), jax.ShapeDtypeStruct((m, w), F32)),
        grid=(m // tm,),
        in_specs=[pl.BlockSpec((tm, w3), lambda i: (i, 0)), pspec, nspec,
                  pl.BlockSpec((3, w3), lambda i: (0, 0)), pl.BlockSpec((1, w3), lambda i: (0, 0))],
        out_specs=(ospec, ospec),
        compiler_params=_cparams("arbitrary"),
        name="hyena_prep",
    )(pb, pb, pb, conv_w, conv_b.reshape(1, w3))


def _hyena_filter_kernel(w1t_ref, w1c_ref, w1s_ref, b1_ref, f1_ref, w2_ref, b2_ref, f2_ref, w3_ref, dec_ref,
                         filt_ref, asum_ref, *, l_real):
    i = pl.program_id(0)
    tm = filt_ref.shape[0]
    w = GROUP_W
    row = i * tm + lax.broadcasted_iota(jnp.int32, (tm, 1), 0)
    rf = row.astype(F32)
    t = rf * (1.0 / (l_real - 1))
    band = lax.broadcasted_iota(jnp.int32, (1, HY_BANDS), 1).astype(F32)
    bands = 1e-4 + band * ((HY_BANDS - 1 - 1e-4) / (HY_BANDS - 1))
    ang = ((2.0 * math.pi / l_real) * rf) * bands

    def hdot(a, b):
        return jnp.dot(a, b, precision=HIGHEST, preferred_element_type=F32)

    pre = t * w1t_ref[...] + hdot(jnp.cos(ang), w1c_ref[...]) - hdot(jnp.sin(ang), w1s_ref[...])
    hid = jnp.sin(f1_ref[...] * (pre + b1_ref[...]))
    hid = jnp.sin(f2_ref[...] * (hdot(hid, w2_ref[...]) + b2_ref[...]))
    filt = hdot(hid, w3_ref[...])
    window = jnp.exp(-t * jnp.abs(dec_ref[...])) + HY_SHIFT
    filt = jnp.where(row < l_real, filt * window, 0.0)

    @pl.when(i == 0)
    def _():
        asum_ref[...] = jnp.zeros(asum_ref.shape, F32)

    asum_ref[...] += jnp.sum(jnp.abs(filt), axis=0, keepdims=True)
    lane = lax.broadcasted_iota(jnp.int32, (1, 2 * w), 1)
    filt_ref[...] = jnp.where((row == 0) & (lane >= w), 0.0, filt)


def hyena_filter(w1, b1, f1, w2, b2, f2, w3, decay, lp, l_real, tm):
    w = GROUP_W
    nf = w1.shape[1]
    small = lambda a: pl.BlockSpec(a.shape, lambda i: (0,) * a.ndim)
    args = (w1[0:1], w1[1:1 + HY_BANDS], w1[1 + HY_BANDS:], b1.reshape(1, nf), f1.reshape(1, nf),
            w2, b2.reshape(1, nf), f2.reshape(1, nf), w3, decay.reshape(1, 2 * w))
    return pl.pallas_call(
        functools.partial(_hyena_filter_kernel, l_real=l_real),
        out_shape=(jax.ShapeDtypeStruct((lp, 2 * w), F32), jax.ShapeDtypeStruct((1, 2 * w), F32)),
        grid=(lp // tm,),
        in_specs=[small(a) for a in args],
        out_specs=(pl.BlockSpec((tm, 2 * w), lambda i: (i, 0)), pl.BlockSpec((1, 2 * w), lambda i: (0, 0))),
        compiler_params=_cparams("arbitrary"),
        name="hyena_filter",
    )(*args)


def _fft_outer_kernel(f_ref, x_ref, o_ref, pad_ref):
    k = x_ref.shape[1]
    pad_ref[...] = jnp.zeros(pad_ref.shape, F32)
    pad_ref[0:k, :] = x_ref[0]
    o_ref[0] = jnp.dot(f_ref[...], pad_ref[...], precision=HIGHEST, preferred_element_type=F32)


def fft_outer(x, fmat):
    b, k, cols = x.shape
    rows, kp = fmat.shape
    tc = FFT_TC
    return pl.pallas_call(
        _fft_outer_kernel,
        out_shape=jax.ShapeDtypeStruct((b, rows, cols), F32),
        grid=(b, cols // tc),
        in_specs=[pl.BlockSpec((rows, kp), lambda bi, j: (0, 0)),
                  pl.BlockSpec((1, k, tc), lambda bi, j: (bi, 0, j))],
        out_specs=pl.BlockSpec((1, rows, tc), lambda bi, j: (bi, 0, j)),
        scratch_shapes=[pltpu.VMEM((kp, tc), F32)],
        compiler_params=_cparams("arbitrary", "arbitrary"),
        name="fft_outer",
    )(fmat, x)


def _filter_spectrum_kernel(gg_ref, a_ref, asum_ref, o_ref):
    w = GROUP_W
    n2 = FFT_N2
    a = jnp.concatenate([a_ref[0, 0], a_ref[1, 0]], axis=0).astype(BF16)
    x = jnp.dot(gg_ref[0], a, preferred_element_type=F32)
    s = 1.0 / (asum_ref[:, 0:w] + asum_ref[:, w:2 * w] + 1e-6)
    o_ref[0, 0] = (x[0:n2, 0:w] + x[0:n2, w:2 * w]) * s
    o_ref[0, 1] = (x[n2:2 * n2, 0:w] - x[n2:2 * n2, w:2 * w]) * s


def filter_spectrum(a, gg, asum):
    _, n1, n2, w2 = a.shape
    w = w2 // 2
    return pl.pallas_call(
        _filter_spectrum_kernel,
        out_shape=jax.ShapeDtypeStruct((n1, 2, n2, w), F32),
        grid=(n1,),
        in_specs=[pl.BlockSpec((1, 2 * n2, 2 * n2), lambda k: (k, 0, 0)),
                  pl.BlockSpec((2, 1, n2, w2), lambda k: (0, k, 0, 0)),
                  pl.BlockSpec((1, w2), lambda k: (0, 0))],
        out_specs=pl.BlockSpec((1, 2, n2, w), lambda k: (k, 0, 0, 0)),
        compiler_params=_cparams("arbitrary"),
        name="hyena_filter_spectrum",
    )(gg, a, asum)


def _fft_mid_kernel(gg_ref, ggt_ref, kf_ref, a_ref, o_ref):
    n2 = FFT_N2
    a = jnp.concatenate([a_ref[0, 0, 0], a_ref[0, 1, 0]], axis=0).astype(BF16)
    x = jnp.dot(gg_ref[0], a, preferred_element_type=F32)
    xr, xi = x[0:n2], x[n2:2 * n2]
    kr, ki = kf_ref[0, 0], kf_ref[0, 1]
    y = jnp.concatenate([xr * kr - xi * ki, xr * ki + xi * kr], axis=0).astype(BF16)
    bm = jnp.dot(ggt_ref[0], y, preferred_element_type=F32)
    o_ref[0, 0, 0] = bm[0:n2]
    o_ref[0, 1, 0] = bm[n2:2 * n2]


def fft_mid(a, gg, ggt, kf):
    b, _, n1, n2, w = a.shape
    aspec = pl.BlockSpec((1, 2, 1, n2, w), lambda k, bi: (bi, 0, k, 0, 0))
    gspec = pl.BlockSpec((1, 2 * n2, 2 * n2), lambda k, bi: (k, 0, 0))
    return pl.pallas_call(
        _fft_mid_kernel,
        out_shape=jax.ShapeDtypeStruct(a.shape, F32),
        grid=(n1, b),
        in_specs=[gspec, gspec, pl.BlockSpec((1, 2, n2, w), lambda k, bi: (k, 0, 0, 0)), aspec],
        out_specs=aspec,
        compiler_params=_cparams("arbitrary", "arbitrary"),
        name="fft_inner_conv",
    )(gg, ggt, kf, a)


def _fft_final_kernel(f_ref, b_ref, x0_ref, z_ref, d_ref, o_ref):
    k = o_ref.shape[1]
    y = jnp.dot(f_ref[...], b_ref[0], precision=HIGHEST, preferred_element_type=F32)
    o_ref[0] = x0_ref[0] * (y[0:k] + z_ref[0] * d_ref[...])


def fft_final(bm, finv, x0, z, d_tiled):
    b, rows, cols = bm.shape
    k = x0.shape[1]
    kp = finv.shape[0]
    tc = FFT_TC
    xspec = pl.BlockSpec((1, k, tc), lambda bi, j: (bi, 0, j))
    return pl.pallas_call(
        _fft_final_kernel,
        out_shape=jax.ShapeDtypeStruct((b, k, cols), F32),
        grid=(b, cols // tc),
        in_specs=[pl.BlockSpec((kp, rows), lambda bi, j: (0, 0)),
                  pl.BlockSpec((1, rows, tc), lambda bi, j: (bi, 0, j)),
                  xspec, xspec, pl.BlockSpec((1, tc), lambda bi, j: (0, 0))],
        out_specs=xspec,
        compiler_params=_cparams("arbitrary", "arbitrary"),
        name="fft_outer_inverse_gate",
    )(finv, bm, x0, z, d_tiled)


def hyena_mixer(pb, hp, nb, lp, padf, l_real, tm):
    w = GROUP_W
    n2 = FFT_N2
    k1 = lp // n2
    n1 = _fft_n1(lp, l_real)
    n_total = n1 * n2
    ffwd = _outer_dft(n1, k1, False, n_total)
    finv = _outer_dft(n1, k1, True, n_total)
    gg, ggt = hp['gg'], hp['ggt']
    filt, asum = hyena_filter(hp['w1'], hp['b1'], hp['f1'], hp['w2'], hp['b2'], hp['f2'], hp['w3'],
                              hp['decay'], lp, l_real, tm)
    fa = fft_outer(filt.reshape(1, k1, n2 * 2 * w), ffwd)
    kf = filter_spectrum(fa.reshape(2, n1, n2, 2 * w), gg, asum)
    x0, z = hyena_prep(pb, hp['conv_w'], hp['conv_b'], lp, padf, tm)
    a = fft_outer(z.reshape(nb, k1, n2 * w), ffwd)
    bm = fft_mid(a.reshape(nb, 2, n1, n2, w), gg, ggt, kf)
    d_tiled = jnp.tile(hp['d'].reshape(1, w), (1, FFT_TC // w))
    out = fft_final(bm.reshape(nb, 2 * n1, n2 * w), finv, x0.reshape(nb, k1, n2 * w),
                    z.reshape(nb, k1, n2 * w), d_tiled)
    return out.reshape(nb * lp, w)


TM = 640
ATTN_TQ = 1280
ATTN_TKB = 256
ATTN_UNROLL = 8
SCAN_TB = 256
FFN_TK = 1408


def kernel(x_prompt, x_sample, meta, emb_ln_g, emb_ln_b, w_in, lam_q1, lam_k1, lam_q2, lam_k2, attn_norm_g, hy_conv_w, hy_conv_b, hy_w1, hy_b1, hy_f1, hy_w2, hy_b2, hy_f2, hy_w3, hy_decay, hy_d, hg_lb, hg_norm_g, gdn_conv_w, gdn_a_log, gdn_dt_bias, gdn_norm_g, w_out, ln1_g, ln1_b, w_up, ffn_conv_w, ffn_conv_b, w_down, ln2_g, ln2_b):
    n_prompt = x_prompt.shape[0]
    x = jnp.concatenate([x_prompt, x_sample], axis=0)
    nb, seq, d = x.shape
    padf = ROW_ALIGN - N_META
    lp = seq + ROW_ALIGN
    l_real = seq + N_META
    m = nb * lp
    w = GROUP_W

    h = embed(x, meta, emb_ln_g, emb_ln_b, padf).reshape(m, d)
    tables = rope_tables(lp, padf)
    consts = _scan_constants()
    sel = _gate_selectors()
    gg, ggt = inner_dft(_fft_n1(lp, l_real))
    sm = jax.nn.softmax(hg_lb, axis=0)
    lb_all = jnp.cumsum(sm, axis=0) - sm[0]

    for l in range(DEPTH):
        wl = w_in[l].astype(BF16)
        wd = jnp.pad(wl[:, 11 * w:], ((0, 0), (0, 4 * w + GATE_LANES - (wl.shape[1] - 11 * w))))
        pa = matmul(h, wl[:, 0:3 * w], TM, 3 * w)
        pb = matmul(h, wl[:, 3 * w:6 * w], TM, 3 * w)
        pc = matmul(h, wl[:, 6 * w:11 * w], TM, 5 * w // 2)
        pd = matmul(h, wd, TM, wd.shape[1])

        qlo, qhi, ka, va = attn_prep(pa, tables, nb, lp, TM)
        lam_vecs = jnp.stack([lam_q1[l], lam_k1[l], lam_q2[l], lam_k2[l]])
        oa = diff_attention(qlo, qhi, ka, va, lam_vecs, attn_norm_g[l], nb, lp, padf, l,
                            ATTN_TQ, ATTN_TKB, ATTN_UNROLL)

        hp = dict(gg=gg, ggt=ggt, w1=hy_w1[l], b1=hy_b1[l], f1=hy_f1[l], w2=hy_w2[l], b2=hy_b2[l],
                  f2=hy_f2[l], w3=hy_w3[l], decay=hy_decay[l], d=hy_d[l], conv_w=hy_conv_w[l],
                  conv_b=hy_conv_b[l])
        ob = hyena_mixer(pb, hp, nb, lp, padf, l_real, TM)

        qc, vc, gc, sgc = hgrn_prep(pc, 0, lb_all[l], TM)
        oc = gated_norm(hgrn_scan(qc, vc, gc, consts, nb, lp, SCAN_TB), sgc, hg_norm_g[l], TM)

        qd, kd, vd, szd, gates = gdn_prep(pd, gdn_conv_w[l], gdn_a_log[l], gdn_dt_bias[l], lp, padf, TM)
        od = gated_norm(gdn_scan(qd, kd, vd, gates, consts, sel, nb, lp, SCAN_TB), szd, gdn_norm_g[l], TM)

        h = proj_residual_ln([oa, ob, oc, od], w_out[l].astype(BF16), h, ln1_g[l], ln1_b[l], lp, padf, TM)
        act = ffn_up(h, w_up[l].astype(BF16), ffn_conv_w[l], ffn_conv_b[l], TM, 512)
        f = act.shape[1]
        h = ffn_down_ln(act, w_down[l].astype(BF16), h, ln2_g[l], ln2_b[l], lp, padf, TM,
                        FFN_TK if f % FFN_TK == 0 else f)

    y = h.reshape(nb, lp, d)[:, ROW_ALIGN:]
    return (y[:n_prompt], y[n_prompt:])
```

```python
import functools
import math

import jax
import jax.numpy as jnp
import numpy as np
from jax import lax
from jax.experimental import pallas as pl
from jax.experimental.pallas import tpu as pltpu

D_MODEL = 2048
DEPTH = 4
N_META = 16
GROUP_W = 512
N_HEADS = 4
HEAD_W = 128
DA_HD = 64
ROT_DIM = 16
ROPE_THETA = 500000.0
HY_BANDS = 16
HY_SHIFT = 0.05
CHUNK = 64
D_FF = 5632
ALPHA = (2.0 * DEPTH) ** 0.25
LN_EPS = 1e-5
RMS_EPS = 1e-6
F32 = jnp.float32
BF16 = jnp.bfloat16
HIGHEST = lax.Precision.HIGHEST

ROW_ALIGN = 256
VMEM_LIMIT = 56 * 1024 * 1024


def _cparams(*sem):
    return pltpu.CompilerParams(dimension_semantics=sem, vmem_limit_bytes=VMEM_LIMIT)


def _valid_rows(tile_idx, tiles_per_seq, tm, padf):
    base = (tile_idx % tiles_per_seq) * tm
    return (base + lax.broadcasted_iota(jnp.int32, (tm, 1), 0)) >= padf


def _ln_rows(y, g, b):
    mu = jnp.mean(y, axis=-1, keepdims=True)
    d = y - mu
    var = jnp.mean(d * d, axis=-1, keepdims=True)
    return d * lax.rsqrt(var + LN_EPS) * g + b


def _embed_kernel(x_ref, meta_ref, g_ref, b_ref, o_ref, ob_ref, *, padf):
    t = pl.program_id(1)
    g = g_ref[...]
    b = b_ref[...]

    @pl.when(t == 0)
    def _():
        y = jnp.concatenate([jnp.zeros((padf, o_ref.shape[2]), F32), _ln_rows(meta_ref[...], g, b)], axis=0)
        o_ref[0] = y
        ob_ref[0] = y.astype(BF16)

    @pl.when(t > 0)
    def _():
        y = _ln_rows(x_ref[0], g, b)
        o_ref[0] = y
        ob_ref[0] = y.astype(BF16)


def embed(x, meta, g, b, padf):
    nb, s, d = x.shape
    tm = ROW_ALIGN
    assert padf + N_META == tm and s % tm == 0
    lp = s + tm
    ospec = pl.BlockSpec((1, tm, d), lambda bi, t: (bi, t, 0))
    return pl.pallas_call(
        functools.partial(_embed_kernel, padf=padf),
        out_shape=(jax.ShapeDtypeStruct((nb, lp, d), F32), jax.ShapeDtypeStruct((nb, lp, d), BF16)),
        grid=(nb, lp // tm),
        in_specs=[pl.BlockSpec((1, tm, d), lambda bi, t: (bi, jnp.maximum(t - 1, 0), 0)),
                  pl.BlockSpec((N_META, d), lambda bi, t: (0, 0)),
                  pl.BlockSpec((1, d), lambda bi, t: (0, 0)),
                  pl.BlockSpec((1, d), lambda bi, t: (0, 0))],
        out_specs=(ospec, ospec),
        compiler_params=_cparams("arbitrary", "arbitrary"),
        name="embed_ln",
    )(x, meta, g.reshape(1, d), b.reshape(1, d))


def _mm_kernel(x_ref, w_ref, o_ref):
    o_ref[...] = jnp.dot(x_ref[...], w_ref[...], preferred_element_type=F32).astype(o_ref.dtype)


def matmul(x, w, tm, tn, out_dtype=F32):
    m, k = x.shape
    n = w.shape[1]
    assert m % tm == 0 and n % tn == 0
    return pl.pallas_call(
        _mm_kernel,
        out_shape=jax.ShapeDtypeStruct((m, n), out_dtype),
        grid=(n // tn, m // tm),
        in_specs=[pl.BlockSpec((tm, k), lambda j, i: (i, 0)),
                  pl.BlockSpec((k, tn), lambda j, i: (0, j))],
        out_specs=pl.BlockSpec((tm, tn), lambda j, i: (i, j)),
        compiler_params=_cparams("arbitrary", "arbitrary"),
        name="dense_matmul",
    )(x, w)


def _proj_ln_kernel(*refs, n_in, tiles_per_seq, padf):
    o_refs = refs[:n_in]
    w_ref, h_ref, g_ref, b_ref, out_ref, outb_ref = refs[n_in:]
    tm = h_ref.shape[0]
    acc = None
    off = 0
    for r in o_refs:
        kw = r.shape[1]
        part = jnp.dot(r[...].astype(BF16), w_ref[off:off + kw, :], preferred_element_type=F32)
        acc = part if acc is None else acc + part
        off += kw
    y = _ln_rows(ALPHA * h_ref[...] + acc, g_ref[...], b_ref[...])
    valid = _valid_rows(pl.program_id(0), tiles_per_seq, tm, padf)
    y = jnp.where(valid, y, 0.0)
    out_ref[...] = y
    outb_ref[...] = y.astype(BF16)


def proj_residual_ln(parts, w, h, g, b, lp, padf, tm):
    m, d = h.shape
    assert m % tm == 0 and lp % tm == 0
    n_in = len(parts)
    in_specs = [pl.BlockSpec((tm, p.shape[1]), lambda i: (i, 0)) for p in parts]
    in_specs += [pl.BlockSpec(w.shape, lambda i: (0, 0)),
                 pl.BlockSpec((tm, d), lambda i: (i, 0)),
                 pl.BlockSpec((1, d), lambda i: (0, 0)),
                 pl.BlockSpec((1, d), lambda i: (0, 0))]
    ospec = pl.BlockSpec((tm, d), lambda i: (i, 0))
    return pl.pallas_call(
        functools.partial(_proj_ln_kernel, n_in=n_in, tiles_per_seq=lp // tm, padf=padf),
        out_shape=(jax.ShapeDtypeStruct((m, d), F32), jax.ShapeDtypeStruct((m, d), BF16)),
        grid=(m // tm,),
        in_specs=in_specs,
        out_specs=(ospec, ospec),
        compiler_params=_cparams("arbitrary"),
        name="proj_residual_ln",
    )(*parts, w, h, g.reshape(1, d), b.reshape(1, d))


HALO = 8
HALO_BF16 = 16


def _halo_specs(tm, width, m, col_block=0, halo=HALO):
    nblk = m // halo
    per = tm // halo

    def prev_map(*idx):
        i = idx[-1]
        return (jnp.maximum(i * per - 1, 0), col_block)

    def next_map(*idx):
        i = idx[-1]
        return (jnp.minimum((i + 1) * per, nblk - 1), col_block)

    return (pl.BlockSpec((halo, width), prev_map), pl.BlockSpec((halo, width), next_map))


def _conv3_rows(pm, prev_row, next_row, cw):
    tm = pm.shape[0]
    rid = lax.broadcasted_iota(jnp.int32, (tm, 1), 0)
    down = jnp.where(rid == 0, prev_row, pltpu.roll(pm, 1, 0))
    up = jnp.where(rid == tm - 1, next_row, pltpu.roll(pm, tm - 1, 0))
    return down * cw[0:1, :] + pm * cw[1:2, :] + up * cw[2:3, :]


def _ffn_up_kernel(x_ref, xp_ref, xn_ref, wg_ref, wu_ref, cwg_ref, cwu_ref, cbg_ref, cbu_ref, o_ref):
    i = pl.program_id(1)
    last = pl.num_programs(1) - 1
    x = x_ref[...]
    hr = xp_ref.shape[0]
    halo = jnp.concatenate([xp_ref[...], xn_ref[...]], axis=0)
    nmask = jnp.where(i == last, 0.0, 1.0)

    def branch(w_ref, cw_ref, cb_ref, sl):
        w = w_ref[:, sl]
        pm = jnp.dot(x, w, preferred_element_type=F32)
        ph = jnp.dot(halo, w, preferred_element_type=F32)
        return _conv3_rows(pm, ph[hr - 1:hr, :], ph[hr:hr + 1, :] * nmask, cw_ref[:, sl]) + cb_ref[:, sl]

    tn = o_ref.shape[1]
    for c0 in range(0, tn, FFN_SUB):
        sl = slice(c0, min(c0 + FFN_SUB, tn))
        g = branch(wg_ref, cwg_ref, cbg_ref, sl)
        u = branch(wu_ref, cwu_ref, cbu_ref, sl)
        o_ref[:, sl] = (g * jax.nn.sigmoid(g) * u).astype(o_ref.dtype)


FFN_SUB = 256


def ffn_up(h, w_up, conv_w, conv_b, tm, tn):
    m, k = h.shape
    f = w_up.shape[1] // 2
    assert m % tm == 0 and f % tn == 0 and h.dtype == BF16
    nj = f // tn
    xp_spec, xn_spec = _halo_specs(tm, k, m, halo=HALO_BF16)
    cb = conv_b.reshape(1, 2 * f)
    return pl.pallas_call(
        _ffn_up_kernel,
        out_shape=jax.ShapeDtypeStruct((m, f), BF16),
        grid=(nj, m // tm),
        in_specs=[pl.BlockSpec((tm, k), lambda j, i: (i, 0)), xp_spec, xn_spec,
                  pl.BlockSpec((k, tn), lambda j, i: (0, j)),
                  pl.BlockSpec((k, tn), lambda j, i: (0, j + nj)),
                  pl.BlockSpec((3, tn), lambda j, i: (0, j)),
                  pl.BlockSpec((3, tn), lambda j, i: (0, j + nj)),
                  pl.BlockSpec((1, tn), lambda j, i: (0, j)),
                  pl.BlockSpec((1, tn), lambda j, i: (0, j + nj))],
        out_specs=pl.BlockSpec((tm, tn), lambda j, i: (i, j)),
        compiler_params=_cparams("arbitrary", "arbitrary"),
        name="ffn_up_conv_gate",
    )(h, h, h, w_up, w_up, conv_w, conv_w, cb, cb)


def _ffn_down_kernel(a_ref, w_ref, h_ref, g_ref, b_ref, out_ref, outb_ref, *, tiles_per_seq, padf):
    tm = h_ref.shape[0]
    acc = jnp.dot(a_ref[...], w_ref[...], preferred_element_type=F32)
    y = _ln_rows(ALPHA * h_ref[...] + acc, g_ref[...], b_ref[...])
    valid = _valid_rows(pl.program_id(0), tiles_per_seq, tm, padf)
    y = jnp.where(valid, y, 0.0)
    out_ref[...] = y
    outb_ref[...] = y.astype(BF16)


def ffn_down_ln(a, w_down, h, g, b, lp, padf, tm):
    m, d = h.shape
    f = a.shape[1]
    assert m % tm == 0 and lp % tm == 0
    ospec = pl.BlockSpec((tm, d), lambda i: (i, 0))
    return pl.pallas_call(
        functools.partial(_ffn_down_kernel, tiles_per_seq=lp // tm, padf=padf),
        out_shape=(jax.ShapeDtypeStruct((m, d), F32), jax.ShapeDtypeStruct((m, d), BF16)),
        grid=(m // tm,),
        in_specs=[pl.BlockSpec((tm, f), lambda i: (i, 0)),
                  pl.BlockSpec((f, d), lambda i: (0, 0), pipeline_mode=pl.Buffered(1)),
                  pl.BlockSpec((tm, d), lambda i: (i, 0)),
                  pl.BlockSpec((1, d), lambda i: (0, 0)),
                  pl.BlockSpec((1, d), lambda i: (0, 0))],
        out_specs=(ospec, ospec),
        compiler_params=_cparams("arbitrary"),
        name="ffn_down_residual_ln",
    )(a, w_down, h, g.reshape(1, d), b.reshape(1, d))


def rope_tables(lp, padf):
    half = ROT_DIM // 2
    pos = (jnp.arange(lp) - padf).astype(F32)
    inv = 1.0 / (ROPE_THETA ** (jnp.arange(half, dtype=F32) / half))
    ang = pos[:, None] * inv[None]
    cos, sin = jnp.cos(ang), jnp.sin(ang)
    ones = jnp.ones((lp, DA_HD - ROT_DIM), F32)
    zeros = jnp.zeros((lp, DA_HD - ROT_DIM), F32)
    zh = jnp.zeros((lp, half), F32)
    c = jnp.concatenate([cos, cos, ones], axis=1)
    sa = jnp.concatenate([-sin, zh, zeros], axis=1)
    sb = jnp.concatenate([zh, sin, zeros], axis=1)
    return tuple(jnp.concatenate([t, t], axis=1) for t in (c, sa, sb))


def _attn_prep_kernel(p_ref, c_ref, sa_ref, sb_ref, qlo_ref, qhi_ref, k_ref, vt_ref):
    w = GROUP_W
    reps = w // c_ref.shape[1]
    c = jnp.concatenate([c_ref[...]] * reps, axis=1)
    sa = jnp.concatenate([sa_ref[...]] * reps, axis=1)
    sb = jnp.concatenate([sb_ref[...]] * reps, axis=1)
    half = ROT_DIM // 2

    def rot(x):
        return x * c + pltpu.roll(x, w - half, 1) * sa + pltpu.roll(x, half, 1) * sb

    q = rot(p_ref[:, 0:w]) * (DA_HD ** -0.5)
    lane = lax.broadcasted_iota(jnp.int32, (1, w), 1) % (2 * DA_HD)
    qlo_ref[0] = jnp.where(lane < DA_HD, q, 0.0).T.astype(BF16)
    qhi_ref[0] = jnp.where(lane >= DA_HD, q, 0.0).T.astype(BF16)
    k_ref[...] = rot(p_ref[:, w:2 * w]).astype(BF16)
    vt_ref[0] = p_ref[:, 2 * w:3 * w].T.astype(BF16)


def attn_prep(proj, tables, nb, lp, tm):
    m = proj.shape[0]
    w = GROUP_W
    tps = lp // tm
    tspec = pl.BlockSpec((tm, 2 * DA_HD), lambda i: (i % tps, 0))
    tr_shape = jax.ShapeDtypeStruct((nb, w, lp), BF16)
    tr_spec = pl.BlockSpec((1, w, tm), lambda i: (i // tps, 0, i % tps))
    return pl.pallas_call(
        _attn_prep_kernel,
        out_shape=(tr_shape, tr_shape, jax.ShapeDtypeStruct((m, w), BF16), tr_shape),
        grid=(m // tm,),
        in_specs=[pl.BlockSpec((tm, 3 * w), lambda i: (i, 0)), tspec, tspec, tspec],
        out_specs=(tr_spec, tr_spec, pl.BlockSpec((tm, w), lambda i: (i, 0)), tr_spec),
        compiler_params=_cparams("arbitrary"),
        name="attn_prep_rotary",
    )(proj, *tables)


ATTN_QC = 256
ATTN_ONES = 8


def _attn_kernel(lam_ref, g_ref, qlo_ref, qhi_ref, k_ref, vt_ref, o_ref, qs_ref, acc_ref,
                 *, tkb, padf, lam_init, unroll):
    tq = qlo_ref.shape[2]
    nq2 = 2 * tq
    nkb = k_ref.shape[0] // tkb
    qs_ref[:, :tq] = qlo_ref[0]
    qs_ref[:, tq:] = qhi_ref[0]
    row_ok = lax.broadcasted_iota(jnp.int32, (tkb, 1), 0) >= padf
    ones = jnp.ones((ATTN_ONES, tkb), BF16)

    def scores(kb, c0, masked):
        start = kb * tkb
        if not isinstance(start, int):
            start = pl.multiple_of(start, tkb)
        s = jnp.dot(k_ref[pl.ds(start, tkb), :], qs_ref[:, c0:c0 + ATTN_QC], preferred_element_type=F32)
        if masked:
            s = jnp.where(row_ok, s, -1e30)
        return s, start

    def colmax(s):
        return jnp.max(jnp.max(s.reshape(tkb // 8, 8, ATTN_QC), axis=0), axis=0, keepdims=True)

    for c0 in range(0, nq2, 2 * ATTN_QC):
        cols = (c0, c0 + ATTN_QC)

        def consume(kb, s, m):
            start = kb * tkb
            if not isinstance(start, int):
                start = pl.multiple_of(start, tkb)
            vext = jnp.concatenate([vt_ref[0, :, pl.ds(start, tkb)], ones], axis=0)
            m_out = []
            for i in range(2):
                m_new = jnp.maximum(m[i], colmax(s[i]))
                p = jnp.exp(s[i] - m_new).astype(BF16)
                sl = slice(cols[i], cols[i] + ATTN_QC)
                acc_ref[:, sl] = acc_ref[:, sl] * jnp.exp(m[i] - m_new) + jnp.dot(vext, p, preferred_element_type=F32)
                m_out.append(m_new)
            return tuple(m_out)

        for c in cols:
            acc_ref[:, c:c + ATTN_QC] = jnp.zeros((HEAD_W + ATTN_ONES, ATTN_QC), F32)

        def body(kb, carry):
            s, m = carry
            s_next = tuple(scores(kb + 1, c, False)[0] for c in cols)
            return s_next, consume(kb, s, m)

        m0 = tuple(jnp.full((1, ATTN_QC), -1e30, F32) for _ in cols)
        s_last, m_last = lax.fori_loop(0, nkb - 1, body, (tuple(scores(0, c, True)[0] for c in cols), m0),
                                       unroll=unroll)
        consume(nkb - 1, s_last, m_last)

    lam = (jnp.exp(jnp.sum(lam_ref[0:1, :] * lam_ref[1:2, :], axis=-1, keepdims=True))
           - jnp.exp(jnp.sum(lam_ref[2:3, :] * lam_ref[3:4, :], axis=-1, keepdims=True)) + lam_init)
    o = acc_ref[0:HEAD_W, :] / acc_ref[HEAD_W:HEAD_W + 1, :]
    o = (o[:, :tq] - lam * o[:, tq:]).T
    o = o * lax.rsqrt(jnp.mean(o * o, axis=-1, keepdims=True) + RMS_EPS) * g_ref[...]
    o_ref[...] = (o * (1.0 - lam_init)).astype(o_ref.dtype)


def diff_attention(qlo_t, qhi_t, k, v_t, lam_vecs, norm_g, nb, lp, padf, layer, tq, tkb, unroll):
    m = k.shape[0]
    assert lp % tq == 0 and lp % tkb == 0 and padf < tkb and tq % ATTN_QC == 0
    nq = lp // tq
    lam_init = 0.8 - 0.6 * math.exp(-0.3 * layer)
    qspec = pl.BlockSpec((1, HEAD_W, tq), lambda b, h, qi: (b, h, qi))
    return pl.pallas_call(
        functools.partial(_attn_kernel, tkb=tkb, padf=padf, lam_init=lam_init, unroll=unroll),
        out_shape=jax.ShapeDtypeStruct((m, GROUP_W), BF16),
        grid=(nb, N_HEADS, nq),
        in_specs=[pl.BlockSpec((4, DA_HD), lambda b, h, qi: (0, 0)),
                  pl.BlockSpec((1, HEAD_W), lambda b, h, qi: (0, 0)),
                  qspec, qspec,
                  pl.BlockSpec((lp, HEAD_W), lambda b, h, qi: (b, h)),
                  pl.BlockSpec((1, HEAD_W, lp), lambda b, h, qi: (b, h, 0))],
        out_specs=pl.BlockSpec((tq, HEAD_W), lambda b, h, qi: (b * nq + qi, h)),
        scratch_shapes=[pltpu.VMEM((HEAD_W, 2 * tq), BF16),
                        pltpu.VMEM((HEAD_W + ATTN_ONES, 2 * tq), F32)],
        compiler_params=_cparams("arbitrary", "arbitrary", "arbitrary"),
        name="diff_attention",
    )(lam_vecs, norm_g.reshape(1, HEAD_W), qlo_t, qhi_t, k, v_t)


N_LEVELS = 6


def _scan_constants():
    c = CHUNK
    idx = np.arange(c)
    cs = np.zeros((2, (N_LEVELS + 2) * c, c), np.float32)
    mask = np.zeros((2, N_LEVELS + 2, c, c), np.float32)
    for d in range(2):
        cum = (idx[None, :] <= idx[:, None]) if d == 0 else (idx[None, :] >= idx[:, None])
        cs[d, 0:c] = cum
        for l in range(N_LEVELS):
            s = (c // 2) >> l
            blk = idx // s
            ref = (blk | 1) * s - 1 if d == 0 else (blk | 1) * s
            cs[d, (l + 1) * c:(l + 2) * c] = cum[ref]
            if d == 0:
                mask[d, l] = ((blk[:, None] & 1) == 1) & (blk[None, :] == blk[:, None] - 1)
            else:
                mask[d, l] = ((blk[:, None] & 1) == 0) & (blk[None, :] == blk[:, None] + 1)
        cs[d, (N_LEVELS + 1) * c:] = 1.0
        mask[d, N_LEVELS] = cum
        mask[d, N_LEVELS + 1] = cum & (idx[None, :] != idx[:, None])
    return jnp.asarray(np.concatenate([cs] * 3, axis=2), BF16), jnp.asarray(mask)


def _split3(x):
    hi = x.astype(BF16)
    r = x - hi.astype(F32)
    mid = r.astype(BF16)
    return hi, mid, (r - mid.astype(F32)).astype(BF16)


def _split3_rows(x):
    return jnp.concatenate(_split3(x), axis=0)


def _split3_lanes(x):
    return jnp.concatenate(_split3(x), axis=1)


def _nt(a, b):
    return lax.dot_general(a.astype(BF16), b.astype(BF16), (((1,), (1,)), ((), ())),
                           preferred_element_type=F32)


def _tn(a, b):
    return lax.dot_general(a.astype(BF16), b.astype(BF16), (((0,), (0,)), ((), ())),
                           preferred_element_type=F32)


def _nn(a, b):
    return jnp.dot(a.astype(BF16), b.astype(BF16), preferred_element_type=F32)


def _hgrn_prep_kernel(p_ref, lb_ref, q_ref, v_ref, g_ref, sg_ref):
    w = GROUP_W
    q = p_ref[:, 0:w]
    q_ref[...] = (q * jax.nn.sigmoid(q) * (HEAD_W ** -0.5)).astype(BF16)
    for d in range(2):
        lb = lb_ref[d:d + 1, :]
        f = p_ref[:, (1 + d) * w:(2 + d) * w]
        g_ref[d] = jnp.log(lb + (1.0 - lb) * jax.nn.sigmoid(f))
    v_ref[...] = p_ref[:, 3 * w:4 * w].astype(BF16)
    gate = p_ref[:, 4 * w:5 * w]
    sg_ref[...] = (gate * jax.nn.sigmoid(gate)).astype(BF16)


def hgrn_prep(proj, col_block, lb, tm):
    m = proj.shape[0]
    w = GROUP_W
    ospec = pl.BlockSpec((tm, w), lambda i: (i, 0))
    return pl.pallas_call(
        _hgrn_prep_kernel,
        out_shape=(jax.ShapeDtypeStruct((m, w), BF16), jax.ShapeDtypeStruct((m, w), BF16),
                   jax.ShapeDtypeStruct((2, m, w), F32), jax.ShapeDtypeStruct((m, w), BF16)),
        grid=(m // tm,),
        in_specs=[pl.BlockSpec((tm, 5 * w), lambda i: (i, col_block)),
                  pl.BlockSpec((2, w), lambda i: (0, 0))],
        out_specs=(ospec, ospec, pl.BlockSpec((2, tm, w), lambda i: (0, i, 0)), ospec),
        compiler_params=_cparams("arbitrary"),
        name="hgrn_prep",
    )(proj, lb)


def _head(x, h):
    return x[:, h * HEAD_W:(h + 1) * HEAD_W]


def _chunk_rows(ci, nch, d):
    return pl.multiple_of((ci + d * (nch - 1 - 2 * ci)) * CHUNK, CHUNK)


def _hgrn_scan_kernel(cs_ref, mask_ref, q_ref, v_ref, g_ref, o_ref, s_ref, *, nch):
    c = CHUNK
    d = pl.program_id(1)
    heads = range(N_HEADS)
    chunks = range(nch)

    @pl.when(pl.program_id(2) == 0)
    def _():
        s_ref[...] = jnp.zeros(s_ref.shape, F32)

    eye = mask_ref[0, N_LEVELS] - mask_ref[0, N_LEVELS + 1]
    rows = [_chunk_rows(ci, nch, d) for ci in chunks]
    g = [g_ref[0, pl.ds(r0, c), :] for r0 in rows]
    q = [q_ref[pl.ds(r0, c), :].astype(F32) for r0 in rows]
    v = [v_ref[pl.ds(r0, c), :] for r0 in rows]
    big = [jnp.dot(cs_ref[0], _split3_rows(gi), preferred_element_type=F32) for gi in g]
    b = [x[0:c] for x in big]
    tot = [x[(N_LEVELS + 1) * c:(N_LEVELS + 2) * c] for x in big]
    k = [1.0 - jnp.exp(gi) for gi in g]
    a = [[eye * jnp.sum(_head(q[ci] * k[ci], h), axis=-1, keepdims=True) for h in heads] for ci in chunks]
    for l in range(N_LEVELS):
        e = [jnp.exp(-jnp.abs(b[ci] - big[ci][(l + 1) * c:(l + 2) * c])) for ci in chunks]
        qe = [(q[ci] * e[ci]).astype(BF16) for ci in chunks]
        ke = [(k[ci] * e[ci]).astype(BF16) for ci in chunks]
        lm = mask_ref[0, l]
        a = [[a[ci][h] + lm * _nt(_head(qe[ci], h), _head(ke[ci], h)) for h in heads] for ci in chunks]
    o_intra = [[_nn(a[ci][h], _head(v[ci], h)) for h in heads] for ci in chunks]
    qb = [(q[ci] * jnp.exp(b[ci])).astype(BF16) for ci in chunks]
    kd = [(k[ci] * jnp.exp(tot[ci] - b[ci])).astype(BF16) for ci in chunks]
    dec = [jnp.exp(tot[ci][0:1, :]) for ci in chunks]
    s = [s_ref[h] for h in heads]
    for ci in chunks:
        for h in heads:
            o = o_intra[ci][h] + _nt(_head(qb[ci], h), s[h])
            o_ref[0, pl.ds(rows[ci], c), h * HEAD_W:(h + 1) * HEAD_W] = o
        s = [s[h] * _head(dec[ci], h) + _tn(_head(v[ci], h), _head(kd[ci], h)) for h in heads]
    for h in heads:
        s_ref[h] = s[h]


def _scan_row_block(nblk):
    def row(b, d, t):
        return b * nblk + t + d * (nblk - 1 - 2 * t)
    return row


def hgrn_scan(q, v, g, consts, nb, lp, tb):
    m, w = q.shape
    cs, mask = consts
    row = _scan_row_block(lp // tb)
    return pl.pallas_call(
        functools.partial(_hgrn_scan_kernel, nch=tb // CHUNK),
        out_shape=jax.ShapeDtypeStruct((2, m, w), F32),
        grid=(nb, 2, lp // tb),
        in_specs=[pl.BlockSpec((1,) + cs.shape[1:], lambda b, d, t: (d, 0, 0)),
                  pl.BlockSpec((1,) + mask.shape[1:], lambda b, d, t: (d, 0, 0, 0)),
                  pl.BlockSpec((tb, w), lambda b, d, t: (row(b, d, t), 0)),
                  pl.BlockSpec((tb, w), lambda b, d, t: (row(b, d, t), 0)),
                  pl.BlockSpec((1, tb, w), lambda b, d, t: (d, row(b, d, t), 0))],
        out_specs=pl.BlockSpec((1, tb, w), lambda b, d, t: (d, row(b, d, t), 0)),
        scratch_shapes=[pltpu.VMEM((N_HEADS, HEAD_W, HEAD_W), F32)],
        compiler_params=_cparams("arbitrary", "arbitrary", "arbitrary"),
        name="hgrn_scan",
    )(cs, mask, q, v, g)


def _gated_norm_kernel(o_ref, sg_ref, g_ref, out_ref):
    o = o_ref[0] + o_ref[1]
    gain = g_ref[...]
    for h in range(N_HEADS):
        sl = slice(h * HEAD_W, (h + 1) * HEAD_W)
        oh = o[:, sl]
        y = oh * lax.rsqrt(jnp.mean(oh * oh, axis=-1, keepdims=True) + RMS_EPS) * gain
        out_ref[:, sl] = (y * sg_ref[:, sl].astype(F32)).astype(out_ref.dtype)


def gated_norm(o2, sgate, norm_g, tm):
    m = sgate.shape[0]
    w = GROUP_W
    return pl.pallas_call(
        _gated_norm_kernel,
        out_shape=jax.ShapeDtypeStruct((m, w), BF16),
        grid=(m // tm,),
        in_specs=[pl.BlockSpec((2, tm, w), lambda i: (0, i, 0)),
                  pl.BlockSpec((tm, w), lambda i: (i, 0)),
                  pl.BlockSpec((1, HEAD_W), lambda i: (0, 0))],
        out_specs=pl.BlockSpec((tm, w), lambda i: (i, 0)),
        compiler_params=_cparams("arbitrary"),
        name="gated_rms_norm",
    )(o2, sgate, norm_g.reshape(1, HEAD_W))


GATE_LANES = 128


def _softplus(x):
    return jnp.maximum(x, 0.0) + jnp.log(1.0 + jnp.exp(-jnp.abs(x)))


def _gdn_prep_kernel(p_ref, pp_ref, pn_ref, cw_ref, acoef_ref, dtb_ref,
                     q_ref, k_ref, v_ref, sz_ref, gates_ref, *, tiles_per_seq, padf):
    w = GROUP_W
    i = pl.program_id(0)
    tm = p_ref.shape[0]
    nmask = jnp.where(i == pl.num_programs(0) - 1, 0.0, 1.0)
    x = _conv3_rows(p_ref[:, 0:3 * w], pp_ref[HALO - 1:HALO, :], pn_ref[0:1, :] * nmask, cw_ref[...])
    x = x * jax.nn.sigmoid(x)
    valid = _valid_rows(i, tiles_per_seq, tm, padf)
    for h in range(N_HEADS):
        sl = slice(h * HEAD_W, (h + 1) * HEAD_W)
        qh = x[:, sl]
        q_ref[:, sl] = (qh * lax.rsqrt(jnp.sum(qh * qh, axis=-1, keepdims=True) + 1e-6)
                        * (HEAD_W ** -0.5)).astype(BF16)
        kh = x[:, w + h * HEAD_W:w + (h + 1) * HEAD_W]
        k_ref[:, sl] = (kh * lax.rsqrt(jnp.sum(kh * kh, axis=-1, keepdims=True) + 1e-6)).astype(BF16)
    v_ref[...] = jnp.where(valid, x[:, 2 * w:3 * w], 0.0).astype(BF16)
    z = p_ref[:, 3 * w:4 * w]
    sz_ref[...] = (z * jax.nn.sigmoid(z)).astype(BF16)
    gt = p_ref[:, 4 * w:4 * w + GATE_LANES]
    lane = lax.broadcasted_iota(jnp.int32, (1, GATE_LANES), 1)
    decay = -acoef_ref[...] * _softplus(gt + dtb_ref[...])
    gates_ref[...] = jnp.where(lane < 2 * N_HEADS, jax.nn.sigmoid(gt), decay)


def gdn_prep(pd, conv_w, a_log, dt_bias, lp, padf, tm):
    m = pd.shape[0]
    w = GROUP_W
    zeros8 = jnp.zeros((2 * N_HEADS,), F32)
    padl = jnp.zeros((GATE_LANES - 4 * N_HEADS,), F32)
    acoef = jnp.concatenate([zeros8, jnp.exp(a_log).reshape(-1), padl]).reshape(1, GATE_LANES)
    dtb = jnp.concatenate([zeros8, dt_bias.reshape(-1), padl]).reshape(1, GATE_LANES)
    pspec, nspec = _halo_specs(tm, 3 * w, m)
    ospec = pl.BlockSpec((tm, w), lambda i: (i, 0))
    return pl.pallas_call(
        functools.partial(_gdn_prep_kernel, tiles_per_seq=lp // tm, padf=padf),
        out_shape=tuple(jax.ShapeDtypeStruct((m, w), BF16) for _ in range(4))
        + (jax.ShapeDtypeStruct((m, GATE_LANES), F32),),
        grid=(m // tm,),
        in_specs=[pl.BlockSpec((tm, pd.shape[1]), lambda i: (i, 0)), pspec, nspec,
                  pl.BlockSpec((3, 3 * w), lambda i: (0, 0)),
                  pl.BlockSpec((1, GATE_LANES), lambda i: (0, 0)),
                  pl.BlockSpec((1, GATE_LANES), lambda i: (0, 0))],
        out_specs=(ospec, ospec, ospec, ospec, pl.BlockSpec((tm, GATE_LANES), lambda i: (i, 0))),
        compiler_params=_cparams("arbitrary"),
        name="gdn_prep",
    )(pd, pd, pd, conv_w, acoef, dtb)


def _gate_selectors():
    sel = np.zeros((2, GATE_LANES, 2 * GROUP_W), np.float32)
    for d in range(2):
        for h in range(N_HEADS):
            sel[d, N_HEADS * d + h, h * HEAD_W:(h + 1) * HEAD_W] = 1.0
            sel[d, 2 * N_HEADS + N_HEADS * d + h, GROUP_W + h * HEAD_W:GROUP_W + (h + 1) * HEAD_W] = 1.0
    return jnp.asarray(np.concatenate([sel] * 3, axis=1), BF16)


def _gdn_scan_kernel(cs_ref, mask_ref, sel_ref, q_ref, k_ref, v_ref, gt_ref, o_ref, s_ref, *, nch):
    c = CHUNK
    w = GROUP_W
    d = pl.program_id(1)
    heads = range(N_HEADS)
    chunks = range(nch)
    items = [(ci, h) for ci in chunks for h in heads]

    @pl.when(pl.program_id(2) == 0)
    def _():
        s_ref[...] = jnp.zeros(s_ref.shape, F32)

    incl = mask_ref[0, N_LEVELS]
    strict = mask_ref[0, N_LEVELS + 1]
    eye = incl - strict
    cum_tot = jnp.concatenate([cs_ref[0, 0:c, :], cs_ref[0, (N_LEVELS + 1) * c:(N_LEVELS + 2) * c, :]], axis=0)
    ones3 = cs_ref[0, (N_LEVELS + 1) * c:(N_LEVELS + 2) * c, :]
    rows = [_chunk_rows(ci, nch, d) for ci in chunks]
    q = [q_ref[pl.ds(r0, c), :].astype(F32) for r0 in rows]
    k = [k_ref[pl.ds(r0, c), :].astype(F32) for r0 in rows]
    v = [v_ref[pl.ds(r0, c), :].astype(F32) for r0 in rows]
    bg = [jnp.dot(_split3_lanes(gt_ref[pl.ds(r0, c), :]), sel_ref[0], preferred_element_type=F32)
          for r0 in rows]
    beta = [x[:, 0:w] for x in bg]
    cb = [jnp.dot(cum_tot, _split3_rows(x[:, w:2 * w]), preferred_element_type=F32) for x in bg]
    bb = [x[0:c] for x in cb]
    tot = [x[c:2 * c] for x in cb]
    bc = {(ci, h): bb[ci][:, h * HEAD_W:h * HEAD_W + c] for ci, h in items}
    br = {it: jnp.dot(ones3, _split3_rows(eye * bc[it]), preferred_element_type=F32) for it in items}
    dec = {it: incl * jnp.exp(jnp.minimum(bc[it] - br[it], 0.0)) for it in items}
    kb = [k[ci] * beta[ci] for ci in chunks]
    n = {(ci, h): strict * _nt(_head(kb[ci], h), _head(k[ci], h)) * dec[(ci, h)] for ci, h in items}
    t = {it: eye - n[it] for it in items}
    pw = {it: _nn(n[it], n[it]) for it in items}
    for _ in range(N_LEVELS - 2):
        t = {it: t[it] + _nn(t[it], pw[it]) for it in items}
        pw = {it: _nn(pw[it], pw[it]) for it in items}
    t = {it: t[it] + _nn(t[it], pw[it]) for it in items}
    eb = [jnp.exp(x) for x in bb]
    rhs_u = [v[ci] * beta[ci] for ci in chunks]
    rhs_w = [kb[ci] * eb[ci] for ci in chunks]
    uw = {(ci, h): _nn(t[(ci, h)], jnp.concatenate([_head(rhs_u[ci], h), _head(rhs_w[ci], h)], axis=1))
          for ci, h in items}
    qk = {(ci, h): _nt(_head(q[ci], h), _head(k[ci], h)) * dec[(ci, h)] for ci, h in items}
    qe = [(q[ci] * eb[ci]).astype(BF16) for ci in chunks]
    kd = [(k[ci] * jnp.exp(tot[ci] - bb[ci])).astype(BF16) for ci in chunks]
    sdec = [jnp.exp(tot[ci][0:1, :]) for ci in chunks]
    s = [s_ref[h] for h in heads]
    for ci in chunks:
        wq = {h: _nn(jnp.concatenate([uw[(ci, h)][:, HEAD_W:].astype(BF16), _head(qe[ci], h)], axis=0), s[h])
              for h in heads}
        vnew = {h: uw[(ci, h)][:, 0:HEAD_W] - wq[h][0:c] for h in heads}
        for h in heads:
            o_ref[0, pl.ds(rows[ci], c), h * HEAD_W:(h + 1) * HEAD_W] = wq[h][c:2 * c] + _nn(qk[(ci, h)], vnew[h])
        s = [s[h] * _head(sdec[ci], h) + _tn(_head(kd[ci], h), vnew[h]) for h in heads]
    for h in heads:
        s_ref[h] = s[h]


def gdn_scan(q, k, v, gates, consts, sel, nb, lp, tb):
    m, w = q.shape
    cs, mask = consts
    row = _scan_row_block(lp // tb)
    rspec = pl.BlockSpec((tb, w), lambda b, d, t: (row(b, d, t), 0))
    return pl.pallas_call(
        functools.partial(_gdn_scan_kernel, nch=tb // CHUNK),
        out_shape=jax.ShapeDtypeStruct((2, m, w), F32),
        grid=(nb, 2, lp // tb),
        in_specs=[pl.BlockSpec((1,) + cs.shape[1:], lambda b, d, t: (d, 0, 0)),
                  pl.BlockSpec((1,) + mask.shape[1:], lambda b, d, t: (d, 0, 0, 0)),
                  pl.BlockSpec((1,) + sel.shape[1:], lambda b, d, t: (d, 0, 0)),
                  rspec, rspec, rspec,
                  pl.BlockSpec((tb, GATE_LANES), lambda b, d, t: (row(b, d, t), 0))],
        out_specs=pl.BlockSpec((1, tb, w), lambda b, d, t: (d, row(b, d, t), 0)),
        scratch_shapes=[pltpu.VMEM((N_HEADS, HEAD_W, HEAD_W), F32)],
        compiler_params=_cparams("arbitrary", "arbitrary", "arbitrary"),
        name="gdn_scan",
    )(cs, mask, sel, q, k, v, gates)


FFT_N2 = 256
FFT_TS = 8


def _fft_n1(lp, l_real):
    need = -(-(2 * l_real - 1) // FFT_N2)
    return max(-(-need // 8) * 8, -(-(lp // FFT_N2) // 8) * 8)


def _outer_dft(n1, k_rows, inverse, n_total):
    kp = -(-k_rows // 8) * 8
    a = np.arange(n1)[:, None] * np.arange(kp)[None, :]
    ang = 2.0 * np.pi * (a % n1) / n1
    c, s = np.cos(ang), np.sin(ang)
    live = (np.arange(kp) < k_rows)[None, :]
    c, s = c * live, s * live
    if not inverse:
        return jnp.asarray(np.concatenate([c, -s], axis=0), F32)
    return jnp.asarray(np.concatenate([c.T, -s.T], axis=1) / n_total, F32)


def inner_dft(n1):
    n2 = FFT_N2
    n = n1 * n2
    k = jnp.arange(n1, dtype=jnp.int32)[:, None, None] + n1 * jnp.arange(n2, dtype=jnp.int32)[None, :, None]
    r = (k * jnp.arange(n2, dtype=jnp.int32)[None, None, :]) % n
    ang = r.astype(F32) * (2.0 * math.pi / n)
    gr, gi = jnp.cos(ang), -jnp.sin(ang)
    gg = jnp.concatenate([jnp.concatenate([gr, -gi], axis=2), jnp.concatenate([gi, gr], axis=2)], axis=1)
    return gg.astype(BF16), jnp.swapaxes(gg, 1, 2).astype(BF16)


def _hyena_prep_kernel(p_ref, pp_ref, pn_ref, cw_ref, cb_ref, x0_ref, z_ref, *, tiles_per_seq, padf):
    w = GROUP_W
    i = pl.program_id(0)
    tm = p_ref.shape[0]
    nmask = jnp.where(i == pl.num_programs(0) - 1, 0.0, 1.0)
    u = _conv3_rows(p_ref[...], pp_ref[HALO - 1:HALO, :], pn_ref[0:1, :] * nmask, cw_ref[...]) + cb_ref[...]
    valid = _valid_rows(i, tiles_per_seq, tm, padf)
    x0_ref[...] = u[:, 0:w]
    z_ref[...] = jnp.where(valid, u[:, w:2 * w] * u[:, 2 * w:3 * w], 0.0)


def hyena_prep(pb, conv_w, conv_b, lp, padf, tm):
    m, w3 = pb.shape
    w = GROUP_W
    pspec, nspec = _halo_specs(tm, w3, m)
    ospec = pl.BlockSpec((tm, w), lambda i: (i, 0))
    return pl.pallas_call(
        functools.partial(_hyena_prep_kernel, tiles_per_seq=lp // tm, padf=padf),
        out_shape=(jax.ShapeDtypeStruct((m, w), F32), jax.ShapeDtypeStruct((m, w), F32)),
        grid=(m // tm,),
        in_specs=[pl.BlockSpec((tm, w3), lambda i: (i, 0)), pspec, nspec,
                  pl.BlockSpec((3, w3), lambda i: (0, 0)), pl.BlockSpec((1, w3), lambda i: (0, 0))],
        out_specs=(ospec, ospec),
        compiler_params=_cparams("arbitrary"),
        name="hyena_prep",
    )(pb, pb, pb, conv_w, conv_b.reshape(1, w3))


def _hyena_filter_kernel(w1t_ref, w1c_ref, w1s_ref, b1_ref, f1_ref, w2_ref, b2_ref, f2_ref, w3_ref, dec_ref,
                         filt_ref, asum_ref, *, l_real):
    i = pl.program_id(0)
    tm = filt_ref.shape[0]
    w = GROUP_W
    row = i * tm + lax.broadcasted_iota(jnp.int32, (tm, 1), 0)
    rf = row.astype(F32)
    t = rf * (1.0 / (l_real - 1))
    band = lax.broadcasted_iota(jnp.int32, (1, HY_BANDS), 1).astype(F32)
    bands = 1e-4 + band * ((HY_BANDS - 1 - 1e-4) / (HY_BANDS - 1))
    ang = ((2.0 * math.pi / l_real) * rf) * bands

    def hdot(a, b):
        return jnp.dot(a, b, precision=HIGHEST, preferred_element_type=F32)

    pre = t * w1t_ref[...] + hdot(jnp.cos(ang), w1c_ref[...]) - hdot(jnp.sin(ang), w1s_ref[...])
    hid = jnp.sin(f1_ref[...] * (pre + b1_ref[...]))
    hid = jnp.sin(f2_ref[...] * (hdot(hid, w2_ref[...]) + b2_ref[...]))
    filt = hdot(hid, w3_ref[...])
    window = jnp.exp(-t * jnp.abs(dec_ref[...])) + HY_SHIFT
    filt = jnp.where(row < l_real, filt * window, 0.0)

    @pl.when(i == 0)
    def _():
        asum_ref[...] = jnp.zeros(asum_ref.shape, F32)

    asum_ref[...] += jnp.sum(jnp.abs(filt), axis=0, keepdims=True)
    lane = lax.broadcasted_iota(jnp.int32, (1, 2 * w), 1)
    filt_ref[...] = jnp.where((row == 0) & (lane >= w), 0.0, filt)


def hyena_filter(w1, b1, f1, w2, b2, f2, w3, decay, lp, l_real, tm):
    w = GROUP_W
    nf = w1.shape[1]
    small = lambda a: pl.BlockSpec(a.shape, lambda i: (0,) * a.ndim)
    args = (w1[0:1], w1[1:1 + HY_BANDS], w1[1 + HY_BANDS:], b1.reshape(1, nf), f1.reshape(1, nf),
            w2, b2.reshape(1, nf), f2.reshape(1, nf), w3, decay.reshape(1, 2 * w))
    return pl.pallas_call(
        functools.partial(_hyena_filter_kernel, l_real=l_real),
        out_shape=(jax.ShapeDtypeStruct((lp, 2 * w), F32), jax.ShapeDtypeStruct((1, 2 * w), F32)),
        grid=(lp // tm,),
        in_specs=[small(a) for a in args],
        out_specs=(pl.BlockSpec((tm, 2 * w), lambda i: (i, 0)), pl.BlockSpec((1, 2 * w), lambda i: (0, 0))),
        compiler_params=_cparams("arbitrary"),
        name="hyena_filter",
    )(*args)


def _fft_outer_kernel(f_ref, x_ref, o_ref, pad_ref):
    k = x_ref.shape[1]
    pad_ref[...] = jnp.zeros(pad_ref.shape, F32)
    for s in range(x_ref.shape[2]):
        pad_ref[0:k, :] = x_ref[0, :, s, :]
        o_ref[0, :, s, :] = jnp.dot(f_ref[...], pad_ref[...], precision=HIGHEST, preferred_element_type=F32)


def fft_outer(x, fmat):
    b, k, n2, c = x.shape
    rows, kp = fmat.shape
    ts = FFT_TS
    return pl.pallas_call(
        _fft_outer_kernel,
        out_shape=jax.ShapeDtypeStruct((b, rows, n2, c), F32),
        grid=(b, n2 // ts),
        in_specs=[pl.BlockSpec((rows, kp), lambda bi, j: (0, 0)),
                  pl.BlockSpec((1, k, ts, c), lambda bi, j: (bi, 0, j, 0))],
        out_specs=pl.BlockSpec((1, rows, ts, c), lambda bi, j: (bi, 0, j, 0)),
        scratch_shapes=[pltpu.VMEM((kp, c), F32)],
        compiler_params=_cparams("arbitrary", "arbitrary"),
        name="fft_outer",
    )(fmat, x)


def _filter_spectrum_kernel(gg_ref, a_ref, asum_ref, o_ref):
    w = GROUP_W
    n2 = FFT_N2
    a = jnp.concatenate([a_ref[0, 0], a_ref[1, 0]], axis=0).astype(BF16)
    x = jnp.dot(gg_ref[0], a, preferred_element_type=F32)
    s = 1.0 / (asum_ref[:, 0:w] + asum_ref[:, w:2 * w] + 1e-6)
    o_ref[0, 0] = (x[0:n2, 0:w] + x[0:n2, w:2 * w]) * s
    o_ref[0, 1] = (x[n2:2 * n2, 0:w] - x[n2:2 * n2, w:2 * w]) * s


def filter_spectrum(a, gg, asum):
    _, n1, n2, w2 = a.shape
    w = w2 // 2
    return pl.pallas_call(
        _filter_spectrum_kernel,
        out_shape=jax.ShapeDtypeStruct((n1, 2, n2, w), F32),
        grid=(n1,),
        in_specs=[pl.BlockSpec((1, 2 * n2, 2 * n2), lambda k: (k, 0, 0)),
                  pl.BlockSpec((2, 1, n2, w2), lambda k: (0, k, 0, 0)),
                  pl.BlockSpec((1, w2), lambda k: (0, 0))],
        out_specs=pl.BlockSpec((1, 2, n2, w), lambda k: (k, 0, 0, 0)),
        compiler_params=_cparams("arbitrary"),
        name="hyena_filter_spectrum",
    )(gg, a, asum)


def _fft_mid_kernel(gg_ref, ggt_ref, kf_ref, a_ref, o_ref):
    n2 = FFT_N2
    a = jnp.concatenate([a_ref[0, 0, 0], a_ref[0, 1, 0]], axis=0).astype(BF16)
    x = jnp.dot(gg_ref[0], a, preferred_element_type=F32)
    xr, xi = x[0:n2], x[n2:2 * n2]
    kr, ki = kf_ref[0, 0], kf_ref[0, 1]
    y = jnp.concatenate([xr * kr - xi * ki, xr * ki + xi * kr], axis=0).astype(BF16)
    bm = jnp.dot(ggt_ref[0], y, preferred_element_type=F32)
    o_ref[0, 0, 0] = bm[0:n2]
    o_ref[0, 1, 0] = bm[n2:2 * n2]


def fft_mid(a, gg, ggt, kf):
    b, _, n1, n2, w = a.shape
    aspec = pl.BlockSpec((1, 2, 1, n2, w), lambda k, bi: (bi, 0, k, 0, 0))
    gspec = pl.BlockSpec((1, 2 * n2, 2 * n2), lambda k, bi: (k, 0, 0))
    return pl.pallas_call(
        _fft_mid_kernel,
        out_shape=jax.ShapeDtypeStruct(a.shape, F32),
        grid=(n1, b),
        in_specs=[gspec, gspec, pl.BlockSpec((1, 2, n2, w), lambda k, bi: (k, 0, 0, 0)), aspec],
        out_specs=aspec,
        compiler_params=_cparams("arbitrary", "arbitrary"),
        name="fft_inner_conv",
    )(gg, ggt, kf, a)


def _fft_final_kernel(f_ref, b_ref, x0_ref, z_ref, d_ref, o_ref):
    k = o_ref.shape[1]
    for s in range(o_ref.shape[2]):
        y = jnp.dot(f_ref[...], b_ref[0, :, s, :], precision=HIGHEST, preferred_element_type=F32)
        o_ref[0, :, s, :] = x0_ref[0, :, s, :] * (y[0:k] + z_ref[0, :, s, :] * d_ref[...])


def fft_final(bm, finv, x0, z, d):
    b, rows, n2, c = bm.shape
    k = x0.shape[1]
    kp = finv.shape[0]
    ts = FFT_TS
    xspec = pl.BlockSpec((1, k, ts, c), lambda bi, j: (bi, 0, j, 0))
    return pl.pallas_call(
        _fft_final_kernel,
        out_shape=jax.ShapeDtypeStruct((b, k, n2, c), F32),
        grid=(b, n2 // ts),
        in_specs=[pl.BlockSpec((kp, rows), lambda bi, j: (0, 0)),
                  pl.BlockSpec((1, rows, ts, c), lambda bi, j: (bi, 0, j, 0)),
                  xspec, xspec, pl.BlockSpec((1, c), lambda bi, j: (0, 0))],
        out_specs=xspec,
        compiler_params=_cparams("arbitrary", "arbitrary"),
        name="fft_outer_inverse_gate",
    )(finv, bm, x0, z, d)


def hyena_mixer(pb, hp, nb, lp, padf, l_real, tm):
    w = GROUP_W
    n2 = FFT_N2
    k1 = lp // n2
    n1 = _fft_n1(lp, l_real)
    n_total = n1 * n2
    ffwd = _outer_dft(n1, k1, False, n_total)
    finv = _outer_dft(n1, k1, True, n_total)
    gg, ggt = hp['gg'], hp['ggt']
    filt, asum = hyena_filter(hp['w1'], hp['b1'], hp['f1'], hp['w2'], hp['b2'], hp['f2'], hp['w3'],
                              hp['decay'], lp, l_real, tm)
    fa = fft_outer(filt.reshape(1, k1, n2, 2 * w), ffwd)
    kf = filter_spectrum(fa.reshape(2, n1, n2, 2 * w), gg, asum)
    x0, z = hyena_prep(pb, hp['conv_w'], hp['conv_b'], lp, padf, tm)
    z4 = z.reshape(nb, k1, n2, w)
    a = fft_outer(z4, ffwd)
    bm = fft_mid(a.reshape(nb, 2, n1, n2, w), gg, ggt, kf)
    out = fft_final(bm.reshape(nb, 2 * n1, n2, w), finv, x0.reshape(nb, k1, n2, w), z4, hp['d'].reshape(1, w))
    return out.reshape(nb * lp, w)


TM = 640
ATTN_TQ = 1280
ATTN_TKB = 256
ATTN_UNROLL = 8
SCAN_TB = 256
FFN_TN = 512
FFN_DOWN_TM = 320


def kernel(x_prompt, x_sample, meta, emb_ln_g, emb_ln_b, w_in, lam_q1, lam_k1, lam_q2, lam_k2, attn_norm_g, hy_conv_w, hy_conv_b, hy_w1, hy_b1, hy_f1, hy_w2, hy_b2, hy_f2, hy_w3, hy_decay, hy_d, hg_lb, hg_norm_g, gdn_conv_w, gdn_a_log, gdn_dt_bias, gdn_norm_g, w_out, ln1_g, ln1_b, w_up, ffn_conv_w, ffn_conv_b, w_down, ln2_g, ln2_b):
    n_prompt = x_prompt.shape[0]
    x = jnp.concatenate([x_prompt, x_sample], axis=0)
    nb, seq, d = x.shape
    padf = ROW_ALIGN - N_META
    lp = seq + ROW_ALIGN
    l_real = seq + N_META
    m = nb * lp
    w = GROUP_W

    h, hb = (a.reshape(m, d) for a in embed(x, meta, emb_ln_g, emb_ln_b, padf))
    tables = rope_tables(lp, padf)
    consts = _scan_constants()
    sel = _gate_selectors()
    gg, ggt = inner_dft(_fft_n1(lp, l_real))
    sm = jax.nn.softmax(hg_lb, axis=0)
    lb_all = jnp.cumsum(sm, axis=0) - sm[0]

    for l in range(DEPTH):
        wl = w_in[l].astype(BF16)
        wd = jnp.pad(wl[:, 11 * w:], ((0, 0), (0, 4 * w + GATE_LANES - (wl.shape[1] - 11 * w))))
        pa = matmul(hb, wl[:, 0:3 * w], TM, 3 * w)
        pb = matmul(hb, wl[:, 3 * w:6 * w], TM, 3 * w)
        pc = matmul(hb, wl[:, 6 * w:11 * w], TM, 5 * w // 2)
        pd = matmul(hb, wd, TM, wd.shape[1])

        qlo, qhi, ka, va = attn_prep(pa, tables, nb, lp, TM)
        lam_vecs = jnp.stack([lam_q1[l], lam_k1[l], lam_q2[l], lam_k2[l]])
        oa = diff_attention(qlo, qhi, ka, va, lam_vecs, attn_norm_g[l], nb, lp, padf, l,
                            ATTN_TQ, ATTN_TKB, ATTN_UNROLL)

        hp = dict(gg=gg, ggt=ggt, w1=hy_w1[l], b1=hy_b1[l], f1=hy_f1[l], w2=hy_w2[l], b2=hy_b2[l],
                  f2=hy_f2[l], w3=hy_w3[l], decay=hy_decay[l], d=hy_d[l], conv_w=hy_conv_w[l],
                  conv_b=hy_conv_b[l])
        ob = hyena_mixer(pb, hp, nb, lp, padf, l_real, TM)

        qc, vc, gc, sgc = hgrn_prep(pc, 0, lb_all[l], TM)
        oc = gated_norm(hgrn_scan(qc, vc, gc, consts, nb, lp, SCAN_TB), sgc, hg_norm_g[l], TM)

        qd, kd, vd, szd, gates = gdn_prep(pd, gdn_conv_w[l], gdn_a_log[l], gdn_dt_bias[l], lp, padf, TM)
        od = gated_norm(gdn_scan(qd, kd, vd, gates, consts, sel, nb, lp, SCAN_TB), szd, gdn_norm_g[l], TM)

        h, hb = proj_residual_ln([oa, ob, oc, od], w_out[l].astype(BF16), h, ln1_g[l], ln1_b[l], lp, padf, TM)
        act = ffn_up(hb, w_up[l].astype(BF16), ffn_conv_w[l], ffn_conv_b[l], TM, FFN_TN)
        h, hb = ffn_down_ln(act, w_down[l].astype(BF16), h, ln2_g[l], ln2_b[l], lp, padf, FFN_DOWN_TM)

    y = h.reshape(nb, lp, d)[:, ROW_ALIGN:]
    return (y[:n_prompt], y[n_prompt:])
```

```python
import functools
import math

import jax
import jax.numpy as jnp
import numpy as np
from jax import lax
from jax.experimental import pallas as pl
from jax.experimental.pallas import tpu as pltpu

D_MODEL = 2048
DEPTH = 4
N_META = 16
GROUP_W = 512
N_HEADS = 4
HEAD_W = 128
DA_HD = 64
ROT_DIM = 16
ROPE_THETA = 500000.0
HY_BANDS = 16
HY_SHIFT = 0.05
CHUNK = 64
D_FF = 5632
ALPHA = (2.0 * DEPTH) ** 0.25
LN_EPS = 1e-5
RMS_EPS = 1e-6
F32 = jnp.float32
BF16 = jnp.bfloat16
HIGHEST = lax.Precision.HIGHEST

ROW_ALIGN = 256
VMEM_LIMIT = 56 * 1024 * 1024


def _cparams(*sem):
    return pltpu.CompilerParams(dimension_semantics=sem, vmem_limit_bytes=VMEM_LIMIT)


def _valid_rows(tile_idx, tiles_per_seq, tm, padf):
    base = (tile_idx % tiles_per_seq) * tm
    return (base + lax.broadcasted_iota(jnp.int32, (tm, 1), 0)) >= padf


def _ln_rows(y, g, b):
    mu = jnp.mean(y, axis=-1, keepdims=True)
    d = y - mu
    var = jnp.mean(d * d, axis=-1, keepdims=True)
    return d * lax.rsqrt(var + LN_EPS) * g + b


def _embed_kernel(x_ref, meta_ref, g_ref, b_ref, o_ref, ob_ref, *, padf):
    t = pl.program_id(1)
    g = g_ref[...]
    b = b_ref[...]

    @pl.when(t == 0)
    def _():
        y = jnp.concatenate([jnp.zeros((padf, o_ref.shape[2]), F32), _ln_rows(meta_ref[...], g, b)], axis=0)
        o_ref[0] = y
        ob_ref[0] = y.astype(BF16)

    @pl.when(t > 0)
    def _():
        y = _ln_rows(x_ref[0], g, b)
        o_ref[0] = y
        ob_ref[0] = y.astype(BF16)


def embed(x, meta, g, b, padf):
    nb, s, d = x.shape
    tm = ROW_ALIGN
    assert padf + N_META == tm and s % tm == 0
    lp = s + tm
    ospec = pl.BlockSpec((1, tm, d), lambda bi, t: (bi, t, 0))
    return pl.pallas_call(
        functools.partial(_embed_kernel, padf=padf),
        out_shape=(jax.ShapeDtypeStruct((nb, lp, d), F32), jax.ShapeDtypeStruct((nb, lp, d), BF16)),
        grid=(nb, lp // tm),
        in_specs=[pl.BlockSpec((1, tm, d), lambda bi, t: (bi, jnp.maximum(t - 1, 0), 0)),
                  pl.BlockSpec((N_META, d), lambda bi, t: (0, 0)),
                  pl.BlockSpec((1, d), lambda bi, t: (0, 0)),
                  pl.BlockSpec((1, d), lambda bi, t: (0, 0))],
        out_specs=(ospec, ospec),
        compiler_params=_cparams("arbitrary", "arbitrary"),
        name="embed_ln",
    )(x, meta, g.reshape(1, d), b.reshape(1, d))


def _mm_kernel(x_ref, w_ref, o_ref):
    o_ref[...] = jnp.dot(x_ref[...], w_ref[...], preferred_element_type=F32).astype(o_ref.dtype)


def matmul(x, w, tm, tn, out_dtype=F32):
    m, k = x.shape
    n = w.shape[1]
    assert m % tm == 0 and n % tn == 0
    return pl.pallas_call(
        _mm_kernel,
        out_shape=jax.ShapeDtypeStruct((m, n), out_dtype),
        grid=(n // tn, m // tm),
        in_specs=[pl.BlockSpec((tm, k), lambda j, i: (i, 0)),
                  pl.BlockSpec((k, tn), lambda j, i: (0, j))],
        out_specs=pl.BlockSpec((tm, tn), lambda j, i: (i, j)),
        compiler_params=_cparams("arbitrary", "arbitrary"),
        name="dense_matmul",
    )(x, w)


def _proj_ln_kernel(*refs, n_in, tiles_per_seq, padf):
    o_refs = refs[:n_in]
    w_ref, h_ref, g_ref, b_ref, out_ref, outb_ref = refs[n_in:]
    tm = h_ref.shape[0]
    acc = None
    off = 0
    for r in o_refs:
        kw = r.shape[1]
        part = jnp.dot(r[...].astype(BF16), w_ref[off:off + kw, :], preferred_element_type=F32)
        acc = part if acc is None else acc + part
        off += kw
    y = _ln_rows(ALPHA * h_ref[...] + acc, g_ref[...], b_ref[...])
    valid = _valid_rows(pl.program_id(0), tiles_per_seq, tm, padf)
    y = jnp.where(valid, y, 0.0)
    out_ref[...] = y
    outb_ref[...] = y.astype(BF16)


def proj_residual_ln(parts, w, h, g, b, lp, padf, tm):
    m, d = h.shape
    assert m % tm == 0 and lp % tm == 0
    n_in = len(parts)
    in_specs = [pl.BlockSpec((tm, p.shape[1]), lambda i: (i, 0)) for p in parts]
    in_specs += [pl.BlockSpec(w.shape, lambda i: (0, 0)),
                 pl.BlockSpec((tm, d), lambda i: (i, 0)),
                 pl.BlockSpec((1, d), lambda i: (0, 0)),
                 pl.BlockSpec((1, d), lambda i: (0, 0))]
    ospec = pl.BlockSpec((tm, d), lambda i: (i, 0))
    return pl.pallas_call(
        functools.partial(_proj_ln_kernel, n_in=n_in, tiles_per_seq=lp // tm, padf=padf),
        out_shape=(jax.ShapeDtypeStruct((m, d), F32), jax.ShapeDtypeStruct((m, d), BF16)),
        grid=(m // tm,),
        in_specs=in_specs,
        out_specs=(ospec, ospec),
        compiler_params=_cparams("arbitrary"),
        name="proj_residual_ln",
    )(*parts, w, h, g.reshape(1, d), b.reshape(1, d))


HALO = 8
HALO_BF16 = 16


def _halo_specs(tm, width, m, col_block=0, halo=HALO):
    nblk = m // halo
    per = tm // halo

    def prev_map(*idx):
        i = idx[-1]
        return (jnp.maximum(i * per - 1, 0), col_block)

    def next_map(*idx):
        i = idx[-1]
        return (jnp.minimum((i + 1) * per, nblk - 1), col_block)

    return (pl.BlockSpec((halo, width), prev_map), pl.BlockSpec((halo, width), next_map))


def _conv3_rows(pm, prev_row, next_row, cw):
    tm = pm.shape[0]
    rid = lax.broadcasted_iota(jnp.int32, (tm, 1), 0)
    down = jnp.where(rid == 0, prev_row, pltpu.roll(pm, 1, 0))
    up = jnp.where(rid == tm - 1, next_row, pltpu.roll(pm, tm - 1, 0))
    return down * cw[0:1, :] + pm * cw[1:2, :] + up * cw[2:3, :]


def _ffn_up_kernel(x_ref, xp_ref, xn_ref, wg_ref, wu_ref, cwg_ref, cwu_ref, cbg_ref, cbu_ref, o_ref):
    i = pl.program_id(1)
    last = pl.num_programs(1) - 1
    x = x_ref[...]
    hr = xp_ref.shape[0]
    halo = jnp.concatenate([xp_ref[...], xn_ref[...]], axis=0)
    nmask = jnp.where(i == last, 0.0, 1.0)

    def branch(w_ref, cw_ref, cb_ref, sl):
        w = w_ref[:, sl]
        pm = jnp.dot(x, w, preferred_element_type=F32)
        ph = jnp.dot(halo, w, preferred_element_type=F32)
        return _conv3_rows(pm, ph[hr - 1:hr, :], ph[hr:hr + 1, :] * nmask, cw_ref[:, sl]) + cb_ref[:, sl]

    tn = o_ref.shape[1]
    for c0 in range(0, tn, FFN_SUB):
        sl = slice(c0, min(c0 + FFN_SUB, tn))
        g = branch(wg_ref, cwg_ref, cbg_ref, sl)
        u = branch(wu_ref, cwu_ref, cbu_ref, sl)
        o_ref[:, sl] = (g * jax.nn.sigmoid(g) * u).astype(o_ref.dtype)


FFN_SUB = 256


def ffn_up(h, w_up, conv_w, conv_b, tm, tn):
    m, k = h.shape
    f = w_up.shape[1] // 2
    assert m % tm == 0 and f % tn == 0 and h.dtype == BF16
    nj = f // tn
    xp_spec, xn_spec = _halo_specs(tm, k, m, halo=HALO_BF16)
    cb = conv_b.reshape(1, 2 * f)
    return pl.pallas_call(
        _ffn_up_kernel,
        out_shape=jax.ShapeDtypeStruct((m, f), BF16),
        grid=(nj, m // tm),
        in_specs=[pl.BlockSpec((tm, k), lambda j, i: (i, 0)), xp_spec, xn_spec,
                  pl.BlockSpec((k, tn), lambda j, i: (0, j)),
                  pl.BlockSpec((k, tn), lambda j, i: (0, j + nj)),
                  pl.BlockSpec((3, tn), lambda j, i: (0, j)),
                  pl.BlockSpec((3, tn), lambda j, i: (0, j + nj)),
                  pl.BlockSpec((1, tn), lambda j, i: (0, j)),
                  pl.BlockSpec((1, tn), lambda j, i: (0, j + nj))],
        out_specs=pl.BlockSpec((tm, tn), lambda j, i: (i, j)),
        compiler_params=_cparams("arbitrary", "arbitrary"),
        name="ffn_up_conv_gate",
    )(h, h, h, w_up, w_up, conv_w, conv_w, cb, cb)


def _ffn_down_kernel(a_ref, w_ref, h_ref, g_ref, b_ref, out_ref, outb_ref, *, tiles_per_seq, padf):
    tm = h_ref.shape[0]
    acc = jnp.dot(a_ref[...], w_ref[...], preferred_element_type=F32)
    y = _ln_rows(ALPHA * h_ref[...] + acc, g_ref[...], b_ref[...])
    valid = _valid_rows(pl.program_id(0), tiles_per_seq, tm, padf)
    y = jnp.where(valid, y, 0.0)
    out_ref[...] = y
    outb_ref[...] = y.astype(BF16)


def ffn_down_ln(a, w_down, h, g, b, lp, padf, tm):
    m, d = h.shape
    f = a.shape[1]
    assert m % tm == 0 and lp % tm == 0
    ospec = pl.BlockSpec((tm, d), lambda i: (i, 0))
    return pl.pallas_call(
        functools.partial(_ffn_down_kernel, tiles_per_seq=lp // tm, padf=padf),
        out_shape=(jax.ShapeDtypeStruct((m, d), F32), jax.ShapeDtypeStruct((m, d), BF16)),
        grid=(m // tm,),
        in_specs=[pl.BlockSpec((tm, f), lambda i: (i, 0)),
                  pl.BlockSpec((f, d), lambda i: (0, 0), pipeline_mode=pl.Buffered(1)),
                  pl.BlockSpec((tm, d), lambda i: (i, 0)),
                  pl.BlockSpec((1, d), lambda i: (0, 0)),
                  pl.BlockSpec((1, d), lambda i: (0, 0))],
        out_specs=(ospec, ospec),
        compiler_params=_cparams("arbitrary"),
        name="ffn_down_residual_ln",
    )(a, w_down, h, g.reshape(1, d), b.reshape(1, d))


def rope_tables(lp, padf):
    half = ROT_DIM // 2
    pos = (jnp.arange(lp) - padf).astype(F32)
    inv = 1.0 / (ROPE_THETA ** (jnp.arange(half, dtype=F32) / half))
    ang = pos[:, None] * inv[None]
    cos, sin = jnp.cos(ang), jnp.sin(ang)
    ones = jnp.ones((lp, DA_HD - ROT_DIM), F32)
    zeros = jnp.zeros((lp, DA_HD - ROT_DIM), F32)
    zh = jnp.zeros((lp, half), F32)
    c = jnp.concatenate([cos, cos, ones], axis=1)
    sa = jnp.concatenate([-sin, zh, zeros], axis=1)
    sb = jnp.concatenate([zh, sin, zeros], axis=1)
    return tuple(jnp.concatenate([t, t], axis=1) for t in (c, sa, sb))


def _attn_prep_kernel(p_ref, c_ref, sa_ref, sb_ref, qlo_ref, qhi_ref, k_ref, vt_ref):
    w = GROUP_W
    reps = w // c_ref.shape[1]
    c = jnp.concatenate([c_ref[...]] * reps, axis=1)
    sa = jnp.concatenate([sa_ref[...]] * reps, axis=1)
    sb = jnp.concatenate([sb_ref[...]] * reps, axis=1)
    half = ROT_DIM // 2

    def rot(x):
        return x * c + pltpu.roll(x, w - half, 1) * sa + pltpu.roll(x, half, 1) * sb

    q = rot(p_ref[:, 0:w]) * (DA_HD ** -0.5)
    lane = lax.broadcasted_iota(jnp.int32, (1, w), 1) % (2 * DA_HD)
    qlo_ref[0] = jnp.where(lane < DA_HD, q, 0.0).T.astype(BF16)
    qhi_ref[0] = jnp.where(lane >= DA_HD, q, 0.0).T.astype(BF16)
    k_ref[...] = rot(p_ref[:, w:2 * w]).astype(BF16)
    vt_ref[0] = p_ref[:, 2 * w:3 * w].T.astype(BF16)


def attn_prep(proj, tables, nb, lp, tm):
    m = proj.shape[0]
    w = GROUP_W
    tps = lp // tm
    tspec = pl.BlockSpec((tm, 2 * DA_HD), lambda i: (i % tps, 0))
    tr_shape = jax.ShapeDtypeStruct((nb, w, lp), BF16)
    tr_spec = pl.BlockSpec((1, w, tm), lambda i: (i // tps, 0, i % tps))
    return pl.pallas_call(
        _attn_prep_kernel,
        out_shape=(tr_shape, tr_shape, jax.ShapeDtypeStruct((m, w), BF16), tr_shape),
        grid=(m // tm,),
        in_specs=[pl.BlockSpec((tm, 3 * w), lambda i: (i, 0)), tspec, tspec, tspec],
        out_specs=(tr_spec, tr_spec, pl.BlockSpec((tm, w), lambda i: (i, 0)), tr_spec),
        compiler_params=_cparams("arbitrary"),
        name="attn_prep_rotary",
    )(proj, *tables)


ATTN_QC = 256
ATTN_ONES = 8


def _attn_kernel(lam_ref, g_ref, qlo_ref, qhi_ref, k_ref, vt_ref, o_ref, qs_ref, acc_ref,
                 *, tkb, padf, lam_init, unroll):
    tq = qlo_ref.shape[2]
    nq2 = 2 * tq
    nkb = k_ref.shape[0] // tkb
    qs_ref[:, :tq] = qlo_ref[0]
    qs_ref[:, tq:] = qhi_ref[0]
    row_ok = lax.broadcasted_iota(jnp.int32, (tkb, 1), 0) >= padf
    ones = jnp.ones((ATTN_ONES, tkb), BF16)

    def scores(kb, c0, masked):
        start = kb * tkb
        if not isinstance(start, int):
            start = pl.multiple_of(start, tkb)
        s = jnp.dot(k_ref[pl.ds(start, tkb), :], qs_ref[:, c0:c0 + ATTN_QC], preferred_element_type=F32)
        if masked:
            s = jnp.where(row_ok, s, -1e30)
        return s, start

    def colmax(s):
        return jnp.max(jnp.max(s.reshape(tkb // 8, 8, ATTN_QC), axis=0), axis=0, keepdims=True)

    for c0 in range(0, nq2, 2 * ATTN_QC):
        cols = (c0, c0 + ATTN_QC)

        def consume(kb, s, m):
            start = kb * tkb
            if not isinstance(start, int):
                start = pl.multiple_of(start, tkb)
            vext = jnp.concatenate([vt_ref[0, :, pl.ds(start, tkb)], ones], axis=0)
            m_out = []
            for i in range(2):
                m_new = jnp.maximum(m[i], colmax(s[i]))
                p = jnp.exp(s[i] - m_new).astype(BF16)
                sl = slice(cols[i], cols[i] + ATTN_QC)
                acc_ref[:, sl] = acc_ref[:, sl] * jnp.exp(m[i] - m_new) + jnp.dot(vext, p, preferred_element_type=F32)
                m_out.append(m_new)
            return tuple(m_out)

        for c in cols:
            acc_ref[:, c:c + ATTN_QC] = jnp.zeros((HEAD_W + ATTN_ONES, ATTN_QC), F32)

        def body(kb, carry):
            s, m = carry
            s_next = tuple(scores(kb + 1, c, False)[0] for c in cols)
            return s_next, consume(kb, s, m)

        m0 = tuple(jnp.full((1, ATTN_QC), -1e30, F32) for _ in cols)
        s_last, m_last = lax.fori_loop(0, nkb - 1, body, (tuple(scores(0, c, True)[0] for c in cols), m0),
                                       unroll=unroll)
        consume(nkb - 1, s_last, m_last)

    lam = (jnp.exp(jnp.sum(lam_ref[0:1, :] * lam_ref[1:2, :], axis=-1, keepdims=True))
           - jnp.exp(jnp.sum(lam_ref[2:3, :] * lam_ref[3:4, :], axis=-1, keepdims=True)) + lam_init)
    o = acc_ref[0:HEAD_W, :] / acc_ref[HEAD_W:HEAD_W + 1, :]
    o = (o[:, :tq] - lam * o[:, tq:]).T
    o = o * lax.rsqrt(jnp.mean(o * o, axis=-1, keepdims=True) + RMS_EPS) * g_ref[...]
    o_ref[...] = (o * (1.0 - lam_init)).astype(o_ref.dtype)


def diff_attention(qlo_t, qhi_t, k, v_t, lam_vecs, norm_g, nb, lp, padf, layer, tq, tkb, unroll):
    m = k.shape[0]
    assert lp % tq == 0 and lp % tkb == 0 and padf < tkb and tq % ATTN_QC == 0
    nq = lp // tq
    lam_init = 0.8 - 0.6 * math.exp(-0.3 * layer)
    qspec = pl.BlockSpec((1, HEAD_W, tq), lambda b, h, qi: (b, h, qi))
    return pl.pallas_call(
        functools.partial(_attn_kernel, tkb=tkb, padf=padf, lam_init=lam_init, unroll=unroll),
        out_shape=jax.ShapeDtypeStruct((m, GROUP_W), BF16),
        grid=(nb, N_HEADS, nq),
        in_specs=[pl.BlockSpec((4, DA_HD), lambda b, h, qi: (0, 0)),
                  pl.BlockSpec((1, HEAD_W), lambda b, h, qi: (0, 0)),
                  qspec, qspec,
                  pl.BlockSpec((lp, HEAD_W), lambda b, h, qi: (b, h)),
                  pl.BlockSpec((1, HEAD_W, lp), lambda b, h, qi: (b, h, 0))],
        out_specs=pl.BlockSpec((tq, HEAD_W), lambda b, h, qi: (b * nq + qi, h)),
        scratch_shapes=[pltpu.VMEM((HEAD_W, 2 * tq), BF16),
                        pltpu.VMEM((HEAD_W + ATTN_ONES, 2 * tq), F32)],
        compiler_params=_cparams("arbitrary", "arbitrary", "arbitrary"),
        name="diff_attention",
    )(lam_vecs, norm_g.reshape(1, HEAD_W), qlo_t, qhi_t, k, v_t)


N_LEVELS = 6


def _scan_constants():
    c = CHUNK
    idx = np.arange(c)
    cs = np.zeros((2, (N_LEVELS + 2) * c, c), np.float32)
    mask = np.zeros((2, N_LEVELS + 2, c, c), np.float32)
    for d in range(2):
        cum = (idx[None, :] <= idx[:, None]) if d == 0 else (idx[None, :] >= idx[:, None])
        cs[d, 0:c] = cum
        for l in range(N_LEVELS):
            s = (c // 2) >> l
            blk = idx // s
            ref = (blk | 1) * s - 1 if d == 0 else (blk | 1) * s
            cs[d, (l + 1) * c:(l + 2) * c] = cum[ref]
            if d == 0:
                mask[d, l] = ((blk[:, None] & 1) == 1) & (blk[None, :] == blk[:, None] - 1)
            else:
                mask[d, l] = ((blk[:, None] & 1) == 0) & (blk[None, :] == blk[:, None] + 1)
        cs[d, (N_LEVELS + 1) * c:] = 1.0
        mask[d, N_LEVELS] = cum
        mask[d, N_LEVELS + 1] = cum & (idx[None, :] != idx[:, None])
    return jnp.asarray(np.concatenate([cs] * 3, axis=2), BF16), jnp.asarray(mask)


def _split3(x):
    hi = x.astype(BF16)
    r = x - hi.astype(F32)
    mid = r.astype(BF16)
    return hi, mid, (r - mid.astype(F32)).astype(BF16)


def _split3_rows(x):
    return jnp.concatenate(_split3(x), axis=0)


def _split3_lanes(x):
    return jnp.concatenate(_split3(x), axis=1)


def _nt(a, b):
    return lax.dot_general(a.astype(BF16), b.astype(BF16), (((1,), (1,)), ((), ())),
                           preferred_element_type=F32)


def _tn(a, b):
    return lax.dot_general(a.astype(BF16), b.astype(BF16), (((0,), (0,)), ((), ())),
                           preferred_element_type=F32)


def _nn(a, b):
    return jnp.dot(a.astype(BF16), b.astype(BF16), preferred_element_type=F32)


def _hgrn_prep_kernel(p_ref, lb_ref, q_ref, v_ref, g_ref, sg_ref):
    w = GROUP_W
    q = p_ref[:, 0:w]
    q_ref[...] = (q * jax.nn.sigmoid(q) * (HEAD_W ** -0.5)).astype(BF16)
    for d in range(2):
        lb = lb_ref[d:d + 1, :]
        f = p_ref[:, (1 + d) * w:(2 + d) * w]
        g_ref[d] = jnp.log(lb + (1.0 - lb) * jax.nn.sigmoid(f))
    v_ref[...] = p_ref[:, 3 * w:4 * w].astype(BF16)
    gate = p_ref[:, 4 * w:5 * w]
    sg_ref[...] = (gate * jax.nn.sigmoid(gate)).astype(BF16)


def hgrn_prep(proj, col_block, lb, tm):
    m = proj.shape[0]
    w = GROUP_W
    ospec = pl.BlockSpec((tm, w), lambda i: (i, 0))
    return pl.pallas_call(
        _hgrn_prep_kernel,
        out_shape=(jax.ShapeDtypeStruct((m, w), BF16), jax.ShapeDtypeStruct((m, w), BF16),
                   jax.ShapeDtypeStruct((2, m, w), F32), jax.ShapeDtypeStruct((m, w), BF16)),
        grid=(m // tm,),
        in_specs=[pl.BlockSpec((tm, 5 * w), lambda i: (i, col_block)),
                  pl.BlockSpec((2, w), lambda i: (0, 0))],
        out_specs=(ospec, ospec, pl.BlockSpec((2, tm, w), lambda i: (0, i, 0)), ospec),
        compiler_params=_cparams("arbitrary"),
        name="hgrn_prep",
    )(proj, lb)


def _head(x, h):
    return x[:, h * HEAD_W:(h + 1) * HEAD_W]


def _chunk_rows(ci, nch, d):
    return pl.multiple_of((ci + d * (nch - 1 - 2 * ci)) * CHUNK, CHUNK)


def _hgrn_scan_kernel(cs_ref, mask_ref, q_ref, v_ref, g_ref, o_ref, s_ref, *, nch):
    c = CHUNK
    d = pl.program_id(1)
    heads = range(N_HEADS)
    chunks = range(nch)

    @pl.when(pl.program_id(2) == 0)
    def _():
        s_ref[...] = jnp.zeros(s_ref.shape, F32)

    eye = mask_ref[0, N_LEVELS] - mask_ref[0, N_LEVELS + 1]
    rows = [_chunk_rows(ci, nch, d) for ci in chunks]
    g = [g_ref[0, pl.ds(r0, c), :] for r0 in rows]
    q = [q_ref[pl.ds(r0, c), :].astype(F32) for r0 in rows]
    v = [v_ref[pl.ds(r0, c), :] for r0 in rows]
    big = [jnp.dot(cs_ref[0], _split3_rows(gi), preferred_element_type=F32) for gi in g]
    b = [x[0:c] for x in big]
    tot = [x[(N_LEVELS + 1) * c:(N_LEVELS + 2) * c] for x in big]
    k = [1.0 - jnp.exp(gi) for gi in g]
    a = [[eye * jnp.sum(_head(q[ci] * k[ci], h), axis=-1, keepdims=True) for h in heads] for ci in chunks]
    for l in range(N_LEVELS):
        e = [jnp.exp(-jnp.abs(b[ci] - big[ci][(l + 1) * c:(l + 2) * c])) for ci in chunks]
        qe = [(q[ci] * e[ci]).astype(BF16) for ci in chunks]
        ke = [(k[ci] * e[ci]).astype(BF16) for ci in chunks]
        lm = mask_ref[0, l]
        a = [[a[ci][h] + lm * _nt(_head(qe[ci], h), _head(ke[ci], h)) for h in heads] for ci in chunks]
    o_intra = [[_nn(a[ci][h], _head(v[ci], h)) for h in heads] for ci in chunks]
    qb = [(q[ci] * jnp.exp(b[ci])).astype(BF16) for ci in chunks]
    kd = [(k[ci] * jnp.exp(tot[ci] - b[ci])).astype(BF16) for ci in chunks]
    dec = [jnp.exp(tot[ci][0:1, :]) for ci in chunks]
    s = [s_ref[h] for h in heads]
    for ci in chunks:
        for h in heads:
            o = o_intra[ci][h] + _nt(_head(qb[ci], h), s[h])
            o_ref[0, pl.ds(rows[ci], c), h * HEAD_W:(h + 1) * HEAD_W] = o
        s = [s[h] * _head(dec[ci], h) + _tn(_head(v[ci], h), _head(kd[ci], h)) for h in heads]
    for h in heads:
        s_ref[h] = s[h]


def _scan_row_block(nblk):
    def row(b, d, t):
        return b * nblk + t + d * (nblk - 1 - 2 * t)
    return row


def hgrn_scan(q, v, g, consts, nb, lp, tb):
    m, w = q.shape
    cs, mask = consts
    row = _scan_row_block(lp // tb)
    return pl.pallas_call(
        functools.partial(_hgrn_scan_kernel, nch=tb // CHUNK),
        out_shape=jax.ShapeDtypeStruct((2, m, w), F32),
        grid=(nb, 2, lp // tb),
        in_specs=[pl.BlockSpec((1,) + cs.shape[1:], lambda b, d, t: (d, 0, 0)),
                  pl.BlockSpec((1,) + mask.shape[1:], lambda b, d, t: (d, 0, 0, 0)),
                  pl.BlockSpec((tb, w), lambda b, d, t: (row(b, d, t), 0)),
                  pl.BlockSpec((tb, w), lambda b, d, t: (row(b, d, t), 0)),
                  pl.BlockSpec((1, tb, w), lambda b, d, t: (d, row(b, d, t), 0))],
        out_specs=pl.BlockSpec((1, tb, w), lambda b, d, t: (d, row(b, d, t), 0)),
        scratch_shapes=[pltpu.VMEM((N_HEADS, HEAD_W, HEAD_W), F32)],
        compiler_params=_cparams("arbitrary", "arbitrary", "arbitrary"),
        name="hgrn_scan",
    )(cs, mask, q, v, g)


def _gated_norm_kernel(o_ref, sg_ref, g_ref, out_ref):
    o = o_ref[0] + o_ref[1]
    gain = g_ref[...]
    for h in range(N_HEADS):
        sl = slice(h * HEAD_W, (h + 1) * HEAD_W)
        oh = o[:, sl]
        y = oh * lax.rsqrt(jnp.mean(oh * oh, axis=-1, keepdims=True) + RMS_EPS) * gain
        out_ref[:, sl] = (y * sg_ref[:, sl].astype(F32)).astype(out_ref.dtype)


def gated_norm(o2, sgate, norm_g, tm):
    m = sgate.shape[0]
    w = GROUP_W
    return pl.pallas_call(
        _gated_norm_kernel,
        out_shape=jax.ShapeDtypeStruct((m, w), BF16),
        grid=(m // tm,),
        in_specs=[pl.BlockSpec((2, tm, w), lambda i: (0, i, 0)),
                  pl.BlockSpec((tm, w), lambda i: (i, 0)),
                  pl.BlockSpec((1, HEAD_W), lambda i: (0, 0))],
        out_specs=pl.BlockSpec((tm, w), lambda i: (i, 0)),
        compiler_params=_cparams("arbitrary"),
        name="gated_rms_norm",
    )(o2, sgate, norm_g.reshape(1, HEAD_W))


GATE_LANES = 128


def _softplus(x):
    return jnp.maximum(x, 0.0) + jnp.log(1.0 + jnp.exp(-jnp.abs(x)))


def _gdn_prep_kernel(p_ref, pp_ref, pn_ref, cw_ref, acoef_ref, dtb_ref,
                     q_ref, k_ref, v_ref, sz_ref, gates_ref, *, tiles_per_seq, padf):
    w = GROUP_W
    i = pl.program_id(0)
    tm = p_ref.shape[0]
    nmask = jnp.where(i == pl.num_programs(0) - 1, 0.0, 1.0)
    x = _conv3_rows(p_ref[:, 0:3 * w], pp_ref[HALO - 1:HALO, :], pn_ref[0:1, :] * nmask, cw_ref[...])
    x = x * jax.nn.sigmoid(x)
    valid = _valid_rows(i, tiles_per_seq, tm, padf)
    for h in range(N_HEADS):
        sl = slice(h * HEAD_W, (h + 1) * HEAD_W)
        qh = x[:, sl]
        q_ref[:, sl] = (qh * lax.rsqrt(jnp.sum(qh * qh, axis=-1, keepdims=True) + 1e-6)
                        * (HEAD_W ** -0.5)).astype(BF16)
        kh = x[:, w + h * HEAD_W:w + (h + 1) * HEAD_W]
        k_ref[:, sl] = (kh * lax.rsqrt(jnp.sum(kh * kh, axis=-1, keepdims=True) + 1e-6)).astype(BF16)
    v_ref[...] = jnp.where(valid, x[:, 2 * w:3 * w], 0.0).astype(BF16)
    z = p_ref[:, 3 * w:4 * w]
    sz_ref[...] = (z * jax.nn.sigmoid(z)).astype(BF16)
    gt = p_ref[:, 4 * w:4 * w + GATE_LANES]
    lane = lax.broadcasted_iota(jnp.int32, (1, GATE_LANES), 1)
    decay = -acoef_ref[...] * _softplus(gt + dtb_ref[...])
    gates_ref[...] = jnp.where(lane < 2 * N_HEADS, jax.nn.sigmoid(gt), decay)


def gdn_prep(pd, conv_w, a_log, dt_bias, lp, padf, tm):
    m = pd.shape[0]
    w = GROUP_W
    zeros8 = jnp.zeros((2 * N_HEADS,), F32)
    padl = jnp.zeros((GATE_LANES - 4 * N_HEADS,), F32)
    acoef = jnp.concatenate([zeros8, jnp.exp(a_log).reshape(-1), padl]).reshape(1, GATE_LANES)
    dtb = jnp.concatenate([zeros8, dt_bias.reshape(-1), padl]).reshape(1, GATE_LANES)
    pspec, nspec = _halo_specs(tm, 3 * w, m)
    ospec = pl.BlockSpec((tm, w), lambda i: (i, 0))
    return pl.pallas_call(
        functools.partial(_gdn_prep_kernel, tiles_per_seq=lp // tm, padf=padf),
        out_shape=tuple(jax.ShapeDtypeStruct((m, w), BF16) for _ in range(4))
        + (jax.ShapeDtypeStruct((m, GATE_LANES), F32),),
        grid=(m // tm,),
        in_specs=[pl.BlockSpec((tm, pd.shape[1]), lambda i: (i, 0)), pspec, nspec,
                  pl.BlockSpec((3, 3 * w), lambda i: (0, 0)),
                  pl.BlockSpec((1, GATE_LANES), lambda i: (0, 0)),
                  pl.BlockSpec((1, GATE_LANES), lambda i: (0, 0))],
        out_specs=(ospec, ospec, ospec, ospec, pl.BlockSpec((tm, GATE_LANES), lambda i: (i, 0))),
        compiler_params=_cparams("arbitrary"),
        name="gdn_prep",
    )(pd, pd, pd, conv_w, acoef, dtb)


def _gate_selectors():
    sel = np.zeros((2, GATE_LANES, 2 * GROUP_W), np.float32)
    for d in range(2):
        for h in range(N_HEADS):
            sel[d, N_HEADS * d + h, h * HEAD_W:(h + 1) * HEAD_W] = 1.0
            sel[d, 2 * N_HEADS + N_HEADS * d + h, GROUP_W + h * HEAD_W:GROUP_W + (h + 1) * HEAD_W] = 1.0
    return jnp.asarray(np.concatenate([sel] * 3, axis=1), BF16)


def _gdn_scan_kernel(cs_ref, mask_ref, sel_ref, q_ref, k_ref, v_ref, gt_ref, o_ref, s_ref, *, nch):
    c = CHUNK
    w = GROUP_W
    d = pl.program_id(1)
    heads = range(N_HEADS)
    chunks = range(nch)
    items = [(ci, h) for ci in chunks for h in heads]

    @pl.when(pl.program_id(2) == 0)
    def _():
        s_ref[...] = jnp.zeros(s_ref.shape, F32)

    incl = mask_ref[0, N_LEVELS]
    strict = mask_ref[0, N_LEVELS + 1]
    eye = incl - strict
    cum_tot = jnp.concatenate([cs_ref[0, 0:c, :], cs_ref[0, (N_LEVELS + 1) * c:(N_LEVELS + 2) * c, :]], axis=0)
    ones3 = cs_ref[0, (N_LEVELS + 1) * c:(N_LEVELS + 2) * c, :]
    rows = [_chunk_rows(ci, nch, d) for ci in chunks]
    q = [q_ref[pl.ds(r0, c), :].astype(F32) for r0 in rows]
    k = [k_ref[pl.ds(r0, c), :].astype(F32) for r0 in rows]
    v = [v_ref[pl.ds(r0, c), :].astype(F32) for r0 in rows]
    bg = [jnp.dot(_split3_lanes(gt_ref[pl.ds(r0, c), :]), sel_ref[0], preferred_element_type=F32)
          for r0 in rows]
    beta = [x[:, 0:w] for x in bg]
    cb = [jnp.dot(cum_tot, _split3_rows(x[:, w:2 * w]), preferred_element_type=F32) for x in bg]
    bb = [x[0:c] for x in cb]
    tot = [x[c:2 * c] for x in cb]
    bc = {(ci, h): bb[ci][:, h * HEAD_W:h * HEAD_W + c] for ci, h in items}
    br = {it: jnp.dot(ones3, _split3_rows(eye * bc[it]), preferred_element_type=F32) for it in items}
    dec = {it: incl * jnp.exp(jnp.minimum(bc[it] - br[it], 0.0)) for it in items}
    kb = [k[ci] * beta[ci] for ci in chunks]
    n = {(ci, h): strict * _nt(_head(kb[ci], h), _head(k[ci], h)) * dec[(ci, h)] for ci, h in items}
    t = {it: eye - n[it] for it in items}
    pw = {it: _nn(n[it], n[it]) for it in items}
    for _ in range(N_LEVELS - 2):
        t = {it: t[it] + _nn(t[it], pw[it]) for it in items}
        pw = {it: _nn(pw[it], pw[it]) for it in items}
    t = {it: t[it] + _nn(t[it], pw[it]) for it in items}
    eb = [jnp.exp(x) for x in bb]
    rhs_u = [v[ci] * beta[ci] for ci in chunks]
    rhs_w = [kb[ci] * eb[ci] for ci in chunks]
    uw = {(ci, h): _nn(t[(ci, h)], jnp.concatenate([_head(rhs_u[ci], h), _head(rhs_w[ci], h)], axis=1))
          for ci, h in items}
    qk = {(ci, h): _nt(_head(q[ci], h), _head(k[ci], h)) * dec[(ci, h)] for ci, h in items}
    qe = [(q[ci] * eb[ci]).astype(BF16) for ci in chunks]
    kd = [(k[ci] * jnp.exp(tot[ci] - bb[ci])).astype(BF16) for ci in chunks]
    sdec = [jnp.exp(tot[ci][0:1, :]) for ci in chunks]
    s = [s_ref[h] for h in heads]
    for ci in chunks:
        wq = {h: _nn(jnp.concatenate([uw[(ci, h)][:, HEAD_W:].astype(BF16), _head(qe[ci], h)], axis=0), s[h])
              for h in heads}
        vnew = {h: uw[(ci, h)][:, 0:HEAD_W] - wq[h][0:c] for h in heads}
        for h in heads:
            o_ref[0, pl.ds(rows[ci], c), h * HEAD_W:(h + 1) * HEAD_W] = wq[h][c:2 * c] + _nn(qk[(ci, h)], vnew[h])
        s = [s[h] * _head(sdec[ci], h) + _tn(_head(kd[ci], h), vnew[h]) for h in heads]
    for h in heads:
        s_ref[h] = s[h]


def gdn_scan(q, k, v, gates, consts, sel, nb, lp, tb):
    m, w = q.shape
    cs, mask = consts
    row = _scan_row_block(lp // tb)
    rspec = pl.BlockSpec((tb, w), lambda b, d, t: (row(b, d, t), 0))
    return pl.pallas_call(
        functools.partial(_gdn_scan_kernel, nch=tb // CHUNK),
        out_shape=jax.ShapeDtypeStruct((2, m, w), F32),
        grid=(nb, 2, lp // tb),
        in_specs=[pl.BlockSpec((1,) + cs.shape[1:], lambda b, d, t: (d, 0, 0)),
                  pl.BlockSpec((1,) + mask.shape[1:], lambda b, d, t: (d, 0, 0, 0)),
                  pl.BlockSpec((1,) + sel.shape[1:], lambda b, d, t: (d, 0, 0)),
                  rspec, rspec, rspec,
                  pl.BlockSpec((tb, GATE_LANES), lambda b, d, t: (row(b, d, t), 0))],
        out_specs=pl.BlockSpec((1, tb, w), lambda b, d, t: (d, row(b, d, t), 0)),
        scratch_shapes=[pltpu.VMEM((N_HEADS, HEAD_W, HEAD_W), F32)],
        compiler_params=_cparams("arbitrary", "arbitrary", "arbitrary"),
        name="gdn_scan",
    )(cs, mask, sel, q, k, v, gates)


FFT_N2 = 256
FFT_TS = 8


def _fft_n1(lp, l_real):
    need = -(-(2 * l_real - 1) // FFT_N2)
    return max(-(-need // 8) * 8, -(-(lp // FFT_N2) // 8) * 8)


def _outer_dft(n1, k_rows, inverse, n_total):
    kp = -(-k_rows // 16) * 16
    a = np.arange(n1)[:, None] * np.arange(kp)[None, :]
    ang = 2.0 * np.pi * (a % n1) / n1
    c, s = np.cos(ang), np.sin(ang)
    live = (np.arange(kp) < k_rows)[None, :]
    c, s = c * live, s * live
    if not inverse:
        return jnp.asarray(np.concatenate([c, -s], axis=0), BF16)
    return jnp.asarray(np.concatenate([c.T, -s.T], axis=1), BF16)


def inner_dft(n1):
    n2 = FFT_N2
    n = n1 * n2
    k = jnp.arange(n1, dtype=jnp.int32)[:, None, None] + n1 * jnp.arange(n2, dtype=jnp.int32)[None, :, None]
    r = (k * jnp.arange(n2, dtype=jnp.int32)[None, None, :]) % n
    ang = r.astype(F32) * (2.0 * math.pi / n)
    gr, gi = jnp.cos(ang), -jnp.sin(ang)
    gg = jnp.concatenate([jnp.concatenate([gr, -gi], axis=2), jnp.concatenate([gi, gr], axis=2)], axis=1)
    return gg.astype(BF16), jnp.swapaxes(gg, 1, 2).astype(BF16)


def _hyena_prep_kernel(p_ref, pp_ref, pn_ref, cw_ref, cb_ref, x0_ref, z_ref, *, tiles_per_seq, padf):
    w = GROUP_W
    i = pl.program_id(0)
    tm = p_ref.shape[0]
    nmask = jnp.where(i == pl.num_programs(0) - 1, 0.0, 1.0)
    u = _conv3_rows(p_ref[...], pp_ref[HALO - 1:HALO, :], pn_ref[0:1, :] * nmask, cw_ref[...]) + cb_ref[...]
    valid = _valid_rows(i, tiles_per_seq, tm, padf)
    x0_ref[...] = u[:, 0:w]
    z_ref[...] = jnp.where(valid, u[:, w:2 * w] * u[:, 2 * w:3 * w], 0.0)


def hyena_prep(pb, conv_w, conv_b, lp, padf, tm):
    m, w3 = pb.shape
    w = GROUP_W
    pspec, nspec = _halo_specs(tm, w3, m)
    ospec = pl.BlockSpec((tm, w), lambda i: (i, 0))
    return pl.pallas_call(
        functools.partial(_hyena_prep_kernel, tiles_per_seq=lp // tm, padf=padf),
        out_shape=(jax.ShapeDtypeStruct((m, w), F32), jax.ShapeDtypeStruct((m, w), F32)),
        grid=(m // tm,),
        in_specs=[pl.BlockSpec((tm, w3), lambda i: (i, 0)), pspec, nspec,
                  pl.BlockSpec((3, w3), lambda i: (0, 0)), pl.BlockSpec((1, w3), lambda i: (0, 0))],
        out_specs=(ospec, ospec),
        compiler_params=_cparams("arbitrary"),
        name="hyena_prep",
    )(pb, pb, pb, conv_w, conv_b.reshape(1, w3))


def _hyena_filter_kernel(w1t_ref, w1c_ref, w1s_ref, b1_ref, f1_ref, w2_ref, b2_ref, f2_ref, w3_ref, dec_ref,
                         filt_ref, asum_ref, *, l_real):
    i = pl.program_id(0)
    tm = filt_ref.shape[0]
    w = GROUP_W
    row = i * tm + lax.broadcasted_iota(jnp.int32, (tm, 1), 0)
    rf = row.astype(F32)
    t = rf * (1.0 / (l_real - 1))
    band = lax.broadcasted_iota(jnp.int32, (1, HY_BANDS), 1).astype(F32)
    bands = 1e-4 + band * ((HY_BANDS - 1 - 1e-4) / (HY_BANDS - 1))
    ang = ((2.0 * math.pi / l_real) * rf) * bands

    def hdot(a, b):
        return jnp.dot(a, b, precision=HIGHEST, preferred_element_type=F32)

    pre = t * w1t_ref[...] + hdot(jnp.cos(ang), w1c_ref[...]) - hdot(jnp.sin(ang), w1s_ref[...])
    hid = jnp.sin(f1_ref[...] * (pre + b1_ref[...]))
    hid = jnp.sin(f2_ref[...] * (hdot(hid, w2_ref[...]) + b2_ref[...]))
    filt = hdot(hid, w3_ref[...])
    window = jnp.exp(-t * jnp.abs(dec_ref[...])) + HY_SHIFT
    filt = jnp.where(row < l_real, filt * window, 0.0)

    @pl.when(i == 0)
    def _():
        asum_ref[...] = jnp.zeros(asum_ref.shape, F32)

    asum_ref[...] += jnp.sum(jnp.abs(filt), axis=0, keepdims=True)
    lane = lax.broadcasted_iota(jnp.int32, (1, 2 * w), 1)
    filt_ref[...] = jnp.where((row == 0) & (lane >= w), 0.0, filt)


def hyena_filter(w1, b1, f1, w2, b2, f2, w3, decay, lp, l_real, tm):
    w = GROUP_W
    nf = w1.shape[1]
    small = lambda a: pl.BlockSpec(a.shape, lambda i: (0,) * a.ndim)
    args = (w1[0:1], w1[1:1 + HY_BANDS], w1[1 + HY_BANDS:], b1.reshape(1, nf), f1.reshape(1, nf),
            w2, b2.reshape(1, nf), f2.reshape(1, nf), w3, decay.reshape(1, 2 * w))
    return pl.pallas_call(
        functools.partial(_hyena_filter_kernel, l_real=l_real),
        out_shape=(jax.ShapeDtypeStruct((lp, 2 * w), F32), jax.ShapeDtypeStruct((1, 2 * w), F32)),
        grid=(lp // tm,),
        in_specs=[small(a) for a in args],
        out_specs=(pl.BlockSpec((tm, 2 * w), lambda i: (i, 0)), pl.BlockSpec((1, 2 * w), lambda i: (0, 0))),
        compiler_params=_cparams("arbitrary"),
        name="hyena_filter",
    )(*args)


def _fft_outer_kernel(f_ref, x_ref, o_ref, pad_ref):
    k = x_ref.shape[1]
    pad_ref[...] = jnp.zeros(pad_ref.shape, F32)
    for s in range(x_ref.shape[2]):
        pad_ref[0:k, :] = x_ref[0, :, s, :]
        o_ref[0, :, s, :] = jnp.dot(f_ref[...], pad_ref[...].astype(BF16), preferred_element_type=F32)


def fft_outer(x, fmat):
    b, k, n2, c = x.shape
    rows, kp = fmat.shape
    ts = FFT_TS
    return pl.pallas_call(
        _fft_outer_kernel,
        out_shape=jax.ShapeDtypeStruct((b, rows, n2, c), F32),
        grid=(b, n2 // ts),
        in_specs=[pl.BlockSpec((rows, kp), lambda bi, j: (0, 0)),
                  pl.BlockSpec((1, k, ts, c), lambda bi, j: (bi, 0, j, 0))],
        out_specs=pl.BlockSpec((1, rows, ts, c), lambda bi, j: (bi, 0, j, 0)),
        scratch_shapes=[pltpu.VMEM((kp, c), F32)],
        compiler_params=_cparams("arbitrary", "arbitrary"),
        name="fft_outer",
    )(fmat, x)


def _filter_spectrum_kernel(gg_ref, a_ref, asum_ref, o_ref):
    w = GROUP_W
    n2 = FFT_N2
    a = jnp.concatenate([a_ref[0, 0], a_ref[1, 0]], axis=0).astype(BF16)
    x = jnp.dot(gg_ref[0], a, preferred_element_type=F32)
    s = 1.0 / (asum_ref[:, 0:w] + asum_ref[:, w:2 * w] + 1e-6)
    o_ref[0, 0] = (x[0:n2, 0:w] + x[0:n2, w:2 * w]) * s
    o_ref[0, 1] = (x[n2:2 * n2, 0:w] - x[n2:2 * n2, w:2 * w]) * s


def filter_spectrum(a, gg, asum):
    _, n1, n2, w2 = a.shape
    w = w2 // 2
    return pl.pallas_call(
        _filter_spectrum_kernel,
        out_shape=jax.ShapeDtypeStruct((n1, 2, n2, w), F32),
        grid=(n1,),
        in_specs=[pl.BlockSpec((1, 2 * n2, 2 * n2), lambda k: (k, 0, 0)),
                  pl.BlockSpec((2, 1, n2, w2), lambda k: (0, k, 0, 0)),
                  pl.BlockSpec((1, w2), lambda k: (0, 0))],
        out_specs=pl.BlockSpec((1, 2, n2, w), lambda k: (k, 0, 0, 0)),
        compiler_params=_cparams("arbitrary"),
        name="hyena_filter_spectrum",
    )(gg, a, asum)


def _fft_mid_kernel(gg_ref, ggt_ref, kf_ref, a_ref, o_ref):
    n2 = FFT_N2
    a = jnp.concatenate([a_ref[0, 0, 0], a_ref[0, 1, 0]], axis=0).astype(BF16)
    x = jnp.dot(gg_ref[0], a, preferred_element_type=F32)
    xr, xi = x[0:n2], x[n2:2 * n2]
    kr, ki = kf_ref[0, 0], kf_ref[0, 1]
    y = jnp.concatenate([xr * kr - xi * ki, xr * ki + xi * kr], axis=0).astype(BF16)
    bm = jnp.dot(ggt_ref[0], y, preferred_element_type=F32)
    o_ref[0, 0, 0] = bm[0:n2]
    o_ref[0, 1, 0] = bm[n2:2 * n2]


def fft_mid(a, gg, ggt, kf):
    b, _, n1, n2, w = a.shape
    aspec = pl.BlockSpec((1, 2, 1, n2, w), lambda k, bi: (bi, 0, k, 0, 0))
    gspec = pl.BlockSpec((1, 2 * n2, 2 * n2), lambda k, bi: (k, 0, 0))
    return pl.pallas_call(
        _fft_mid_kernel,
        out_shape=jax.ShapeDtypeStruct(a.shape, F32),
        grid=(n1, b),
        in_specs=[gspec, gspec, pl.BlockSpec((1, 2, n2, w), lambda k, bi: (k, 0, 0, 0)), aspec],
        out_specs=aspec,
        compiler_params=_cparams("arbitrary", "arbitrary"),
        name="fft_inner_conv",
    )(gg, ggt, kf, a)


def _fft_final_kernel(f_ref, b_ref, x0_ref, z_ref, d_ref, o_ref, *, scale):
    k = o_ref.shape[1]
    for s in range(o_ref.shape[2]):
        y = jnp.dot(f_ref[...], b_ref[0, :, s, :].astype(BF16), preferred_element_type=F32)
        o_ref[0, :, s, :] = x0_ref[0, :, s, :] * (y[0:k] * scale + z_ref[0, :, s, :] * d_ref[...])


def fft_final(bm, finv, x0, z, d, scale):
    b, rows, n2, c = bm.shape
    k = x0.shape[1]
    kp = finv.shape[0]
    ts = FFT_TS
    xspec = pl.BlockSpec((1, k, ts, c), lambda bi, j: (bi, 0, j, 0))
    return pl.pallas_call(
        functools.partial(_fft_final_kernel, scale=scale),
        out_shape=jax.ShapeDtypeStruct((b, k, n2, c), F32),
        grid=(b, n2 // ts),
        in_specs=[pl.BlockSpec((kp, rows), lambda bi, j: (0, 0)),
                  pl.BlockSpec((1, rows, ts, c), lambda bi, j: (bi, 0, j, 0)),
                  xspec, xspec, pl.BlockSpec((1, c), lambda bi, j: (0, 0))],
        out_specs=xspec,
        compiler_params=_cparams("arbitrary", "arbitrary"),
        name="fft_outer_inverse_gate",
    )(finv, bm, x0, z, d)


def hyena_mixer(pb, hp, nb, lp, padf, l_real, tm):
    w = GROUP_W
    n2 = FFT_N2
    k1 = lp // n2
    n1 = _fft_n1(lp, l_real)
    n_total = n1 * n2
    ffwd = _outer_dft(n1, k1, False, n_total)
    finv = _outer_dft(n1, k1, True, n_total)
    gg, ggt = hp['gg'], hp['ggt']
    filt, asum = hyena_filter(hp['w1'], hp['b1'], hp['f1'], hp['w2'], hp['b2'], hp['f2'], hp['w3'],
                              hp['decay'], lp, l_real, tm)
    fa = fft_outer(filt.reshape(1, k1, n2, 2 * w), ffwd)
    kf = filter_spectrum(fa.reshape(2, n1, n2, 2 * w), gg, asum)
    x0, z = hyena_prep(pb, hp['conv_w'], hp['conv_b'], lp, padf, tm)
    z4 = z.reshape(nb, k1, n2, w)
    a = fft_outer(z4, ffwd)
    bm = fft_mid(a.reshape(nb, 2, n1, n2, w), gg, ggt, kf)
    out = fft_final(bm.reshape(nb, 2 * n1, n2, w), finv, x0.reshape(nb, k1, n2, w), z4, hp['d'].reshape(1, w),
                    1.0 / n_total)
    return out.reshape(nb * lp, w)


TM = 640
ATTN_TQ = 1280
ATTN_TKB = 256
ATTN_UNROLL = 16
SCAN_TB = 256
FFN_TN = 512
FFN_DOWN_TM = 320


def kernel(x_prompt, x_sample, meta, emb_ln_g, emb_ln_b, w_in, lam_q1, lam_k1, lam_q2, lam_k2, attn_norm_g, hy_conv_w, hy_conv_b, hy_w1, hy_b1, hy_f1, hy_w2, hy_b2, hy_f2, hy_w3, hy_decay, hy_d, hg_lb, hg_norm_g, gdn_conv_w, gdn_a_log, gdn_dt_bias, gdn_norm_g, w_out, ln1_g, ln1_b, w_up, ffn_conv_w, ffn_conv_b, w_down, ln2_g, ln2_b):
    n_prompt = x_prompt.shape[0]
    x = jnp.concatenate([x_prompt, x_sample], axis=0)
    nb, seq, d = x.shape
    padf = ROW_ALIGN - N_META
    lp = seq + ROW_ALIGN
    l_real = seq + N_META
    m = nb * lp
    w = GROUP_W

    h, hb = (a.reshape(m, d) for a in embed(x, meta, emb_ln_g, emb_ln_b, padf))
    tables = rope_tables(lp, padf)
    consts = _scan_constants()
    sel = _gate_selectors()
    gg, ggt = inner_dft(_fft_n1(lp, l_real))
    sm = jax.nn.softmax(hg_lb, axis=0)
    lb_all = jnp.cumsum(sm, axis=0) - sm[0]

    for l in range(DEPTH):
        wl = w_in[l].astype(BF16)
        wd = jnp.pad(wl[:, 11 * w:], ((0, 0), (0, 4 * w + GATE_LANES - (wl.shape[1] - 11 * w))))
        pa = matmul(hb, wl[:, 0:3 * w], TM, 3 * w)
        pb = matmul(hb, wl[:, 3 * w:6 * w], TM, 3 * w)
        pc = matmul(hb, wl[:, 6 * w:11 * w], TM, 5 * w // 2)
        pd = matmul(hb, wd, TM, wd.shape[1])

        qlo, qhi, ka, va = attn_prep(pa, tables, nb, lp, TM)
        lam_vecs = jnp.stack([lam_q1[l], lam_k1[l], lam_q2[l], lam_k2[l]])
        oa = diff_attention(qlo, qhi, ka, va, lam_vecs, attn_norm_g[l], nb, lp, padf, l,
                            ATTN_TQ, ATTN_TKB, ATTN_UNROLL)

        hp = dict(gg=gg, ggt=ggt, w1=hy_w1[l], b1=hy_b1[l], f1=hy_f1[l], w2=hy_w2[l], b2=hy_b2[l],
                  f2=hy_f2[l], w3=hy_w3[l], decay=hy_decay[l], d=hy_d[l], conv_w=hy_conv_w[l],
                  conv_b=hy_conv_b[l])
        ob = hyena_mixer(pb, hp, nb, lp, padf, l_real, TM)

        qc, vc, gc, sgc = hgrn_prep(pc, 0, lb_all[l], TM)
        oc = gated_norm(hgrn_scan(qc, vc, gc, consts, nb, lp, SCAN_TB), sgc, hg_norm_g[l], TM)

        qd, kd, vd, szd, gates = gdn_prep(pd, gdn_conv_w[l], gdn_a_log[l], gdn_dt_bias[l], lp, padf, TM)
        od = gated_norm(gdn_scan(qd, kd, vd, gates, consts, sel, nb, lp, SCAN_TB), szd, gdn_norm_g[l], TM)

        h, hb = proj_residual_ln([oa, ob, oc, od], w_out[l].astype(BF16), h, ln1_g[l], ln1_b[l], lp, padf, TM)
        act = ffn_up(hb, w_up[l].astype(BF16), ffn_conv_w[l], ffn_conv_b[l], TM, FFN_TN)
        h, hb = ffn_down_ln(act, w_down[l].astype(BF16), h, ln2_g[l], ln2_b[l], lp, padf, FFN_DOWN_TM)

    y = h.reshape(nb, lp, d)[:, ROW_ALIGN:]
    return (y[:n_prompt], y[n_prompt:])
```

```python
import functools
import math

import jax
import jax.numpy as jnp
import numpy as np
from jax import lax
from jax.experimental import pallas as pl
from jax.experimental.pallas import tpu as pltpu

D_MODEL = 2048
DEPTH = 4
N_META = 16
GROUP_W = 512
N_HEADS = 4
HEAD_W = 128
DA_HD = 64
ROT_DIM = 16
ROPE_THETA = 500000.0
HY_BANDS = 16
HY_SHIFT = 0.05
CHUNK = 64
D_FF = 5632
ALPHA = (2.0 * DEPTH) ** 0.25
LN_EPS = 1e-5
RMS_EPS = 1e-6
F32 = jnp.float32
BF16 = jnp.bfloat16
HIGHEST = lax.Precision.HIGHEST

ROW_ALIGN = 256
VMEM_LIMIT = 56 * 1024 * 1024


def _cparams(*sem):
    return pltpu.CompilerParams(dimension_semantics=sem, vmem_limit_bytes=VMEM_LIMIT)


def _valid_rows(tile_idx, tiles_per_seq, tm, padf):
    base = (tile_idx % tiles_per_seq) * tm
    return (base + lax.broadcasted_iota(jnp.int32, (tm, 1), 0)) >= padf


def _ln_rows(y, g, b):
    mu = jnp.mean(y, axis=-1, keepdims=True)
    d = y - mu
    var = jnp.mean(d * d, axis=-1, keepdims=True)
    return d * lax.rsqrt(var + LN_EPS) * g + b


def _embed_kernel(x_ref, meta_ref, g_ref, b_ref, o_ref, ob_ref, *, padf):
    t = pl.program_id(1)
    g = g_ref[...]
    b = b_ref[...]

    @pl.when(t == 0)
    def _():
        y = jnp.concatenate([jnp.zeros((padf, o_ref.shape[2]), F32), _ln_rows(meta_ref[...], g, b)], axis=0)
        o_ref[0] = y
        ob_ref[0] = y.astype(BF16)

    @pl.when(t > 0)
    def _():
        y = _ln_rows(x_ref[0], g, b)
        o_ref[0] = y
        ob_ref[0] = y.astype(BF16)


def embed(x, meta, g, b, padf):
    nb, s, d = x.shape
    tm = ROW_ALIGN
    assert padf + N_META == tm and s % tm == 0
    lp = s + tm
    ospec = pl.BlockSpec((1, tm, d), lambda bi, t: (bi, t, 0))
    return pl.pallas_call(
        functools.partial(_embed_kernel, padf=padf),
        out_shape=(jax.ShapeDtypeStruct((nb, lp, d), F32), jax.ShapeDtypeStruct((nb, lp, d), BF16)),
        grid=(nb, lp // tm),
        in_specs=[pl.BlockSpec((1, tm, d), lambda bi, t: (bi, jnp.maximum(t - 1, 0), 0)),
                  pl.BlockSpec((N_META, d), lambda bi, t: (0, 0)),
                  pl.BlockSpec((1, d), lambda bi, t: (0, 0)),
                  pl.BlockSpec((1, d), lambda bi, t: (0, 0))],
        out_specs=(ospec, ospec),
        compiler_params=_cparams("arbitrary", "arbitrary"),
        name="embed_ln",
    )(x, meta, g.reshape(1, d), b.reshape(1, d))


def _mm_kernel(x_ref, w_ref, o_ref):
    o_ref[...] = jnp.dot(x_ref[...], w_ref[...], preferred_element_type=F32).astype(o_ref.dtype)


def matmul(x, w, tm, tn, out_dtype=F32):
    m, k = x.shape
    n = w.shape[1]
    assert m % tm == 0 and n % tn == 0
    return pl.pallas_call(
        _mm_kernel,
        out_shape=jax.ShapeDtypeStruct((m, n), out_dtype),
        grid=(n // tn, m // tm),
        in_specs=[pl.BlockSpec((tm, k), lambda j, i: (i, 0)),
                  pl.BlockSpec((k, tn), lambda j, i: (0, j))],
        out_specs=pl.BlockSpec((tm, tn), lambda j, i: (i, j)),
        compiler_params=_cparams("arbitrary", "arbitrary"),
        name="dense_matmul",
    )(x, w)


def _proj_ln_kernel(*refs, n_in, tiles_per_seq, padf):
    o_refs = refs[:n_in]
    w_ref, h_ref, g_ref, b_ref, out_ref, outb_ref = refs[n_in:]
    tm = h_ref.shape[0]
    acc = None
    off = 0
    for r in o_refs:
        kw = r.shape[1]
        part = jnp.dot(r[...].astype(BF16), w_ref[off:off + kw, :], preferred_element_type=F32)
        acc = part if acc is None else acc + part
        off += kw
    y = _ln_rows(ALPHA * h_ref[...] + acc, g_ref[...], b_ref[...])
    valid = _valid_rows(pl.program_id(0), tiles_per_seq, tm, padf)
    y = jnp.where(valid, y, 0.0)
    out_ref[...] = y
    outb_ref[...] = y.astype(BF16)


def proj_residual_ln(parts, w, h, g, b, lp, padf, tm):
    m, d = h.shape
    assert m % tm == 0 and lp % tm == 0
    n_in = len(parts)
    in_specs = [pl.BlockSpec((tm, p.shape[1]), lambda i: (i, 0)) for p in parts]
    in_specs += [pl.BlockSpec(w.shape, lambda i: (0, 0)),
                 pl.BlockSpec((tm, d), lambda i: (i, 0)),
                 pl.BlockSpec((1, d), lambda i: (0, 0)),
                 pl.BlockSpec((1, d), lambda i: (0, 0))]
    ospec = pl.BlockSpec((tm, d), lambda i: (i, 0))
    return pl.pallas_call(
        functools.partial(_proj_ln_kernel, n_in=n_in, tiles_per_seq=lp // tm, padf=padf),
        out_shape=(jax.ShapeDtypeStruct((m, d), F32), jax.ShapeDtypeStruct((m, d), BF16)),
        grid=(m // tm,),
        in_specs=in_specs,
        out_specs=(ospec, ospec),
        compiler_params=_cparams("arbitrary"),
        name="proj_residual_ln",
    )(*parts, w, h, g.reshape(1, d), b.reshape(1, d))


HALO = 8
HALO_BF16 = 16


def _halo_specs(tm, width, m, col_block=0, halo=HALO):
    nblk = m // halo
    per = tm // halo

    def prev_map(*idx):
        i = idx[-1]
        return (jnp.maximum(i * per - 1, 0), col_block)

    def next_map(*idx):
        i = idx[-1]
        return (jnp.minimum((i + 1) * per, nblk - 1), col_block)

    return (pl.BlockSpec((halo, width), prev_map), pl.BlockSpec((halo, width), next_map))


def _conv3_rows(pm, prev_row, next_row, cw):
    tm = pm.shape[0]
    rid = lax.broadcasted_iota(jnp.int32, (tm, 1), 0)
    down = jnp.where(rid == 0, prev_row, pltpu.roll(pm, 1, 0))
    up = jnp.where(rid == tm - 1, next_row, pltpu.roll(pm, tm - 1, 0))
    return down * cw[0:1, :] + pm * cw[1:2, :] + up * cw[2:3, :]


def _ffn_up_kernel(x_ref, xp_ref, xn_ref, wg_ref, wu_ref, cwg_ref, cwu_ref, cbg_ref, cbu_ref, o_ref):
    i = pl.program_id(1)
    last = pl.num_programs(1) - 1
    x = x_ref[...]
    hr = xp_ref.shape[0]
    halo = jnp.concatenate([xp_ref[...], xn_ref[...]], axis=0)
    nmask = jnp.where(i == last, 0.0, 1.0)

    def branch(w_ref, cw_ref, cb_ref, sl):
        w = w_ref[:, sl]
        half = x.shape[0] // 2
        pm = jnp.concatenate([jnp.dot(x[:half], w, preferred_element_type=F32),
                              jnp.dot(x[half:], w, preferred_element_type=F32)], axis=0)
        ph = jnp.dot(halo, w, preferred_element_type=F32)
        return _conv3_rows(pm, ph[hr - 1:hr, :], ph[hr:hr + 1, :] * nmask, cw_ref[:, sl]) + cb_ref[:, sl]

    tn = o_ref.shape[1]
    for c0 in range(0, tn, FFN_SUB):
        sl = slice(c0, min(c0 + FFN_SUB, tn))
        g = branch(wg_ref, cwg_ref, cbg_ref, sl)
        u = branch(wu_ref, cwu_ref, cbu_ref, sl)
        o_ref[:, sl] = (g * jax.nn.sigmoid(g) * u).astype(o_ref.dtype)


FFN_SUB = 256


def ffn_up(h, w_up, conv_w, conv_b, tm, tn):
    m, k = h.shape
    f = w_up.shape[1] // 2
    assert m % tm == 0 and f % tn == 0 and h.dtype == BF16
    nj = f // tn
    xp_spec, xn_spec = _halo_specs(tm, k, m, halo=HALO_BF16)
    cb = conv_b.reshape(1, 2 * f)
    return pl.pallas_call(
        _ffn_up_kernel,
        out_shape=jax.ShapeDtypeStruct((m, f), BF16),
        grid=(nj, m // tm),
        in_specs=[pl.BlockSpec((tm, k), lambda j, i: (i, 0)), xp_spec, xn_spec,
                  pl.BlockSpec((k, tn), lambda j, i: (0, j)),
                  pl.BlockSpec((k, tn), lambda j, i: (0, j + nj)),
                  pl.BlockSpec((3, tn), lambda j, i: (0, j)),
                  pl.BlockSpec((3, tn), lambda j, i: (0, j + nj)),
                  pl.BlockSpec((1, tn), lambda j, i: (0, j)),
                  pl.BlockSpec((1, tn), lambda j, i: (0, j + nj))],
        out_specs=pl.BlockSpec((tm, tn), lambda j, i: (i, j)),
        compiler_params=_cparams("arbitrary", "arbitrary"),
        name="ffn_up_conv_gate",
    )(h, h, h, w_up, w_up, conv_w, conv_w, cb, cb)


def _ffn_down_kernel(a_ref, w_ref, h_ref, g_ref, b_ref, out_ref, outb_ref, *, tiles_per_seq, padf):
    tm = h_ref.shape[0]
    acc = jnp.dot(a_ref[...], w_ref[...], preferred_element_type=F32)
    y = _ln_rows(ALPHA * h_ref[...] + acc, g_ref[...], b_ref[...])
    valid = _valid_rows(pl.program_id(0), tiles_per_seq, tm, padf)
    y = jnp.where(valid, y, 0.0)
    out_ref[...] = y
    outb_ref[...] = y.astype(BF16)


def ffn_down_ln(a, w_down, h, g, b, lp, padf, tm):
    m, d = h.shape
    f = a.shape[1]
    assert m % tm == 0 and lp % tm == 0
    ospec = pl.BlockSpec((tm, d), lambda i: (i, 0))
    return pl.pallas_call(
        functools.partial(_ffn_down_kernel, tiles_per_seq=lp // tm, padf=padf),
        out_shape=(jax.ShapeDtypeStruct((m, d), F32), jax.ShapeDtypeStruct((m, d), BF16)),
        grid=(m // tm,),
        in_specs=[pl.BlockSpec((tm, f), lambda i: (i, 0)),
                  pl.BlockSpec((f, d), lambda i: (0, 0), pipeline_mode=pl.Buffered(1)),
                  pl.BlockSpec((tm, d), lambda i: (i, 0)),
                  pl.BlockSpec((1, d), lambda i: (0, 0)),
                  pl.BlockSpec((1, d), lambda i: (0, 0))],
        out_specs=(ospec, ospec),
        compiler_params=_cparams("arbitrary"),
        name="ffn_down_residual_ln",
    )(a, w_down, h, g.reshape(1, d), b.reshape(1, d))


def rope_tables(lp, padf):
    half = ROT_DIM // 2
    pos = (jnp.arange(lp) - padf).astype(F32)
    inv = 1.0 / (ROPE_THETA ** (jnp.arange(half, dtype=F32) / half))
    ang = pos[:, None] * inv[None]
    cos, sin = jnp.cos(ang), jnp.sin(ang)
    ones = jnp.ones((lp, DA_HD - ROT_DIM), F32)
    zeros = jnp.zeros((lp, DA_HD - ROT_DIM), F32)
    zh = jnp.zeros((lp, half), F32)
    c = jnp.concatenate([cos, cos, ones], axis=1)
    sa = jnp.concatenate([-sin, zh, zeros], axis=1)
    sb = jnp.concatenate([zh, sin, zeros], axis=1)
    return tuple(jnp.concatenate([t, t], axis=1) for t in (c, sa, sb))


def _attn_prep_kernel(p_ref, c_ref, sa_ref, sb_ref, qlo_ref, qhi_ref, k_ref, vt_ref):
    w = GROUP_W
    reps = w // c_ref.shape[1]
    c = jnp.concatenate([c_ref[...]] * reps, axis=1)
    sa = jnp.concatenate([sa_ref[...]] * reps, axis=1)
    sb = jnp.concatenate([sb_ref[...]] * reps, axis=1)
    half = ROT_DIM // 2

    def rot(x):
        return x * c + pltpu.roll(x, w - half, 1) * sa + pltpu.roll(x, half, 1) * sb

    q = rot(p_ref[:, 0:w]) * (DA_HD ** -0.5)
    lane = lax.broadcasted_iota(jnp.int32, (1, w), 1) % (2 * DA_HD)
    qlo_ref[0] = jnp.where(lane < DA_HD, q, 0.0).T.astype(BF16)
    qhi_ref[0] = jnp.where(lane >= DA_HD, q, 0.0).T.astype(BF16)
    k_ref[...] = rot(p_ref[:, w:2 * w]).astype(BF16)
    vt_ref[0] = p_ref[:, 2 * w:3 * w].T.astype(BF16)


def attn_prep(proj, tables, nb, lp, tm):
    m = proj.shape[0]
    w = GROUP_W
    tps = lp // tm
    tspec = pl.BlockSpec((tm, 2 * DA_HD), lambda i: (i % tps, 0))
    tr_shape = jax.ShapeDtypeStruct((nb, w, lp), BF16)
    tr_spec = pl.BlockSpec((1, w, tm), lambda i: (i // tps, 0, i % tps))
    return pl.pallas_call(
        _attn_prep_kernel,
        out_shape=(tr_shape, tr_shape, jax.ShapeDtypeStruct((m, w), BF16), tr_shape),
        grid=(m // tm,),
        in_specs=[pl.BlockSpec((tm, 3 * w), lambda i: (i, 0)), tspec, tspec, tspec],
        out_specs=(tr_spec, tr_spec, pl.BlockSpec((tm, w), lambda i: (i, 0)), tr_spec),
        compiler_params=_cparams("arbitrary"),
        name="attn_prep_rotary",
    )(proj, *tables)


ATTN_QC = 256
ATTN_ONES = 16


def _attn_kernel(lam_ref, g_ref, qlo_ref, qhi_ref, k_ref, vt_ref, o_ref, qs_ref, acc_ref,
                 *, tkb, padf, lam_init, unroll):
    tq = qlo_ref.shape[2]
    nq2 = 2 * tq
    nkb = k_ref.shape[0] // tkb
    qs_ref[:, :tq] = qlo_ref[0]
    qs_ref[:, tq:] = qhi_ref[0]
    row_ok = lax.broadcasted_iota(jnp.int32, (tkb, 1), 0) >= padf
    ones = jnp.ones((ATTN_ONES, tkb), BF16)

    def scores(kb, c0, masked):
        start = kb * tkb
        if not isinstance(start, int):
            start = pl.multiple_of(start, tkb)
        s = jnp.dot(k_ref[pl.ds(start, tkb), :], qs_ref[:, c0:c0 + ATTN_QC], preferred_element_type=F32)
        if masked:
            s = jnp.where(row_ok, s, -1e30)
        return s, start

    def colmax(s):
        return jnp.max(jnp.max(s.reshape(tkb // 8, 8, ATTN_QC), axis=0), axis=0, keepdims=True)

    for c0 in range(0, nq2, 2 * ATTN_QC):
        cols = (c0, c0 + ATTN_QC)

        def consume(kb, s, m):
            start = kb * tkb
            if not isinstance(start, int):
                start = pl.multiple_of(start, tkb)
            vext = jnp.concatenate([vt_ref[0, :, pl.ds(start, tkb)], ones], axis=0)
            m_out = []
            for i in range(2):
                m_new = jnp.maximum(m[i], colmax(s[i]))
                p = jnp.exp(s[i] - m_new).astype(BF16)
                sl = slice(cols[i], cols[i] + ATTN_QC)
                acc_ref[:, sl] = acc_ref[:, sl] * jnp.exp(m[i] - m_new) + jnp.dot(vext, p, preferred_element_type=F32)
                m_out.append(m_new)
            return tuple(m_out)

        for c in cols:
            acc_ref[:, c:c + ATTN_QC] = jnp.zeros((HEAD_W + ATTN_ONES, ATTN_QC), F32)

        def body(kb, carry):
            s, m = carry
            s_next = tuple(scores(kb + 1, c, False)[0] for c in cols)
            return s_next, consume(kb, s, m)

        m0 = tuple(jnp.full((1, ATTN_QC), -1e30, F32) for _ in cols)
        s_last, m_last = lax.fori_loop(0, nkb - 1, body, (tuple(scores(0, c, True)[0] for c in cols), m0),
                                       unroll=unroll)
        consume(nkb - 1, s_last, m_last)

    lam = (jnp.exp(jnp.sum(lam_ref[0:1, :] * lam_ref[1:2, :], axis=-1, keepdims=True))
           - jnp.exp(jnp.sum(lam_ref[2:3, :] * lam_ref[3:4, :], axis=-1, keepdims=True)) + lam_init)
    o = acc_ref[0:HEAD_W, :] / acc_ref[HEAD_W:HEAD_W + 1, :]
    o = (o[:, :tq] - lam * o[:, tq:]).T
    o = o * lax.rsqrt(jnp.mean(o * o, axis=-1, keepdims=True) + RMS_EPS) * g_ref[...]
    o_ref[...] = (o * (1.0 - lam_init)).astype(o_ref.dtype)


def diff_attention(qlo_t, qhi_t, k, v_t, lam_vecs, norm_g, nb, lp, padf, layer, tq, tkb, unroll):
    m = k.shape[0]
    assert lp % tq == 0 and lp % tkb == 0 and padf < tkb and tq % ATTN_QC == 0
    nq = lp // tq
    lam_init = 0.8 - 0.6 * math.exp(-0.3 * layer)
    qspec = pl.BlockSpec((1, HEAD_W, tq), lambda b, h, qi: (b, h, qi))
    return pl.pallas_call(
        functools.partial(_attn_kernel, tkb=tkb, padf=padf, lam_init=lam_init, unroll=unroll),
        out_shape=jax.ShapeDtypeStruct((m, GROUP_W), BF16),
        grid=(nb, N_HEADS, nq),
        in_specs=[pl.BlockSpec((4, DA_HD), lambda b, h, qi: (0, 0)),
                  pl.BlockSpec((1, HEAD_W), lambda b, h, qi: (0, 0)),
                  qspec, qspec,
                  pl.BlockSpec((lp, HEAD_W), lambda b, h, qi: (b, h)),
                  pl.BlockSpec((1, HEAD_W, lp), lambda b, h, qi: (b, h, 0))],
        out_specs=pl.BlockSpec((tq, HEAD_W), lambda b, h, qi: (b * nq + qi, h)),
        scratch_shapes=[pltpu.VMEM((HEAD_W, 2 * tq), BF16),
                        pltpu.VMEM((HEAD_W + ATTN_ONES, 2 * tq), F32)],
        compiler_params=_cparams("arbitrary", "arbitrary", "arbitrary"),
        name="diff_attention",
    )(lam_vecs, norm_g.reshape(1, HEAD_W), qlo_t, qhi_t, k, v_t)


N_LEVELS = 6


def _scan_constants():
    c = CHUNK
    idx = np.arange(c)
    cs = np.zeros((2, (N_LEVELS + 2) * c, c), np.float32)
    mask = np.zeros((2, N_LEVELS + 2, c, c), np.float32)
    for d in range(2):
        cum = (idx[None, :] <= idx[:, None]) if d == 0 else (idx[None, :] >= idx[:, None])
        cs[d, 0:c] = cum
        for l in range(N_LEVELS):
            s = (c // 2) >> l
            blk = idx // s
            ref = (blk | 1) * s - 1 if d == 0 else (blk | 1) * s
            cs[d, (l + 1) * c:(l + 2) * c] = cum[ref]
            if d == 0:
                mask[d, l] = ((blk[:, None] & 1) == 1) & (blk[None, :] == blk[:, None] - 1)
            else:
                mask[d, l] = ((blk[:, None] & 1) == 0) & (blk[None, :] == blk[:, None] + 1)
        cs[d, (N_LEVELS + 1) * c:] = 1.0
        mask[d, N_LEVELS] = cum
        mask[d, N_LEVELS + 1] = cum & (idx[None, :] != idx[:, None])
    return jnp.asarray(np.concatenate([cs] * 3, axis=2), BF16), jnp.asarray(mask)


def _split3(x):
    hi = x.astype(BF16)
    r = x - hi.astype(F32)
    mid = r.astype(BF16)
    return hi, mid, (r - mid.astype(F32)).astype(BF16)


def _split3_rows(x):
    return jnp.concatenate(_split3(x), axis=0)


def _split3_lanes(x):
    return jnp.concatenate(_split3(x), axis=1)


def _nt(a, b):
    return lax.dot_general(a.astype(BF16), b.astype(BF16), (((1,), (1,)), ((), ())),
                           preferred_element_type=F32)


def _tn(a, b):
    return lax.dot_general(a.astype(BF16), b.astype(BF16), (((0,), (0,)), ((), ())),
                           preferred_element_type=F32)


def _nn(a, b):
    return jnp.dot(a.astype(BF16), b.astype(BF16), preferred_element_type=F32)


def _hgrn_prep_kernel(p_ref, lb_ref, q_ref, v_ref, g_ref, sg_ref):
    w = GROUP_W
    q = p_ref[:, 0:w]
    q_ref[...] = (q * jax.nn.sigmoid(q) * (HEAD_W ** -0.5)).astype(BF16)
    for d in range(2):
        lb = lb_ref[d:d + 1, :]
        f = p_ref[:, (1 + d) * w:(2 + d) * w]
        g_ref[d] = jnp.log(lb + (1.0 - lb) * jax.nn.sigmoid(f))
    v_ref[...] = p_ref[:, 3 * w:4 * w].astype(BF16)
    gate = p_ref[:, 4 * w:5 * w]
    sg_ref[...] = (gate * jax.nn.sigmoid(gate)).astype(BF16)


def hgrn_prep(proj, col_block, lb, tm):
    m = proj.shape[0]
    w = GROUP_W
    ospec = pl.BlockSpec((tm, w), lambda i: (i, 0))
    return pl.pallas_call(
        _hgrn_prep_kernel,
        out_shape=(jax.ShapeDtypeStruct((m, w), BF16), jax.ShapeDtypeStruct((m, w), BF16),
                   jax.ShapeDtypeStruct((2, m, w), F32), jax.ShapeDtypeStruct((m, w), BF16)),
        grid=(m // tm,),
        in_specs=[pl.BlockSpec((tm, 5 * w), lambda i: (i, col_block)),
                  pl.BlockSpec((2, w), lambda i: (0, 0))],
        out_specs=(ospec, ospec, pl.BlockSpec((2, tm, w), lambda i: (0, i, 0)), ospec),
        compiler_params=_cparams("arbitrary"),
        name="hgrn_prep",
    )(proj, lb)


def _head(x, h):
    return x[:, h * HEAD_W:(h + 1) * HEAD_W]


def _chunk_rows(ci, nch, d):
    return pl.multiple_of((ci + d * (nch - 1 - 2 * ci)) * CHUNK, CHUNK)


def _hgrn_scan_kernel(cs_ref, mask_ref, q_ref, v_ref, g_ref, o_ref, s_ref, *, nch):
    c = CHUNK
    d = pl.program_id(1)
    heads = range(N_HEADS)
    chunks = range(nch)

    @pl.when(pl.program_id(2) == 0)
    def _():
        s_ref[...] = jnp.zeros(s_ref.shape, F32)

    eye = mask_ref[0, N_LEVELS] - mask_ref[0, N_LEVELS + 1]
    rows = [_chunk_rows(ci, nch, d) for ci in chunks]
    g = [g_ref[0, pl.ds(r0, c), :] for r0 in rows]
    q = [q_ref[pl.ds(r0, c), :].astype(F32) for r0 in rows]
    v = [v_ref[pl.ds(r0, c), :] for r0 in rows]
    big = [jnp.dot(cs_ref[0], _split3_rows(gi), preferred_element_type=F32) for gi in g]
    b = [x[0:c] for x in big]
    tot = [x[(N_LEVELS + 1) * c:(N_LEVELS + 2) * c] for x in big]
    k = [1.0 - jnp.exp(gi) for gi in g]
    a = [[eye * jnp.sum(_head(q[ci] * k[ci], h), axis=-1, keepdims=True) for h in heads] for ci in chunks]
    for l in range(N_LEVELS):
        e = [jnp.exp(-jnp.abs(b[ci] - big[ci][(l + 1) * c:(l + 2) * c])) for ci in chunks]
        qe = [(q[ci] * e[ci]).astype(BF16) for ci in chunks]
        ke = [(k[ci] * e[ci]).astype(BF16) for ci in chunks]
        lm = mask_ref[0, l]
        a = [[a[ci][h] + lm * _nt(_head(qe[ci], h), _head(ke[ci], h)) for h in heads] for ci in chunks]
    o_intra = [[_nn(a[ci][h], _head(v[ci], h)) for h in heads] for ci in chunks]
    qb = [(q[ci] * jnp.exp(b[ci])).astype(BF16) for ci in chunks]
    kd = [(k[ci] * jnp.exp(tot[ci] - b[ci])).astype(BF16) for ci in chunks]
    dec = [jnp.exp(tot[ci][0:1, :]) for ci in chunks]
    s = [s_ref[h] for h in heads]
    for ci in chunks:
        for h in heads:
            o = o_intra[ci][h] + _nt(_head(qb[ci], h), s[h])
            o_ref[0, pl.ds(rows[ci], c), h * HEAD_W:(h + 1) * HEAD_W] = o
        s = [s[h] * _head(dec[ci], h) + _tn(_head(v[ci], h), _head(kd[ci], h)) for h in heads]
    for h in heads:
        s_ref[h] = s[h]


def _scan_row_block(nblk):
    def row(b, d, t):
        return b * nblk + t + d * (nblk - 1 - 2 * t)
    return row


def hgrn_scan(q, v, g, consts, nb, lp, tb):
    m, w = q.shape
    cs, mask = consts
    row = _scan_row_block(lp // tb)
    return pl.pallas_call(
        functools.partial(_hgrn_scan_kernel, nch=tb // CHUNK),
        out_shape=jax.ShapeDtypeStruct((2, m, w), F32),
        grid=(nb, 2, lp // tb),
        in_specs=[pl.BlockSpec((1,) + cs.shape[1:], lambda b, d, t: (d, 0, 0)),
                  pl.BlockSpec((1,) + mask.shape[1:], lambda b, d, t: (d, 0, 0, 0)),
                  pl.BlockSpec((tb, w), lambda b, d, t: (row(b, d, t), 0)),
                  pl.BlockSpec((tb, w), lambda b, d, t: (row(b, d, t), 0)),
                  pl.BlockSpec((1, tb, w), lambda b, d, t: (d, row(b, d, t), 0))],
        out_specs=pl.BlockSpec((1, tb, w), lambda b, d, t: (d, row(b, d, t), 0)),
        scratch_shapes=[pltpu.VMEM((N_HEADS, HEAD_W, HEAD_W), F32)],
        compiler_params=_cparams("arbitrary", "arbitrary", "arbitrary"),
        name="hgrn_scan",
    )(cs, mask, q, v, g)


def _gated_norm_kernel(o_ref, sg_ref, g_ref, out_ref):
    o = o_ref[0] + o_ref[1]
    gain = g_ref[...]
    for h in range(N_HEADS):
        sl = slice(h * HEAD_W, (h + 1) * HEAD_W)
        oh = o[:, sl]
        y = oh * lax.rsqrt(jnp.mean(oh * oh, axis=-1, keepdims=True) + RMS_EPS) * gain
        out_ref[:, sl] = (y * sg_ref[:, sl].astype(F32)).astype(out_ref.dtype)


def gated_norm(o2, sgate, norm_g, tm):
    m = sgate.shape[0]
    w = GROUP_W
    return pl.pallas_call(
        _gated_norm_kernel,
        out_shape=jax.ShapeDtypeStruct((m, w), BF16),
        grid=(m // tm,),
        in_specs=[pl.BlockSpec((2, tm, w), lambda i: (0, i, 0)),
                  pl.BlockSpec((tm, w), lambda i: (i, 0)),
                  pl.BlockSpec((1, HEAD_W), lambda i: (0, 0))],
        out_specs=pl.BlockSpec((tm, w), lambda i: (i, 0)),
        compiler_params=_cparams("arbitrary"),
        name="gated_rms_norm",
    )(o2, sgate, norm_g.reshape(1, HEAD_W))


GATE_LANES = 128


def _softplus(x):
    return jnp.maximum(x, 0.0) + jnp.log(1.0 + jnp.exp(-jnp.abs(x)))


def _gdn_prep_kernel(p_ref, pp_ref, pn_ref, cw_ref, acoef_ref, dtb_ref,
                     q_ref, k_ref, v_ref, sz_ref, gates_ref, *, tiles_per_seq, padf):
    w = GROUP_W
    i = pl.program_id(0)
    tm = p_ref.shape[0]
    nmask = jnp.where(i == pl.num_programs(0) - 1, 0.0, 1.0)
    x = _conv3_rows(p_ref[:, 0:3 * w], pp_ref[HALO - 1:HALO, :], pn_ref[0:1, :] * nmask, cw_ref[...])
    x = x * jax.nn.sigmoid(x)
    valid = _valid_rows(i, tiles_per_seq, tm, padf)
    for h in range(N_HEADS):
        sl = slice(h * HEAD_W, (h + 1) * HEAD_W)
        qh = x[:, sl]
        q_ref[:, sl] = (qh * lax.rsqrt(jnp.sum(qh * qh, axis=-1, keepdims=True) + 1e-6)
                        * (HEAD_W ** -0.5)).astype(BF16)
        kh = x[:, w + h * HEAD_W:w + (h + 1) * HEAD_W]
        k_ref[:, sl] = (kh * lax.rsqrt(jnp.sum(kh * kh, axis=-1, keepdims=True) + 1e-6)).astype(BF16)
    v_ref[...] = jnp.where(valid, x[:, 2 * w:3 * w], 0.0).astype(BF16)
    z = p_ref[:, 3 * w:4 * w]
    sz_ref[...] = (z * jax.nn.sigmoid(z)).astype(BF16)
    gt = p_ref[:, 4 * w:4 * w + GATE_LANES]
    lane = lax.broadcasted_iota(jnp.int32, (1, GATE_LANES), 1)
    decay = -acoef_ref[...] * _softplus(gt + dtb_ref[...])
    gates_ref[...] = jnp.where(lane < 2 * N_HEADS, jax.nn.sigmoid(gt), decay)


def gdn_prep(pd, conv_w, a_log, dt_bias, lp, padf, tm):
    m = pd.shape[0]
    w = GROUP_W
    zeros8 = jnp.zeros((2 * N_HEADS,), F32)
    padl = jnp.zeros((GATE_LANES - 4 * N_HEADS,), F32)
    acoef = jnp.concatenate([zeros8, jnp.exp(a_log).reshape(-1), padl]).reshape(1, GATE_LANES)
    dtb = jnp.concatenate([zeros8, dt_bias.reshape(-1), padl]).reshape(1, GATE_LANES)
    pspec, nspec = _halo_specs(tm, 3 * w, m)
    ospec = pl.BlockSpec((tm, w), lambda i: (i, 0))
    return pl.pallas_call(
        functools.partial(_gdn_prep_kernel, tiles_per_seq=lp // tm, padf=padf),
        out_shape=tuple(jax.ShapeDtypeStruct((m, w), BF16) for _ in range(4))
        + (jax.ShapeDtypeStruct((m, GATE_LANES), F32),),
        grid=(m // tm,),
        in_specs=[pl.BlockSpec((tm, pd.shape[1]), lambda i: (i, 0)), pspec, nspec,
                  pl.BlockSpec((3, 3 * w), lambda i: (0, 0)),
                  pl.BlockSpec((1, GATE_LANES), lambda i: (0, 0)),
                  pl.BlockSpec((1, GATE_LANES), lambda i: (0, 0))],
        out_specs=(ospec, ospec, ospec, ospec, pl.BlockSpec((tm, GATE_LANES), lambda i: (i, 0))),
        compiler_params=_cparams("arbitrary"),
        name="gdn_prep",
    )(pd, pd, pd, conv_w, acoef, dtb)


def _gate_selectors():
    sel = np.zeros((2, GATE_LANES, 2 * GROUP_W), np.float32)
    for d in range(2):
        for h in range(N_HEADS):
            sel[d, N_HEADS * d + h, h * HEAD_W:(h + 1) * HEAD_W] = 1.0
            sel[d, 2 * N_HEADS + N_HEADS * d + h, GROUP_W + h * HEAD_W:GROUP_W + (h + 1) * HEAD_W] = 1.0
    return jnp.asarray(np.concatenate([sel] * 3, axis=1), BF16)


def _gdn_scan_kernel(cs_ref, mask_ref, sel_ref, q_ref, k_ref, v_ref, gt_ref, o_ref, s_ref, *, nch):
    c = CHUNK
    w = GROUP_W
    d = pl.program_id(1)
    heads = range(N_HEADS)
    chunks = range(nch)
    items = [(ci, h) for ci in chunks for h in heads]

    @pl.when(pl.program_id(2) == 0)
    def _():
        s_ref[...] = jnp.zeros(s_ref.shape, F32)

    incl = mask_ref[0, N_LEVELS]
    strict = mask_ref[0, N_LEVELS + 1]
    eye = incl - strict
    cum_tot = jnp.concatenate([cs_ref[0, 0:c, :], cs_ref[0, (N_LEVELS + 1) * c:(N_LEVELS + 2) * c, :]], axis=0)
    ones3 = cs_ref[0, (N_LEVELS + 1) * c:(N_LEVELS + 2) * c, :]
    rows = [_chunk_rows(ci, nch, d) for ci in chunks]
    q = [q_ref[pl.ds(r0, c), :].astype(F32) for r0 in rows]
    k = [k_ref[pl.ds(r0, c), :].astype(F32) for r0 in rows]
    v = [v_ref[pl.ds(r0, c), :].astype(F32) for r0 in rows]
    bg = [jnp.dot(_split3_lanes(gt_ref[pl.ds(r0, c), :]), sel_ref[0], preferred_element_type=F32)
          for r0 in rows]
    beta = [x[:, 0:w] for x in bg]
    cb = [jnp.dot(cum_tot, _split3_rows(x[:, w:2 * w]), preferred_element_type=F32) for x in bg]
    bb = [x[0:c] for x in cb]
    tot = [x[c:2 * c] for x in cb]
    bc = {(ci, h): bb[ci][:, h * HEAD_W:h * HEAD_W + c] for ci, h in items}
    br = {it: jnp.dot(ones3, _split3_rows(eye * bc[it]), preferred_element_type=F32) for it in items}
    dec = {it: incl * jnp.exp(jnp.minimum(bc[it] - br[it], 0.0)) for it in items}
    kb = [k[ci] * beta[ci] for ci in chunks]
    n = {(ci, h): strict * _nt(_head(kb[ci], h), _head(k[ci], h)) * dec[(ci, h)] for ci, h in items}
    t = {it: eye - n[it] for it in items}
    pw = {it: _nn(n[it], n[it]) for it in items}
    for _ in range(N_LEVELS - 2):
        t = {it: t[it] + _nn(t[it], pw[it]) for it in items}
        pw = {it: _nn(pw[it], pw[it]) for it in items}
    t = {it: t[it] + _nn(t[it], pw[it]) for it in items}
    eb = [jnp.exp(x) for x in bb]
    rhs_u = [v[ci] * beta[ci] for ci in chunks]
    rhs_w = [kb[ci] * eb[ci] for ci in chunks]
    uw = {(ci, h): _nn(t[(ci, h)], jnp.concatenate([_head(rhs_u[ci], h), _head(rhs_w[ci], h)], axis=1))
          for ci, h in items}
    qk = {(ci, h): _nt(_head(q[ci], h), _head(k[ci], h)) * dec[(ci, h)] for ci, h in items}
    qe = [(q[ci] * eb[ci]).astype(BF16) for ci in chunks]
    kd = [(k[ci] * jnp.exp(tot[ci] - bb[ci])).astype(BF16) for ci in chunks]
    sdec = [jnp.exp(tot[ci][0:1, :]) for ci in chunks]
    s = [s_ref[h] for h in heads]
    for ci in chunks:
        wq = {h: _nn(jnp.concatenate([uw[(ci, h)][:, HEAD_W:].astype(BF16), _head(qe[ci], h)], axis=0), s[h])
              for h in heads}
        vnew = {h: uw[(ci, h)][:, 0:HEAD_W] - wq[h][0:c] for h in heads}
        for h in heads:
            o_ref[0, pl.ds(rows[ci], c), h * HEAD_W:(h + 1) * HEAD_W] = wq[h][c:2 * c] + _nn(qk[(ci, h)], vnew[h])
        s = [s[h] * _head(sdec[ci], h) + _tn(_head(kd[ci], h), vnew[h]) for h in heads]
    for h in heads:
        s_ref[h] = s[h]


def gdn_scan(q, k, v, gates, consts, sel, nb, lp, tb):
    m, w = q.shape
    cs, mask = consts
    row = _scan_row_block(lp // tb)
    rspec = pl.BlockSpec((tb, w), lambda b, d, t: (row(b, d, t), 0))
    return pl.pallas_call(
        functools.partial(_gdn_scan_kernel, nch=tb // CHUNK),
        out_shape=jax.ShapeDtypeStruct((2, m, w), F32),
        grid=(nb, 2, lp // tb),
        in_specs=[pl.BlockSpec((1,) + cs.shape[1:], lambda b, d, t: (d, 0, 0)),
                  pl.BlockSpec((1,) + mask.shape[1:], lambda b, d, t: (d, 0, 0, 0)),
                  pl.BlockSpec((1,) + sel.shape[1:], lambda b, d, t: (d, 0, 0)),
                  rspec, rspec, rspec,
                  pl.BlockSpec((tb, GATE_LANES), lambda b, d, t: (row(b, d, t), 0))],
        out_specs=pl.BlockSpec((1, tb, w), lambda b, d, t: (d, row(b, d, t), 0)),
        scratch_shapes=[pltpu.VMEM((N_HEADS, HEAD_W, HEAD_W), F32)],
        compiler_params=_cparams("arbitrary", "arbitrary", "arbitrary"),
        name="gdn_scan",
    )(cs, mask, sel, q, k, v, gates)


FFT_N2 = 256
FFT_TS = 8


def _fft_n1(lp, l_real):
    need = -(-(2 * l_real - 1) // FFT_N2)
    return max(-(-need // 8) * 8, -(-(lp // FFT_N2) // 8) * 8)


def _outer_dft(n1, k_rows, inverse, n_total):
    kp = -(-k_rows // 16) * 16
    a = np.arange(n1)[:, None] * np.arange(kp)[None, :]
    ang = 2.0 * np.pi * (a % n1) / n1
    c, s = np.cos(ang), np.sin(ang)
    live = (np.arange(kp) < k_rows)[None, :]
    c, s = c * live, s * live
    if not inverse:
        return jnp.asarray(np.concatenate([c, -s], axis=0), BF16)
    return jnp.asarray(np.concatenate([c.T, -s.T], axis=1) / n_total, F32)


def inner_dft(n1):
    n2 = FFT_N2
    n = n1 * n2
    k = jnp.arange(n1, dtype=jnp.int32)[:, None, None] + n1 * jnp.arange(n2, dtype=jnp.int32)[None, :, None]
    r = (k * jnp.arange(n2, dtype=jnp.int32)[None, None, :]) % n
    ang = r.astype(F32) * (2.0 * math.pi / n)
    gr, gi = jnp.cos(ang), -jnp.sin(ang)
    gg = jnp.concatenate([jnp.concatenate([gr, -gi], axis=2), jnp.concatenate([gi, gr], axis=2)], axis=1)
    return gg.astype(BF16), jnp.swapaxes(gg, 1, 2).astype(BF16)


def _hyena_prep_kernel(p_ref, pp_ref, pn_ref, cw_ref, cb_ref, x0_ref, z_ref, *, tiles_per_seq, padf):
    w = GROUP_W
    i = pl.program_id(0)
    tm = p_ref.shape[0]
    nmask = jnp.where(i == pl.num_programs(0) - 1, 0.0, 1.0)
    u = _conv3_rows(p_ref[...], pp_ref[HALO - 1:HALO, :], pn_ref[0:1, :] * nmask, cw_ref[...]) + cb_ref[...]
    valid = _valid_rows(i, tiles_per_seq, tm, padf)
    x0_ref[...] = u[:, 0:w]
    z_ref[...] = jnp.where(valid, u[:, w:2 * w] * u[:, 2 * w:3 * w], 0.0)


def hyena_prep(pb, conv_w, conv_b, lp, padf, tm):
    m, w3 = pb.shape
    w = GROUP_W
    pspec, nspec = _halo_specs(tm, w3, m)
    ospec = pl.BlockSpec((tm, w), lambda i: (i, 0))
    return pl.pallas_call(
        functools.partial(_hyena_prep_kernel, tiles_per_seq=lp // tm, padf=padf),
        out_shape=(jax.ShapeDtypeStruct((m, w), F32), jax.ShapeDtypeStruct((m, w), F32)),
        grid=(m // tm,),
        in_specs=[pl.BlockSpec((tm, w3), lambda i: (i, 0)), pspec, nspec,
                  pl.BlockSpec((3, w3), lambda i: (0, 0)), pl.BlockSpec((1, w3), lambda i: (0, 0))],
        out_specs=(ospec, ospec),
        compiler_params=_cparams("arbitrary"),
        name="hyena_prep",
    )(pb, pb, pb, conv_w, conv_b.reshape(1, w3))


def _hyena_filter_kernel(w1t_ref, w1c_ref, w1s_ref, b1_ref, f1_ref, w2_ref, b2_ref, f2_ref, w3_ref, dec_ref,
                         filt_ref, asum_ref, *, l_real):
    i = pl.program_id(0)
    tm = filt_ref.shape[0]
    w = GROUP_W
    row = i * tm + lax.broadcasted_iota(jnp.int32, (tm, 1), 0)
    rf = row.astype(F32)
    t = rf * (1.0 / (l_real - 1))
    band = lax.broadcasted_iota(jnp.int32, (1, HY_BANDS), 1).astype(F32)
    bands = 1e-4 + band * ((HY_BANDS - 1 - 1e-4) / (HY_BANDS - 1))
    ang = ((2.0 * math.pi / l_real) * rf) * bands

    def hdot(a, b):
        return jnp.dot(a, b, precision=HIGHEST, preferred_element_type=F32)

    pre = t * w1t_ref[...] + hdot(jnp.cos(ang), w1c_ref[...]) - hdot(jnp.sin(ang), w1s_ref[...])
    hid = jnp.sin(f1_ref[...] * (pre + b1_ref[...]))
    hid = jnp.sin(f2_ref[...] * (hdot(hid, w2_ref[...]) + b2_ref[...]))
    filt = hdot(hid, w3_ref[...])
    window = jnp.exp(-t * jnp.abs(dec_ref[...])) + HY_SHIFT
    filt = jnp.where(row < l_real, filt * window, 0.0)

    @pl.when(i == 0)
    def _():
        asum_ref[...] = jnp.zeros(asum_ref.shape, F32)

    asum_ref[...] += jnp.sum(jnp.abs(filt), axis=0, keepdims=True)
    lane = lax.broadcasted_iota(jnp.int32, (1, 2 * w), 1)
    filt_ref[...] = jnp.where((row == 0) & (lane >= w), 0.0, filt)


def hyena_filter(w1, b1, f1, w2, b2, f2, w3, decay, lp, l_real, tm):
    w = GROUP_W
    nf = w1.shape[1]
    small = lambda a: pl.BlockSpec(a.shape, lambda i: (0,) * a.ndim)
    args = (w1[0:1], w1[1:1 + HY_BANDS], w1[1 + HY_BANDS:], b1.reshape(1, nf), f1.reshape(1, nf),
            w2, b2.reshape(1, nf), f2.reshape(1, nf), w3, decay.reshape(1, 2 * w))
    return pl.pallas_call(
        functools.partial(_hyena_filter_kernel, l_real=l_real),
        out_shape=(jax.ShapeDtypeStruct((lp, 2 * w), F32), jax.ShapeDtypeStruct((1, 2 * w), F32)),
        grid=(lp // tm,),
        in_specs=[small(a) for a in args],
        out_specs=(pl.BlockSpec((tm, 2 * w), lambda i: (i, 0)), pl.BlockSpec((1, 2 * w), lambda i: (0, 0))),
        compiler_params=_cparams("arbitrary"),
        name="hyena_filter",
    )(*args)


def _fft_outer_kernel(f_ref, x_ref, o_ref, pad_ref):
    k = x_ref.shape[1]
    pad_ref[...] = jnp.zeros(pad_ref.shape, F32)
    for s in range(x_ref.shape[2]):
        pad_ref[0:k, :] = x_ref[0, :, s, :]
        o_ref[0, :, s, :] = jnp.dot(f_ref[...], pad_ref[...].astype(BF16), preferred_element_type=F32)


def fft_outer(x, fmat):
    b, k, n2, c = x.shape
    rows, kp = fmat.shape
    ts = FFT_TS
    return pl.pallas_call(
        _fft_outer_kernel,
        out_shape=jax.ShapeDtypeStruct((b, rows, n2, c), F32),
        grid=(b, n2 // ts),
        in_specs=[pl.BlockSpec((rows, kp), lambda bi, j: (0, 0)),
                  pl.BlockSpec((1, k, ts, c), lambda bi, j: (bi, 0, j, 0))],
        out_specs=pl.BlockSpec((1, rows, ts, c), lambda bi, j: (bi, 0, j, 0)),
        scratch_shapes=[pltpu.VMEM((kp, c), F32)],
        compiler_params=_cparams("arbitrary", "arbitrary"),
        name="fft_outer",
    )(fmat, x)


def _filter_spectrum_kernel(gg_ref, a_ref, asum_ref, o_ref):
    w = GROUP_W
    n2 = FFT_N2
    a = jnp.concatenate([a_ref[0, 0], a_ref[1, 0]], axis=0).astype(BF16)
    x = jnp.dot(gg_ref[0], a, preferred_element_type=F32)
    s = 1.0 / (asum_ref[:, 0:w] + asum_ref[:, w:2 * w] + 1e-6)
    o_ref[0, 0] = (x[0:n2, 0:w] + x[0:n2, w:2 * w]) * s
    o_ref[0, 1] = (x[n2:2 * n2, 0:w] - x[n2:2 * n2, w:2 * w]) * s


def filter_spectrum(a, gg, asum):
    _, n1, n2, w2 = a.shape
    w = w2 // 2
    return pl.pallas_call(
        _filter_spectrum_kernel,
        out_shape=jax.ShapeDtypeStruct((n1, 2, n2, w), F32),
        grid=(n1,),
        in_specs=[pl.BlockSpec((1, 2 * n2, 2 * n2), lambda k: (k, 0, 0)),
                  pl.BlockSpec((2, 1, n2, w2), lambda k: (0, k, 0, 0)),
                  pl.BlockSpec((1, w2), lambda k: (0, 0))],
        out_specs=pl.BlockSpec((1, 2, n2, w), lambda k: (k, 0, 0, 0)),
        compiler_params=_cparams("arbitrary"),
        name="hyena_filter_spectrum",
    )(gg, a, asum)


def _fft_mid_kernel(gg_ref, ggt_ref, kf_ref, a_ref, o_ref):
    n2 = FFT_N2
    a = jnp.concatenate([a_ref[0, 0, 0], a_ref[0, 1, 0]], axis=0).astype(BF16)
    x = jnp.dot(gg_ref[0], a, preferred_element_type=F32)
    xr, xi = x[0:n2], x[n2:2 * n2]
    kr, ki = kf_ref[0, 0], kf_ref[0, 1]
    y = jnp.concatenate([xr * kr - xi * ki, xr * ki + xi * kr], axis=0).astype(BF16)
    bm = jnp.dot(ggt_ref[0], y, preferred_element_type=F32)
    o_ref[0, 0, 0] = bm[0:n2]
    o_ref[0, 1, 0] = bm[n2:2 * n2]


def fft_mid(a, gg, ggt, kf):
    b, _, n1, n2, w = a.shape
    aspec = pl.BlockSpec((1, 2, 1, n2, w), lambda k, bi: (bi, 0, k, 0, 0))
    gspec = pl.BlockSpec((1, 2 * n2, 2 * n2), lambda k, bi: (k, 0, 0))
    return pl.pallas_call(
        _fft_mid_kernel,
        out_shape=jax.ShapeDtypeStruct(a.shape, F32),
        grid=(n1, b),
        in_specs=[gspec, gspec, pl.BlockSpec((1, 2, n2, w), lambda k, bi: (k, 0, 0, 0)), aspec],
        out_specs=aspec,
        compiler_params=_cparams("arbitrary", "arbitrary"),
        name="fft_inner_conv",
    )(gg, ggt, kf, a)


def _fft_final_kernel(f_ref, b_ref, x0_ref, z_ref, d_ref, o_ref):
    k = o_ref.shape[1]
    for s in range(o_ref.shape[2]):
        y = jnp.dot(f_ref[...], b_ref[0, :, s, :], precision=HIGHEST, preferred_element_type=F32)
        o_ref[0, :, s, :] = x0_ref[0, :, s, :] * (y[0:k] + z_ref[0, :, s, :] * d_ref[...])


def fft_final(bm, finv, x0, z, d):
    b, rows, n2, c = bm.shape
    k = x0.shape[1]
    kp = finv.shape[0]
    ts = FFT_TS
    xspec = pl.BlockSpec((1, k, ts, c), lambda bi, j: (bi, 0, j, 0))
    return pl.pallas_call(
        _fft_final_kernel,
        out_shape=jax.ShapeDtypeStruct((b, k, n2, c), F32),
        grid=(b, n2 // ts),
        in_specs=[pl.BlockSpec((kp, rows), lambda bi, j: (0, 0)),
                  pl.BlockSpec((1, rows, ts, c), lambda bi, j: (bi, 0, j, 0)),
                  xspec, xspec, pl.BlockSpec((1, c), lambda bi, j: (0, 0))],
        out_specs=xspec,
        compiler_params=_cparams("arbitrary", "arbitrary"),
        name="fft_outer_inverse_gate",
    )(finv, bm, x0, z, d)


def hyena_mixer(pb, hp, nb, lp, padf, l_real, tm):
    w = GROUP_W
    n2 = FFT_N2
    k1 = lp // n2
    n1 = _fft_n1(lp, l_real)
    n_total = n1 * n2
    ffwd = _outer_dft(n1, k1, False, n_total)
    finv = _outer_dft(n1, k1, True, n_total)
    gg, ggt = hp['gg'], hp['ggt']
    filt, asum = hyena_filter(hp['w1'], hp['b1'], hp['f1'], hp['w2'], hp['b2'], hp['f2'], hp['w3'],
                              hp['decay'], lp, l_real, tm)
    fa = fft_outer(filt.reshape(1, k1, n2, 2 * w), ffwd)
    kf = filter_spectrum(fa.reshape(2, n1, n2, 2 * w), gg, asum)
    x0, z = hyena_prep(pb, hp['conv_w'], hp['conv_b'], lp, padf, tm)
    z4 = z.reshape(nb, k1, n2, w)
    a = fft_outer(z4, ffwd)
    bm = fft_mid(a.reshape(nb, 2, n1, n2, w), gg, ggt, kf)
    out = fft_final(bm.reshape(nb, 2 * n1, n2, w), finv, x0.reshape(nb, k1, n2, w), z4, hp['d'].reshape(1, w))
    return out.reshape(nb * lp, w)


TM = 640
ATTN_TQ = 1280
ATTN_TKB = 256
ATTN_UNROLL = 16
SCAN_TB = 256
FFN_TN = 512
FFN_DOWN_TM = 320


def kernel(x_prompt, x_sample, meta, emb_ln_g, emb_ln_b, w_in, lam_q1, lam_k1, lam_q2, lam_k2, attn_norm_g, hy_conv_w, hy_conv_b, hy_w1, hy_b1, hy_f1, hy_w2, hy_b2, hy_f2, hy_w3, hy_decay, hy_d, hg_lb, hg_norm_g, gdn_conv_w, gdn_a_log, gdn_dt_bias, gdn_norm_g, w_out, ln1_g, ln1_b, w_up, ffn_conv_w, ffn_conv_b, w_down, ln2_g, ln2_b):
    n_prompt = x_prompt.shape[0]
    x = jnp.concatenate([x_prompt, x_sample], axis=0)
    nb, seq, d = x.shape
    padf = ROW_ALIGN - N_META
    lp = seq + ROW_ALIGN
    l_real = seq + N_META
    m = nb * lp
    w = GROUP_W

    h, hb = (a.reshape(m, d) for a in embed(x, meta, emb_ln_g, emb_ln_b, padf))
    tables = rope_tables(lp, padf)
    consts = _scan_constants()
    sel = _gate_selectors()
    gg, ggt = inner_dft(_fft_n1(lp, l_real))
    sm = jax.nn.softmax(hg_lb, axis=0)
    lb_all = jnp.cumsum(sm, axis=0) - sm[0]

    for l in range(DEPTH):
        wl = w_in[l].astype(BF16)
        wd = jnp.pad(wl[:, 11 * w:], ((0, 0), (0, 4 * w + GATE_LANES - (wl.shape[1] - 11 * w))))
        pa = matmul(hb, wl[:, 0:3 * w], TM, 3 * w)
        pb = matmul(hb, wl[:, 3 * w:6 * w], TM, 3 * w)
        pc = matmul(hb, wl[:, 6 * w:11 * w], TM, 5 * w // 2)
        pd = matmul(hb, wd, TM, wd.shape[1])

        qlo, qhi, ka, va = attn_prep(pa, tables, nb, lp, TM)
        lam_vecs = jnp.stack([lam_q1[l], lam_k1[l], lam_q2[l], lam_k2[l]])
        oa = diff_attention(qlo, qhi, ka, va, lam_vecs, attn_norm_g[l], nb, lp, padf, l,
                            ATTN_TQ, ATTN_TKB, ATTN_UNROLL)

        hp = dict(gg=gg, ggt=ggt, w1=hy_w1[l], b1=hy_b1[l], f1=hy_f1[l], w2=hy_w2[l], b2=hy_b2[l],
                  f2=hy_f2[l], w3=hy_w3[l], decay=hy_decay[l], d=hy_d[l], conv_w=hy_conv_w[l],
                  conv_b=hy_conv_b[l])
        ob = hyena_mixer(pb, hp, nb, lp, padf, l_real, TM)

        qc, vc, gc, sgc = hgrn_prep(pc, 0, lb_all[l], TM)
        oc = gated_norm(hgrn_scan(qc, vc, gc, consts, nb, lp, SCAN_TB), sgc, hg_norm_g[l], TM)

        qd, kd, vd, szd, gates = gdn_prep(pd, gdn_conv_w[l], gdn_a_log[l], gdn_dt_bias[l], lp, padf, TM)
        od = gated_norm(gdn_scan(qd, kd, vd, gates, consts, sel, nb, lp, SCAN_TB), szd, gdn_norm_g[l], TM)

        h, hb = proj_residual_ln([oa, ob, oc, od], w_out[l].astype(BF16), h, ln1_g[l], ln1_b[l], lp, padf, TM)
        act = ffn_up(hb, w_up[l].astype(BF16), ffn_conv_w[l], ffn_conv_b[l], TM, FFN_TN)
        h, hb = ffn_down_ln(act, w_down[l].astype(BF16), h, ln2_g[l], ln2_b[l], lp, padf, FFN_DOWN_TM)

    y = h.reshape(nb, lp, d)[:, ROW_ALIGN:]
    return (y[:n_prompt], y[n_prompt:])
```

```python
import functools
import math

import jax
import jax.numpy as jnp
import numpy as np
from jax import lax
from jax.experimental import pallas as pl
from jax.experimental.pallas import tpu as pltpu

D_MODEL = 2048
DEPTH = 4
N_META = 16
GROUP_W = 512
N_HEADS = 4
HEAD_W = 128
DA_HD = 64
ROT_DIM = 16
ROPE_THETA = 500000.0
HY_BANDS = 16
HY_SHIFT = 0.05
CHUNK = 64
D_FF = 5632
ALPHA = (2.0 * DEPTH) ** 0.25
LN_EPS = 1e-5
RMS_EPS = 1e-6
F32 = jnp.float32
BF16 = jnp.bfloat16
HIGHEST = lax.Precision.HIGHEST

ROW_ALIGN = 256
VMEM_LIMIT = 56 * 1024 * 1024


def _cparams(*sem):
    return pltpu.CompilerParams(dimension_semantics=sem, vmem_limit_bytes=VMEM_LIMIT)


def _valid_rows(tile_idx, tiles_per_seq, tm, padf):
    base = (tile_idx % tiles_per_seq) * tm
    return (base + lax.broadcasted_iota(jnp.int32, (tm, 1), 0)) >= padf


def _ln_rows(y, g, b):
    mu = jnp.mean(y, axis=-1, keepdims=True)
    d = y - mu
    var = jnp.mean(d * d, axis=-1, keepdims=True)
    return d * lax.rsqrt(var + LN_EPS) * g + b


def _embed_kernel(x_ref, meta_ref, g_ref, b_ref, o_ref, ob_ref, *, padf):
    t = pl.program_id(1)
    g = g_ref[...]
    b = b_ref[...]

    @pl.when(t == 0)
    def _():
        y = jnp.concatenate([jnp.zeros((padf, o_ref.shape[2]), F32), _ln_rows(meta_ref[...], g, b)], axis=0)
        o_ref[0] = y
        ob_ref[0] = y.astype(BF16)

    @pl.when(t > 0)
    def _():
        y = _ln_rows(x_ref[0], g, b)
        o_ref[0] = y
        ob_ref[0] = y.astype(BF16)


def embed(x, meta, g, b, padf):
    nb, s, d = x.shape
    tm = ROW_ALIGN
    assert padf + N_META == tm and s % tm == 0
    lp = s + tm
    ospec = pl.BlockSpec((1, tm, d), lambda bi, t: (bi, t, 0))
    return pl.pallas_call(
        functools.partial(_embed_kernel, padf=padf),
        out_shape=(jax.ShapeDtypeStruct((nb, lp, d), F32), jax.ShapeDtypeStruct((nb, lp, d), BF16)),
        grid=(nb, lp // tm),
        in_specs=[pl.BlockSpec((1, tm, d), lambda bi, t: (bi, jnp.maximum(t - 1, 0), 0)),
                  pl.BlockSpec((N_META, d), lambda bi, t: (0, 0)),
                  pl.BlockSpec((1, d), lambda bi, t: (0, 0)),
                  pl.BlockSpec((1, d), lambda bi, t: (0, 0))],
        out_specs=(ospec, ospec),
        compiler_params=_cparams("arbitrary", "arbitrary"),
        name="embed_ln",
    )(x, meta, g.reshape(1, d), b.reshape(1, d))


def _mm_kernel(x_ref, w_ref, o_ref):
    o_ref[...] = jnp.dot(x_ref[...], w_ref[...], preferred_element_type=F32).astype(o_ref.dtype)


def matmul(x, w, tm, tn, out_dtype=F32):
    m, k = x.shape
    n = w.shape[1]
    assert m % tm == 0 and n % tn == 0
    return pl.pallas_call(
        _mm_kernel,
        out_shape=jax.ShapeDtypeStruct((m, n), out_dtype),
        grid=(n // tn, m // tm),
        in_specs=[pl.BlockSpec((tm, k), lambda j, i: (i, 0)),
                  pl.BlockSpec((k, tn), lambda j, i: (0, j))],
        out_specs=pl.BlockSpec((tm, tn), lambda j, i: (i, j)),
        compiler_params=_cparams("arbitrary", "arbitrary"),
        name="dense_matmul",
    )(x, w)


def _proj_ln_kernel(*refs, n_in, tiles_per_seq, padf):
    o_refs = refs[:n_in]
    w_ref, h_ref, g_ref, b_ref, out_ref, outb_ref = refs[n_in:]
    tm = h_ref.shape[0]
    acc = None
    off = 0
    for r in o_refs:
        kw = r.shape[1]
        part = jnp.dot(r[...].astype(BF16), w_ref[off:off + kw, :], preferred_element_type=F32)
        acc = part if acc is None else acc + part
        off += kw
    y = _ln_rows(ALPHA * h_ref[...] + acc, g_ref[...], b_ref[...])
    valid = _valid_rows(pl.program_id(0), tiles_per_seq, tm, padf)
    y = jnp.where(valid, y, 0.0)
    out_ref[...] = y
    outb_ref[...] = y.astype(BF16)


def proj_residual_ln(parts, w, h, g, b, lp, padf, tm):
    m, d = h.shape
    assert m % tm == 0 and lp % tm == 0
    n_in = len(parts)
    in_specs = [pl.BlockSpec((tm, p.shape[1]), lambda i: (i, 0)) for p in parts]
    in_specs += [pl.BlockSpec(w.shape, lambda i: (0, 0)),
                 pl.BlockSpec((tm, d), lambda i: (i, 0)),
                 pl.BlockSpec((1, d), lambda i: (0, 0)),
                 pl.BlockSpec((1, d), lambda i: (0, 0))]
    ospec = pl.BlockSpec((tm, d), lambda i: (i, 0))
    return pl.pallas_call(
        functools.partial(_proj_ln_kernel, n_in=n_in, tiles_per_seq=lp // tm, padf=padf),
        out_shape=(jax.ShapeDtypeStruct((m, d), F32), jax.ShapeDtypeStruct((m, d), BF16)),
        grid=(m // tm,),
        in_specs=in_specs,
        out_specs=(ospec, ospec),
        compiler_params=_cparams("arbitrary"),
        name="proj_residual_ln",
    )(*parts, w, h, g.reshape(1, d), b.reshape(1, d))


HALO = 8
HALO_BF16 = 16


def _halo_specs(tm, width, m, col_block=0, halo=HALO):
    nblk = m // halo
    per = tm // halo

    def prev_map(*idx):
        i = idx[-1]
        return (jnp.maximum(i * per - 1, 0), col_block)

    def next_map(*idx):
        i = idx[-1]
        return (jnp.minimum((i + 1) * per, nblk - 1), col_block)

    return (pl.BlockSpec((halo, width), prev_map), pl.BlockSpec((halo, width), next_map))


def _conv3_rows(pm, prev_row, next_row, cw):
    tm = pm.shape[0]
    rid = lax.broadcasted_iota(jnp.int32, (tm, 1), 0)
    down = jnp.where(rid == 0, prev_row, pltpu.roll(pm, 1, 0))
    up = jnp.where(rid == tm - 1, next_row, pltpu.roll(pm, tm - 1, 0))
    return down * cw[0:1, :] + pm * cw[1:2, :] + up * cw[2:3, :]


def _ffn_up_kernel(x_ref, xp_ref, xn_ref, wg_ref, wu_ref, cwg_ref, cwu_ref, cbg_ref, cbu_ref, o_ref):
    i = pl.program_id(1)
    last = pl.num_programs(1) - 1
    x = x_ref[...]
    hr = xp_ref.shape[0]
    halo = jnp.concatenate([xp_ref[...], xn_ref[...]], axis=0)
    nmask = jnp.where(i == last, 0.0, 1.0)

    def branch(w_ref, cw_ref, cb_ref, sl):
        w = w_ref[:, sl]
        pm = jnp.dot(x, w, preferred_element_type=F32)
        ph = jnp.dot(halo, w, preferred_element_type=F32)
        return _conv3_rows(pm, ph[hr - 1:hr, :], ph[hr:hr + 1, :] * nmask, cw_ref[:, sl]) + cb_ref[:, sl]

    tn = o_ref.shape[1]
    for c0 in range(0, tn, FFN_SUB):
        sl = slice(c0, min(c0 + FFN_SUB, tn))
        g = branch(wg_ref, cwg_ref, cbg_ref, sl)
        u = branch(wu_ref, cwu_ref, cbu_ref, sl)
        o_ref[:, sl] = (g * jax.nn.sigmoid(g) * u).astype(o_ref.dtype)


FFN_SUB = 256


def ffn_up(h, w_up, conv_w, conv_b, tm, tn):
    m, k = h.shape
    f = w_up.shape[1] // 2
    assert m % tm == 0 and f % tn == 0 and h.dtype == BF16
    nj = f // tn
    xp_spec, xn_spec = _halo_specs(tm, k, m, halo=HALO_BF16)
    cb = conv_b.reshape(1, 2 * f)
    return pl.pallas_call(
        _ffn_up_kernel,
        out_shape=jax.ShapeDtypeStruct((m, f), BF16),
        grid=(nj, m // tm),
        in_specs=[pl.BlockSpec((tm, k), lambda j, i: (i, 0)), xp_spec, xn_spec,
                  pl.BlockSpec((k, tn), lambda j, i: (0, j)),
                  pl.BlockSpec((k, tn), lambda j, i: (0, j + nj)),
                  pl.BlockSpec((3, tn), lambda j, i: (0, j)),
                  pl.BlockSpec((3, tn), lambda j, i: (0, j + nj)),
                  pl.BlockSpec((1, tn), lambda j, i: (0, j)),
                  pl.BlockSpec((1, tn), lambda j, i: (0, j + nj))],
        out_specs=pl.BlockSpec((tm, tn), lambda j, i: (i, j)),
        compiler_params=_cparams("arbitrary", "arbitrary"),
        name="ffn_up_conv_gate",
    )(h, h, h, w_up, w_up, conv_w, conv_w, cb, cb)


def _ffn_down_kernel(a_ref, w_ref, h_ref, g_ref, b_ref, out_ref, outb_ref, *, tiles_per_seq, padf):
    tm = h_ref.shape[0]
    acc = jnp.dot(a_ref[...], w_ref[...], preferred_element_type=F32)
    y = _ln_rows(ALPHA * h_ref[...] + acc, g_ref[...], b_ref[...])
    valid = _valid_rows(pl.program_id(0), tiles_per_seq, tm, padf)
    y = jnp.where(valid, y, 0.0)
    out_ref[...] = y
    outb_ref[...] = y.astype(BF16)


def ffn_down_ln(a, w_down, h, g, b, lp, padf, tm):
    m, d = h.shape
    f = a.shape[1]
    assert m % tm == 0 and lp % tm == 0
    ospec = pl.BlockSpec((tm, d), lambda i: (i, 0))
    return pl.pallas_call(
        functools.partial(_ffn_down_kernel, tiles_per_seq=lp // tm, padf=padf),
        out_shape=(jax.ShapeDtypeStruct((m, d), F32), jax.ShapeDtypeStruct((m, d), BF16)),
        grid=(m // tm,),
        in_specs=[pl.BlockSpec((tm, f), lambda i: (i, 0)),
                  pl.BlockSpec((f, d), lambda i: (0, 0), pipeline_mode=pl.Buffered(1)),
                  pl.BlockSpec((tm, d), lambda i: (i, 0)),
                  pl.BlockSpec((1, d), lambda i: (0, 0)),
                  pl.BlockSpec((1, d), lambda i: (0, 0))],
        out_specs=(ospec, ospec),
        compiler_params=_cparams("arbitrary"),
        name="ffn_down_residual_ln",
    )(a, w_down, h, g.reshape(1, d), b.reshape(1, d))


def rope_tables(lp, padf):
    half = ROT_DIM // 2
    pos = (jnp.arange(lp) - padf).astype(F32)
    inv = 1.0 / (ROPE_THETA ** (jnp.arange(half, dtype=F32) / half))
    ang = pos[:, None] * inv[None]
    cos, sin = jnp.cos(ang), jnp.sin(ang)
    ones = jnp.ones((lp, DA_HD - ROT_DIM), F32)
    zeros = jnp.zeros((lp, DA_HD - ROT_DIM), F32)
    zh = jnp.zeros((lp, half), F32)
    c = jnp.concatenate([cos, cos, ones], axis=1)
    sa = jnp.concatenate([-sin, zh, zeros], axis=1)
    sb = jnp.concatenate([zh, sin, zeros], axis=1)
    return tuple(jnp.concatenate([t, t], axis=1) for t in (c, sa, sb))


def _attn_prep_kernel(p_ref, c_ref, sa_ref, sb_ref, qlo_ref, qhi_ref, k_ref, vt_ref):
    w = GROUP_W
    reps = w // c_ref.shape[1]
    c = jnp.concatenate([c_ref[...]] * reps, axis=1)
    sa = jnp.concatenate([sa_ref[...]] * reps, axis=1)
    sb = jnp.concatenate([sb_ref[...]] * reps, axis=1)
    half = ROT_DIM // 2

    def rot(x):
        return x * c + pltpu.roll(x, w - half, 1) * sa + pltpu.roll(x, half, 1) * sb

    q = rot(p_ref[:, 0:w]) * (DA_HD ** -0.5)
    lane = lax.broadcasted_iota(jnp.int32, (1, w), 1) % (2 * DA_HD)
    qlo_ref[0] = jnp.where(lane < DA_HD, q, 0.0).T.astype(BF16)
    qhi_ref[0] = jnp.where(lane >= DA_HD, q, 0.0).T.astype(BF16)
    k_ref[...] = rot(p_ref[:, w:2 * w]).astype(BF16)
    vt_ref[0] = p_ref[:, 2 * w:3 * w].T.astype(BF16)


def attn_prep(proj, tables, nb, lp, tm):
    m = proj.shape[0]
    w = GROUP_W
    tps = lp // tm
    tspec = pl.BlockSpec((tm, 2 * DA_HD), lambda i: (i % tps, 0))
    tr_shape = jax.ShapeDtypeStruct((nb, w, lp), BF16)
    tr_spec = pl.BlockSpec((1, w, tm), lambda i: (i // tps, 0, i % tps))
    return pl.pallas_call(
        _attn_prep_kernel,
        out_shape=(tr_shape, tr_shape, jax.ShapeDtypeStruct((m, w), BF16), tr_shape),
        grid=(m // tm,),
        in_specs=[pl.BlockSpec((tm, 3 * w), lambda i: (i, 0)), tspec, tspec, tspec],
        out_specs=(tr_spec, tr_spec, pl.BlockSpec((tm, w), lambda i: (i, 0)), tr_spec),
        compiler_params=_cparams("arbitrary"),
        name="attn_prep_rotary",
    )(proj, *tables)


ATTN_QC = 256
ATTN_ONES = 8


def _attn_kernel(lam_ref, g_ref, qlo_ref, qhi_ref, k_ref, vt_ref, o_ref, qs_ref, acc_ref,
                 *, tkb, padf, lam_init, unroll):
    tq = qlo_ref.shape[2]
    nq2 = 2 * tq
    nkb = k_ref.shape[0] // tkb
    qs_ref[:, :tq] = qlo_ref[0]
    qs_ref[:, tq:] = qhi_ref[0]
    row_ok = lax.broadcasted_iota(jnp.int32, (tkb, 1), 0) >= padf
    ones = jnp.ones((ATTN_ONES, tkb), BF16)

    def scores(kb, c0, masked):
        start = kb * tkb
        if not isinstance(start, int):
            start = pl.multiple_of(start, tkb)
        s = jnp.dot(k_ref[pl.ds(start, tkb), :], qs_ref[:, c0:c0 + ATTN_QC], preferred_element_type=F32)
        if masked:
            s = jnp.where(row_ok, s, -1e30)
        return s, start

    def colmax(s):
        return jnp.max(jnp.max(s.reshape(tkb // 8, 8, ATTN_QC), axis=0), axis=0, keepdims=True)

    for c0 in range(0, nq2, 2 * ATTN_QC):
        cols = (c0, c0 + ATTN_QC)

        def consume(kb, s, m):
            start = kb * tkb
            if not isinstance(start, int):
                start = pl.multiple_of(start, tkb)
            vext = jnp.concatenate([vt_ref[0, :, pl.ds(start, tkb)], ones], axis=0)
            m_out = []
            for i in range(2):
                m_new = jnp.maximum(m[i], colmax(s[i]))
                p = jnp.exp(s[i] - m_new).astype(BF16)
                sl = slice(cols[i], cols[i] + ATTN_QC)
                acc_ref[:, sl] = acc_ref[:, sl] * jnp.exp(m[i] - m_new) + jnp.dot(vext, p, preferred_element_type=F32)
                m_out.append(m_new)
            return tuple(m_out)

        for c in cols:
            acc_ref[:, c:c + ATTN_QC] = jnp.zeros((HEAD_W + ATTN_ONES, ATTN_QC), F32)

        def body(kb, carry):
            s, m = carry
            s_next = tuple(scores(kb + 1, c, False)[0] for c in cols)
            return s_next, consume(kb, s, m)

        m0 = tuple(jnp.full((1, ATTN_QC), -1e30, F32) for _ in cols)
        s_last, m_last = lax.fori_loop(0, nkb - 1, body, (tuple(scores(0, c, True)[0] for c in cols), m0),
                                       unroll=unroll)
        consume(nkb - 1, s_last, m_last)

    lam = (jnp.exp(jnp.sum(lam_ref[0:1, :] * lam_ref[1:2, :], axis=-1, keepdims=True))
           - jnp.exp(jnp.sum(lam_ref[2:3, :] * lam_ref[3:4, :], axis=-1, keepdims=True)) + lam_init)
    o = acc_ref[0:HEAD_W, :] / acc_ref[HEAD_W:HEAD_W + 1, :]
    o = (o[:, :tq] - lam * o[:, tq:]).T
    o = o * lax.rsqrt(jnp.mean(o * o, axis=-1, keepdims=True) + RMS_EPS) * g_ref[...]
    o_ref[...] = (o * (1.0 - lam_init)).astype(o_ref.dtype)


def diff_attention(qlo_t, qhi_t, k, v_t, lam_vecs, norm_g, nb, lp, padf, layer, tq, tkb, unroll):
    m = k.shape[0]
    assert lp % tq == 0 and lp % tkb == 0 and padf < tkb and tq % ATTN_QC == 0
    nq = lp // tq
    lam_init = 0.8 - 0.6 * math.exp(-0.3 * layer)
    qspec = pl.BlockSpec((1, HEAD_W, tq), lambda b, h, qi: (b, h, qi))
    return pl.pallas_call(
        functools.partial(_attn_kernel, tkb=tkb, padf=padf, lam_init=lam_init, unroll=unroll),
        out_shape=jax.ShapeDtypeStruct((m, GROUP_W), BF16),
        grid=(nb, N_HEADS, nq),
        in_specs=[pl.BlockSpec((4, DA_HD), lambda b, h, qi: (0, 0)),
                  pl.BlockSpec((1, HEAD_W), lambda b, h, qi: (0, 0)),
                  qspec, qspec,
                  pl.BlockSpec((lp, HEAD_W), lambda b, h, qi: (b, h)),
                  pl.BlockSpec((1, HEAD_W, lp), lambda b, h, qi: (b, h, 0))],
        out_specs=pl.BlockSpec((tq, HEAD_W), lambda b, h, qi: (b * nq + qi, h)),
        scratch_shapes=[pltpu.VMEM((HEAD_W, 2 * tq), BF16),
                        pltpu.VMEM((HEAD_W + ATTN_ONES, 2 * tq), F32)],
        compiler_params=_cparams("arbitrary", "arbitrary", "arbitrary"),
        name="diff_attention",
    )(lam_vecs, norm_g.reshape(1, HEAD_W), qlo_t, qhi_t, k, v_t)


N_LEVELS = 6


def _scan_constants():
    c = CHUNK
    idx = np.arange(c)
    cs = np.zeros((2, (N_LEVELS + 2) * c, c), np.float32)
    mask = np.zeros((2, N_LEVELS + 2, c, c), np.float32)
    for d in range(2):
        cum = (idx[None, :] <= idx[:, None]) if d == 0 else (idx[None, :] >= idx[:, None])
        cs[d, 0:c] = cum
        for l in range(N_LEVELS):
            s = (c // 2) >> l
            blk = idx // s
            ref = (blk | 1) * s - 1 if d == 0 else (blk | 1) * s
            cs[d, (l + 1) * c:(l + 2) * c] = cum[ref]
            if d == 0:
                mask[d, l] = ((blk[:, None] & 1) == 1) & (blk[None, :] == blk[:, None] - 1)
            else:
                mask[d, l] = ((blk[:, None] & 1) == 0) & (blk[None, :] == blk[:, None] + 1)
        cs[d, (N_LEVELS + 1) * c:] = 1.0
        mask[d, N_LEVELS] = cum
        mask[d, N_LEVELS + 1] = cum & (idx[None, :] != idx[:, None])
    return jnp.asarray(np.concatenate([cs] * 3, axis=2), BF16), jnp.asarray(mask)


def _split3(x):
    hi = x.astype(BF16)
    r = x - hi.astype(F32)
    mid = r.astype(BF16)
    return hi, mid, (r - mid.astype(F32)).astype(BF16)


def _split3_rows(x):
    return jnp.concatenate(_split3(x), axis=0)


def _split3_lanes(x):
    return jnp.concatenate(_split3(x), axis=1)


def _nt(a, b):
    return lax.dot_general(a.astype(BF16), b.astype(BF16), (((1,), (1,)), ((), ())),
                           preferred_element_type=F32)


def _tn(a, b):
    return lax.dot_general(a.astype(BF16), b.astype(BF16), (((0,), (0,)), ((), ())),
                           preferred_element_type=F32)


def _nn(a, b):
    return jnp.dot(a.astype(BF16), b.astype(BF16), preferred_element_type=F32)


def _hgrn_prep_kernel(p_ref, lb_ref, q_ref, v_ref, g_ref, sg_ref):
    w = GROUP_W
    q = p_ref[:, 0:w]
    q_ref[...] = (q * jax.nn.sigmoid(q) * (HEAD_W ** -0.5)).astype(BF16)
    for d in range(2):
        lb = lb_ref[d:d + 1, :]
        f = p_ref[:, (1 + d) * w:(2 + d) * w]
        g_ref[d] = jnp.log(lb + (1.0 - lb) * jax.nn.sigmoid(f))
    v_ref[...] = p_ref[:, 3 * w:4 * w].astype(BF16)
    gate = p_ref[:, 4 * w:5 * w]
    sg_ref[...] = (gate * jax.nn.sigmoid(gate)).astype(BF16)


def hgrn_prep(proj, col_block, lb, tm):
    m = proj.shape[0]
    w = GROUP_W
    ospec = pl.BlockSpec((tm, w), lambda i: (i, 0))
    return pl.pallas_call(
        _hgrn_prep_kernel,
        out_shape=(jax.ShapeDtypeStruct((m, w), BF16), jax.ShapeDtypeStruct((m, w), BF16),
                   jax.ShapeDtypeStruct((2, m, w), F32), jax.ShapeDtypeStruct((m, w), BF16)),
        grid=(m // tm,),
        in_specs=[pl.BlockSpec((tm, 5 * w), lambda i: (i, col_block)),
                  pl.BlockSpec((2, w), lambda i: (0, 0))],
        out_specs=(ospec, ospec, pl.BlockSpec((2, tm, w), lambda i: (0, i, 0)), ospec),
        compiler_params=_cparams("arbitrary"),
        name="hgrn_prep",
    )(proj, lb)


def _head(x, h):
    return x[:, h * HEAD_W:(h + 1) * HEAD_W]


def _chunk_rows(ci, nch, d):
    return pl.multiple_of((ci + d * (nch - 1 - 2 * ci)) * CHUNK, CHUNK)


def _hgrn_scan_kernel(cs_ref, mask_ref, q_ref, v_ref, g_ref, o_ref, s_ref, *, nch):
    c = CHUNK
    d = pl.program_id(1)
    heads = range(N_HEADS)
    chunks = range(nch)

    @pl.when(pl.program_id(2) == 0)
    def _():
        s_ref[...] = jnp.zeros(s_ref.shape, F32)

    eye = mask_ref[0, N_LEVELS] - mask_ref[0, N_LEVELS + 1]
    rows = [_chunk_rows(ci, nch, d) for ci in chunks]
    g = [g_ref[0, pl.ds(r0, c), :] for r0 in rows]
    q = [q_ref[pl.ds(r0, c), :].astype(F32) for r0 in rows]
    v = [v_ref[pl.ds(r0, c), :] for r0 in rows]
    big = [jnp.dot(cs_ref[0], _split3_rows(gi), preferred_element_type=F32) for gi in g]
    b = [x[0:c] for x in big]
    tot = [x[(N_LEVELS + 1) * c:(N_LEVELS + 2) * c] for x in big]
    k = [1.0 - jnp.exp(gi) for gi in g]
    a = [[eye * jnp.sum(_head(q[ci] * k[ci], h), axis=-1, keepdims=True) for h in heads] for ci in chunks]
    q16 = [q_ref[pl.ds(r0, c), :] for r0 in rows]
    k16 = [ki.astype(BF16) for ki in k]
    for l in range(N_LEVELS):
        e = [jnp.exp(-jnp.abs(b[ci] - big[ci][(l + 1) * c:(l + 2) * c])).astype(BF16) for ci in chunks]
        qe = [q16[ci] * e[ci] for ci in chunks]
        ke = [k16[ci] * e[ci] for ci in chunks]
        lm = mask_ref[0, l]
        a = [[a[ci][h] + lm * _nt(_head(qe[ci], h), _head(ke[ci], h)) for h in heads] for ci in chunks]
    o_intra = [[_nn(a[ci][h], _head(v[ci], h)) for h in heads] for ci in chunks]
    qb = [(q[ci] * jnp.exp(b[ci])).astype(BF16) for ci in chunks]
    kd = [(k[ci] * jnp.exp(tot[ci] - b[ci])).astype(BF16) for ci in chunks]
    dec = [jnp.exp(tot[ci][0:1, :]) for ci in chunks]
    s = [s_ref[h] for h in heads]
    for ci in chunks:
        for h in heads:
            o = o_intra[ci][h] + _nt(_head(qb[ci], h), s[h])
            o_ref[0, pl.ds(rows[ci], c), h * HEAD_W:(h + 1) * HEAD_W] = o
        s = [s[h] * _head(dec[ci], h) + _tn(_head(v[ci], h), _head(kd[ci], h)) for h in heads]
    for h in heads:
        s_ref[h] = s[h]


def _scan_row_block(nblk):
    def row(b, d, t):
        return b * nblk + t + d * (nblk - 1 - 2 * t)
    return row


def hgrn_scan(q, v, g, consts, nb, lp, tb):
    m, w = q.shape
    cs, mask = consts
    row = _scan_row_block(lp // tb)
    return pl.pallas_call(
        functools.partial(_hgrn_scan_kernel, nch=tb // CHUNK),
        out_shape=jax.ShapeDtypeStruct((2, m, w), F32),
        grid=(nb, 2, lp // tb),
        in_specs=[pl.BlockSpec((1,) + cs.shape[1:], lambda b, d, t: (d, 0, 0)),
                  pl.BlockSpec((1,) + mask.shape[1:], lambda b, d, t: (d, 0, 0, 0)),
                  pl.BlockSpec((tb, w), lambda b, d, t: (row(b, d, t), 0)),
                  pl.BlockSpec((tb, w), lambda b, d, t: (row(b, d, t), 0)),
                  pl.BlockSpec((1, tb, w), lambda b, d, t: (d, row(b, d, t), 0))],
        out_specs=pl.BlockSpec((1, tb, w), lambda b, d, t: (d, row(b, d, t), 0)),
        scratch_shapes=[pltpu.VMEM((N_HEADS, HEAD_W, HEAD_W), F32)],
        compiler_params=_cparams("arbitrary", "arbitrary", "arbitrary"),
        name="hgrn_scan",
    )(cs, mask, q, v, g)


def _gated_norm_kernel(o_ref, sg_ref, g_ref, out_ref):
    o = o_ref[0] + o_ref[1]
    gain = g_ref[...]
    for h in range(N_HEADS):
        sl = slice(h * HEAD_W, (h + 1) * HEAD_W)
        oh = o[:, sl]
        y = oh * lax.rsqrt(jnp.mean(oh * oh, axis=-1, keepdims=True) + RMS_EPS) * gain
        out_ref[:, sl] = (y * sg_ref[:, sl].astype(F32)).astype(out_ref.dtype)


def gated_norm(o2, sgate, norm_g, tm):
    m = sgate.shape[0]
    w = GROUP_W
    return pl.pallas_call(
        _gated_norm_kernel,
        out_shape=jax.ShapeDtypeStruct((m, w), BF16),
        grid=(m // tm,),
        in_specs=[pl.BlockSpec((2, tm, w), lambda i: (0, i, 0)),
                  pl.BlockSpec((tm, w), lambda i: (i, 0)),
                  pl.BlockSpec((1, HEAD_W), lambda i: (0, 0))],
        out_specs=pl.BlockSpec((tm, w), lambda i: (i, 0)),
        compiler_params=_cparams("arbitrary"),
        name="gated_rms_norm",
    )(o2, sgate, norm_g.reshape(1, HEAD_W))


GATE_LANES = 128


def _softplus(x):
    return jnp.maximum(x, 0.0) + jnp.log(1.0 + jnp.exp(-jnp.abs(x)))


def _gdn_prep_kernel(p_ref, pp_ref, pn_ref, cw_ref, acoef_ref, dtb_ref,
                     q_ref, k_ref, v_ref, sz_ref, gates_ref, *, tiles_per_seq, padf):
    w = GROUP_W
    i = pl.program_id(0)
    tm = p_ref.shape[0]
    nmask = jnp.where(i == pl.num_programs(0) - 1, 0.0, 1.0)
    x = _conv3_rows(p_ref[:, 0:3 * w], pp_ref[HALO - 1:HALO, :], pn_ref[0:1, :] * nmask, cw_ref[...])
    x = x * jax.nn.sigmoid(x)
    valid = _valid_rows(i, tiles_per_seq, tm, padf)
    for h in range(N_HEADS):
        sl = slice(h * HEAD_W, (h + 1) * HEAD_W)
        qh = x[:, sl]
        q_ref[:, sl] = (qh * lax.rsqrt(jnp.sum(qh * qh, axis=-1, keepdims=True) + 1e-6)
                        * (HEAD_W ** -0.5)).astype(BF16)
        kh = x[:, w + h * HEAD_W:w + (h + 1) * HEAD_W]
        k_ref[:, sl] = (kh * lax.rsqrt(jnp.sum(kh * kh, axis=-1, keepdims=True) + 1e-6)).astype(BF16)
    v_ref[...] = jnp.where(valid, x[:, 2 * w:3 * w], 0.0).astype(BF16)
    z = p_ref[:, 3 * w:4 * w]
    sz_ref[...] = (z * jax.nn.sigmoid(z)).astype(BF16)
    gt = p_ref[:, 4 * w:4 * w + GATE_LANES]
    lane = lax.broadcasted_iota(jnp.int32, (1, GATE_LANES), 1)
    decay = -acoef_ref[...] * _softplus(gt + dtb_ref[...])
    gates_ref[...] = jnp.where(lane < 2 * N_HEADS, jax.nn.sigmoid(gt), decay)


def gdn_prep(pd, conv_w, a_log, dt_bias, lp, padf, tm):
    m = pd.shape[0]
    w = GROUP_W
    zeros8 = jnp.zeros((2 * N_HEADS,), F32)
    padl = jnp.zeros((GATE_LANES - 4 * N_HEADS,), F32)
    acoef = jnp.concatenate([zeros8, jnp.exp(a_log).reshape(-1), padl]).reshape(1, GATE_LANES)
    dtb = jnp.concatenate([zeros8, dt_bias.reshape(-1), padl]).reshape(1, GATE_LANES)
    pspec, nspec = _halo_specs(tm, 3 * w, m)
    ospec = pl.BlockSpec((tm, w), lambda i: (i, 0))
    return pl.pallas_call(
        functools.partial(_gdn_prep_kernel, tiles_per_seq=lp // tm, padf=padf),
        out_shape=tuple(jax.ShapeDtypeStruct((m, w), BF16) for _ in range(4))
        + (jax.ShapeDtypeStruct((m, GATE_LANES), F32),),
        grid=(m // tm,),
        in_specs=[pl.BlockSpec((tm, pd.shape[1]), lambda i: (i, 0)), pspec, nspec,
                  pl.BlockSpec((3, 3 * w), lambda i: (0, 0)),
                  pl.BlockSpec((1, GATE_LANES), lambda i: (0, 0)),
                  pl.BlockSpec((1, GATE_LANES), lambda i: (0, 0))],
        out_specs=(ospec, ospec, ospec, ospec, pl.BlockSpec((tm, GATE_LANES), lambda i: (i, 0))),
        compiler_params=_cparams("arbitrary"),
        name="gdn_prep",
    )(pd, pd, pd, conv_w, acoef, dtb)


def _gate_selectors():
    sel = np.zeros((2, GATE_LANES, 2 * GROUP_W), np.float32)
    for d in range(2):
        for h in range(N_HEADS):
            sel[d, N_HEADS * d + h, h * HEAD_W:(h + 1) * HEAD_W] = 1.0
            sel[d, 2 * N_HEADS + N_HEADS * d + h, GROUP_W + h * HEAD_W:GROUP_W + (h + 1) * HEAD_W] = 1.0
    return jnp.asarray(np.concatenate([sel] * 3, axis=1), BF16)


def _gdn_scan_kernel(cs_ref, mask_ref, sel_ref, q_ref, k_ref, v_ref, gt_ref, o_ref, s_ref, *, nch):
    c = CHUNK
    w = GROUP_W
    d = pl.program_id(1)
    heads = range(N_HEADS)
    chunks = range(nch)
    items = [(ci, h) for ci in chunks for h in heads]

    @pl.when(pl.program_id(2) == 0)
    def _():
        s_ref[...] = jnp.zeros(s_ref.shape, F32)

    incl = mask_ref[0, N_LEVELS]
    strict = mask_ref[0, N_LEVELS + 1]
    eye = incl - strict
    cum_tot = jnp.concatenate([cs_ref[0, 0:c, :], cs_ref[0, (N_LEVELS + 1) * c:(N_LEVELS + 2) * c, :]], axis=0)
    ones3 = cs_ref[0, (N_LEVELS + 1) * c:(N_LEVELS + 2) * c, :]
    rows = [_chunk_rows(ci, nch, d) for ci in chunks]
    q = [q_ref[pl.ds(r0, c), :].astype(F32) for r0 in rows]
    k = [k_ref[pl.ds(r0, c), :].astype(F32) for r0 in rows]
    v = [v_ref[pl.ds(r0, c), :].astype(F32) for r0 in rows]
    bg = [jnp.dot(_split3_lanes(gt_ref[pl.ds(r0, c), :]), sel_ref[0], preferred_element_type=F32)
          for r0 in rows]
    beta = [x[:, 0:w] for x in bg]
    cb = [jnp.dot(cum_tot, _split3_rows(x[:, w:2 * w]), preferred_element_type=F32) for x in bg]
    bb = [x[0:c] for x in cb]
    tot = [x[c:2 * c] for x in cb]
    bc = {(ci, h): bb[ci][:, h * HEAD_W:h * HEAD_W + c] for ci, h in items}
    br = {it: jnp.dot(ones3, _split3_rows(eye * bc[it]), preferred_element_type=F32) for it in items}
    dec = {it: incl * jnp.exp(jnp.minimum(bc[it] - br[it], 0.0)) for it in items}
    kb = [k[ci] * beta[ci] for ci in chunks]
    n = {(ci, h): strict * _nt(_head(kb[ci], h), _head(k[ci], h)) * dec[(ci, h)] for ci, h in items}
    t = {it: eye - n[it] for it in items}
    pw = {it: _nn(n[it], n[it]) for it in items}
    for _ in range(N_LEVELS - 2):
        t = {it: t[it] + _nn(t[it], pw[it]) for it in items}
        pw = {it: _nn(pw[it], pw[it]) for it in items}
    t = {it: t[it] + _nn(t[it], pw[it]) for it in items}
    eb = [jnp.exp(x) for x in bb]
    rhs_u = [v[ci] * beta[ci] for ci in chunks]
    rhs_w = [kb[ci] * eb[ci] for ci in chunks]
    uw = {(ci, h): _nn(t[(ci, h)], jnp.concatenate([_head(rhs_u[ci], h), _head(rhs_w[ci], h)], axis=1))
          for ci, h in items}
    qk = {(ci, h): _nt(_head(q[ci], h), _head(k[ci], h)) * dec[(ci, h)] for ci, h in items}
    qe = [(q[ci] * eb[ci]).astype(BF16) for ci in chunks]
    kd = [(k[ci] * jnp.exp(tot[ci] - bb[ci])).astype(BF16) for ci in chunks]
    sdec = [jnp.exp(tot[ci][0:1, :]) for ci in chunks]
    s = [s_ref[h] for h in heads]
    for ci in chunks:
        wq = {h: _nn(jnp.concatenate([uw[(ci, h)][:, HEAD_W:].astype(BF16), _head(qe[ci], h)], axis=0), s[h])
              for h in heads}
        vnew = {h: uw[(ci, h)][:, 0:HEAD_W] - wq[h][0:c] for h in heads}
        for h in heads:
            o_ref[0, pl.ds(rows[ci], c), h * HEAD_W:(h + 1) * HEAD_W] = wq[h][c:2 * c] + _nn(qk[(ci, h)], vnew[h])
        s = [s[h] * _head(sdec[ci], h) + _tn(_head(kd[ci], h), vnew[h]) for h in heads]
    for h in heads:
        s_ref[h] = s[h]


def gdn_scan(q, k, v, gates, consts, sel, nb, lp, tb):
    m, w = q.shape
    cs, mask = consts
    row = _scan_row_block(lp // tb)
    rspec = pl.BlockSpec((tb, w), lambda b, d, t: (row(b, d, t), 0))
    return pl.pallas_call(
        functools.partial(_gdn_scan_kernel, nch=tb // CHUNK),
        out_shape=jax.ShapeDtypeStruct((2, m, w), F32),
        grid=(nb, 2, lp // tb),
        in_specs=[pl.BlockSpec((1,) + cs.shape[1:], lambda b, d, t: (d, 0, 0)),
                  pl.BlockSpec((1,) + mask.shape[1:], lambda b, d, t: (d, 0, 0, 0)),
                  pl.BlockSpec((1,) + sel.shape[1:], lambda b, d, t: (d, 0, 0)),
                  rspec, rspec, rspec,
                  pl.BlockSpec((tb, GATE_LANES), lambda b, d, t: (row(b, d, t), 0))],
        out_specs=pl.BlockSpec((1, tb, w), lambda b, d, t: (d, row(b, d, t), 0)),
        scratch_shapes=[pltpu.VMEM((N_HEADS, HEAD_W, HEAD_W), F32)],
        compiler_params=_cparams("arbitrary", "arbitrary", "arbitrary"),
        name="gdn_scan",
    )(cs, mask, sel, q, k, v, gates)


FFT_N2 = 256
FFT_TS = 8


def _fft_n1(lp, l_real):
    need = -(-(2 * l_real - 1) // FFT_N2)
    return max(-(-need // 8) * 8, -(-(lp // FFT_N2) // 8) * 8)


def _outer_dft(n1, k_rows, inverse, n_total):
    kp = -(-k_rows // 16) * 16
    a = np.arange(n1)[:, None] * np.arange(kp)[None, :]
    ang = 2.0 * np.pi * (a % n1) / n1
    c, s = np.cos(ang), np.sin(ang)
    live = (np.arange(kp) < k_rows)[None, :]
    c, s = c * live, s * live
    if not inverse:
        return jnp.asarray(np.concatenate([c, -s], axis=0), BF16)
    return jnp.asarray(np.concatenate([c.T, -s.T], axis=1) / n_total, F32)


def inner_dft(n1):
    n2 = FFT_N2
    n = n1 * n2
    k = jnp.arange(n1, dtype=jnp.int32)[:, None, None] + n1 * jnp.arange(n2, dtype=jnp.int32)[None, :, None]
    r = (k * jnp.arange(n2, dtype=jnp.int32)[None, None, :]) % n
    ang = r.astype(F32) * (2.0 * math.pi / n)
    gr, gi = jnp.cos(ang), -jnp.sin(ang)
    gg = jnp.concatenate([jnp.concatenate([gr, -gi], axis=2), jnp.concatenate([gi, gr], axis=2)], axis=1)
    return gg.astype(BF16), jnp.swapaxes(gg, 1, 2).astype(BF16)


def _hyena_prep_kernel(p_ref, pp_ref, pn_ref, cw_ref, cb_ref, x0_ref, z_ref, *, tiles_per_seq, padf):
    w = GROUP_W
    i = pl.program_id(0)
    tm = p_ref.shape[0]
    nmask = jnp.where(i == pl.num_programs(0) - 1, 0.0, 1.0)
    u = _conv3_rows(p_ref[...], pp_ref[HALO - 1:HALO, :], pn_ref[0:1, :] * nmask, cw_ref[...]) + cb_ref[...]
    valid = _valid_rows(i, tiles_per_seq, tm, padf)
    x0_ref[...] = u[:, 0:w]
    z_ref[...] = jnp.where(valid, u[:, w:2 * w] * u[:, 2 * w:3 * w], 0.0)


def hyena_prep(pb, conv_w, conv_b, lp, padf, tm):
    m, w3 = pb.shape
    w = GROUP_W
    pspec, nspec = _halo_specs(tm, w3, m)
    ospec = pl.BlockSpec((tm, w), lambda i: (i, 0))
    return pl.pallas_call(
        functools.partial(_hyena_prep_kernel, tiles_per_seq=lp // tm, padf=padf),
        out_shape=(jax.ShapeDtypeStruct((m, w), F32), jax.ShapeDtypeStruct((m, w), F32)),
        grid=(m // tm,),
        in_specs=[pl.BlockSpec((tm, w3), lambda i: (i, 0)), pspec, nspec,
                  pl.BlockSpec((3, w3), lambda i: (0, 0)), pl.BlockSpec((1, w3), lambda i: (0, 0))],
        out_specs=(ospec, ospec),
        compiler_params=_cparams("arbitrary"),
        name="hyena_prep",
    )(pb, pb, pb, conv_w, conv_b.reshape(1, w3))


def _hyena_filter_kernel(w1t_ref, w1c_ref, w1s_ref, b1_ref, f1_ref, w2_ref, b2_ref, f2_ref, w3_ref, dec_ref,
                         filt_ref, asum_ref, *, l_real):
    i = pl.program_id(0)
    tm = filt_ref.shape[0]
    w = GROUP_W
    row = i * tm + lax.broadcasted_iota(jnp.int32, (tm, 1), 0)
    rf = row.astype(F32)
    t = rf * (1.0 / (l_real - 1))
    band = lax.broadcasted_iota(jnp.int32, (1, HY_BANDS), 1).astype(F32)
    bands = 1e-4 + band * ((HY_BANDS - 1 - 1e-4) / (HY_BANDS - 1))
    ang = ((2.0 * math.pi / l_real) * rf) * bands

    def hdot(a, b):
        return jnp.dot(a, b, precision=HIGHEST, preferred_element_type=F32)

    pre = t * w1t_ref[...] + hdot(jnp.cos(ang), w1c_ref[...]) - hdot(jnp.sin(ang), w1s_ref[...])
    hid = jnp.sin(f1_ref[...] * (pre + b1_ref[...]))
    hid = jnp.sin(f2_ref[...] * (hdot(hid, w2_ref[...]) + b2_ref[...]))
    filt = hdot(hid, w3_ref[...])
    window = jnp.exp(-t * jnp.abs(dec_ref[...])) + HY_SHIFT
    filt = jnp.where(row < l_real, filt * window, 0.0)

    @pl.when(i == 0)
    def _():
        asum_ref[...] = jnp.zeros(asum_ref.shape, F32)

    asum_ref[...] += jnp.sum(jnp.abs(filt), axis=0, keepdims=True)
    lane = lax.broadcasted_iota(jnp.int32, (1, 2 * w), 1)
    filt_ref[...] = jnp.where((row == 0) & (lane >= w), 0.0, filt)


def hyena_filter(w1, b1, f1, w2, b2, f2, w3, decay, lp, l_real, tm):
    w = GROUP_W
    nf = w1.shape[1]
    small = lambda a: pl.BlockSpec(a.shape, lambda i: (0,) * a.ndim)
    args = (w1[0:1], w1[1:1 + HY_BANDS], w1[1 + HY_BANDS:], b1.reshape(1, nf), f1.reshape(1, nf),
            w2, b2.reshape(1, nf), f2.reshape(1, nf), w3, decay.reshape(1, 2 * w))
    return pl.pallas_call(
        functools.partial(_hyena_filter_kernel, l_real=l_real),
        out_shape=(jax.ShapeDtypeStruct((lp, 2 * w), F32), jax.ShapeDtypeStruct((1, 2 * w), F32)),
        grid=(lp // tm,),
        in_specs=[small(a) for a in args],
        out_specs=(pl.BlockSpec((tm, 2 * w), lambda i: (i, 0)), pl.BlockSpec((1, 2 * w), lambda i: (0, 0))),
        compiler_params=_cparams("arbitrary"),
        name="hyena_filter",
    )(*args)


def _fft_outer_kernel(f_ref, x_ref, o_ref, pad_ref):
    k = x_ref.shape[1]
    pad_ref[...] = jnp.zeros(pad_ref.shape, F32)
    for s in range(x_ref.shape[2]):
        pad_ref[0:k, :] = x_ref[0, :, s, :]
        o_ref[0, :, s, :] = jnp.dot(f_ref[...], pad_ref[...].astype(BF16), preferred_element_type=F32)


def fft_outer(x, fmat):
    b, k, n2, c = x.shape
    rows, kp = fmat.shape
    ts = FFT_TS
    return pl.pallas_call(
        _fft_outer_kernel,
        out_shape=jax.ShapeDtypeStruct((b, rows, n2, c), F32),
        grid=(b, n2 // ts),
        in_specs=[pl.BlockSpec((rows, kp), lambda bi, j: (0, 0)),
                  pl.BlockSpec((1, k, ts, c), lambda bi, j: (bi, 0, j, 0))],
        out_specs=pl.BlockSpec((1, rows, ts, c), lambda bi, j: (bi, 0, j, 0)),
        scratch_shapes=[pltpu.VMEM((kp, c), F32)],
        compiler_params=_cparams("arbitrary", "arbitrary"),
        name="fft_outer",
    )(fmat, x)


def _filter_spectrum_kernel(gg_ref, a_ref, asum_ref, o_ref):
    w = GROUP_W
    n2 = FFT_N2
    a = jnp.concatenate([a_ref[0, 0], a_ref[1, 0]], axis=0).astype(BF16)
    x = jnp.dot(gg_ref[0], a, preferred_element_type=F32)
    s = 1.0 / (asum_ref[:, 0:w] + asum_ref[:, w:2 * w] + 1e-6)
    o_ref[0, 0] = (x[0:n2, 0:w] + x[0:n2, w:2 * w]) * s
    o_ref[0, 1] = (x[n2:2 * n2, 0:w] - x[n2:2 * n2, w:2 * w]) * s


def filter_spectrum(a, gg, asum):
    _, n1, n2, w2 = a.shape
    w = w2 // 2
    return pl.pallas_call(
        _filter_spectrum_kernel,
        out_shape=jax.ShapeDtypeStruct((n1, 2, n2, w), F32),
        grid=(n1,),
        in_specs=[pl.BlockSpec((1, 2 * n2, 2 * n2), lambda k: (k, 0, 0)),
                  pl.BlockSpec((2, 1, n2, w2), lambda k: (0, k, 0, 0)),
                  pl.BlockSpec((1, w2), lambda k: (0, 0))],
        out_specs=pl.BlockSpec((1, 2, n2, w), lambda k: (k, 0, 0, 0)),
        compiler_params=_cparams("arbitrary"),
        name="hyena_filter_spectrum",
    )(gg, a, asum)


def _fft_mid_kernel(gg_ref, ggt_ref, kf_ref, a_ref, o_ref):
    n2 = FFT_N2
    a = jnp.concatenate([a_ref[0, 0, 0], a_ref[0, 1, 0]], axis=0).astype(BF16)
    x = jnp.dot(gg_ref[0], a, preferred_element_type=F32)
    xr, xi = x[0:n2], x[n2:2 * n2]
    kr, ki = kf_ref[0, 0], kf_ref[0, 1]
    y = jnp.concatenate([xr * kr - xi * ki, xr * ki + xi * kr], axis=0).astype(BF16)
    bm = jnp.dot(ggt_ref[0], y, preferred_element_type=F32)
    o_ref[0, 0, 0] = bm[0:n2]
    o_ref[0, 1, 0] = bm[n2:2 * n2]


def fft_mid(a, gg, ggt, kf):
    b, _, n1, n2, w = a.shape
    aspec = pl.BlockSpec((1, 2, 1, n2, w), lambda k, bi: (bi, 0, k, 0, 0))
    gspec = pl.BlockSpec((1, 2 * n2, 2 * n2), lambda k, bi: (k, 0, 0))
    return pl.pallas_call(
        _fft_mid_kernel,
        out_shape=jax.ShapeDtypeStruct(a.shape, F32),
        grid=(n1, b),
        in_specs=[gspec, gspec, pl.BlockSpec((1, 2, n2, w), lambda k, bi: (k, 0, 0, 0)), aspec],
        out_specs=aspec,
        compiler_params=_cparams("arbitrary", "arbitrary"),
        name="fft_inner_conv",
    )(gg, ggt, kf, a)


def _fft_final_kernel(f_ref, b_ref, x0_ref, z_ref, d_ref, o_ref):
    k = o_ref.shape[1]
    for s in range(o_ref.shape[2]):
        y = jnp.dot(f_ref[...], b_ref[0, :, s, :], precision=HIGHEST, preferred_element_type=F32)
        o_ref[0, :, s, :] = x0_ref[0, :, s, :] * (y[0:k] + z_ref[0, :, s, :] * d_ref[...])


def fft_final(bm, finv, x0, z, d):
    b, rows, n2, c = bm.shape
    k = x0.shape[1]
    kp = finv.shape[0]
    ts = FFT_TS
    xspec = pl.BlockSpec((1, k, ts, c), lambda bi, j: (bi, 0, j, 0))
    return pl.pallas_call(
        _fft_final_kernel,
        out_shape=jax.ShapeDtypeStruct((b, k, n2, c), F32),
        grid=(b, n2 // ts),
        in_specs=[pl.BlockSpec((kp, rows), lambda bi, j: (0, 0)),
                  pl.BlockSpec((1, rows, ts, c), lambda bi, j: (bi, 0, j, 0)),
                  xspec, xspec, pl.BlockSpec((1, c), lambda bi, j: (0, 0))],
        out_specs=xspec,
        compiler_params=_cparams("arbitrary", "arbitrary"),
        name="fft_outer_inverse_gate",
    )(finv, bm, x0, z, d)


def hyena_mixer(pb, hp, nb, lp, padf, l_real, tm):
    w = GROUP_W
    n2 = FFT_N2
    k1 = lp // n2
    n1 = _fft_n1(lp, l_real)
    n_total = n1 * n2
    ffwd = _outer_dft(n1, k1, False, n_total)
    finv = _outer_dft(n1, k1, True, n_total)
    gg, ggt = hp['gg'], hp['ggt']
    filt, asum = hyena_filter(hp['w1'], hp['b1'], hp['f1'], hp['w2'], hp['b2'], hp['f2'], hp['w3'],
                              hp['decay'], lp, l_real, tm)
    fa = fft_outer(filt.reshape(1, k1, n2, 2 * w), ffwd)
    kf = filter_spectrum(fa.reshape(2, n1, n2, 2 * w), gg, asum)
    x0, z = hyena_prep(pb, hp['conv_w'], hp['conv_b'], lp, padf, tm)
    z4 = z.reshape(nb, k1, n2, w)
    a = fft_outer(z4, ffwd)
    bm = fft_mid(a.reshape(nb, 2, n1, n2, w), gg, ggt, kf)
    out = fft_final(bm.reshape(nb, 2 * n1, n2, w), finv, x0.reshape(nb, k1, n2, w), z4, hp['d'].reshape(1, w))
    return out.reshape(nb * lp, w)


TM = 640
ATTN_TQ = 1280
ATTN_TKB = 256
ATTN_UNROLL = 16
SCAN_TB = 640
FFN_TN = 512
FFN_DOWN_TM = 320


def kernel(x_prompt, x_sample, meta, emb_ln_g, emb_ln_b, w_in, lam_q1, lam_k1, lam_q2, lam_k2, attn_norm_g, hy_conv_w, hy_conv_b, hy_w1, hy_b1, hy_f1, hy_w2, hy_b2, hy_f2, hy_w3, hy_decay, hy_d, hg_lb, hg_norm_g, gdn_conv_w, gdn_a_log, gdn_dt_bias, gdn_norm_g, w_out, ln1_g, ln1_b, w_up, ffn_conv_w, ffn_conv_b, w_down, ln2_g, ln2_b):
    n_prompt = x_prompt.shape[0]
    x = jnp.concatenate([x_prompt, x_sample], axis=0)
    nb, seq, d = x.shape
    padf = ROW_ALIGN - N_META
    lp = seq + ROW_ALIGN
    l_real = seq + N_META
    m = nb * lp
    w = GROUP_W

    h, hb = (a.reshape(m, d) for a in embed(x, meta, emb_ln_g, emb_ln_b, padf))
    tables = rope_tables(lp, padf)
    consts = _scan_constants()
    sel = _gate_selectors()
    gg, ggt = inner_dft(_fft_n1(lp, l_real))
    sm = jax.nn.softmax(hg_lb, axis=0)
    lb_all = jnp.cumsum(sm, axis=0) - sm[0]

    for l in range(DEPTH):
        wl = w_in[l].astype(BF16)
        wd = jnp.pad(wl[:, 11 * w:], ((0, 0), (0, 4 * w + GATE_LANES - (wl.shape[1] - 11 * w))))
        pa = matmul(hb, wl[:, 0:3 * w], TM, 3 * w)
        pb = matmul(hb, wl[:, 3 * w:6 * w], TM, 3 * w)
        pc = matmul(hb, wl[:, 6 * w:11 * w], TM, 5 * w // 2)
        pd = matmul(hb, wd, TM, wd.shape[1])

        qlo, qhi, ka, va = attn_prep(pa, tables, nb, lp, TM)
        lam_vecs = jnp.stack([lam_q1[l], lam_k1[l], lam_q2[l], lam_k2[l]])
        oa = diff_attention(qlo, qhi, ka, va, lam_vecs, attn_norm_g[l], nb, lp, padf, l,
                            ATTN_TQ, ATTN_TKB, ATTN_UNROLL)

        hp = dict(gg=gg, ggt=ggt, w1=hy_w1[l], b1=hy_b1[l], f1=hy_f1[l], w2=hy_w2[l], b2=hy_b2[l],
                  f2=hy_f2[l], w3=hy_w3[l], decay=hy_decay[l], d=hy_d[l], conv_w=hy_conv_w[l],
                  conv_b=hy_conv_b[l])
        ob = hyena_mixer(pb, hp, nb, lp, padf, l_real, TM)

        qc, vc, gc, sgc = hgrn_prep(pc, 0, lb_all[l], TM)
        oc = gated_norm(hgrn_scan(qc, vc, gc, consts, nb, lp, SCAN_TB), sgc, hg_norm_g[l], TM)

        qd, kd, vd, szd, gates = gdn_prep(pd, gdn_conv_w[l], gdn_a_log[l], gdn_dt_bias[l], lp, padf, TM)
        od = gated_norm(gdn_scan(qd, kd, vd, gates, consts, sel, nb, lp, SCAN_TB), szd, gdn_norm_g[l], TM)

        h, hb = proj_residual_ln([oa, ob, oc, od], w_out[l].astype(BF16), h, ln1_g[l], ln1_b[l], lp, padf, TM)
        act = ffn_up(hb, w_up[l].astype(BF16), ffn_conv_w[l], ffn_conv_b[l], TM, FFN_TN)
        h, hb = ffn_down_ln(act, w_down[l].astype(BF16), h, ln2_g[l], ln2_b[l], lp, padf, FFN_DOWN_TM)

    y = h.reshape(nb, lp, d)[:, ROW_ALIGN:]
    return (y[:n_prompt], y[n_prompt:])
```

```python
import functools
import math

import jax
import jax.numpy as jnp
import numpy as np
from jax import lax
from jax.experimental import pallas as pl
from jax.experimental.pallas import tpu as pltpu

D_MODEL = 2048
DEPTH = 4
N_META = 16
GROUP_W = 512
N_HEADS = 4
HEAD_W = 128
DA_HD = 64
ROT_DIM = 16
ROPE_THETA = 500000.0
HY_BANDS = 16
HY_SHIFT = 0.05
CHUNK = 64
D_FF = 5632
ALPHA = (2.0 * DEPTH) ** 0.25
LN_EPS = 1e-5
RMS_EPS = 1e-6
F32 = jnp.float32
BF16 = jnp.bfloat16
HIGHEST = lax.Precision.HIGHEST

ROW_ALIGN = 256
VMEM_LIMIT = 56 * 1024 * 1024


def _cparams(*sem):
    return pltpu.CompilerParams(dimension_semantics=sem, vmem_limit_bytes=VMEM_LIMIT)


def _valid_rows(tile_idx, tiles_per_seq, tm, padf):
    base = (tile_idx % tiles_per_seq) * tm
    return (base + lax.broadcasted_iota(jnp.int32, (tm, 1), 0)) >= padf


def _ln_rows(y, g, b):
    mu = jnp.mean(y, axis=-1, keepdims=True)
    d = y - mu
    var = jnp.mean(d * d, axis=-1, keepdims=True)
    return d * lax.rsqrt(var + LN_EPS) * g + b


def _embed_kernel(x_ref, meta_ref, g_ref, b_ref, o_ref, ob_ref, *, padf):
    t = pl.program_id(1)
    g = g_ref[...]
    b = b_ref[...]

    @pl.when(t == 0)
    def _():
        y = jnp.concatenate([jnp.zeros((padf, o_ref.shape[2]), F32), _ln_rows(meta_ref[...], g, b)], axis=0)
        o_ref[0] = y
        ob_ref[0] = y.astype(BF16)

    @pl.when(t > 0)
    def _():
        y = _ln_rows(x_ref[0], g, b)
        o_ref[0] = y
        ob_ref[0] = y.astype(BF16)


def embed(x, meta, g, b, padf):
    nb, s, d = x.shape
    tm = ROW_ALIGN
    assert padf + N_META == tm and s % tm == 0
    lp = s + tm
    ospec = pl.BlockSpec((1, tm, d), lambda bi, t: (bi, t, 0))
    return pl.pallas_call(
        functools.partial(_embed_kernel, padf=padf),
        out_shape=(jax.ShapeDtypeStruct((nb, lp, d), F32), jax.ShapeDtypeStruct((nb, lp, d), BF16)),
        grid=(nb, lp // tm),
        in_specs=[pl.BlockSpec((1, tm, d), lambda bi, t: (bi, jnp.maximum(t - 1, 0), 0)),
                  pl.BlockSpec((N_META, d), lambda bi, t: (0, 0)),
                  pl.BlockSpec((1, d), lambda bi, t: (0, 0)),
                  pl.BlockSpec((1, d), lambda bi, t: (0, 0))],
        out_specs=(ospec, ospec),
        compiler_params=_cparams("arbitrary", "arbitrary"),
        name="embed_ln",
    )(x, meta, g.reshape(1, d), b.reshape(1, d))


def _mm_kernel(x_ref, w_ref, o_ref):
    o_ref[...] = jnp.dot(x_ref[...], w_ref[...], preferred_element_type=F32).astype(o_ref.dtype)


def matmul(x, w, tm, tn, out_dtype=F32):
    m, k = x.shape
    n = w.shape[1]
    assert m % tm == 0 and n % tn == 0
    return pl.pallas_call(
        _mm_kernel,
        out_shape=jax.ShapeDtypeStruct((m, n), out_dtype),
        grid=(n // tn, m // tm),
        in_specs=[pl.BlockSpec((tm, k), lambda j, i: (i, 0)),
                  pl.BlockSpec((k, tn), lambda j, i: (0, j))],
        out_specs=pl.BlockSpec((tm, tn), lambda j, i: (i, j)),
        compiler_params=_cparams("arbitrary", "arbitrary"),
        name="dense_matmul",
    )(x, w)


def _proj_ln_kernel(*refs, n_in, tiles_per_seq, padf):
    o_refs = refs[:n_in]
    w_ref, h_ref, g_ref, b_ref, out_ref, outb_ref = refs[n_in:]
    tm = h_ref.shape[0]
    acc = None
    off = 0
    for r in o_refs:
        kw = r.shape[1]
        part = jnp.dot(r[...].astype(BF16), w_ref[off:off + kw, :], preferred_element_type=F32)
        acc = part if acc is None else acc + part
        off += kw
    y = _ln_rows(ALPHA * h_ref[...] + acc, g_ref[...], b_ref[...])
    valid = _valid_rows(pl.program_id(0), tiles_per_seq, tm, padf)
    y = jnp.where(valid, y, 0.0)
    out_ref[...] = y
    outb_ref[...] = y.astype(BF16)


def proj_residual_ln(parts, w, h, g, b, lp, padf, tm):
    m, d = h.shape
    assert m % tm == 0 and lp % tm == 0
    n_in = len(parts)
    in_specs = [pl.BlockSpec((tm, p.shape[1]), lambda i: (i, 0)) for p in parts]
    in_specs += [pl.BlockSpec(w.shape, lambda i: (0, 0)),
                 pl.BlockSpec((tm, d), lambda i: (i, 0)),
                 pl.BlockSpec((1, d), lambda i: (0, 0)),
                 pl.BlockSpec((1, d), lambda i: (0, 0))]
    ospec = pl.BlockSpec((tm, d), lambda i: (i, 0))
    return pl.pallas_call(
        functools.partial(_proj_ln_kernel, n_in=n_in, tiles_per_seq=lp // tm, padf=padf),
        out_shape=(jax.ShapeDtypeStruct((m, d), F32), jax.ShapeDtypeStruct((m, d), BF16)),
        grid=(m // tm,),
        in_specs=in_specs,
        out_specs=(ospec, ospec),
        compiler_params=_cparams("arbitrary"),
        name="proj_residual_ln",
    )(*parts, w, h, g.reshape(1, d), b.reshape(1, d))


HALO = 8
HALO_BF16 = 16


def _halo_specs(tm, width, m, col_block=0, halo=HALO):
    nblk = m // halo
    per = tm // halo

    def prev_map(*idx):
        i = idx[-1]
        return (jnp.maximum(i * per - 1, 0), col_block)

    def next_map(*idx):
        i = idx[-1]
        return (jnp.minimum((i + 1) * per, nblk - 1), col_block)

    return (pl.BlockSpec((halo, width), prev_map), pl.BlockSpec((halo, width), next_map))


def _conv3_rows(pm, prev_row, next_row, cw):
    tm = pm.shape[0]
    rid = lax.broadcasted_iota(jnp.int32, (tm, 1), 0)
    down = jnp.where(rid == 0, prev_row, pltpu.roll(pm, 1, 0))
    up = jnp.where(rid == tm - 1, next_row, pltpu.roll(pm, tm - 1, 0))
    return down * cw[0:1, :] + pm * cw[1:2, :] + up * cw[2:3, :]


def _ffn_up_kernel(x_ref, xp_ref, xn_ref, wg_ref, wu_ref, cwg_ref, cwu_ref, cbg_ref, cbu_ref, o_ref):
    i = pl.program_id(1)
    last = pl.num_programs(1) - 1
    x = x_ref[...]
    hr = xp_ref.shape[0]
    halo = jnp.concatenate([xp_ref[...], xn_ref[...]], axis=0)
    nmask = jnp.where(i == last, 0.0, 1.0)

    def branch(w_ref, cw_ref, cb_ref, sl):
        w = w_ref[:, sl]
        pm = jnp.dot(x, w, preferred_element_type=F32)
        ph = jnp.dot(halo, w, preferred_element_type=F32)
        return _conv3_rows(pm, ph[hr - 1:hr, :], ph[hr:hr + 1, :] * nmask, cw_ref[:, sl]) + cb_ref[:, sl]

    tn = o_ref.shape[1]
    for c0 in range(0, tn, FFN_SUB):
        sl = slice(c0, min(c0 + FFN_SUB, tn))
        g = branch(wg_ref, cwg_ref, cbg_ref, sl)
        u = branch(wu_ref, cwu_ref, cbu_ref, sl)
        o_ref[:, sl] = (g * jax.nn.sigmoid(g) * u).astype(o_ref.dtype)


FFN_SUB = 256


def ffn_up(h, w_up, conv_w, conv_b, tm, tn):
    m, k = h.shape
    f = w_up.shape[1] // 2
    assert m % tm == 0 and f % tn == 0 and h.dtype == BF16
    nj = f // tn
    xp_spec, xn_spec = _halo_specs(tm, k, m, halo=HALO_BF16)
    cb = conv_b.reshape(1, 2 * f)
    return pl.pallas_call(
        _ffn_up_kernel,
        out_shape=jax.ShapeDtypeStruct((m, f), BF16),
        grid=(nj, m // tm),
        in_specs=[pl.BlockSpec((tm, k), lambda j, i: (i, 0)), xp_spec, xn_spec,
                  pl.BlockSpec((k, tn), lambda j, i: (0, j)),
                  pl.BlockSpec((k, tn), lambda j, i: (0, j + nj)),
                  pl.BlockSpec((3, tn), lambda j, i: (0, j)),
                  pl.BlockSpec((3, tn), lambda j, i: (0, j + nj)),
                  pl.BlockSpec((1, tn), lambda j, i: (0, j)),
                  pl.BlockSpec((1, tn), lambda j, i: (0, j + nj))],
        out_specs=pl.BlockSpec((tm, tn), lambda j, i: (i, j)),
        compiler_params=_cparams("arbitrary", "arbitrary"),
        name="ffn_up_conv_gate",
    )(h, h, h, w_up, w_up, conv_w, conv_w, cb, cb)


def _ffn_down_kernel(a_ref, w_ref, h_ref, g_ref, b_ref, out_ref, outb_ref, *, tiles_per_seq, padf):
    tm = h_ref.shape[0]
    acc = jnp.dot(a_ref[...], w_ref[...], preferred_element_type=F32)
    y = _ln_rows(ALPHA * h_ref[...] + acc, g_ref[...], b_ref[...])
    valid = _valid_rows(pl.program_id(0), tiles_per_seq, tm, padf)
    y = jnp.where(valid, y, 0.0)
    out_ref[...] = y
    outb_ref[...] = y.astype(BF16)


def ffn_down_ln(a, w_down, h, g, b, lp, padf, tm):
    m, d = h.shape
    f = a.shape[1]
    assert m % tm == 0 and lp % tm == 0
    ospec = pl.BlockSpec((tm, d), lambda i: (i, 0))
    return pl.pallas_call(
        functools.partial(_ffn_down_kernel, tiles_per_seq=lp // tm, padf=padf),
        out_shape=(jax.ShapeDtypeStruct((m, d), F32), jax.ShapeDtypeStruct((m, d), BF16)),
        grid=(m // tm,),
        in_specs=[pl.BlockSpec((tm, f), lambda i: (i, 0)),
                  pl.BlockSpec((f, d), lambda i: (0, 0), pipeline_mode=pl.Buffered(1)),
                  pl.BlockSpec((tm, d), lambda i: (i, 0)),
                  pl.BlockSpec((1, d), lambda i: (0, 0)),
                  pl.BlockSpec((1, d), lambda i: (0, 0))],
        out_specs=(ospec, ospec),
        compiler_params=_cparams("arbitrary"),
        name="ffn_down_residual_ln",
    )(a, w_down, h, g.reshape(1, d), b.reshape(1, d))


def rope_tables(lp, padf):
    half = ROT_DIM // 2
    pos = (jnp.arange(lp) - padf).astype(F32)
    inv = 1.0 / (ROPE_THETA ** (jnp.arange(half, dtype=F32) / half))
    ang = pos[:, None] * inv[None]
    cos, sin = jnp.cos(ang), jnp.sin(ang)
    ones = jnp.ones((lp, DA_HD - ROT_DIM), F32)
    zeros = jnp.zeros((lp, DA_HD - ROT_DIM), F32)
    zh = jnp.zeros((lp, half), F32)
    c = jnp.concatenate([cos, cos, ones], axis=1)
    sa = jnp.concatenate([-sin, zh, zeros], axis=1)
    sb = jnp.concatenate([zh, sin, zeros], axis=1)
    return tuple(jnp.concatenate([t, t], axis=1) for t in (c, sa, sb))


def _attn_prep_kernel(p_ref, c_ref, sa_ref, sb_ref, qlo_ref, qhi_ref, k_ref, vt_ref):
    w = GROUP_W
    reps = w // c_ref.shape[1]
    c = jnp.concatenate([c_ref[...]] * reps, axis=1)
    sa = jnp.concatenate([sa_ref[...]] * reps, axis=1)
    sb = jnp.concatenate([sb_ref[...]] * reps, axis=1)
    half = ROT_DIM // 2

    def rot(x):
        return x * c + pltpu.roll(x, w - half, 1) * sa + pltpu.roll(x, half, 1) * sb

    q = rot(p_ref[:, 0:w]) * (DA_HD ** -0.5)
    lane = lax.broadcasted_iota(jnp.int32, (1, w), 1) % (2 * DA_HD)
    qlo_ref[0] = jnp.where(lane < DA_HD, q, 0.0).T.astype(BF16)
    qhi_ref[0] = jnp.where(lane >= DA_HD, q, 0.0).T.astype(BF16)
    k_ref[...] = rot(p_ref[:, w:2 * w]).astype(BF16)
    vt_ref[0] = p_ref[:, 2 * w:3 * w].T.astype(BF16)


def attn_prep(proj, tables, nb, lp, tm):
    m = proj.shape[0]
    w = GROUP_W
    tps = lp // tm
    tspec = pl.BlockSpec((tm, 2 * DA_HD), lambda i: (i % tps, 0))
    tr_shape = jax.ShapeDtypeStruct((nb, w, lp), BF16)
    tr_spec = pl.BlockSpec((1, w, tm), lambda i: (i // tps, 0, i % tps))
    return pl.pallas_call(
        _attn_prep_kernel,
        out_shape=(tr_shape, tr_shape, jax.ShapeDtypeStruct((m, w), BF16), tr_shape),
        grid=(m // tm,),
        in_specs=[pl.BlockSpec((tm, 3 * w), lambda i: (i, 0)), tspec, tspec, tspec],
        out_specs=(tr_spec, tr_spec, pl.BlockSpec((tm, w), lambda i: (i, 0)), tr_spec),
        compiler_params=_cparams("arbitrary"),
        name="attn_prep_rotary",
    )(proj, *tables)


ATTN_QC = 256
ATTN_ONES = 8


def _attn_kernel(lam_ref, g_ref, qlo_ref, qhi_ref, k_ref, vt_ref, o_ref, qs_ref, acc_ref,
                 *, tkb, padf, lam_init, unroll):
    tq = qlo_ref.shape[2]
    nq2 = 2 * tq
    nkb = k_ref.shape[0] // tkb
    qs_ref[:, :tq] = qlo_ref[0]
    qs_ref[:, tq:] = qhi_ref[0]
    row_ok = lax.broadcasted_iota(jnp.int32, (tkb, 1), 0) >= padf
    ones = jnp.ones((ATTN_ONES, tkb), BF16)

    def scores(kb, c0, masked):
        start = kb * tkb
        if not isinstance(start, int):
            start = pl.multiple_of(start, tkb)
        s = jnp.dot(k_ref[pl.ds(start, tkb), :], qs_ref[:, c0:c0 + ATTN_QC], preferred_element_type=F32)
        if masked:
            s = jnp.where(row_ok, s, -1e30)
        return s, start

    def colmax(s):
        return jnp.max(jnp.max(s.reshape(tkb // 8, 8, ATTN_QC), axis=0), axis=0, keepdims=True)

    for c0 in range(0, nq2, 2 * ATTN_QC):
        cols = (c0, c0 + ATTN_QC)

        def consume(kb, s, m):
            start = kb * tkb
            if not isinstance(start, int):
                start = pl.multiple_of(start, tkb)
            vext = jnp.concatenate([vt_ref[0, :, pl.ds(start, tkb)], ones], axis=0)
            m_out = []
            for i in range(2):
                m_new = jnp.maximum(m[i], colmax(s[i]))
                p = jnp.exp(s[i] - m_new).astype(BF16)
                sl = slice(cols[i], cols[i] + ATTN_QC)
                acc_ref[:, sl] = acc_ref[:, sl] * jnp.exp(m[i] - m_new) + jnp.dot(vext, p, preferred_element_type=F32)
                m_out.append(m_new)
            return tuple(m_out)

        for c in cols:
            acc_ref[:, c:c + ATTN_QC] = jnp.zeros((HEAD_W + ATTN_ONES, ATTN_QC), F32)

        def body(kb, carry):
            s, m = carry
            s_next = tuple(scores(kb + 1, c, False)[0] for c in cols)
            return s_next, consume(kb, s, m)

        m0 = tuple(jnp.full((1, ATTN_QC), -1e30, F32) for _ in cols)
        s_last, m_last = lax.fori_loop(0, nkb - 1, body, (tuple(scores(0, c, True)[0] for c in cols), m0),
                                       unroll=unroll)
        consume(nkb - 1, s_last, m_last)

    lam = (jnp.exp(jnp.sum(lam_ref[0:1, :] * lam_ref[1:2, :], axis=-1, keepdims=True))
           - jnp.exp(jnp.sum(lam_ref[2:3, :] * lam_ref[3:4, :], axis=-1, keepdims=True)) + lam_init)
    o = acc_ref[0:HEAD_W, :] / acc_ref[HEAD_W:HEAD_W + 1, :]
    o = (o[:, :tq] - lam * o[:, tq:]).T
    o = o * lax.rsqrt(jnp.mean(o * o, axis=-1, keepdims=True) + RMS_EPS) * g_ref[...]
    o_ref[...] = (o * (1.0 - lam_init)).astype(o_ref.dtype)


def diff_attention(qlo_t, qhi_t, k, v_t, lam_vecs, norm_g, nb, lp, padf, layer, tq, tkb, unroll):
    m = k.shape[0]
    assert lp % tq == 0 and lp % tkb == 0 and padf < tkb and tq % ATTN_QC == 0
    nq = lp // tq
    lam_init = 0.8 - 0.6 * math.exp(-0.3 * layer)
    qspec = pl.BlockSpec((1, HEAD_W, tq), lambda b, h, qi: (b, h, qi))
    return pl.pallas_call(
        functools.partial(_attn_kernel, tkb=tkb, padf=padf, lam_init=lam_init, unroll=unroll),
        out_shape=jax.ShapeDtypeStruct((m, GROUP_W), BF16),
        grid=(nb, N_HEADS, nq),
        in_specs=[pl.BlockSpec((4, DA_HD), lambda b, h, qi: (0, 0)),
                  pl.BlockSpec((1, HEAD_W), lambda b, h, qi: (0, 0)),
                  qspec, qspec,
                  pl.BlockSpec((lp, HEAD_W), lambda b, h, qi: (b, h)),
                  pl.BlockSpec((1, HEAD_W, lp), lambda b, h, qi: (b, h, 0))],
        out_specs=pl.BlockSpec((tq, HEAD_W), lambda b, h, qi: (b * nq + qi, h)),
        scratch_shapes=[pltpu.VMEM((HEAD_W, 2 * tq), BF16),
                        pltpu.VMEM((HEAD_W + ATTN_ONES, 2 * tq), F32)],
        compiler_params=_cparams("arbitrary", "arbitrary", "arbitrary"),
        name="diff_attention",
    )(lam_vecs, norm_g.reshape(1, HEAD_W), qlo_t, qhi_t, k, v_t)


N_LEVELS = 6


def _scan_constants():
    c = CHUNK
    idx = np.arange(c)
    cs = np.zeros((2, (N_LEVELS + 2) * c, c), np.float32)
    mask = np.zeros((2, N_LEVELS + 2, c, c), np.float32)
    for d in range(2):
        cum = (idx[None, :] <= idx[:, None]) if d == 0 else (idx[None, :] >= idx[:, None])
        cs[d, 0:c] = cum
        for l in range(N_LEVELS):
            s = (c // 2) >> l
            blk = idx // s
            ref = (blk | 1) * s - 1 if d == 0 else (blk | 1) * s
            cs[d, (l + 1) * c:(l + 2) * c] = cum[ref]
            if d == 0:
                mask[d, l] = ((blk[:, None] & 1) == 1) & (blk[None, :] == blk[:, None] - 1)
            else:
                mask[d, l] = ((blk[:, None] & 1) == 0) & (blk[None, :] == blk[:, None] + 1)
        cs[d, (N_LEVELS + 1) * c:] = 1.0
        mask[d, N_LEVELS] = cum
        mask[d, N_LEVELS + 1] = cum & (idx[None, :] != idx[:, None])
    return jnp.asarray(np.concatenate([cs] * 3, axis=2), BF16), jnp.asarray(mask)


def _split3(x):
    hi = x.astype(BF16)
    r = x - hi.astype(F32)
    mid = r.astype(BF16)
    return hi, mid, (r - mid.astype(F32)).astype(BF16)


def _split3_rows(x):
    return jnp.concatenate(_split3(x), axis=0)


def _split3_lanes(x):
    return jnp.concatenate(_split3(x), axis=1)


def _nt(a, b):
    return lax.dot_general(a.astype(BF16), b.astype(BF16), (((1,), (1,)), ((), ())),
                           preferred_element_type=F32)


def _tn(a, b):
    return lax.dot_general(a.astype(BF16), b.astype(BF16), (((0,), (0,)), ((), ())),
                           preferred_element_type=F32)


def _nn(a, b):
    return jnp.dot(a.astype(BF16), b.astype(BF16), preferred_element_type=F32)


def _hgrn_prep_kernel(p_ref, lb_ref, q_ref, v_ref, g_ref, sg_ref):
    w = GROUP_W
    q = p_ref[:, 0:w]
    q_ref[...] = (q * jax.nn.sigmoid(q) * (HEAD_W ** -0.5)).astype(BF16)
    for d in range(2):
        lb = lb_ref[d:d + 1, :]
        f = p_ref[:, (1 + d) * w:(2 + d) * w]
        g_ref[d] = jnp.log(lb + (1.0 - lb) * jax.nn.sigmoid(f))
    v_ref[...] = p_ref[:, 3 * w:4 * w].astype(BF16)
    gate = p_ref[:, 4 * w:5 * w]
    sg_ref[...] = (gate * jax.nn.sigmoid(gate)).astype(BF16)


def hgrn_prep(proj, col_block, lb, tm):
    m = proj.shape[0]
    w = GROUP_W
    ospec = pl.BlockSpec((tm, w), lambda i: (i, 0))
    return pl.pallas_call(
        _hgrn_prep_kernel,
        out_shape=(jax.ShapeDtypeStruct((m, w), BF16), jax.ShapeDtypeStruct((m, w), BF16),
                   jax.ShapeDtypeStruct((2, m, w), F32), jax.ShapeDtypeStruct((m, w), BF16)),
        grid=(m // tm,),
        in_specs=[pl.BlockSpec((tm, 5 * w), lambda i: (i, col_block)),
                  pl.BlockSpec((2, w), lambda i: (0, 0))],
        out_specs=(ospec, ospec, pl.BlockSpec((2, tm, w), lambda i: (0, i, 0)), ospec),
        compiler_params=_cparams("arbitrary"),
        name="hgrn_prep",
    )(proj, lb)


def _head(x, h):
    return x[:, h * HEAD_W:(h + 1) * HEAD_W]


def _chunk_rows(ci, nch, d):
    return pl.multiple_of((ci + d * (nch - 1 - 2 * ci)) * CHUNK, CHUNK)


def _hgrn_scan_kernel(cs_ref, mask_ref, q_ref, v_ref, g_ref, o_ref, s_ref, *, nch):
    c = CHUNK
    d = pl.program_id(1)
    heads = range(N_HEADS)
    chunks = range(nch)

    @pl.when(pl.program_id(2) == 0)
    def _():
        s_ref[...] = jnp.zeros(s_ref.shape, F32)

    eye = mask_ref[0, N_LEVELS] - mask_ref[0, N_LEVELS + 1]
    rows = [_chunk_rows(ci, nch, d) for ci in chunks]
    g = [g_ref[0, pl.ds(r0, c), :] for r0 in rows]
    q = [q_ref[pl.ds(r0, c), :].astype(F32) for r0 in rows]
    v = [v_ref[pl.ds(r0, c), :] for r0 in rows]
    big = [jnp.dot(cs_ref[0], _split3_rows(gi), preferred_element_type=F32) for gi in g]
    b = [x[0:c] for x in big]
    tot = [x[(N_LEVELS + 1) * c:(N_LEVELS + 2) * c] for x in big]
    k = [1.0 - jnp.exp(gi) for gi in g]
    a = [[eye * jnp.sum(_head(q[ci] * k[ci], h), axis=-1, keepdims=True) for h in heads] for ci in chunks]
    q16 = [q_ref[pl.ds(r0, c), :] for r0 in rows]
    k16 = [ki.astype(BF16) for ki in k]
    for l in range(N_LEVELS):
        e = [jnp.exp(-jnp.abs(b[ci] - big[ci][(l + 1) * c:(l + 2) * c])).astype(BF16) for ci in chunks]
        qe = [q16[ci] * e[ci] for ci in chunks]
        ke = [k16[ci] * e[ci] for ci in chunks]
        lm = mask_ref[0, l]
        a = [[a[ci][h] + lm * _nt(_head(qe[ci], h), _head(ke[ci], h)) for h in heads] for ci in chunks]
    o_intra = [[_nn(a[ci][h], _head(v[ci], h)) for h in heads] for ci in chunks]
    qb = [(q[ci] * jnp.exp(b[ci])).astype(BF16) for ci in chunks]
    kd = [(k[ci] * jnp.exp(tot[ci] - b[ci])).astype(BF16) for ci in chunks]
    dec = [jnp.exp(tot[ci][0:1, :]) for ci in chunks]
    s = [s_ref[h] for h in heads]
    for ci in chunks:
        for h in heads:
            o = o_intra[ci][h] + _nt(_head(qb[ci], h), s[h])
            o_ref[0, pl.ds(rows[ci], c), h * HEAD_W:(h + 1) * HEAD_W] = o
        s = [s[h] * _head(dec[ci], h) + _tn(_head(v[ci], h), _head(kd[ci], h)) for h in heads]
    for h in heads:
        s_ref[h] = s[h]


def _scan_row_block(nblk):
    def row(b, d, t):
        return b * nblk + t + d * (nblk - 1 - 2 * t)
    return row


def hgrn_scan(q, v, g, consts, nb, lp, tb):
    m, w = q.shape
    cs, mask = consts
    row = _scan_row_block(lp // tb)
    return pl.pallas_call(
        functools.partial(_hgrn_scan_kernel, nch=tb // CHUNK),
        out_shape=jax.ShapeDtypeStruct((2, m, w), F32),
        grid=(nb, 2, lp // tb),
        in_specs=[pl.BlockSpec((1,) + cs.shape[1:], lambda b, d, t: (d, 0, 0)),
                  pl.BlockSpec((1,) + mask.shape[1:], lambda b, d, t: (d, 0, 0, 0)),
                  pl.BlockSpec((tb, w), lambda b, d, t: (row(b, d, t), 0)),
                  pl.BlockSpec((tb, w), lambda b, d, t: (row(b, d, t), 0)),
                  pl.BlockSpec((1, tb, w), lambda b, d, t: (d, row(b, d, t), 0))],
        out_specs=pl.BlockSpec((1, tb, w), lambda b, d, t: (d, row(b, d, t), 0)),
        scratch_shapes=[pltpu.VMEM((N_HEADS, HEAD_W, HEAD_W), F32)],
        compiler_params=_cparams("arbitrary", "arbitrary", "arbitrary"),
        name="hgrn_scan",
    )(cs, mask, q, v, g)


def _gated_norm_kernel(o_ref, sg_ref, g_ref, out_ref):
    o = o_ref[0] + o_ref[1]
    gain = g_ref[...]
    for h in range(N_HEADS):
        sl = slice(h * HEAD_W, (h + 1) * HEAD_W)
        oh = o[:, sl]
        y = oh * lax.rsqrt(jnp.mean(oh * oh, axis=-1, keepdims=True) + RMS_EPS) * gain
        out_ref[:, sl] = (y * sg_ref[:, sl].astype(F32)).astype(out_ref.dtype)


def gated_norm(o2, sgate, norm_g, tm):
    m = sgate.shape[0]
    w = GROUP_W
    return pl.pallas_call(
        _gated_norm_kernel,
        out_shape=jax.ShapeDtypeStruct((m, w), BF16),
        grid=(m // tm,),
        in_specs=[pl.BlockSpec((2, tm, w), lambda i: (0, i, 0)),
                  pl.BlockSpec((tm, w), lambda i: (i, 0)),
                  pl.BlockSpec((1, HEAD_W), lambda i: (0, 0))],
        out_specs=pl.BlockSpec((tm, w), lambda i: (i, 0)),
        compiler_params=_cparams("arbitrary"),
        name="gated_rms_norm",
    )(o2, sgate, norm_g.reshape(1, HEAD_W))


GATE_LANES = 128


def _softplus(x):
    return jnp.maximum(x, 0.0) + jnp.log(1.0 + jnp.exp(-jnp.abs(x)))


def _gdn_prep_kernel(p_ref, pp_ref, pn_ref, cw_ref, acoef_ref, dtb_ref,
                     q_ref, k_ref, v_ref, sz_ref, gates_ref, *, tiles_per_seq, padf):
    w = GROUP_W
    i = pl.program_id(0)
    tm = p_ref.shape[0]
    nmask = jnp.where(i == pl.num_programs(0) - 1, 0.0, 1.0)
    x = _conv3_rows(p_ref[:, 0:3 * w], pp_ref[HALO - 1:HALO, :], pn_ref[0:1, :] * nmask, cw_ref[...])
    x = x * jax.nn.sigmoid(x)
    valid = _valid_rows(i, tiles_per_seq, tm, padf)
    for h in range(N_HEADS):
        sl = slice(h * HEAD_W, (h + 1) * HEAD_W)
        qh = x[:, sl]
        q_ref[:, sl] = (qh * lax.rsqrt(jnp.sum(qh * qh, axis=-1, keepdims=True) + 1e-6)
                        * (HEAD_W ** -0.5)).astype(BF16)
        kh = x[:, w + h * HEAD_W:w + (h + 1) * HEAD_W]
        k_ref[:, sl] = (kh * lax.rsqrt(jnp.sum(kh * kh, axis=-1, keepdims=True) + 1e-6)).astype(BF16)
    v_ref[...] = jnp.where(valid, x[:, 2 * w:3 * w], 0.0).astype(BF16)
    z = p_ref[:, 3 * w:4 * w]
    sz_ref[...] = (z * jax.nn.sigmoid(z)).astype(BF16)
    gt = p_ref[:, 4 * w:4 * w + GATE_LANES]
    lane = lax.broadcasted_iota(jnp.int32, (1, GATE_LANES), 1)
    decay = -acoef_ref[...] * _softplus(gt + dtb_ref[...])
    gates_ref[...] = jnp.where(lane < 2 * N_HEADS, jax.nn.sigmoid(gt), decay)


def gdn_prep(pd, conv_w, a_log, dt_bias, lp, padf, tm):
    m = pd.shape[0]
    w = GROUP_W
    zeros8 = jnp.zeros((2 * N_HEADS,), F32)
    padl = jnp.zeros((GATE_LANES - 4 * N_HEADS,), F32)
    acoef = jnp.concatenate([zeros8, jnp.exp(a_log).reshape(-1), padl]).reshape(1, GATE_LANES)
    dtb = jnp.concatenate([zeros8, dt_bias.reshape(-1), padl]).reshape(1, GATE_LANES)
    pspec, nspec = _halo_specs(tm, 3 * w, m)
    ospec = pl.BlockSpec((tm, w), lambda i: (i, 0))
    return pl.pallas_call(
        functools.partial(_gdn_prep_kernel, tiles_per_seq=lp // tm, padf=padf),
        out_shape=tuple(jax.ShapeDtypeStruct((m, w), BF16) for _ in range(4))
        + (jax.ShapeDtypeStruct((m, GATE_LANES), F32),),
        grid=(m // tm,),
        in_specs=[pl.BlockSpec((tm, pd.shape[1]), lambda i: (i, 0)), pspec, nspec,
                  pl.BlockSpec((3, 3 * w), lambda i: (0, 0)),
                  pl.BlockSpec((1, GATE_LANES), lambda i: (0, 0)),
                  pl.BlockSpec((1, GATE_LANES), lambda i: (0, 0))],
        out_specs=(ospec, ospec, ospec, ospec, pl.BlockSpec((tm, GATE_LANES), lambda i: (i, 0))),
        compiler_params=_cparams("arbitrary"),
        name="gdn_prep",
    )(pd, pd, pd, conv_w, acoef, dtb)


def _gate_selectors():
    sel = np.zeros((2, GATE_LANES, 2 * GROUP_W), np.float32)
    for d in range(2):
        for h in range(N_HEADS):
            sel[d, N_HEADS * d + h, h * HEAD_W:(h + 1) * HEAD_W] = 1.0
            sel[d, 2 * N_HEADS + N_HEADS * d + h, GROUP_W + h * HEAD_W:GROUP_W + (h + 1) * HEAD_W] = 1.0
    return jnp.asarray(np.concatenate([sel] * 3, axis=1), BF16)


def _gdn_scan_kernel(cs_ref, mask_ref, sel_ref, q_ref, k_ref, v_ref, gt_ref, o_ref, s_ref, *, nch):
    c = CHUNK
    w = GROUP_W
    d = pl.program_id(1)
    heads = range(N_HEADS)
    chunks = range(nch)
    items = [(ci, h) for ci in chunks for h in heads]

    @pl.when(pl.program_id(2) == 0)
    def _():
        s_ref[...] = jnp.zeros(s_ref.shape, F32)

    incl = mask_ref[0, N_LEVELS]
    strict = mask_ref[0, N_LEVELS + 1]
    eye = incl - strict
    cum_tot = jnp.concatenate([cs_ref[0, 0:c, :], cs_ref[0, (N_LEVELS + 1) * c:(N_LEVELS + 2) * c, :]], axis=0)
    ones3 = cs_ref[0, (N_LEVELS + 1) * c:(N_LEVELS + 2) * c, :]
    rows = [_chunk_rows(ci, nch, d) for ci in chunks]
    q = [q_ref[pl.ds(r0, c), :].astype(F32) for r0 in rows]
    k = [k_ref[pl.ds(r0, c), :].astype(F32) for r0 in rows]
    v = [v_ref[pl.ds(r0, c), :].astype(F32) for r0 in rows]
    bg = [jnp.dot(_split3_lanes(gt_ref[pl.ds(r0, c), :]), sel_ref[0], preferred_element_type=F32)
          for r0 in rows]
    beta = [x[:, 0:w] for x in bg]
    cb = [jnp.dot(cum_tot, _split3_rows(x[:, w:2 * w]), preferred_element_type=F32) for x in bg]
    bb = [x[0:c] for x in cb]
    tot = [x[c:2 * c] for x in cb]
    bc = {(ci, h): bb[ci][:, h * HEAD_W:h * HEAD_W + c] for ci, h in items}
    br = {it: jnp.dot(ones3, _split3_rows(eye * bc[it]), preferred_element_type=F32) for it in items}
    dec = {it: incl * jnp.exp(jnp.minimum(bc[it] - br[it], 0.0)) for it in items}
    kb = [k[ci] * beta[ci] for ci in chunks]
    n = {(ci, h): strict * _nt(_head(kb[ci], h), _head(k[ci], h)) * dec[(ci, h)] for ci, h in items}
    t = {it: eye - n[it] for it in items}
    pw = {it: _nn(n[it], n[it]) for it in items}
    for _ in range(N_LEVELS - 2):
        t = {it: t[it] + _nn(t[it], pw[it]) for it in items}
        pw = {it: _nn(pw[it], pw[it]) for it in items}
    t = {it: t[it] + _nn(t[it], pw[it]) for it in items}
    eb = [jnp.exp(x) for x in bb]
    rhs_u = [v[ci] * beta[ci] for ci in chunks]
    rhs_w = [kb[ci] * eb[ci] for ci in chunks]
    uw = {(ci, h): _nn(t[(ci, h)], jnp.concatenate([_head(rhs_u[ci], h), _head(rhs_w[ci], h)], axis=1))
          for ci, h in items}
    qk = {(ci, h): _nt(_head(q[ci], h), _head(k[ci], h)) * dec[(ci, h)] for ci, h in items}
    qe = [(q[ci] * eb[ci]).astype(BF16) for ci in chunks]
    kd = [(k[ci] * jnp.exp(tot[ci] - bb[ci])).astype(BF16) for ci in chunks]
    sdec = [jnp.exp(tot[ci][0:1, :]) for ci in chunks]
    s = [s_ref[h] for h in heads]
    for ci in chunks:
        wq = {h: _nn(jnp.concatenate([uw[(ci, h)][:, HEAD_W:].astype(BF16), _head(qe[ci], h)], axis=0), s[h])
              for h in heads}
        vnew = {h: uw[(ci, h)][:, 0:HEAD_W] - wq[h][0:c] for h in heads}
        for h in heads:
            o_ref[0, pl.ds(rows[ci], c), h * HEAD_W:(h + 1) * HEAD_W] = wq[h][c:2 * c] + _nn(qk[(ci, h)], vnew[h])
        s = [s[h] * _head(sdec[ci], h) + _tn(_head(kd[ci], h), vnew[h]) for h in heads]
    for h in heads:
        s_ref[h] = s[h]


def gdn_scan(q, k, v, gates, consts, sel, nb, lp, tb):
    m, w = q.shape
    cs, mask = consts
    row = _scan_row_block(lp // tb)
    rspec = pl.BlockSpec((tb, w), lambda b, d, t: (row(b, d, t), 0))
    return pl.pallas_call(
        functools.partial(_gdn_scan_kernel, nch=tb // CHUNK),
        out_shape=jax.ShapeDtypeStruct((2, m, w), F32),
        grid=(nb, 2, lp // tb),
        in_specs=[pl.BlockSpec((1,) + cs.shape[1:], lambda b, d, t: (d, 0, 0)),
                  pl.BlockSpec((1,) + mask.shape[1:], lambda b, d, t: (d, 0, 0, 0)),
                  pl.BlockSpec((1,) + sel.shape[1:], lambda b, d, t: (d, 0, 0)),
                  rspec, rspec, rspec,
                  pl.BlockSpec((tb, GATE_LANES), lambda b, d, t: (row(b, d, t), 0))],
        out_specs=pl.BlockSpec((1, tb, w), lambda b, d, t: (d, row(b, d, t), 0)),
        scratch_shapes=[pltpu.VMEM((N_HEADS, HEAD_W, HEAD_W), F32)],
        compiler_params=_cparams("arbitrary", "arbitrary", "arbitrary"),
        name="gdn_scan",
    )(cs, mask, sel, q, k, v, gates)


FFT_N2 = 256
FFT_TS = 8


def _fft_n1(lp, l_real):
    need = -(-(2 * l_real - 1) // FFT_N2)
    return max(-(-need // 8) * 8, -(-(lp // FFT_N2) // 8) * 8)


def _outer_dft(n1, k_rows, inverse, n_total):
    kp = -(-k_rows // 16) * 16
    a = np.arange(n1)[:, None] * np.arange(kp)[None, :]
    ang = 2.0 * np.pi * (a % n1) / n1
    c, s = np.cos(ang), np.sin(ang)
    live = (np.arange(kp) < k_rows)[None, :]
    c, s = c * live, s * live
    if not inverse:
        return jnp.asarray(np.concatenate([c, -s], axis=0), BF16)
    return jnp.asarray(np.concatenate([c.T, -s.T], axis=1) / n_total, F32)


def inner_dft(n1):
    n2 = FFT_N2
    n = n1 * n2
    k = jnp.arange(n1, dtype=jnp.int32)[:, None, None] + n1 * jnp.arange(n2, dtype=jnp.int32)[None, :, None]
    r = (k * jnp.arange(n2, dtype=jnp.int32)[None, None, :]) % n
    ang = r.astype(F32) * (2.0 * math.pi / n)
    gr, gi = jnp.cos(ang), -jnp.sin(ang)
    gg = jnp.concatenate([jnp.concatenate([gr, -gi], axis=2), jnp.concatenate([gi, gr], axis=2)], axis=1)
    return gg.astype(BF16), jnp.swapaxes(gg, 1, 2).astype(BF16)


def _hyena_prep_kernel(p_ref, pp_ref, pn_ref, cw_ref, cb_ref, x0_ref, z_ref, *, tiles_per_seq, padf):
    w = GROUP_W
    i = pl.program_id(0)
    tm = p_ref.shape[0]
    nmask = jnp.where(i == pl.num_programs(0) - 1, 0.0, 1.0)
    u = _conv3_rows(p_ref[...], pp_ref[HALO - 1:HALO, :], pn_ref[0:1, :] * nmask, cw_ref[...]) + cb_ref[...]
    valid = _valid_rows(i, tiles_per_seq, tm, padf)
    x0_ref[...] = u[:, 0:w]
    z_ref[...] = jnp.where(valid, u[:, w:2 * w] * u[:, 2 * w:3 * w], 0.0)


def hyena_prep(pb, conv_w, conv_b, lp, padf, tm):
    m, w3 = pb.shape
    w = GROUP_W
    pspec, nspec = _halo_specs(tm, w3, m)
    ospec = pl.BlockSpec((tm, w), lambda i: (i, 0))
    return pl.pallas_call(
        functools.partial(_hyena_prep_kernel, tiles_per_seq=lp // tm, padf=padf),
        out_shape=(jax.ShapeDtypeStruct((m, w), F32), jax.ShapeDtypeStruct((m, w), F32)),
        grid=(m // tm,),
        in_specs=[pl.BlockSpec((tm, w3), lambda i: (i, 0)), pspec, nspec,
                  pl.BlockSpec((3, w3), lambda i: (0, 0)), pl.BlockSpec((1, w3), lambda i: (0, 0))],
        out_specs=(ospec, ospec),
        compiler_params=_cparams("arbitrary"),
        name="hyena_prep",
    )(pb, pb, pb, conv_w, conv_b.reshape(1, w3))


def _hyena_filter_kernel(w1t_ref, w1c_ref, w1s_ref, b1_ref, f1_ref, w2_ref, b2_ref, f2_ref, w3_ref, dec_ref,
                         filt_ref, asum_ref, *, l_real):
    i = pl.program_id(0)
    tm = filt_ref.shape[0]
    w = GROUP_W
    row = i * tm + lax.broadcasted_iota(jnp.int32, (tm, 1), 0)
    rf = row.astype(F32)
    t = rf * (1.0 / (l_real - 1))
    band = lax.broadcasted_iota(jnp.int32, (1, HY_BANDS), 1).astype(F32)
    bands = 1e-4 + band * ((HY_BANDS - 1 - 1e-4) / (HY_BANDS - 1))
    ang = ((2.0 * math.pi / l_real) * rf) * bands

    def hdot(a, b):
        return jnp.dot(a, b, precision=HIGHEST, preferred_element_type=F32)

    pre = t * w1t_ref[...] + hdot(jnp.cos(ang), w1c_ref[...]) - hdot(jnp.sin(ang), w1s_ref[...])
    hid = jnp.sin(f1_ref[...] * (pre + b1_ref[...]))
    hid = jnp.sin(f2_ref[...] * (hdot(hid, w2_ref[...]) + b2_ref[...]))
    filt = hdot(hid, w3_ref[...])
    window = jnp.exp(-t * jnp.abs(dec_ref[...])) + HY_SHIFT
    filt = jnp.where(row < l_real, filt * window, 0.0)

    @pl.when(i == 0)
    def _():
        asum_ref[...] = jnp.zeros(asum_ref.shape, F32)

    asum_ref[...] += jnp.sum(jnp.abs(filt), axis=0, keepdims=True)
    lane = lax.broadcasted_iota(jnp.int32, (1, 2 * w), 1)
    filt_ref[...] = jnp.where((row == 0) & (lane >= w), 0.0, filt)


def hyena_filter(w1, b1, f1, w2, b2, f2, w3, decay, lp, l_real, tm):
    w = GROUP_W
    nf = w1.shape[1]
    small = lambda a: pl.BlockSpec(a.shape, lambda i: (0,) * a.ndim)
    args = (w1[0:1], w1[1:1 + HY_BANDS], w1[1 + HY_BANDS:], b1.reshape(1, nf), f1.reshape(1, nf),
            w2, b2.reshape(1, nf), f2.reshape(1, nf), w3, decay.reshape(1, 2 * w))
    return pl.pallas_call(
        functools.partial(_hyena_filter_kernel, l_real=l_real),
        out_shape=(jax.ShapeDtypeStruct((lp, 2 * w), F32), jax.ShapeDtypeStruct((1, 2 * w), F32)),
        grid=(lp // tm,),
        in_specs=[small(a) for a in args],
        out_specs=(pl.BlockSpec((tm, 2 * w), lambda i: (i, 0)), pl.BlockSpec((1, 2 * w), lambda i: (0, 0))),
        compiler_params=_cparams("arbitrary"),
        name="hyena_filter",
    )(*args)


def _fft_outer_kernel(f_ref, x_ref, o_ref, pad_ref):
    k = x_ref.shape[1]
    pad_ref[...] = jnp.zeros(pad_ref.shape, F32)
    for s in range(x_ref.shape[2]):
        pad_ref[0:k, :] = x_ref[0, :, s, :]
        o_ref[0, :, s, :] = jnp.dot(f_ref[...], pad_ref[...].astype(BF16), preferred_element_type=F32)


def fft_outer(x, fmat):
    b, k, n2, c = x.shape
    rows, kp = fmat.shape
    ts = FFT_TS
    return pl.pallas_call(
        _fft_outer_kernel,
        out_shape=jax.ShapeDtypeStruct((b, rows, n2, c), F32),
        grid=(b, n2 // ts),
        in_specs=[pl.BlockSpec((rows, kp), lambda bi, j: (0, 0)),
                  pl.BlockSpec((1, k, ts, c), lambda bi, j: (bi, 0, j, 0))],
        out_specs=pl.BlockSpec((1, rows, ts, c), lambda bi, j: (bi, 0, j, 0)),
        scratch_shapes=[pltpu.VMEM((kp, c), F32)],
        compiler_params=_cparams("arbitrary", "arbitrary"),
        name="fft_outer",
    )(fmat, x)


def _filter_spectrum_kernel(gg_ref, a_ref, asum_ref, o_ref):
    w = GROUP_W
    n2 = FFT_N2
    a = jnp.concatenate([a_ref[0, 0], a_ref[1, 0]], axis=0).astype(BF16)
    x = jnp.dot(gg_ref[0], a, preferred_element_type=F32)
    s = 1.0 / (asum_ref[:, 0:w] + asum_ref[:, w:2 * w] + 1e-6)
    o_ref[0, 0] = (x[0:n2, 0:w] + x[0:n2, w:2 * w]) * s
    o_ref[0, 1] = (x[n2:2 * n2, 0:w] - x[n2:2 * n2, w:2 * w]) * s


def filter_spectrum(a, gg, asum):
    _, n1, n2, w2 = a.shape
    w = w2 // 2
    return pl.pallas_call(
        _filter_spectrum_kernel,
        out_shape=jax.ShapeDtypeStruct((n1, 2, n2, w), F32),
        grid=(n1,),
        in_specs=[pl.BlockSpec((1, 2 * n2, 2 * n2), lambda k: (k, 0, 0)),
                  pl.BlockSpec((2, 1, n2, w2), lambda k: (0, k, 0, 0)),
                  pl.BlockSpec((1, w2), lambda k: (0, 0))],
        out_specs=pl.BlockSpec((1, 2, n2, w), lambda k: (k, 0, 0, 0)),
        compiler_params=_cparams("arbitrary"),
        name="hyena_filter_spectrum",
    )(gg, a, asum)


def _fft_mid_kernel(gg_ref, ggt_ref, kf_ref, a_ref, o_ref):
    n2 = FFT_N2
    a = jnp.concatenate([a_ref[0, 0, 0], a_ref[0, 1, 0]], axis=0).astype(BF16)
    x = jnp.dot(gg_ref[0], a, preferred_element_type=F32)
    xr, xi = x[0:n2], x[n2:2 * n2]
    kr, ki = kf_ref[0, 0], kf_ref[0, 1]
    y = jnp.concatenate([xr * kr - xi * ki, xr * ki + xi * kr], axis=0).astype(BF16)
    bm = jnp.dot(ggt_ref[0], y, preferred_element_type=F32)
    o_ref[0, 0, 0] = bm[0:n2]
    o_ref[0, 1, 0] = bm[n2:2 * n2]


def fft_mid(a, gg, ggt, kf):
    b, _, n1, n2, w = a.shape
    aspec = pl.BlockSpec((1, 2, 1, n2, w), lambda k, bi: (bi, 0, k, 0, 0))
    gspec = pl.BlockSpec((1, 2 * n2, 2 * n2), lambda k, bi: (k, 0, 0))
    return pl.pallas_call(
        _fft_mid_kernel,
        out_shape=jax.ShapeDtypeStruct(a.shape, F32),
        grid=(n1, b),
        in_specs=[gspec, gspec, pl.BlockSpec((1, 2, n2, w), lambda k, bi: (k, 0, 0, 0)), aspec],
        out_specs=aspec,
        compiler_params=_cparams("arbitrary", "arbitrary"),
        name="fft_inner_conv",
    )(gg, ggt, kf, a)


def _fft_final_kernel(f_ref, b_ref, x0_ref, z_ref, d_ref, o_ref):
    k = o_ref.shape[1]
    for s in range(o_ref.shape[2]):
        y = jnp.dot(f_ref[...], b_ref[0, :, s, :], precision=HIGHEST, preferred_element_type=F32)
        o_ref[0, :, s, :] = x0_ref[0, :, s, :] * (y[0:k] + z_ref[0, :, s, :] * d_ref[...])


def fft_final(bm, finv, x0, z, d):
    b, rows, n2, c = bm.shape
    k = x0.shape[1]
    kp = finv.shape[0]
    ts = FFT_TS
    xspec = pl.BlockSpec((1, k, ts, c), lambda bi, j: (bi, 0, j, 0))
    return pl.pallas_call(
        _fft_final_kernel,
        out_shape=jax.ShapeDtypeStruct((b, k, n2, c), F32),
        grid=(b, n2 // ts),
        in_specs=[pl.BlockSpec((kp, rows), lambda bi, j: (0, 0)),
                  pl.BlockSpec((1, rows, ts, c), lambda bi, j: (bi, 0, j, 0)),
                  xspec, xspec, pl.BlockSpec((1, c), lambda bi, j: (0, 0))],
        out_specs=xspec,
        compiler_params=_cparams("arbitrary", "arbitrary"),
        name="fft_outer_inverse_gate",
    )(finv, bm, x0, z, d)


def hyena_mixer(pb, hp, nb, lp, padf, l_real, tm):
    w = GROUP_W
    n2 = FFT_N2
    k1 = lp // n2
    n1 = _fft_n1(lp, l_real)
    n_total = n1 * n2
    ffwd = _outer_dft(n1, k1, False, n_total)
    finv = _outer_dft(n1, k1, True, n_total)
    gg, ggt = hp['gg'], hp['ggt']
    filt, asum = hyena_filter(hp['w1'], hp['b1'], hp['f1'], hp['w2'], hp['b2'], hp['f2'], hp['w3'],
                              hp['decay'], lp, l_real, tm)
    fa = fft_outer(filt.reshape(1, k1, n2, 2 * w), ffwd)
    kf = filter_spectrum(fa.reshape(2, n1, n2, 2 * w), gg, asum)
    x0, z = hyena_prep(pb, hp['conv_w'], hp['conv_b'], lp, padf, tm)
    z4 = z.reshape(nb, k1, n2, w)
    a = fft_outer(z4, ffwd)
    bm = fft_mid(a.reshape(nb, 2, n1, n2, w), gg, ggt, kf)
    out = fft_final(bm.reshape(nb, 2 * n1, n2, w), finv, x0.reshape(nb, k1, n2, w), z4, hp['d'].reshape(1, w))
    return out.reshape(nb * lp, w)


TM = 640
ATTN_TQ = 1280
ATTN_TKB = 1280
ATTN_UNROLL = 4
SCAN_TB = 640
FFN_TN = 512
FFN_DOWN_TM = 320


def kernel(x_prompt, x_sample, meta, emb_ln_g, emb_ln_b, w_in, lam_q1, lam_k1, lam_q2, lam_k2, attn_norm_g, hy_conv_w, hy_conv_b, hy_w1, hy_b1, hy_f1, hy_w2, hy_b2, hy_f2, hy_w3, hy_decay, hy_d, hg_lb, hg_norm_g, gdn_conv_w, gdn_a_log, gdn_dt_bias, gdn_norm_g, w_out, ln1_g, ln1_b, w_up, ffn_conv_w, ffn_conv_b, w_down, ln2_g, ln2_b):
    n_prompt = x_prompt.shape[0]
    x = jnp.concatenate([x_prompt, x_sample], axis=0)
    nb, seq, d = x.shape
    padf = ROW_ALIGN - N_META
    lp = seq + ROW_ALIGN
    l_real = seq + N_META
    m = nb * lp
    w = GROUP_W

    h, hb = (a.reshape(m, d) for a in embed(x, meta, emb_ln_g, emb_ln_b, padf))
    tables = rope_tables(lp, padf)
    consts = _scan_constants()
    sel = _gate_selectors()
    gg, ggt = inner_dft(_fft_n1(lp, l_real))
    sm = jax.nn.softmax(hg_lb, axis=0)
    lb_all = jnp.cumsum(sm, axis=0) - sm[0]

    for l in range(DEPTH):
        wl = w_in[l].astype(BF16)
        wd = jnp.pad(wl[:, 11 * w:], ((0, 0), (0, 4 * w + GATE_LANES - (wl.shape[1] - 11 * w))))
        pa = matmul(hb, wl[:, 0:3 * w], TM, 3 * w)
        pb = matmul(hb, wl[:, 3 * w:6 * w], TM, 3 * w)
        pc = matmul(hb, wl[:, 6 * w:11 * w], TM, 5 * w // 2)
        pd = matmul(hb, wd, TM, wd.shape[1])

        qlo, qhi, ka, va = attn_prep(pa, tables, nb, lp, TM)
        lam_vecs = jnp.stack([lam_q1[l], lam_k1[l], lam_q2[l], lam_k2[l]])
        oa = diff_attention(qlo, qhi, ka, va, lam_vecs, attn_norm_g[l], nb, lp, padf, l,
                            ATTN_TQ, ATTN_TKB, ATTN_UNROLL)

        hp = dict(gg=gg, ggt=ggt, w1=hy_w1[l], b1=hy_b1[l], f1=hy_f1[l], w2=hy_w2[l], b2=hy_b2[l],
                  f2=hy_f2[l], w3=hy_w3[l], decay=hy_decay[l], d=hy_d[l], conv_w=hy_conv_w[l],
                  conv_b=hy_conv_b[l])
        ob = hyena_mixer(pb, hp, nb, lp, padf, l_real, TM)

        qc, vc, gc, sgc = hgrn_prep(pc, 0, lb_all[l], TM)
        oc = gated_norm(hgrn_scan(qc, vc, gc, consts, nb, lp, SCAN_TB), sgc, hg_norm_g[l], TM)

        qd, kd, vd, szd, gates = gdn_prep(pd, gdn_conv_w[l], gdn_a_log[l], gdn_dt_bias[l], lp, padf, TM)
        od = gated_norm(gdn_scan(qd, kd, vd, gates, consts, sel, nb, lp, SCAN_TB), szd, gdn_norm_g[l], TM)

        h, hb = proj_residual_ln([oa, ob, oc, od], w_out[l].astype(BF16), h, ln1_g[l], ln1_b[l], lp, padf, TM)
        act = ffn_up(hb, w_up[l].astype(BF16), ffn_conv_w[l], ffn_conv_b[l], TM, FFN_TN)
        h, hb = ffn_down_ln(act, w_down[l].astype(BF16), h, ln2_g[l], ln2_b[l], lp, padf, FFN_DOWN_TM)

    y = h.reshape(nb, lp, d)[:, ROW_ALIGN:]
    return (y[:n_prompt], y[n_prompt:])
```

```python
import functools
import math

import jax
import jax.numpy as jnp
import numpy as np
from jax import lax
from jax.experimental import pallas as pl
from jax.experimental.pallas import tpu as pltpu

D_MODEL = 2048
DEPTH = 4
N_META = 16
GROUP_W = 512
N_HEADS = 4
HEAD_W = 128
DA_HD = 64
ROT_DIM = 16
ROPE_THETA = 500000.0
HY_BANDS = 16
HY_SHIFT = 0.05
CHUNK = 64
D_FF = 5632
ALPHA = (2.0 * DEPTH) ** 0.25
LN_EPS = 1e-5
RMS_EPS = 1e-6
F32 = jnp.float32
BF16 = jnp.bfloat16
HIGHEST = lax.Precision.HIGHEST

ROW_ALIGN = 256
VMEM_LIMIT = 56 * 1024 * 1024


def _cparams(*sem):
    return pltpu.CompilerParams(dimension_semantics=sem, vmem_limit_bytes=VMEM_LIMIT)


def _valid_rows(tile_idx, tiles_per_seq, tm, padf):
    base = (tile_idx % tiles_per_seq) * tm
    return (base + lax.broadcasted_iota(jnp.int32, (tm, 1), 0)) >= padf


def _ln_rows(y, g, b):
    mu = jnp.mean(y, axis=-1, keepdims=True)
    d = y - mu
    var = jnp.mean(d * d, axis=-1, keepdims=True)
    return d * lax.rsqrt(var + LN_EPS) * g + b


def _embed_kernel(x_ref, meta_ref, g_ref, b_ref, o_ref, ob_ref, *, padf):
    t = pl.program_id(1)
    g = g_ref[...]
    b = b_ref[...]

    @pl.when(t == 0)
    def _():
        y = jnp.concatenate([jnp.zeros((padf, o_ref.shape[2]), F32), _ln_rows(meta_ref[...], g, b)], axis=0)
        o_ref[0] = y
        ob_ref[0] = y.astype(BF16)

    @pl.when(t > 0)
    def _():
        y = _ln_rows(x_ref[0], g, b)
        o_ref[0] = y
        ob_ref[0] = y.astype(BF16)


def embed(x, meta, g, b, padf):
    nb, s, d = x.shape
    tm = ROW_ALIGN
    assert padf + N_META == tm and s % tm == 0
    lp = s + tm
    ospec = pl.BlockSpec((1, tm, d), lambda bi, t: (bi, t, 0))
    return pl.pallas_call(
        functools.partial(_embed_kernel, padf=padf),
        out_shape=(jax.ShapeDtypeStruct((nb, lp, d), F32), jax.ShapeDtypeStruct((nb, lp, d), BF16)),
        grid=(nb, lp // tm),
        in_specs=[pl.BlockSpec((1, tm, d), lambda bi, t: (bi, jnp.maximum(t - 1, 0), 0)),
                  pl.BlockSpec((N_META, d), lambda bi, t: (0, 0)),
                  pl.BlockSpec((1, d), lambda bi, t: (0, 0)),
                  pl.BlockSpec((1, d), lambda bi, t: (0, 0))],
        out_specs=(ospec, ospec),
        compiler_params=_cparams("arbitrary", "arbitrary"),
        name="embed_ln",
    )(x, meta, g.reshape(1, d), b.reshape(1, d))


def _mm_kernel(x_ref, w_ref, o_ref):
    o_ref[...] = jnp.dot(x_ref[...], w_ref[...], preferred_element_type=F32).astype(o_ref.dtype)


def matmul(x, w, tm, tn, out_dtype=F32):
    m, k = x.shape
    n = w.shape[1]
    assert m % tm == 0 and n % tn == 0
    return pl.pallas_call(
        _mm_kernel,
        out_shape=jax.ShapeDtypeStruct((m, n), out_dtype),
        grid=(n // tn, m // tm),
        in_specs=[pl.BlockSpec((tm, k), lambda j, i: (i, 0)),
                  pl.BlockSpec((k, tn), lambda j, i: (0, j))],
        out_specs=pl.BlockSpec((tm, tn), lambda j, i: (i, j)),
        compiler_params=_cparams("arbitrary", "arbitrary"),
        name="dense_matmul",
    )(x, w)


def _proj_ln_kernel(*refs, n_in, tiles_per_seq, padf):
    o_refs = refs[:n_in]
    w_ref, h_ref, g_ref, b_ref, out_ref, outb_ref = refs[n_in:]
    tm = h_ref.shape[0]
    acc = None
    off = 0
    for r in o_refs:
        kw = r.shape[1]
        part = jnp.dot(r[...].astype(BF16), w_ref[off:off + kw, :], preferred_element_type=F32)
        acc = part if acc is None else acc + part
        off += kw
    y = _ln_rows(ALPHA * h_ref[...] + acc, g_ref[...], b_ref[...])
    valid = _valid_rows(pl.program_id(0), tiles_per_seq, tm, padf)
    y = jnp.where(valid, y, 0.0)
    out_ref[...] = y
    outb_ref[...] = y.astype(BF16)


def proj_residual_ln(parts, w, h, g, b, lp, padf, tm):
    m, d = h.shape
    assert m % tm == 0 and lp % tm == 0
    n_in = len(parts)
    in_specs = [pl.BlockSpec((tm, p.shape[1]), lambda i: (i, 0)) for p in parts]
    in_specs += [pl.BlockSpec(w.shape, lambda i: (0, 0)),
                 pl.BlockSpec((tm, d), lambda i: (i, 0)),
                 pl.BlockSpec((1, d), lambda i: (0, 0)),
                 pl.BlockSpec((1, d), lambda i: (0, 0))]
    ospec = pl.BlockSpec((tm, d), lambda i: (i, 0))
    return pl.pallas_call(
        functools.partial(_proj_ln_kernel, n_in=n_in, tiles_per_seq=lp // tm, padf=padf),
        out_shape=(jax.ShapeDtypeStruct((m, d), F32), jax.ShapeDtypeStruct((m, d), BF16)),
        grid=(m // tm,),
        in_specs=in_specs,
        out_specs=(ospec, ospec),
        compiler_params=_cparams("arbitrary"),
        name="proj_residual_ln",
    )(*parts, w, h, g.reshape(1, d), b.reshape(1, d))


HALO = 8
HALO_BF16 = 16


def _halo_specs(tm, width, m, col_block=0, halo=HALO):
    nblk = m // halo
    per = tm // halo

    def prev_map(*idx):
        i = idx[-1]
        return (jnp.maximum(i * per - 1, 0), col_block)

    def next_map(*idx):
        i = idx[-1]
        return (jnp.minimum((i + 1) * per, nblk - 1), col_block)

    return (pl.BlockSpec((halo, width), prev_map), pl.BlockSpec((halo, width), next_map))


def _conv3_rows(pm, prev_row, next_row, cw):
    tm = pm.shape[0]
    rid = lax.broadcasted_iota(jnp.int32, (tm, 1), 0)
    down = jnp.where(rid == 0, prev_row, pltpu.roll(pm, 1, 0))
    up = jnp.where(rid == tm - 1, next_row, pltpu.roll(pm, tm - 1, 0))
    return down * cw[0:1, :] + pm * cw[1:2, :] + up * cw[2:3, :]


def _ffn_up_kernel(x_ref, xp_ref, xn_ref, wg_ref, wu_ref, cwg_ref, cwu_ref, cbg_ref, cbu_ref, o_ref):
    i = pl.program_id(1)
    last = pl.num_programs(1) - 1
    x = x_ref[...]
    hr = xp_ref.shape[0]
    halo = jnp.concatenate([xp_ref[...], xn_ref[...]], axis=0)
    nmask = jnp.where(i == last, 0.0, 1.0)

    def branch(w_ref, cw_ref, cb_ref, sl):
        w = w_ref[:, sl]
        pm = jnp.dot(x, w, preferred_element_type=F32)
        ph = jnp.dot(halo, w, preferred_element_type=F32)
        return _conv3_rows(pm, ph[hr - 1:hr, :], ph[hr:hr + 1, :] * nmask, cw_ref[:, sl]) + cb_ref[:, sl]

    tn = o_ref.shape[1]
    for c0 in range(0, tn, FFN_SUB):
        sl = slice(c0, min(c0 + FFN_SUB, tn))
        g = branch(wg_ref, cwg_ref, cbg_ref, sl)
        u = branch(wu_ref, cwu_ref, cbu_ref, sl)
        o_ref[:, sl] = (g * jax.nn.sigmoid(g) * u).astype(o_ref.dtype)


FFN_SUB = 256


def ffn_up(h, w_up, conv_w, conv_b, tm, tn):
    m, k = h.shape
    f = w_up.shape[1] // 2
    assert m % tm == 0 and f % tn == 0 and h.dtype == BF16
    nj = f // tn
    xp_spec, xn_spec = _halo_specs(tm, k, m, halo=HALO_BF16)
    cb = conv_b.reshape(1, 2 * f)
    return pl.pallas_call(
        _ffn_up_kernel,
        out_shape=jax.ShapeDtypeStruct((m, f), BF16),
        grid=(nj, m // tm),
        in_specs=[pl.BlockSpec((tm, k), lambda j, i: (i, 0)), xp_spec, xn_spec,
                  pl.BlockSpec((k, tn), lambda j, i: (0, j)),
                  pl.BlockSpec((k, tn), lambda j, i: (0, j + nj)),
                  pl.BlockSpec((3, tn), lambda j, i: (0, j)),
                  pl.BlockSpec((3, tn), lambda j, i: (0, j + nj)),
                  pl.BlockSpec((1, tn), lambda j, i: (0, j)),
                  pl.BlockSpec((1, tn), lambda j, i: (0, j + nj))],
        out_specs=pl.BlockSpec((tm, tn), lambda j, i: (i, j)),
        compiler_params=_cparams("arbitrary", "arbitrary"),
        name="ffn_up_conv_gate",
    )(h, h, h, w_up, w_up, conv_w, conv_w, cb, cb)


def _ffn_down_kernel(a_ref, w_ref, h_ref, g_ref, b_ref, out_ref, outb_ref, *, tiles_per_seq, padf):
    tm = h_ref.shape[0]
    acc = jnp.dot(a_ref[...], w_ref[...], preferred_element_type=F32)
    y = _ln_rows(ALPHA * h_ref[...] + acc, g_ref[...], b_ref[...])
    valid = _valid_rows(pl.program_id(0), tiles_per_seq, tm, padf)
    y = jnp.where(valid, y, 0.0)
    out_ref[...] = y
    outb_ref[...] = y.astype(BF16)


def ffn_down_ln(a, w_down, h, g, b, lp, padf, tm):
    m, d = h.shape
    f = a.shape[1]
    assert m % tm == 0 and lp % tm == 0
    ospec = pl.BlockSpec((tm, d), lambda i: (i, 0))
    return pl.pallas_call(
        functools.partial(_ffn_down_kernel, tiles_per_seq=lp // tm, padf=padf),
        out_shape=(jax.ShapeDtypeStruct((m, d), F32), jax.ShapeDtypeStruct((m, d), BF16)),
        grid=(m // tm,),
        in_specs=[pl.BlockSpec((tm, f), lambda i: (i, 0)),
                  pl.BlockSpec((f, d), lambda i: (0, 0), pipeline_mode=pl.Buffered(1)),
                  pl.BlockSpec((tm, d), lambda i: (i, 0)),
                  pl.BlockSpec((1, d), lambda i: (0, 0)),
                  pl.BlockSpec((1, d), lambda i: (0, 0))],
        out_specs=(ospec, ospec),
        compiler_params=_cparams("arbitrary"),
        name="ffn_down_residual_ln",
    )(a, w_down, h, g.reshape(1, d), b.reshape(1, d))


def rope_tables(lp, padf):
    half = ROT_DIM // 2
    pos = (jnp.arange(lp) - padf).astype(F32)
    inv = 1.0 / (ROPE_THETA ** (jnp.arange(half, dtype=F32) / half))
    ang = pos[:, None] * inv[None]
    cos, sin = jnp.cos(ang), jnp.sin(ang)
    ones = jnp.ones((lp, DA_HD - ROT_DIM), F32)
    zeros = jnp.zeros((lp, DA_HD - ROT_DIM), F32)
    zh = jnp.zeros((lp, half), F32)
    c = jnp.concatenate([cos, cos, ones], axis=1)
    sa = jnp.concatenate([-sin, zh, zeros], axis=1)
    sb = jnp.concatenate([zh, sin, zeros], axis=1)
    return tuple(jnp.concatenate([t, t], axis=1) for t in (c, sa, sb))


def _attn_prep_kernel(p_ref, c_ref, sa_ref, sb_ref, qlo_ref, qhi_ref, k_ref, vt_ref):
    w = GROUP_W
    reps = w // c_ref.shape[1]
    c = jnp.concatenate([c_ref[...]] * reps, axis=1)
    sa = jnp.concatenate([sa_ref[...]] * reps, axis=1)
    sb = jnp.concatenate([sb_ref[...]] * reps, axis=1)
    half = ROT_DIM // 2

    def rot(x):
        return x * c + pltpu.roll(x, w - half, 1) * sa + pltpu.roll(x, half, 1) * sb

    q = rot(p_ref[:, 0:w]) * (DA_HD ** -0.5)
    lane = lax.broadcasted_iota(jnp.int32, (1, w), 1) % (2 * DA_HD)
    qlo_ref[0] = jnp.where(lane < DA_HD, q, 0.0).T.astype(BF16)
    qhi_ref[0] = jnp.where(lane >= DA_HD, q, 0.0).T.astype(BF16)
    k_ref[...] = rot(p_ref[:, w:2 * w]).astype(BF16)
    vt_ref[0] = p_ref[:, 2 * w:3 * w].T.astype(BF16)


def attn_prep(proj, tables, nb, lp, tm):
    m = proj.shape[0]
    w = GROUP_W
    tps = lp // tm
    tspec = pl.BlockSpec((tm, 2 * DA_HD), lambda i: (i % tps, 0))
    tr_shape = jax.ShapeDtypeStruct((nb, w, lp), BF16)
    tr_spec = pl.BlockSpec((1, w, tm), lambda i: (i // tps, 0, i % tps))
    return pl.pallas_call(
        _attn_prep_kernel,
        out_shape=(tr_shape, tr_shape, jax.ShapeDtypeStruct((m, w), BF16), tr_shape),
        grid=(m // tm,),
        in_specs=[pl.BlockSpec((tm, 3 * w), lambda i: (i, 0)), tspec, tspec, tspec],
        out_specs=(tr_spec, tr_spec, pl.BlockSpec((tm, w), lambda i: (i, 0)), tr_spec),
        compiler_params=_cparams("arbitrary"),
        name="attn_prep_rotary",
    )(proj, *tables)


ATTN_QC = 256
ATTN_GROUP = 5
ATTN_ONES = 8


def _attn_kernel(lam_ref, g_ref, qlo_ref, qhi_ref, k_ref, vt_ref, o_ref, qs_ref, acc_ref,
                 *, tkb, padf, lam_init, unroll):
    tq = qlo_ref.shape[2]
    nq2 = 2 * tq
    nkb = k_ref.shape[0] // tkb
    qs_ref[:, :tq] = qlo_ref[0]
    qs_ref[:, tq:] = qhi_ref[0]
    row_ok = lax.broadcasted_iota(jnp.int32, (tkb, 1), 0) >= padf
    ones = jnp.ones((ATTN_ONES, tkb), BF16)

    def scores(kb, c0, masked):
        start = kb * tkb
        if not isinstance(start, int):
            start = pl.multiple_of(start, tkb)
        s = jnp.dot(k_ref[pl.ds(start, tkb), :], qs_ref[:, c0:c0 + ATTN_QC], preferred_element_type=F32)
        if masked:
            s = jnp.where(row_ok, s, -1e30)
        return s, start

    def colmax(s):
        return jnp.max(jnp.max(s.reshape(tkb // 8, 8, ATTN_QC), axis=0), axis=0, keepdims=True)

    for c0 in range(0, nq2, ATTN_GROUP * ATTN_QC):
        cols = tuple(c0 + i * ATTN_QC for i in range(ATTN_GROUP))

        def consume(kb, s, m):
            start = kb * tkb
            if not isinstance(start, int):
                start = pl.multiple_of(start, tkb)
            vext = jnp.concatenate([vt_ref[0, :, pl.ds(start, tkb)], ones], axis=0)
            m_out = []
            for i in range(ATTN_GROUP):
                m_new = jnp.maximum(m[i], colmax(s[i]))
                p = jnp.exp(s[i] - m_new).astype(BF16)
                sl = slice(cols[i], cols[i] + ATTN_QC)
                acc_ref[:, sl] = acc_ref[:, sl] * jnp.exp(m[i] - m_new) + jnp.dot(vext, p, preferred_element_type=F32)
                m_out.append(m_new)
            return tuple(m_out)

        for c in cols:
            acc_ref[:, c:c + ATTN_QC] = jnp.zeros((HEAD_W + ATTN_ONES, ATTN_QC), F32)

        def body(kb, carry):
            s, m = carry
            s_next = tuple(scores(kb + 1, c, False)[0] for c in cols)
            return s_next, consume(kb, s, m)

        m0 = tuple(jnp.full((1, ATTN_QC), -1e30, F32) for _ in cols)
        s_last, m_last = lax.fori_loop(0, nkb - 1, body, (tuple(scores(0, c, True)[0] for c in cols), m0),
                                       unroll=unroll)
        consume(nkb - 1, s_last, m_last)

    lam = (jnp.exp(jnp.sum(lam_ref[0:1, :] * lam_ref[1:2, :], axis=-1, keepdims=True))
           - jnp.exp(jnp.sum(lam_ref[2:3, :] * lam_ref[3:4, :], axis=-1, keepdims=True)) + lam_init)
    o = acc_ref[0:HEAD_W, :] / acc_ref[HEAD_W:HEAD_W + 1, :]
    o = (o[:, :tq] - lam * o[:, tq:]).T
    o = o * lax.rsqrt(jnp.mean(o * o, axis=-1, keepdims=True) + RMS_EPS) * g_ref[...]
    o_ref[...] = (o * (1.0 - lam_init)).astype(o_ref.dtype)


def diff_attention(qlo_t, qhi_t, k, v_t, lam_vecs, norm_g, nb, lp, padf, layer, tq, tkb, unroll):
    m = k.shape[0]
    assert lp % tq == 0 and lp % tkb == 0 and padf < tkb and (2 * tq) % (ATTN_GROUP * ATTN_QC) == 0
    nq = lp // tq
    lam_init = 0.8 - 0.6 * math.exp(-0.3 * layer)
    qspec = pl.BlockSpec((1, HEAD_W, tq), lambda b, h, qi: (b, h, qi))
    return pl.pallas_call(
        functools.partial(_attn_kernel, tkb=tkb, padf=padf, lam_init=lam_init, unroll=unroll),
        out_shape=jax.ShapeDtypeStruct((m, GROUP_W), BF16),
        grid=(nb, N_HEADS, nq),
        in_specs=[pl.BlockSpec((4, DA_HD), lambda b, h, qi: (0, 0)),
                  pl.BlockSpec((1, HEAD_W), lambda b, h, qi: (0, 0)),
                  qspec, qspec,
                  pl.BlockSpec((lp, HEAD_W), lambda b, h, qi: (b, h)),
                  pl.BlockSpec((1, HEAD_W, lp), lambda b, h, qi: (b, h, 0))],
        out_specs=pl.BlockSpec((tq, HEAD_W), lambda b, h, qi: (b * nq + qi, h)),
        scratch_shapes=[pltpu.VMEM((HEAD_W, 2 * tq), BF16),
                        pltpu.VMEM((HEAD_W + ATTN_ONES, 2 * tq), F32)],
        compiler_params=_cparams("arbitrary", "arbitrary", "arbitrary"),
        name="diff_attention",
    )(lam_vecs, norm_g.reshape(1, HEAD_W), qlo_t, qhi_t, k, v_t)


N_LEVELS = 6


def _scan_constants():
    c = CHUNK
    idx = np.arange(c)
    cs = np.zeros((2, (N_LEVELS + 2) * c, c), np.float32)
    mask = np.zeros((2, N_LEVELS + 2, c, c), np.float32)
    for d in range(2):
        cum = (idx[None, :] <= idx[:, None]) if d == 0 else (idx[None, :] >= idx[:, None])
        cs[d, 0:c] = cum
        for l in range(N_LEVELS):
            s = (c // 2) >> l
            blk = idx // s
            ref = (blk | 1) * s - 1 if d == 0 else (blk | 1) * s
            cs[d, (l + 1) * c:(l + 2) * c] = cum[ref]
            if d == 0:
                mask[d, l] = ((blk[:, None] & 1) == 1) & (blk[None, :] == blk[:, None] - 1)
            else:
                mask[d, l] = ((blk[:, None] & 1) == 0) & (blk[None, :] == blk[:, None] + 1)
        cs[d, (N_LEVELS + 1) * c:] = 1.0
        mask[d, N_LEVELS] = cum
        mask[d, N_LEVELS + 1] = cum & (idx[None, :] != idx[:, None])
    return jnp.asarray(np.concatenate([cs] * 3, axis=2), BF16), jnp.asarray(mask)


def _split3(x):
    hi = x.astype(BF16)
    r = x - hi.astype(F32)
    mid = r.astype(BF16)
    return hi, mid, (r - mid.astype(F32)).astype(BF16)


def _split3_rows(x):
    return jnp.concatenate(_split3(x), axis=0)


def _split3_lanes(x):
    return jnp.concatenate(_split3(x), axis=1)


def _nt(a, b):
    return lax.dot_general(a.astype(BF16), b.astype(BF16), (((1,), (1,)), ((), ())),
                           preferred_element_type=F32)


def _tn(a, b):
    return lax.dot_general(a.astype(BF16), b.astype(BF16), (((0,), (0,)), ((), ())),
                           preferred_element_type=F32)


def _nn(a, b):
    return jnp.dot(a.astype(BF16), b.astype(BF16), preferred_element_type=F32)


def _hgrn_prep_kernel(p_ref, lb_ref, q_ref, v_ref, g_ref, sg_ref):
    w = GROUP_W
    q = p_ref[:, 0:w]
    q_ref[...] = (q * jax.nn.sigmoid(q) * (HEAD_W ** -0.5)).astype(BF16)
    for d in range(2):
        lb = lb_ref[d:d + 1, :]
        f = p_ref[:, (1 + d) * w:(2 + d) * w]
        g_ref[d] = jnp.log(lb + (1.0 - lb) * jax.nn.sigmoid(f))
    v_ref[...] = p_ref[:, 3 * w:4 * w].astype(BF16)
    gate = p_ref[:, 4 * w:5 * w]
    sg_ref[...] = (gate * jax.nn.sigmoid(gate)).astype(BF16)


def hgrn_prep(proj, col_block, lb, tm):
    m = proj.shape[0]
    w = GROUP_W
    ospec = pl.BlockSpec((tm, w), lambda i: (i, 0))
    return pl.pallas_call(
        _hgrn_prep_kernel,
        out_shape=(jax.ShapeDtypeStruct((m, w), BF16), jax.ShapeDtypeStruct((m, w), BF16),
                   jax.ShapeDtypeStruct((2, m, w), F32), jax.ShapeDtypeStruct((m, w), BF16)),
        grid=(m // tm,),
        in_specs=[pl.BlockSpec((tm, 5 * w), lambda i: (i, col_block)),
                  pl.BlockSpec((2, w), lambda i: (0, 0))],
        out_specs=(ospec, ospec, pl.BlockSpec((2, tm, w), lambda i: (0, i, 0)), ospec),
        compiler_params=_cparams("arbitrary"),
        name="hgrn_prep",
    )(proj, lb)


def _head(x, h):
    return x[:, h * HEAD_W:(h + 1) * HEAD_W]


def _chunk_rows(ci, nch, d):
    return pl.multiple_of((ci + d * (nch - 1 - 2 * ci)) * CHUNK, CHUNK)


def _hgrn_scan_kernel(cs_ref, mask_ref, q_ref, v_ref, g_ref, o_ref, s_ref, *, nch):
    c = CHUNK
    d = pl.program_id(1)
    heads = range(N_HEADS)
    chunks = range(nch)

    @pl.when(pl.program_id(2) == 0)
    def _():
        s_ref[...] = jnp.zeros(s_ref.shape, F32)

    eye = mask_ref[0, N_LEVELS] - mask_ref[0, N_LEVELS + 1]
    rows = [_chunk_rows(ci, nch, d) for ci in chunks]
    g = [g_ref[0, pl.ds(r0, c), :] for r0 in rows]
    q = [q_ref[pl.ds(r0, c), :].astype(F32) for r0 in rows]
    v = [v_ref[pl.ds(r0, c), :] for r0 in rows]
    big = [jnp.dot(cs_ref[0], _split3_rows(gi), preferred_element_type=F32) for gi in g]
    b = [x[0:c] for x in big]
    tot = [x[(N_LEVELS + 1) * c:(N_LEVELS + 2) * c] for x in big]
    k = [1.0 - jnp.exp(gi) for gi in g]
    a = [[eye * jnp.sum(_head(q[ci] * k[ci], h), axis=-1, keepdims=True) for h in heads] for ci in chunks]
    q16 = [q_ref[pl.ds(r0, c), :] for r0 in rows]
    k16 = [ki.astype(BF16) for ki in k]
    for l in range(N_LEVELS):
        e = [jnp.exp(-jnp.abs(b[ci] - big[ci][(l + 1) * c:(l + 2) * c])).astype(BF16) for ci in chunks]
        qe = [q16[ci] * e[ci] for ci in chunks]
        ke = [k16[ci] * e[ci] for ci in chunks]
        lm = mask_ref[0, l]
        a = [[a[ci][h] + lm * _nt(_head(qe[ci], h), _head(ke[ci], h)) for h in heads] for ci in chunks]
    o_intra = [[_nn(a[ci][h], _head(v[ci], h)) for h in heads] for ci in chunks]
    qb = [(q[ci] * jnp.exp(b[ci])).astype(BF16) for ci in chunks]
    kd = [(k[ci] * jnp.exp(tot[ci] - b[ci])).astype(BF16) for ci in chunks]
    dec = [jnp.exp(tot[ci][0:1, :]) for ci in chunks]
    s = [s_ref[h] for h in heads]
    for ci in chunks:
        for h in heads:
            o = o_intra[ci][h] + _nt(_head(qb[ci], h), s[h])
            o_ref[0, pl.ds(rows[ci], c), h * HEAD_W:(h + 1) * HEAD_W] = o
        s = [s[h] * _head(dec[ci], h) + _tn(_head(v[ci], h), _head(kd[ci], h)) for h in heads]
    for h in heads:
        s_ref[h] = s[h]


def _scan_row_block(nblk):
    def row(b, d, t):
        return b * nblk + t + d * (nblk - 1 - 2 * t)
    return row


def hgrn_scan(q, v, g, consts, nb, lp, tb):
    m, w = q.shape
    cs, mask = consts
    row = _scan_row_block(lp // tb)
    return pl.pallas_call(
        functools.partial(_hgrn_scan_kernel, nch=tb // CHUNK),
        out_shape=jax.ShapeDtypeStruct((2, m, w), F32),
        grid=(nb, 2, lp // tb),
        in_specs=[pl.BlockSpec((1,) + cs.shape[1:], lambda b, d, t: (d, 0, 0)),
                  pl.BlockSpec((1,) + mask.shape[1:], lambda b, d, t: (d, 0, 0, 0)),
                  pl.BlockSpec((tb, w), lambda b, d, t: (row(b, d, t), 0)),
                  pl.BlockSpec((tb, w), lambda b, d, t: (row(b, d, t), 0)),
                  pl.BlockSpec((1, tb, w), lambda b, d, t: (d, row(b, d, t), 0))],
        out_specs=pl.BlockSpec((1, tb, w), lambda b, d, t: (d, row(b, d, t), 0)),
        scratch_shapes=[pltpu.VMEM((N_HEADS, HEAD_W, HEAD_W), F32)],
        compiler_params=_cparams("arbitrary", "arbitrary", "arbitrary"),
        name="hgrn_scan",
    )(cs, mask, q, v, g)


def _gated_norm_kernel(o_ref, sg_ref, g_ref, out_ref):
    o = o_ref[0] + o_ref[1]
    gain = g_ref[...]
    for h in range(N_HEADS):
        sl = slice(h * HEAD_W, (h + 1) * HEAD_W)
        oh = o[:, sl]
        y = oh * lax.rsqrt(jnp.mean(oh * oh, axis=-1, keepdims=True) + RMS_EPS) * gain
        out_ref[:, sl] = (y * sg_ref[:, sl].astype(F32)).astype(out_ref.dtype)


def gated_norm(o2, sgate, norm_g, tm):
    m = sgate.shape[0]
    w = GROUP_W
    return pl.pallas_call(
        _gated_norm_kernel,
        out_shape=jax.ShapeDtypeStruct((m, w), BF16),
        grid=(m // tm,),
        in_specs=[pl.BlockSpec((2, tm, w), lambda i: (0, i, 0)),
                  pl.BlockSpec((tm, w), lambda i: (i, 0)),
                  pl.BlockSpec((1, HEAD_W), lambda i: (0, 0))],
        out_specs=pl.BlockSpec((tm, w), lambda i: (i, 0)),
        compiler_params=_cparams("arbitrary"),
        name="gated_rms_norm",
    )(o2, sgate, norm_g.reshape(1, HEAD_W))


GATE_LANES = 128


def _softplus(x):
    return jnp.maximum(x, 0.0) + jnp.log(1.0 + jnp.exp(-jnp.abs(x)))


def _gdn_prep_kernel(p_ref, pp_ref, pn_ref, cw_ref, acoef_ref, dtb_ref,
                     q_ref, k_ref, v_ref, sz_ref, gates_ref, *, tiles_per_seq, padf):
    w = GROUP_W
    i = pl.program_id(0)
    tm = p_ref.shape[0]
    nmask = jnp.where(i == pl.num_programs(0) - 1, 0.0, 1.0)
    x = _conv3_rows(p_ref[:, 0:3 * w], pp_ref[HALO - 1:HALO, :], pn_ref[0:1, :] * nmask, cw_ref[...])
    x = x * jax.nn.sigmoid(x)
    valid = _valid_rows(i, tiles_per_seq, tm, padf)
    for h in range(N_HEADS):
        sl = slice(h * HEAD_W, (h + 1) * HEAD_W)
        qh = x[:, sl]
        q_ref[:, sl] = (qh * lax.rsqrt(jnp.sum(qh * qh, axis=-1, keepdims=True) + 1e-6)
                        * (HEAD_W ** -0.5)).astype(BF16)
        kh = x[:, w + h * HEAD_W:w + (h + 1) * HEAD_W]
        k_ref[:, sl] = (kh * lax.rsqrt(jnp.sum(kh * kh, axis=-1, keepdims=True) + 1e-6)).astype(BF16)
    v_ref[...] = jnp.where(valid, x[:, 2 * w:3 * w], 0.0).astype(BF16)
    z = p_ref[:, 3 * w:4 * w]
    sz_ref[...] = (z * jax.nn.sigmoid(z)).astype(BF16)
    gt = p_ref[:, 4 * w:4 * w + GATE_LANES]
    lane = lax.broadcasted_iota(jnp.int32, (1, GATE_LANES), 1)
    decay = -acoef_ref[...] * _softplus(gt + dtb_ref[...])
    gates_ref[...] = jnp.where(lane < 2 * N_HEADS, jax.nn.sigmoid(gt), decay)


def gdn_prep(pd, conv_w, a_log, dt_bias, lp, padf, tm):
    m = pd.shape[0]
    w = GROUP_W
    zeros8 = jnp.zeros((2 * N_HEADS,), F32)
    padl = jnp.zeros((GATE_LANES - 4 * N_HEADS,), F32)
    acoef = jnp.concatenate([zeros8, jnp.exp(a_log).reshape(-1), padl]).reshape(1, GATE_LANES)
    dtb = jnp.concatenate([zeros8, dt_bias.reshape(-1), padl]).reshape(1, GATE_LANES)
    pspec, nspec = _halo_specs(tm, 3 * w, m)
    ospec = pl.BlockSpec((tm, w), lambda i: (i, 0))
    return pl.pallas_call(
        functools.partial(_gdn_prep_kernel, tiles_per_seq=lp // tm, padf=padf),
        out_shape=tuple(jax.ShapeDtypeStruct((m, w), BF16) for _ in range(4))
        + (jax.ShapeDtypeStruct((m, GATE_LANES), F32),),
        grid=(m // tm,),
        in_specs=[pl.BlockSpec((tm, pd.shape[1]), lambda i: (i, 0)), pspec, nspec,
                  pl.BlockSpec((3, 3 * w), lambda i: (0, 0)),
                  pl.BlockSpec((1, GATE_LANES), lambda i: (0, 0)),
                  pl.BlockSpec((1, GATE_LANES), lambda i: (0, 0))],
        out_specs=(ospec, ospec, ospec, ospec, pl.BlockSpec((tm, GATE_LANES), lambda i: (i, 0))),
        compiler_params=_cparams("arbitrary"),
        name="gdn_prep",
    )(pd, pd, pd, conv_w, acoef, dtb)


def _gate_selectors():
    sel = np.zeros((2, GATE_LANES, 2 * GROUP_W), np.float32)
    for d in range(2):
        for h in range(N_HEADS):
            sel[d, N_HEADS * d + h, h * HEAD_W:(h + 1) * HEAD_W] = 1.0
            sel[d, 2 * N_HEADS + N_HEADS * d + h, GROUP_W + h * HEAD_W:GROUP_W + (h + 1) * HEAD_W] = 1.0
    return jnp.asarray(np.concatenate([sel] * 3, axis=1), BF16)


def _gdn_scan_kernel(cs_ref, mask_ref, sel_ref, q_ref, k_ref, v_ref, gt_ref, o_ref, s_ref, *, nch):
    c = CHUNK
    w = GROUP_W
    d = pl.program_id(1)
    heads = range(N_HEADS)
    chunks = range(nch)
    items = [(ci, h) for ci in chunks for h in heads]

    @pl.when(pl.program_id(2) == 0)
    def _():
        s_ref[...] = jnp.zeros(s_ref.shape, F32)

    incl = mask_ref[0, N_LEVELS]
    strict = mask_ref[0, N_LEVELS + 1]
    eye = incl - strict
    cum_tot = jnp.concatenate([cs_ref[0, 0:c, :], cs_ref[0, (N_LEVELS + 1) * c:(N_LEVELS + 2) * c, :]], axis=0)
    ones3 = cs_ref[0, (N_LEVELS + 1) * c:(N_LEVELS + 2) * c, :]
    rows = [_chunk_rows(ci, nch, d) for ci in chunks]
    q = [q_ref[pl.ds(r0, c), :].astype(F32) for r0 in rows]
    k = [k_ref[pl.ds(r0, c), :].astype(F32) for r0 in rows]
    v = [v_ref[pl.ds(r0, c), :].astype(F32) for r0 in rows]
    bg = [jnp.dot(_split3_lanes(gt_ref[pl.ds(r0, c), :]), sel_ref[0], preferred_element_type=F32)
          for r0 in rows]
    beta = [x[:, 0:w] for x in bg]
    cb = [jnp.dot(cum_tot, _split3_rows(x[:, w:2 * w]), preferred_element_type=F32) for x in bg]
    bb = [x[0:c] for x in cb]
    tot = [x[c:2 * c] for x in cb]
    bc = {(ci, h): bb[ci][:, h * HEAD_W:h * HEAD_W + c] for ci, h in items}
    br = {it: jnp.dot(ones3, _split3_rows(eye * bc[it]), preferred_element_type=F32) for it in items}
    dec = {it: incl * jnp.exp(jnp.minimum(bc[it] - br[it], 0.0)) for it in items}
    kb = [k[ci] * beta[ci] for ci in chunks]
    n = {(ci, h): strict * _nt(_head(kb[ci], h), _head(k[ci], h)) * dec[(ci, h)] for ci, h in items}
    t = {it: eye - n[it] for it in items}
    pw = {it: _nn(n[it], n[it]) for it in items}
    for _ in range(N_LEVELS - 2):
        t = {it: t[it] + _nn(t[it], pw[it]) for it in items}
        pw = {it: _nn(pw[it], pw[it]) for it in items}
    t = {it: t[it] + _nn(t[it], pw[it]) for it in items}
    eb = [jnp.exp(x) for x in bb]
    rhs_u = [v[ci] * beta[ci] for ci in chunks]
    rhs_w = [kb[ci] * eb[ci] for ci in chunks]
    uw = {(ci, h): _nn(t[(ci, h)], jnp.concatenate([_head(rhs_u[ci], h), _head(rhs_w[ci], h)], axis=1))
          for ci, h in items}
    qk = {(ci, h): _nt(_head(q[ci], h), _head(k[ci], h)) * dec[(ci, h)] for ci, h in items}
    qe = [(q[ci] * eb[ci]).astype(BF16) for ci in chunks]
    kd = [(k[ci] * jnp.exp(tot[ci] - bb[ci])).astype(BF16) for ci in chunks]
    sdec = [jnp.exp(tot[ci][0:1, :]) for ci in chunks]
    s = [s_ref[h] for h in heads]
    for ci in chunks:
        wq = {h: _nn(jnp.concatenate([uw[(ci, h)][:, HEAD_W:].astype(BF16), _head(qe[ci], h)], axis=0), s[h])
              for h in heads}
        vnew = {h: uw[(ci, h)][:, 0:HEAD_W] - wq[h][0:c] for h in heads}
        for h in heads:
            o_ref[0, pl.ds(rows[ci], c), h * HEAD_W:(h + 1) * HEAD_W] = wq[h][c:2 * c] + _nn(qk[(ci, h)], vnew[h])
        s = [s[h] * _head(sdec[ci], h) + _tn(_head(kd[ci], h), vnew[h]) for h in heads]
    for h in heads:
        s_ref[h] = s[h]


def gdn_scan(q, k, v, gates, consts, sel, nb, lp, tb):
    m, w = q.shape
    cs, mask = consts
    row = _scan_row_block(lp // tb)
    rspec = pl.BlockSpec((tb, w), lambda b, d, t: (row(b, d, t), 0))
    return pl.pallas_call(
        functools.partial(_gdn_scan_kernel, nch=tb // CHUNK),
        out_shape=jax.ShapeDtypeStruct((2, m, w), F32),
        grid=(nb, 2, lp // tb),
        in_specs=[pl.BlockSpec((1,) + cs.shape[1:], lambda b, d, t: (d, 0, 0)),
                  pl.BlockSpec((1,) + mask.shape[1:], lambda b, d, t: (d, 0, 0, 0)),
                  pl.BlockSpec((1,) + sel.shape[1:], lambda b, d, t: (d, 0, 0)),
                  rspec, rspec, rspec,
                  pl.BlockSpec((tb, GATE_LANES), lambda b, d, t: (row(b, d, t), 0))],
        out_specs=pl.BlockSpec((1, tb, w), lambda b, d, t: (d, row(b, d, t), 0)),
        scratch_shapes=[pltpu.VMEM((N_HEADS, HEAD_W, HEAD_W), F32)],
        compiler_params=_cparams("arbitrary", "arbitrary", "arbitrary"),
        name="gdn_scan",
    )(cs, mask, sel, q, k, v, gates)


FFT_N2 = 256
FFT_TS = 8


def _fft_n1(lp, l_real):
    need = -(-(2 * l_real - 1) // FFT_N2)
    return max(-(-need // 8) * 8, -(-(lp // FFT_N2) // 8) * 8)


def _outer_dft(n1, k_rows, inverse, n_total):
    kp = -(-k_rows // 16) * 16
    a = np.arange(n1)[:, None] * np.arange(kp)[None, :]
    ang = 2.0 * np.pi * (a % n1) / n1
    c, s = np.cos(ang), np.sin(ang)
    live = (np.arange(kp) < k_rows)[None, :]
    c, s = c * live, s * live
    if not inverse:
        return jnp.asarray(np.concatenate([c, -s], axis=0), BF16)
    return jnp.asarray(np.concatenate([c.T, -s.T], axis=1) / n_total, F32)


def inner_dft(n1):
    n2 = FFT_N2
    n = n1 * n2
    k = jnp.arange(n1, dtype=jnp.int32)[:, None, None] + n1 * jnp.arange(n2, dtype=jnp.int32)[None, :, None]
    r = (k * jnp.arange(n2, dtype=jnp.int32)[None, None, :]) % n
    ang = r.astype(F32) * (2.0 * math.pi / n)
    gr, gi = jnp.cos(ang), -jnp.sin(ang)
    gg = jnp.concatenate([jnp.concatenate([gr, -gi], axis=2), jnp.concatenate([gi, gr], axis=2)], axis=1)
    return gg.astype(BF16), jnp.swapaxes(gg, 1, 2).astype(BF16)


def _hyena_prep_kernel(p_ref, pp_ref, pn_ref, cw_ref, cb_ref, x0_ref, z_ref, *, tiles_per_seq, padf):
    w = GROUP_W
    i = pl.program_id(0)
    tm = p_ref.shape[0]
    nmask = jnp.where(i == pl.num_programs(0) - 1, 0.0, 1.0)
    u = _conv3_rows(p_ref[...], pp_ref[HALO - 1:HALO, :], pn_ref[0:1, :] * nmask, cw_ref[...]) + cb_ref[...]
    valid = _valid_rows(i, tiles_per_seq, tm, padf)
    x0_ref[...] = u[:, 0:w]
    z_ref[...] = jnp.where(valid, u[:, w:2 * w] * u[:, 2 * w:3 * w], 0.0)


def hyena_prep(pb, conv_w, conv_b, lp, padf, tm):
    m, w3 = pb.shape
    w = GROUP_W
    pspec, nspec = _halo_specs(tm, w3, m)
    ospec = pl.BlockSpec((tm, w), lambda i: (i, 0))
    return pl.pallas_call(
        functools.partial(_hyena_prep_kernel, tiles_per_seq=lp // tm, padf=padf),
        out_shape=(jax.ShapeDtypeStruct((m, w), F32), jax.ShapeDtypeStruct((m, w), F32)),
        grid=(m // tm,),
        in_specs=[pl.BlockSpec((tm, w3), lambda i: (i, 0)), pspec, nspec,
                  pl.BlockSpec((3, w3), lambda i: (0, 0)), pl.BlockSpec((1, w3), lambda i: (0, 0))],
        out_specs=(ospec, ospec),
        compiler_params=_cparams("arbitrary"),
        name="hyena_prep",
    )(pb, pb, pb, conv_w, conv_b.reshape(1, w3))


def _hyena_filter_kernel(w1t_ref, w1c_ref, w1s_ref, b1_ref, f1_ref, w2_ref, b2_ref, f2_ref, w3_ref, dec_ref,
                         filt_ref, asum_ref, *, l_real):
    i = pl.program_id(0)
    tm = filt_ref.shape[0]
    w = GROUP_W
    row = i * tm + lax.broadcasted_iota(jnp.int32, (tm, 1), 0)
    rf = row.astype(F32)
    t = rf * (1.0 / (l_real - 1))
    band = lax.broadcasted_iota(jnp.int32, (1, HY_BANDS), 1).astype(F32)
    bands = 1e-4 + band * ((HY_BANDS - 1 - 1e-4) / (HY_BANDS - 1))
    ang = ((2.0 * math.pi / l_real) * rf) * bands

    def hdot(a, b):
        return jnp.dot(a, b, precision=HIGHEST, preferred_element_type=F32)

    pre = t * w1t_ref[...] + hdot(jnp.cos(ang), w1c_ref[...]) - hdot(jnp.sin(ang), w1s_ref[...])
    hid = jnp.sin(f1_ref[...] * (pre + b1_ref[...]))
    hid = jnp.sin(f2_ref[...] * (hdot(hid, w2_ref[...]) + b2_ref[...]))
    filt = hdot(hid, w3_ref[...])
    window = jnp.exp(-t * jnp.abs(dec_ref[...])) + HY_SHIFT
    filt = jnp.where(row < l_real, filt * window, 0.0)

    @pl.when(i == 0)
    def _():
        asum_ref[...] = jnp.zeros(asum_ref.shape, F32)

    asum_ref[...] += jnp.sum(jnp.abs(filt), axis=0, keepdims=True)
    lane = lax.broadcasted_iota(jnp.int32, (1, 2 * w), 1)
    filt_ref[...] = jnp.where((row == 0) & (lane >= w), 0.0, filt)


def hyena_filter(w1, b1, f1, w2, b2, f2, w3, decay, lp, l_real, tm):
    w = GROUP_W
    nf = w1.shape[1]
    small = lambda a: pl.BlockSpec(a.shape, lambda i: (0,) * a.ndim)
    args = (w1[0:1], w1[1:1 + HY_BANDS], w1[1 + HY_BANDS:], b1.reshape(1, nf), f1.reshape(1, nf),
            w2, b2.reshape(1, nf), f2.reshape(1, nf), w3, decay.reshape(1, 2 * w))
    return pl.pallas_call(
        functools.partial(_hyena_filter_kernel, l_real=l_real),
        out_shape=(jax.ShapeDtypeStruct((lp, 2 * w), F32), jax.ShapeDtypeStruct((1, 2 * w), F32)),
        grid=(lp // tm,),
        in_specs=[small(a) for a in args],
        out_specs=(pl.BlockSpec((tm, 2 * w), lambda i: (i, 0)), pl.BlockSpec((1, 2 * w), lambda i: (0, 0))),
        compiler_params=_cparams("arbitrary"),
        name="hyena_filter",
    )(*args)


def _fft_outer_kernel(f_ref, x_ref, o_ref, pad_ref):
    k = x_ref.shape[1]
    pad_ref[...] = jnp.zeros(pad_ref.shape, F32)
    for s in range(x_ref.shape[2]):
        pad_ref[0:k, :] = x_ref[0, :, s, :]
        o_ref[0, :, s, :] = jnp.dot(f_ref[...], pad_ref[...].astype(BF16), preferred_element_type=F32)


def fft_outer(x, fmat):
    b, k, n2, c = x.shape
    rows, kp = fmat.shape
    ts = FFT_TS
    return pl.pallas_call(
        _fft_outer_kernel,
        out_shape=jax.ShapeDtypeStruct((b, rows, n2, c), F32),
        grid=(b, n2 // ts),
        in_specs=[pl.BlockSpec((rows, kp), lambda bi, j: (0, 0)),
                  pl.BlockSpec((1, k, ts, c), lambda bi, j: (bi, 0, j, 0))],
        out_specs=pl.BlockSpec((1, rows, ts, c), lambda bi, j: (bi, 0, j, 0)),
        scratch_shapes=[pltpu.VMEM((kp, c), F32)],
        compiler_params=_cparams("arbitrary", "arbitrary"),
        name="fft_outer",
    )(fmat, x)


def _filter_spectrum_kernel(gg_ref, a_ref, asum_ref, o_ref):
    w = GROUP_W
    n2 = FFT_N2
    a = jnp.concatenate([a_ref[0, 0], a_ref[1, 0]], axis=0).astype(BF16)
    x = jnp.dot(gg_ref[0], a, preferred_element_type=F32)
    s = 1.0 / (asum_ref[:, 0:w] + asum_ref[:, w:2 * w] + 1e-6)
    o_ref[0, 0] = (x[0:n2, 0:w] + x[0:n2, w:2 * w]) * s
    o_ref[0, 1] = (x[n2:2 * n2, 0:w] - x[n2:2 * n2, w:2 * w]) * s


def filter_spectrum(a, gg, asum):
    _, n1, n2, w2 = a.shape
    w = w2 // 2
    return pl.pallas_call(
        _filter_spectrum_kernel,
        out_shape=jax.ShapeDtypeStruct((n1, 2, n2, w), F32),
        grid=(n1,),
        in_specs=[pl.BlockSpec((1, 2 * n2, 2 * n2), lambda k: (k, 0, 0)),
                  pl.BlockSpec((2, 1, n2, w2), lambda k: (0, k, 0, 0)),
                  pl.BlockSpec((1, w2), lambda k: (0, 0))],
        out_specs=pl.BlockSpec((1, 2, n2, w), lambda k: (k, 0, 0, 0)),
        compiler_params=_cparams("arbitrary"),
        name="hyena_filter_spectrum",
    )(gg, a, asum)


def _fft_mid_kernel(gg_ref, ggt_ref, kf_ref, a_ref, o_ref):
    n2 = FFT_N2
    a = jnp.concatenate([a_ref[0, 0, 0], a_ref[0, 1, 0]], axis=0).astype(BF16)
    x = jnp.dot(gg_ref[0], a, preferred_element_type=F32)
    xr, xi = x[0:n2], x[n2:2 * n2]
    kr, ki = kf_ref[0, 0], kf_ref[0, 1]
    y = jnp.concatenate([xr * kr - xi * ki, xr * ki + xi * kr], axis=0).astype(BF16)
    bm = jnp.dot(ggt_ref[0], y, preferred_element_type=F32)
    o_ref[0, 0, 0] = bm[0:n2]
    o_ref[0, 1, 0] = bm[n2:2 * n2]


def fft_mid(a, gg, ggt, kf):
    b, _, n1, n2, w = a.shape
    aspec = pl.BlockSpec((1, 2, 1, n2, w), lambda k, bi: (bi, 0, k, 0, 0))
    gspec = pl.BlockSpec((1, 2 * n2, 2 * n2), lambda k, bi: (k, 0, 0))
    return pl.pallas_call(
        _fft_mid_kernel,
        out_shape=jax.ShapeDtypeStruct(a.shape, F32),
        grid=(n1, b),
        in_specs=[gspec, gspec, pl.BlockSpec((1, 2, n2, w), lambda k, bi: (k, 0, 0, 0)), aspec],
        out_specs=aspec,
        compiler_params=_cparams("arbitrary", "arbitrary"),
        name="fft_inner_conv",
    )(gg, ggt, kf, a)


def _fft_final_kernel(f_ref, b_ref, x0_ref, z_ref, d_ref, o_ref):
    k = o_ref.shape[1]
    for s in range(o_ref.shape[2]):
        y = jnp.dot(f_ref[...], b_ref[0, :, s, :], precision=HIGHEST, preferred_element_type=F32)
        o_ref[0, :, s, :] = x0_ref[0, :, s, :] * (y[0:k] + z_ref[0, :, s, :] * d_ref[...])


def fft_final(bm, finv, x0, z, d):
    b, rows, n2, c = bm.shape
    k = x0.shape[1]
    kp = finv.shape[0]
    ts = FFT_TS
    xspec = pl.BlockSpec((1, k, ts, c), lambda bi, j: (bi, 0, j, 0))
    return pl.pallas_call(
        _fft_final_kernel,
        out_shape=jax.ShapeDtypeStruct((b, k, n2, c), F32),
        grid=(b, n2 // ts),
        in_specs=[pl.BlockSpec((kp, rows), lambda bi, j: (0, 0)),
                  pl.BlockSpec((1, rows, ts, c), lambda bi, j: (bi, 0, j, 0)),
                  xspec, xspec, pl.BlockSpec((1, c), lambda bi, j: (0, 0))],
        out_specs=xspec,
        compiler_params=_cparams("arbitrary", "arbitrary"),
        name="fft_outer_inverse_gate",
    )(finv, bm, x0, z, d)


def hyena_mixer(pb, hp, nb, lp, padf, l_real, tm):
    w = GROUP_W
    n2 = FFT_N2
    k1 = lp // n2
    n1 = _fft_n1(lp, l_real)
    n_total = n1 * n2
    ffwd = _outer_dft(n1, k1, False, n_total)
    finv = _outer_dft(n1, k1, True, n_total)
    gg, ggt = hp['gg'], hp['ggt']
    filt, asum = hyena_filter(hp['w1'], hp['b1'], hp['f1'], hp['w2'], hp['b2'], hp['f2'], hp['w3'],
                              hp['decay'], lp, l_real, tm)
    fa = fft_outer(filt.reshape(1, k1, n2, 2 * w), ffwd)
    kf = filter_spectrum(fa.reshape(2, n1, n2, 2 * w), gg, asum)
    x0, z = hyena_prep(pb, hp['conv_w'], hp['conv_b'], lp, padf, tm)
    z4 = z.reshape(nb, k1, n2, w)
    a = fft_outer(z4, ffwd)
    bm = fft_mid(a.reshape(nb, 2, n1, n2, w), gg, ggt, kf)
    out = fft_final(bm.reshape(nb, 2 * n1, n2, w), finv, x0.reshape(nb, k1, n2, w), z4, hp['d'].reshape(1, w))
    return out.reshape(nb * lp, w)


TM = 640
ATTN_TQ = 1280
ATTN_TKB = 1280
ATTN_UNROLL = 4
SCAN_TB = 640
FFN_TN = 512
FFN_DOWN_TM = 320


def kernel(x_prompt, x_sample, meta, emb_ln_g, emb_ln_b, w_in, lam_q1, lam_k1, lam_q2, lam_k2, attn_norm_g, hy_conv_w, hy_conv_b, hy_w1, hy_b1, hy_f1, hy_w2, hy_b2, hy_f2, hy_w3, hy_decay, hy_d, hg_lb, hg_norm_g, gdn_conv_w, gdn_a_log, gdn_dt_bias, gdn_norm_g, w_out, ln1_g, ln1_b, w_up, ffn_conv_w, ffn_conv_b, w_down, ln2_g, ln2_b):
    n_prompt = x_prompt.shape[0]
    x = jnp.concatenate([x_prompt, x_sample], axis=0)
    nb, seq, d = x.shape
    padf = ROW_ALIGN - N_META
    lp = seq + ROW_ALIGN
    l_real = seq + N_META
    m = nb * lp
    w = GROUP_W

    h, hb = (a.reshape(m, d) for a in embed(x, meta, emb_ln_g, emb_ln_b, padf))
    tables = rope_tables(lp, padf)
    consts = _scan_constants()
    sel = _gate_selectors()
    gg, ggt = inner_dft(_fft_n1(lp, l_real))
    sm = jax.nn.softmax(hg_lb, axis=0)
    lb_all = jnp.cumsum(sm, axis=0) - sm[0]

    for l in range(DEPTH):
        wl = w_in[l].astype(BF16)
        wd = jnp.pad(wl[:, 11 * w:], ((0, 0), (0, 4 * w + GATE_LANES - (wl.shape[1] - 11 * w))))
        pa = matmul(hb, wl[:, 0:3 * w], TM, 3 * w)
        pb = matmul(hb, wl[:, 3 * w:6 * w], TM, 3 * w)
        pc = matmul(hb, wl[:, 6 * w:11 * w], TM, 5 * w // 2)
        pd = matmul(hb, wd, TM, wd.shape[1])

        qlo, qhi, ka, va = attn_prep(pa, tables, nb, lp, TM)
        lam_vecs = jnp.stack([lam_q1[l], lam_k1[l], lam_q2[l], lam_k2[l]])
        oa = diff_attention(qlo, qhi, ka, va, lam_vecs, attn_norm_g[l], nb, lp, padf, l,
                            ATTN_TQ, ATTN_TKB, ATTN_UNROLL)

        hp = dict(gg=gg, ggt=ggt, w1=hy_w1[l], b1=hy_b1[l], f1=hy_f1[l], w2=hy_w2[l], b2=hy_b2[l],
                  f2=hy_f2[l], w3=hy_w3[l], decay=hy_decay[l], d=hy_d[l], conv_w=hy_conv_w[l],
                  conv_b=hy_conv_b[l])
        ob = hyena_mixer(pb, hp, nb, lp, padf, l_real, TM)

        qc, vc, gc, sgc = hgrn_prep(pc, 0, lb_all[l], TM)
        oc = gated_norm(hgrn_scan(qc, vc, gc, consts, nb, lp, SCAN_TB), sgc, hg_norm_g[l], TM)

        qd, kd, vd, szd, gates = gdn_prep(pd, gdn_conv_w[l], gdn_a_log[l], gdn_dt_bias[l], lp, padf, TM)
        od = gated_norm(gdn_scan(qd, kd, vd, gates, consts, sel, nb, lp, SCAN_TB), szd, gdn_norm_g[l], TM)

        h, hb = proj_residual_ln([oa, ob, oc, od], w_out[l].astype(BF16), h, ln1_g[l], ln1_b[l], lp, padf, TM)
        act = ffn_up(hb, w_up[l].astype(BF16), ffn_conv_w[l], ffn_conv_b[l], TM, FFN_TN)
        h, hb = ffn_down_ln(act, w_down[l].astype(BF16), h, ln2_g[l], ln2_b[l], lp, padf, FFN_DOWN_TM)

    y = h.reshape(nb, lp, d)[:, ROW_ALIGN:]
    return (y[:n_prompt], y[n_prompt:])
```

```python
import functools
import math

import jax
import jax.numpy as jnp
import numpy as np
from jax import lax
from jax.experimental import pallas as pl
from jax.experimental.pallas import tpu as pltpu

D_MODEL = 2048
DEPTH = 4
N_META = 16
GROUP_W = 512
N_HEADS = 4
HEAD_W = 128
DA_HD = 64
ROT_DIM = 16
ROPE_THETA = 500000.0
HY_BANDS = 16
HY_SHIFT = 0.05
CHUNK = 64
D_FF = 5632
ALPHA = (2.0 * DEPTH) ** 0.25
LN_EPS = 1e-5
RMS_EPS = 1e-6
F32 = jnp.float32
BF16 = jnp.bfloat16
HIGHEST = lax.Precision.HIGHEST

ROW_ALIGN = 256
VMEM_LIMIT = 56 * 1024 * 1024


def _cparams(*sem):
    return pltpu.CompilerParams(dimension_semantics=sem, vmem_limit_bytes=VMEM_LIMIT)


def _valid_rows(tile_idx, tiles_per_seq, tm, padf):
    base = (tile_idx % tiles_per_seq) * tm
    return (base + lax.broadcasted_iota(jnp.int32, (tm, 1), 0)) >= padf


def _ln_rows(y, g, b):
    mu = jnp.mean(y, axis=-1, keepdims=True)
    d = y - mu
    var = jnp.mean(d * d, axis=-1, keepdims=True)
    return d * lax.rsqrt(var + LN_EPS) * g + b


def _embed_kernel(x_ref, meta_ref, g_ref, b_ref, o_ref, ob_ref, *, padf):
    t = pl.program_id(1)
    g = g_ref[...]
    b = b_ref[...]

    @pl.when(t == 0)
    def _():
        y = jnp.concatenate([jnp.zeros((padf, o_ref.shape[2]), F32), _ln_rows(meta_ref[...], g, b)], axis=0)
        o_ref[0] = y
        ob_ref[0] = y.astype(BF16)

    @pl.when(t > 0)
    def _():
        y = _ln_rows(x_ref[0], g, b)
        o_ref[0] = y
        ob_ref[0] = y.astype(BF16)


def embed(x, meta, g, b, padf):
    nb, s, d = x.shape
    tm = ROW_ALIGN
    assert padf + N_META == tm and s % tm == 0
    lp = s + tm
    ospec = pl.BlockSpec((1, tm, d), lambda bi, t: (bi, t, 0))
    return pl.pallas_call(
        functools.partial(_embed_kernel, padf=padf),
        out_shape=(jax.ShapeDtypeStruct((nb, lp, d), F32), jax.ShapeDtypeStruct((nb, lp, d), BF16)),
        grid=(nb, lp // tm),
        in_specs=[pl.BlockSpec((1, tm, d), lambda bi, t: (bi, jnp.maximum(t - 1, 0), 0)),
                  pl.BlockSpec((N_META, d), lambda bi, t: (0, 0)),
                  pl.BlockSpec((1, d), lambda bi, t: (0, 0)),
                  pl.BlockSpec((1, d), lambda bi, t: (0, 0))],
        out_specs=(ospec, ospec),
        compiler_params=_cparams("arbitrary", "arbitrary"),
        name="embed_ln",
    )(x, meta, g.reshape(1, d), b.reshape(1, d))


def _mm_kernel(x_ref, w_ref, o_ref):
    o_ref[...] = jnp.dot(x_ref[...], w_ref[...], preferred_element_type=F32).astype(o_ref.dtype)


def matmul(x, w, tm, tn, out_dtype=F32):
    m, k = x.shape
    n = w.shape[1]
    assert m % tm == 0 and n % tn == 0
    return pl.pallas_call(
        _mm_kernel,
        out_shape=jax.ShapeDtypeStruct((m, n), out_dtype),
        grid=(n // tn, m // tm),
        in_specs=[pl.BlockSpec((tm, k), lambda j, i: (i, 0)),
                  pl.BlockSpec((k, tn), lambda j, i: (0, j))],
        out_specs=pl.BlockSpec((tm, tn), lambda j, i: (i, j)),
        compiler_params=_cparams("arbitrary", "arbitrary"),
        name="dense_matmul",
    )(x, w)


def _proj_ln_kernel(*refs, n_in, tiles_per_seq, padf):
    o_refs = refs[:n_in]
    w_ref, h_ref, g_ref, b_ref, out_ref, outb_ref = refs[n_in:]
    tm = h_ref.shape[0]
    acc = None
    off = 0
    for r in o_refs:
        kw = r.shape[1]
        part = jnp.dot(r[...].astype(BF16), w_ref[off:off + kw, :], preferred_element_type=F32)
        acc = part if acc is None else acc + part
        off += kw
    y = _ln_rows(ALPHA * h_ref[...] + acc, g_ref[...], b_ref[...])
    valid = _valid_rows(pl.program_id(0), tiles_per_seq, tm, padf)
    y = jnp.where(valid, y, 0.0)
    out_ref[...] = y
    outb_ref[...] = y.astype(BF16)


def proj_residual_ln(parts, w, h, g, b, lp, padf, tm):
    m, d = h.shape
    assert m % tm == 0 and lp % tm == 0
    n_in = len(parts)
    in_specs = [pl.BlockSpec((tm, p.shape[1]), lambda i: (i, 0)) for p in parts]
    in_specs += [pl.BlockSpec(w.shape, lambda i: (0, 0)),
                 pl.BlockSpec((tm, d), lambda i: (i, 0)),
                 pl.BlockSpec((1, d), lambda i: (0, 0)),
                 pl.BlockSpec((1, d), lambda i: (0, 0))]
    ospec = pl.BlockSpec((tm, d), lambda i: (i, 0))
    return pl.pallas_call(
        functools.partial(_proj_ln_kernel, n_in=n_in, tiles_per_seq=lp // tm, padf=padf),
        out_shape=(jax.ShapeDtypeStruct((m, d), F32), jax.ShapeDtypeStruct((m, d), BF16)),
        grid=(m // tm,),
        in_specs=in_specs,
        out_specs=(ospec, ospec),
        compiler_params=_cparams("arbitrary"),
        name="proj_residual_ln",
    )(*parts, w, h, g.reshape(1, d), b.reshape(1, d))


HALO = 8
HALO_BF16 = 16


def _halo_specs(tm, width, m, col_block=0, halo=HALO):
    nblk = m // halo
    per = tm // halo

    def prev_map(*idx):
        i = idx[-1]
        return (jnp.maximum(i * per - 1, 0), col_block)

    def next_map(*idx):
        i = idx[-1]
        return (jnp.minimum((i + 1) * per, nblk - 1), col_block)

    return (pl.BlockSpec((halo, width), prev_map), pl.BlockSpec((halo, width), next_map))


def _conv3_rows(pm, prev_row, next_row, cw):
    tm = pm.shape[0]
    rid = lax.broadcasted_iota(jnp.int32, (tm, 1), 0)
    down = jnp.where(rid == 0, prev_row, pltpu.roll(pm, 1, 0))
    up = jnp.where(rid == tm - 1, next_row, pltpu.roll(pm, tm - 1, 0))
    return down * cw[0:1, :] + pm * cw[1:2, :] + up * cw[2:3, :]


def _ffn_up_kernel(x_ref, xp_ref, xn_ref, wg_ref, wu_ref, cwg_ref, cwu_ref, cbg_ref, cbu_ref, o_ref):
    i = pl.program_id(1)
    last = pl.num_programs(1) - 1
    x = x_ref[...]
    hr = xp_ref.shape[0]
    halo = jnp.concatenate([xp_ref[...], xn_ref[...]], axis=0)
    nmask = jnp.where(i == last, 0.0, 1.0)

    def branch(w_ref, cw_ref, cb_ref, sl):
        w = w_ref[:, sl]
        pm = jnp.dot(x, w, preferred_element_type=F32)
        ph = jnp.dot(halo, w, preferred_element_type=F32)
        return _conv3_rows(pm, ph[hr - 1:hr, :], ph[hr:hr + 1, :] * nmask, cw_ref[:, sl]) + cb_ref[:, sl]

    tn = o_ref.shape[1]
    for c0 in range(0, tn, FFN_SUB):
        sl = slice(c0, min(c0 + FFN_SUB, tn))
        g = branch(wg_ref, cwg_ref, cbg_ref, sl)
        u = branch(wu_ref, cwu_ref, cbu_ref, sl)
        o_ref[:, sl] = (g * jax.nn.sigmoid(g) * u).astype(o_ref.dtype)


FFN_SUB = 256


def ffn_up(h, w_up, conv_w, conv_b, tm, tn):
    m, k = h.shape
    f = w_up.shape[1] // 2
    assert m % tm == 0 and f % tn == 0 and h.dtype == BF16
    nj = f // tn
    xp_spec, xn_spec = _halo_specs(tm, k, m, halo=HALO_BF16)
    cb = conv_b.reshape(1, 2 * f)
    return pl.pallas_call(
        _ffn_up_kernel,
        out_shape=jax.ShapeDtypeStruct((m, f), BF16),
        grid=(nj, m // tm),
        in_specs=[pl.BlockSpec((tm, k), lambda j, i: (i, 0)), xp_spec, xn_spec,
                  pl.BlockSpec((k, tn), lambda j, i: (0, j)),
                  pl.BlockSpec((k, tn), lambda j, i: (0, j + nj)),
                  pl.BlockSpec((3, tn), lambda j, i: (0, j)),
                  pl.BlockSpec((3, tn), lambda j, i: (0, j + nj)),
                  pl.BlockSpec((1, tn), lambda j, i: (0, j)),
                  pl.BlockSpec((1, tn), lambda j, i: (0, j + nj))],
        out_specs=pl.BlockSpec((tm, tn), lambda j, i: (i, j)),
        compiler_params=_cparams("arbitrary", "arbitrary"),
        name="ffn_up_conv_gate",
    )(h, h, h, w_up, w_up, conv_w, conv_w, cb, cb)


def _ffn_down_kernel(a_ref, w_ref, h_ref, g_ref, b_ref, out_ref, outb_ref, *, tiles_per_seq, padf):
    tm = h_ref.shape[0]
    acc = jnp.dot(a_ref[...], w_ref[...], preferred_element_type=F32)
    y = _ln_rows(ALPHA * h_ref[...] + acc, g_ref[...], b_ref[...])
    valid = _valid_rows(pl.program_id(0), tiles_per_seq, tm, padf)
    y = jnp.where(valid, y, 0.0)
    out_ref[...] = y
    outb_ref[...] = y.astype(BF16)


def ffn_down_ln(a, w_down, h, g, b, lp, padf, tm):
    m, d = h.shape
    f = a.shape[1]
    assert m % tm == 0 and lp % tm == 0
    ospec = pl.BlockSpec((tm, d), lambda i: (i, 0))
    return pl.pallas_call(
        functools.partial(_ffn_down_kernel, tiles_per_seq=lp // tm, padf=padf),
        out_shape=(jax.ShapeDtypeStruct((m, d), F32), jax.ShapeDtypeStruct((m, d), BF16)),
        grid=(m // tm,),
        in_specs=[pl.BlockSpec((tm, f), lambda i: (i, 0)),
                  pl.BlockSpec((f, d), lambda i: (0, 0), pipeline_mode=pl.Buffered(1)),
                  pl.BlockSpec((tm, d), lambda i: (i, 0)),
                  pl.BlockSpec((1, d), lambda i: (0, 0)),
                  pl.BlockSpec((1, d), lambda i: (0, 0))],
        out_specs=(ospec, ospec),
        compiler_params=_cparams("arbitrary"),
        name="ffn_down_residual_ln",
    )(a, w_down, h, g.reshape(1, d), b.reshape(1, d))


def rope_tables(lp, padf):
    half = ROT_DIM // 2
    pos = (jnp.arange(lp) - padf).astype(F32)
    inv = 1.0 / (ROPE_THETA ** (jnp.arange(half, dtype=F32) / half))
    ang = pos[:, None] * inv[None]
    cos, sin = jnp.cos(ang), jnp.sin(ang)
    ones = jnp.ones((lp, DA_HD - ROT_DIM), F32)
    zeros = jnp.zeros((lp, DA_HD - ROT_DIM), F32)
    zh = jnp.zeros((lp, half), F32)
    c = jnp.concatenate([cos, cos, ones], axis=1)
    sa = jnp.concatenate([-sin, zh, zeros], axis=1)
    sb = jnp.concatenate([zh, sin, zeros], axis=1)
    return tuple(jnp.concatenate([t, t], axis=1) for t in (c, sa, sb))


def _attn_prep_kernel(p_ref, c_ref, sa_ref, sb_ref, qlo_ref, qhi_ref, k_ref, vt_ref):
    w = GROUP_W
    reps = w // c_ref.shape[1]
    c = jnp.concatenate([c_ref[...]] * reps, axis=1)
    sa = jnp.concatenate([sa_ref[...]] * reps, axis=1)
    sb = jnp.concatenate([sb_ref[...]] * reps, axis=1)
    half = ROT_DIM // 2

    def rot(x):
        return x * c + pltpu.roll(x, w - half, 1) * sa + pltpu.roll(x, half, 1) * sb

    q = rot(p_ref[:, 0:w]) * (DA_HD ** -0.5)
    lane = lax.broadcasted_iota(jnp.int32, (1, w), 1) % (2 * DA_HD)
    qlo_ref[0] = jnp.where(lane < DA_HD, q, 0.0).T.astype(BF16)
    qhi_ref[0] = jnp.where(lane >= DA_HD, q, 0.0).T.astype(BF16)
    k_ref[...] = rot(p_ref[:, w:2 * w]).astype(BF16)
    vt_ref[0] = p_ref[:, 2 * w:3 * w].T.astype(BF16)


def attn_prep(proj, tables, nb, lp, tm):
    m = proj.shape[0]
    w = GROUP_W
    tps = lp // tm
    tspec = pl.BlockSpec((tm, 2 * DA_HD), lambda i: (i % tps, 0))
    tr_shape = jax.ShapeDtypeStruct((nb, w, lp), BF16)
    tr_spec = pl.BlockSpec((1, w, tm), lambda i: (i // tps, 0, i % tps))
    return pl.pallas_call(
        _attn_prep_kernel,
        out_shape=(tr_shape, tr_shape, jax.ShapeDtypeStruct((m, w), BF16), tr_shape),
        grid=(m // tm,),
        in_specs=[pl.BlockSpec((tm, 3 * w), lambda i: (i, 0)), tspec, tspec, tspec],
        out_specs=(tr_spec, tr_spec, pl.BlockSpec((tm, w), lambda i: (i, 0)), tr_spec),
        compiler_params=_cparams("arbitrary"),
        name="attn_prep_rotary",
    )(proj, *tables)


ATTN_QC = 256
ATTN_GROUP = 10
ATTN_ONES = 8


def _attn_kernel(lam_ref, g_ref, qlo_ref, qhi_ref, k_ref, vt_ref, o_ref, qs_ref, acc_ref,
                 *, tkb, padf, lam_init, unroll):
    tq = qlo_ref.shape[2]
    nq2 = 2 * tq
    nkb = k_ref.shape[0] // tkb
    qs_ref[:, :tq] = qlo_ref[0]
    qs_ref[:, tq:] = qhi_ref[0]
    row_ok = lax.broadcasted_iota(jnp.int32, (tkb, 1), 0) >= padf
    ones = jnp.ones((ATTN_ONES, tkb), BF16)

    def scores(kb, c0, masked):
        start = kb * tkb
        if not isinstance(start, int):
            start = pl.multiple_of(start, tkb)
        s = jnp.dot(k_ref[pl.ds(start, tkb), :], qs_ref[:, c0:c0 + ATTN_QC], preferred_element_type=F32)
        if masked:
            s = jnp.where(row_ok, s, -1e30)
        return s, start

    def colmax(s):
        return jnp.max(jnp.max(s.reshape(tkb // 8, 8, ATTN_QC), axis=0), axis=0, keepdims=True)

    for c0 in range(0, nq2, ATTN_GROUP * ATTN_QC):
        cols = tuple(c0 + i * ATTN_QC for i in range(ATTN_GROUP))

        def consume(kb, s, m):
            start = kb * tkb
            if not isinstance(start, int):
                start = pl.multiple_of(start, tkb)
            vext = jnp.concatenate([vt_ref[0, :, pl.ds(start, tkb)], ones], axis=0)
            m_out = []
            for i in range(ATTN_GROUP):
                m_new = jnp.maximum(m[i], colmax(s[i]))
                p = jnp.exp(s[i] - m_new).astype(BF16)
                sl = slice(cols[i], cols[i] + ATTN_QC)
                acc_ref[:, sl] = acc_ref[:, sl] * jnp.exp(m[i] - m_new) + jnp.dot(vext, p, preferred_element_type=F32)
                m_out.append(m_new)
            return tuple(m_out)

        for c in cols:
            acc_ref[:, c:c + ATTN_QC] = jnp.zeros((HEAD_W + ATTN_ONES, ATTN_QC), F32)

        def body(kb, carry):
            s, m = carry
            s_next = tuple(scores(kb + 1, c, False)[0] for c in cols)
            return s_next, consume(kb, s, m)

        m0 = tuple(jnp.full((1, ATTN_QC), -1e30, F32) for _ in cols)
        s_last, m_last = lax.fori_loop(0, nkb - 1, body, (tuple(scores(0, c, True)[0] for c in cols), m0),
                                       unroll=unroll)
        consume(nkb - 1, s_last, m_last)

    lam = (jnp.exp(jnp.sum(lam_ref[0:1, :] * lam_ref[1:2, :], axis=-1, keepdims=True))
           - jnp.exp(jnp.sum(lam_ref[2:3, :] * lam_ref[3:4, :], axis=-1, keepdims=True)) + lam_init)
    o = acc_ref[0:HEAD_W, :] / acc_ref[HEAD_W:HEAD_W + 1, :]
    o = (o[:, :tq] - lam * o[:, tq:]).T
    o = o * lax.rsqrt(jnp.mean(o * o, axis=-1, keepdims=True) + RMS_EPS) * g_ref[...]
    o_ref[...] = (o * (1.0 - lam_init)).astype(o_ref.dtype)


def diff_attention(qlo_t, qhi_t, k, v_t, lam_vecs, norm_g, nb, lp, padf, layer, tq, tkb, unroll):
    m = k.shape[0]
    assert lp % tq == 0 and lp % tkb == 0 and padf < tkb and (2 * tq) % (ATTN_GROUP * ATTN_QC) == 0
    nq = lp // tq
    lam_init = 0.8 - 0.6 * math.exp(-0.3 * layer)
    qspec = pl.BlockSpec((1, HEAD_W, tq), lambda b, h, qi: (b, h, qi))
    return pl.pallas_call(
        functools.partial(_attn_kernel, tkb=tkb, padf=padf, lam_init=lam_init, unroll=unroll),
        out_shape=jax.ShapeDtypeStruct((m, GROUP_W), BF16),
        grid=(nb, N_HEADS, nq),
        in_specs=[pl.BlockSpec((4, DA_HD), lambda b, h, qi: (0, 0)),
                  pl.BlockSpec((1, HEAD_W), lambda b, h, qi: (0, 0)),
                  qspec, qspec,
                  pl.BlockSpec((lp, HEAD_W), lambda b, h, qi: (b, h), pipeline_mode=pl.Buffered(1)),
                  pl.BlockSpec((1, HEAD_W, lp), lambda b, h, qi: (b, h, 0), pipeline_mode=pl.Buffered(1))],
        out_specs=pl.BlockSpec((tq, HEAD_W), lambda b, h, qi: (b * nq + qi, h)),
        scratch_shapes=[pltpu.VMEM((HEAD_W, 2 * tq), BF16),
                        pltpu.VMEM((HEAD_W + ATTN_ONES, 2 * tq), F32)],
        compiler_params=_cparams("arbitrary", "arbitrary", "arbitrary"),
        name="diff_attention",
    )(lam_vecs, norm_g.reshape(1, HEAD_W), qlo_t, qhi_t, k, v_t)


N_LEVELS = 6


def _scan_constants():
    c = CHUNK
    idx = np.arange(c)
    cs = np.zeros((2, (N_LEVELS + 2) * c, c), np.float32)
    mask = np.zeros((2, N_LEVELS + 2, c, c), np.float32)
    for d in range(2):
        cum = (idx[None, :] <= idx[:, None]) if d == 0 else (idx[None, :] >= idx[:, None])
        cs[d, 0:c] = cum
        for l in range(N_LEVELS):
            s = (c // 2) >> l
            blk = idx // s
            ref = (blk | 1) * s - 1 if d == 0 else (blk | 1) * s
            cs[d, (l + 1) * c:(l + 2) * c] = cum[ref]
            if d == 0:
                mask[d, l] = ((blk[:, None] & 1) == 1) & (blk[None, :] == blk[:, None] - 1)
            else:
                mask[d, l] = ((blk[:, None] & 1) == 0) & (blk[None, :] == blk[:, None] + 1)
        cs[d, (N_LEVELS + 1) * c:] = 1.0
        mask[d, N_LEVELS] = cum
        mask[d, N_LEVELS + 1] = cum & (idx[None, :] != idx[:, None])
    return jnp.asarray(np.concatenate([cs] * 3, axis=2), BF16), jnp.asarray(mask)


def _split3(x):
    hi = x.astype(BF16)
    r = x - hi.astype(F32)
    mid = r.astype(BF16)
    return hi, mid, (r - mid.astype(F32)).astype(BF16)


def _split3_rows(x):
    return jnp.concatenate(_split3(x), axis=0)


def _split3_lanes(x):
    return jnp.concatenate(_split3(x), axis=1)


def _nt(a, b):
    return lax.dot_general(a.astype(BF16), b.astype(BF16), (((1,), (1,)), ((), ())),
                           preferred_element_type=F32)


def _tn(a, b):
    return lax.dot_general(a.astype(BF16), b.astype(BF16), (((0,), (0,)), ((), ())),
                           preferred_element_type=F32)


def _nn(a, b):
    return jnp.dot(a.astype(BF16), b.astype(BF16), preferred_element_type=F32)


def _hgrn_prep_kernel(p_ref, lb_ref, q_ref, v_ref, g_ref, sg_ref):
    w = GROUP_W
    q = p_ref[:, 0:w]
    q_ref[...] = (q * jax.nn.sigmoid(q) * (HEAD_W ** -0.5)).astype(BF16)
    for d in range(2):
        lb = lb_ref[d:d + 1, :]
        f = p_ref[:, (1 + d) * w:(2 + d) * w]
        g_ref[d] = jnp.log(lb + (1.0 - lb) * jax.nn.sigmoid(f))
    v_ref[...] = p_ref[:, 3 * w:4 * w].astype(BF16)
    gate = p_ref[:, 4 * w:5 * w]
    sg_ref[...] = (gate * jax.nn.sigmoid(gate)).astype(BF16)


def hgrn_prep(proj, col_block, lb, tm):
    m = proj.shape[0]
    w = GROUP_W
    ospec = pl.BlockSpec((tm, w), lambda i: (i, 0))
    return pl.pallas_call(
        _hgrn_prep_kernel,
        out_shape=(jax.ShapeDtypeStruct((m, w), BF16), jax.ShapeDtypeStruct((m, w), BF16),
                   jax.ShapeDtypeStruct((2, m, w), F32), jax.ShapeDtypeStruct((m, w), BF16)),
        grid=(m // tm,),
        in_specs=[pl.BlockSpec((tm, 5 * w), lambda i: (i, col_block)),
                  pl.BlockSpec((2, w), lambda i: (0, 0))],
        out_specs=(ospec, ospec, pl.BlockSpec((2, tm, w), lambda i: (0, i, 0)), ospec),
        compiler_params=_cparams("arbitrary"),
        name="hgrn_prep",
    )(proj, lb)


def _head(x, h):
    return x[:, h * HEAD_W:(h + 1) * HEAD_W]


def _chunk_rows(ci, nch, d):
    return pl.multiple_of((ci + d * (nch - 1 - 2 * ci)) * CHUNK, CHUNK)


def _hgrn_scan_kernel(cs_ref, mask_ref, q_ref, v_ref, g_ref, o_ref, s_ref, *, nch):
    c = CHUNK
    d = pl.program_id(1)
    heads = range(N_HEADS)
    chunks = range(nch)

    @pl.when(pl.program_id(2) == 0)
    def _():
        s_ref[...] = jnp.zeros(s_ref.shape, F32)

    eye = mask_ref[0, N_LEVELS] - mask_ref[0, N_LEVELS + 1]
    rows = [_chunk_rows(ci, nch, d) for ci in chunks]
    g = [g_ref[0, pl.ds(r0, c), :] for r0 in rows]
    q = [q_ref[pl.ds(r0, c), :].astype(F32) for r0 in rows]
    v = [v_ref[pl.ds(r0, c), :] for r0 in rows]
    big = [jnp.dot(cs_ref[0], _split3_rows(gi), preferred_element_type=F32) for gi in g]
    b = [x[0:c] for x in big]
    tot = [x[(N_LEVELS + 1) * c:(N_LEVELS + 2) * c] for x in big]
    k = [1.0 - jnp.exp(gi) for gi in g]
    a = [[eye * jnp.sum(_head(q[ci] * k[ci], h), axis=-1, keepdims=True) for h in heads] for ci in chunks]
    q16 = [q_ref[pl.ds(r0, c), :] for r0 in rows]
    k16 = [ki.astype(BF16) for ki in k]
    for l in range(N_LEVELS):
        e = [jnp.exp(-jnp.abs(b[ci] - big[ci][(l + 1) * c:(l + 2) * c])).astype(BF16) for ci in chunks]
        qe = [q16[ci] * e[ci] for ci in chunks]
        ke = [k16[ci] * e[ci] for ci in chunks]
        lm = mask_ref[0, l]
        a = [[a[ci][h] + lm * _nt(_head(qe[ci], h), _head(ke[ci], h)) for h in heads] for ci in chunks]
    o_intra = [[_nn(a[ci][h], _head(v[ci], h)) for h in heads] for ci in chunks]
    qb = [(q[ci] * jnp.exp(b[ci])).astype(BF16) for ci in chunks]
    kd = [(k[ci] * jnp.exp(tot[ci] - b[ci])).astype(BF16) for ci in chunks]
    dec = [jnp.exp(tot[ci][0:1, :]) for ci in chunks]
    s = [s_ref[h] for h in heads]
    for ci in chunks:
        for h in heads:
            o = o_intra[ci][h] + _nt(_head(qb[ci], h), s[h])
            o_ref[0, pl.ds(rows[ci], c), h * HEAD_W:(h + 1) * HEAD_W] = o
        s = [s[h] * _head(dec[ci], h) + _tn(_head(v[ci], h), _head(kd[ci], h)) for h in heads]
    for h in heads:
        s_ref[h] = s[h]


def _scan_row_block(nblk):
    def row(b, d, t):
        return b * nblk + t + d * (nblk - 1 - 2 * t)
    return row


def hgrn_scan(q, v, g, consts, nb, lp, tb):
    m, w = q.shape
    cs, mask = consts
    row = _scan_row_block(lp // tb)
    return pl.pallas_call(
        functools.partial(_hgrn_scan_kernel, nch=tb // CHUNK),
        out_shape=jax.ShapeDtypeStruct((2, m, w), F32),
        grid=(nb, 2, lp // tb),
        in_specs=[pl.BlockSpec((1,) + cs.shape[1:], lambda b, d, t: (d, 0, 0)),
                  pl.BlockSpec((1,) + mask.shape[1:], lambda b, d, t: (d, 0, 0, 0)),
                  pl.BlockSpec((tb, w), lambda b, d, t: (row(b, d, t), 0)),
                  pl.BlockSpec((tb, w), lambda b, d, t: (row(b, d, t), 0)),
                  pl.BlockSpec((1, tb, w), lambda b, d, t: (d, row(b, d, t), 0))],
        out_specs=pl.BlockSpec((1, tb, w), lambda b, d, t: (d, row(b, d, t), 0)),
        scratch_shapes=[pltpu.VMEM((N_HEADS, HEAD_W, HEAD_W), F32)],
        compiler_params=_cparams("arbitrary", "arbitrary", "arbitrary"),
        name="hgrn_scan",
    )(cs, mask, q, v, g)


def _gated_norm_kernel(o_ref, sg_ref, g_ref, out_ref):
    o = o_ref[0] + o_ref[1]
    gain = g_ref[...]
    for h in range(N_HEADS):
        sl = slice(h * HEAD_W, (h + 1) * HEAD_W)
        oh = o[:, sl]
        y = oh * lax.rsqrt(jnp.mean(oh * oh, axis=-1, keepdims=True) + RMS_EPS) * gain
        out_ref[:, sl] = (y * sg_ref[:, sl].astype(F32)).astype(out_ref.dtype)


def gated_norm(o2, sgate, norm_g, tm):
    m = sgate.shape[0]
    w = GROUP_W
    return pl.pallas_call(
        _gated_norm_kernel,
        out_shape=jax.ShapeDtypeStruct((m, w), BF16),
        grid=(m // tm,),
        in_specs=[pl.BlockSpec((2, tm, w), lambda i: (0, i, 0)),
                  pl.BlockSpec((tm, w), lambda i: (i, 0)),
                  pl.BlockSpec((1, HEAD_W), lambda i: (0, 0))],
        out_specs=pl.BlockSpec((tm, w), lambda i: (i, 0)),
        compiler_params=_cparams("arbitrary"),
        name="gated_rms_norm",
    )(o2, sgate, norm_g.reshape(1, HEAD_W))


GATE_LANES = 128


def _softplus(x):
    return jnp.maximum(x, 0.0) + jnp.log(1.0 + jnp.exp(-jnp.abs(x)))


def _gdn_prep_kernel(p_ref, pp_ref, pn_ref, cw_ref, acoef_ref, dtb_ref,
                     q_ref, k_ref, v_ref, sz_ref, gates_ref, *, tiles_per_seq, padf):
    w = GROUP_W
    i = pl.program_id(0)
    tm = p_ref.shape[0]
    nmask = jnp.where(i == pl.num_programs(0) - 1, 0.0, 1.0)
    x = _conv3_rows(p_ref[:, 0:3 * w], pp_ref[HALO - 1:HALO, :], pn_ref[0:1, :] * nmask, cw_ref[...])
    x = x * jax.nn.sigmoid(x)
    valid = _valid_rows(i, tiles_per_seq, tm, padf)
    for h in range(N_HEADS):
        sl = slice(h * HEAD_W, (h + 1) * HEAD_W)
        qh = x[:, sl]
        q_ref[:, sl] = (qh * lax.rsqrt(jnp.sum(qh * qh, axis=-1, keepdims=True) + 1e-6)
                        * (HEAD_W ** -0.5)).astype(BF16)
        kh = x[:, w + h * HEAD_W:w + (h + 1) * HEAD_W]
        k_ref[:, sl] = (kh * lax.rsqrt(jnp.sum(kh * kh, axis=-1, keepdims=True) + 1e-6)).astype(BF16)
    v_ref[...] = jnp.where(valid, x[:, 2 * w:3 * w], 0.0).astype(BF16)
    z = p_ref[:, 3 * w:4 * w]
    sz_ref[...] = (z * jax.nn.sigmoid(z)).astype(BF16)
    gt = p_ref[:, 4 * w:4 * w + GATE_LANES]
    lane = lax.broadcasted_iota(jnp.int32, (1, GATE_LANES), 1)
    decay = -acoef_ref[...] * _softplus(gt + dtb_ref[...])
    gates_ref[...] = jnp.where(lane < 2 * N_HEADS, jax.nn.sigmoid(gt), decay)


def gdn_prep(pd, conv_w, a_log, dt_bias, lp, padf, tm):
    m = pd.shape[0]
    w = GROUP_W
    zeros8 = jnp.zeros((2 * N_HEADS,), F32)
    padl = jnp.zeros((GATE_LANES - 4 * N_HEADS,), F32)
    acoef = jnp.concatenate([zeros8, jnp.exp(a_log).reshape(-1), padl]).reshape(1, GATE_LANES)
    dtb = jnp.concatenate([zeros8, dt_bias.reshape(-1), padl]).reshape(1, GATE_LANES)
    pspec, nspec = _halo_specs(tm, 3 * w, m)
    ospec = pl.BlockSpec((tm, w), lambda i: (i, 0))
    return pl.pallas_call(
        functools.partial(_gdn_prep_kernel, tiles_per_seq=lp // tm, padf=padf),
        out_shape=tuple(jax.ShapeDtypeStruct((m, w), BF16) for _ in range(4))
        + (jax.ShapeDtypeStruct((m, GATE_LANES), F32),),
        grid=(m // tm,),
        in_specs=[pl.BlockSpec((tm, pd.shape[1]), lambda i: (i, 0)), pspec, nspec,
                  pl.BlockSpec((3, 3 * w), lambda i: (0, 0)),
                  pl.BlockSpec((1, GATE_LANES), lambda i: (0, 0)),
                  pl.BlockSpec((1, GATE_LANES), lambda i: (0, 0))],
        out_specs=(ospec, ospec, ospec, ospec, pl.BlockSpec((tm, GATE_LANES), lambda i: (i, 0))),
        compiler_params=_cparams("arbitrary"),
        name="gdn_prep",
    )(pd, pd, pd, conv_w, acoef, dtb)


def _gate_selectors():
    sel = np.zeros((2, GATE_LANES, 2 * GROUP_W), np.float32)
    for d in range(2):
        for h in range(N_HEADS):
            sel[d, N_HEADS * d + h, h * HEAD_W:(h + 1) * HEAD_W] = 1.0
            sel[d, 2 * N_HEADS + N_HEADS * d + h, GROUP_W + h * HEAD_W:GROUP_W + (h + 1) * HEAD_W] = 1.0
    return jnp.asarray(np.concatenate([sel] * 3, axis=1), BF16)


def _gdn_scan_kernel(cs_ref, mask_ref, sel_ref, q_ref, k_ref, v_ref, gt_ref, o_ref, s_ref, *, nch):
    c = CHUNK
    w = GROUP_W
    d = pl.program_id(1)
    heads = range(N_HEADS)
    chunks = range(nch)
    items = [(ci, h) for ci in chunks for h in heads]

    @pl.when(pl.program_id(2) == 0)
    def _():
        s_ref[...] = jnp.zeros(s_ref.shape, F32)

    incl = mask_ref[0, N_LEVELS]
    strict = mask_ref[0, N_LEVELS + 1]
    eye = incl - strict
    cum_tot = jnp.concatenate([cs_ref[0, 0:c, :], cs_ref[0, (N_LEVELS + 1) * c:(N_LEVELS + 2) * c, :]], axis=0)
    ones3 = cs_ref[0, (N_LEVELS + 1) * c:(N_LEVELS + 2) * c, :]
    rows = [_chunk_rows(ci, nch, d) for ci in chunks]
    q = [q_ref[pl.ds(r0, c), :].astype(F32) for r0 in rows]
    k = [k_ref[pl.ds(r0, c), :].astype(F32) for r0 in rows]
    v = [v_ref[pl.ds(r0, c), :].astype(F32) for r0 in rows]
    bg = [jnp.dot(_split3_lanes(gt_ref[pl.ds(r0, c), :]), sel_ref[0], preferred_element_type=F32)
          for r0 in rows]
    beta = [x[:, 0:w] for x in bg]
    cb = [jnp.dot(cum_tot, _split3_rows(x[:, w:2 * w]), preferred_element_type=F32) for x in bg]
    bb = [x[0:c] for x in cb]
    tot = [x[c:2 * c] for x in cb]
    bc = {(ci, h): bb[ci][:, h * HEAD_W:h * HEAD_W + c] for ci, h in items}
    br = {it: jnp.dot(ones3, _split3_rows(eye * bc[it]), preferred_element_type=F32) for it in items}
    dec = {it: incl * jnp.exp(jnp.minimum(bc[it] - br[it], 0.0)) for it in items}
    kb = [k[ci] * beta[ci] for ci in chunks]
    n = {(ci, h): strict * _nt(_head(kb[ci], h), _head(k[ci], h)) * dec[(ci, h)] for ci, h in items}
    t = {it: eye - n[it] for it in items}
    pw = {it: _nn(n[it], n[it]) for it in items}
    for _ in range(N_LEVELS - 2):
        t = {it: t[it] + _nn(t[it], pw[it]) for it in items}
        pw = {it: _nn(pw[it], pw[it]) for it in items}
    t = {it: t[it] + _nn(t[it], pw[it]) for it in items}
    eb = [jnp.exp(x) for x in bb]
    rhs_u = [v[ci] * beta[ci] for ci in chunks]
    rhs_w = [kb[ci] * eb[ci] for ci in chunks]
    uw = {(ci, h): _nn(t[(ci, h)], jnp.concatenate([_head(rhs_u[ci], h), _head(rhs_w[ci], h)], axis=1))
          for ci, h in items}
    qk = {(ci, h): _nt(_head(q[ci], h), _head(k[ci], h)) * dec[(ci, h)] for ci, h in items}
    qe = [(q[ci] * eb[ci]).astype(BF16) for ci in chunks]
    kd = [(k[ci] * jnp.exp(tot[ci] - bb[ci])).astype(BF16) for ci in chunks]
    sdec = [jnp.exp(tot[ci][0:1, :]) for ci in chunks]
    s = [s_ref[h] for h in heads]
    for ci in chunks:
        wq = {h: _nn(jnp.concatenate([uw[(ci, h)][:, HEAD_W:].astype(BF16), _head(qe[ci], h)], axis=0), s[h])
              for h in heads}
        vnew = {h: uw[(ci, h)][:, 0:HEAD_W] - wq[h][0:c] for h in heads}
        for h in heads:
            o_ref[0, pl.ds(rows[ci], c), h * HEAD_W:(h + 1) * HEAD_W] = wq[h][c:2 * c] + _nn(qk[(ci, h)], vnew[h])
        s = [s[h] * _head(sdec[ci], h) + _tn(_head(kd[ci], h), vnew[h]) for h in heads]
    for h in heads:
        s_ref[h] = s[h]


def gdn_scan(q, k, v, gates, consts, sel, nb, lp, tb):
    m, w = q.shape
    cs, mask = consts
    row = _scan_row_block(lp // tb)
    rspec = pl.BlockSpec((tb, w), lambda b, d, t: (row(b, d, t), 0))
    return pl.pallas_call(
        functools.partial(_gdn_scan_kernel, nch=tb // CHUNK),
        out_shape=jax.ShapeDtypeStruct((2, m, w), F32),
        grid=(nb, 2, lp // tb),
        in_specs=[pl.BlockSpec((1,) + cs.shape[1:], lambda b, d, t: (d, 0, 0)),
                  pl.BlockSpec((1,) + mask.shape[1:], lambda b, d, t: (d, 0, 0, 0)),
                  pl.BlockSpec((1,) + sel.shape[1:], lambda b, d, t: (d, 0, 0)),
                  rspec, rspec, rspec,
                  pl.BlockSpec((tb, GATE_LANES), lambda b, d, t: (row(b, d, t), 0))],
        out_specs=pl.BlockSpec((1, tb, w), lambda b, d, t: (d, row(b, d, t), 0)),
        scratch_shapes=[pltpu.VMEM((N_HEADS, HEAD_W, HEAD_W), F32)],
        compiler_params=_cparams("arbitrary", "arbitrary", "arbitrary"),
        name="gdn_scan",
    )(cs, mask, sel, q, k, v, gates)


FFT_N2 = 256
FFT_TS = 8


def _fft_n1(lp, l_real):
    need = -(-(2 * l_real - 1) // FFT_N2)
    return max(-(-need // 8) * 8, -(-(lp // FFT_N2) // 8) * 8)


def _outer_dft(n1, k_rows, inverse, n_total):
    kp = -(-k_rows // 16) * 16
    a = np.arange(n1)[:, None] * np.arange(kp)[None, :]
    ang = 2.0 * np.pi * (a % n1) / n1
    c, s = np.cos(ang), np.sin(ang)
    live = (np.arange(kp) < k_rows)[None, :]
    c, s = c * live, s * live
    if not inverse:
        return jnp.asarray(np.concatenate([c, -s], axis=0), BF16)
    return jnp.asarray(np.concatenate([c.T, -s.T], axis=1) / n_total, F32)


def inner_dft(n1):
    n2 = FFT_N2
    n = n1 * n2
    k = jnp.arange(n1, dtype=jnp.int32)[:, None, None] + n1 * jnp.arange(n2, dtype=jnp.int32)[None, :, None]
    r = (k * jnp.arange(n2, dtype=jnp.int32)[None, None, :]) % n
    ang = r.astype(F32) * (2.0 * math.pi / n)
    gr, gi = jnp.cos(ang), -jnp.sin(ang)
    gg = jnp.concatenate([jnp.concatenate([gr, -gi], axis=2), jnp.concatenate([gi, gr], axis=2)], axis=1)
    return gg.astype(BF16), jnp.swapaxes(gg, 1, 2).astype(BF16)


def _hyena_prep_kernel(p_ref, pp_ref, pn_ref, cw_ref, cb_ref, x0_ref, z_ref, *, tiles_per_seq, padf):
    w = GROUP_W
    i = pl.program_id(0)
    tm = p_ref.shape[0]
    nmask = jnp.where(i == pl.num_programs(0) - 1, 0.0, 1.0)
    u = _conv3_rows(p_ref[...], pp_ref[HALO - 1:HALO, :], pn_ref[0:1, :] * nmask, cw_ref[...]) + cb_ref[...]
    valid = _valid_rows(i, tiles_per_seq, tm, padf)
    x0_ref[...] = u[:, 0:w]
    z_ref[...] = jnp.where(valid, u[:, w:2 * w] * u[:, 2 * w:3 * w], 0.0)


def hyena_prep(pb, conv_w, conv_b, lp, padf, tm):
    m, w3 = pb.shape
    w = GROUP_W
    pspec, nspec = _halo_specs(tm, w3, m)
    ospec = pl.BlockSpec((tm, w), lambda i: (i, 0))
    return pl.pallas_call(
        functools.partial(_hyena_prep_kernel, tiles_per_seq=lp // tm, padf=padf),
        out_shape=(jax.ShapeDtypeStruct((m, w), F32), jax.ShapeDtypeStruct((m, w), F32)),
        grid=(m // tm,),
        in_specs=[pl.BlockSpec((tm, w3), lambda i: (i, 0)), pspec, nspec,
                  pl.BlockSpec((3, w3), lambda i: (0, 0)), pl.BlockSpec((1, w3), lambda i: (0, 0))],
        out_specs=(ospec, ospec),
        compiler_params=_cparams("arbitrary"),
        name="hyena_prep",
    )(pb, pb, pb, conv_w, conv_b.reshape(1, w3))


def _hyena_filter_kernel(w1t_ref, w1c_ref, w1s_ref, b1_ref, f1_ref, w2_ref, b2_ref, f2_ref, w3_ref, dec_ref,
                         filt_ref, asum_ref, *, l_real):
    i = pl.program_id(0)
    tm = filt_ref.shape[0]
    w = GROUP_W
    row = i * tm + lax.broadcasted_iota(jnp.int32, (tm, 1), 0)
    rf = row.astype(F32)
    t = rf * (1.0 / (l_real - 1))
    band = lax.broadcasted_iota(jnp.int32, (1, HY_BANDS), 1).astype(F32)
    bands = 1e-4 + band * ((HY_BANDS - 1 - 1e-4) / (HY_BANDS - 1))
    ang = ((2.0 * math.pi / l_real) * rf) * bands

    def hdot(a, b):
        return jnp.dot(a, b, precision=HIGHEST, preferred_element_type=F32)

    pre = t * w1t_ref[...] + hdot(jnp.cos(ang), w1c_ref[...]) - hdot(jnp.sin(ang), w1s_ref[...])
    hid = jnp.sin(f1_ref[...] * (pre + b1_ref[...]))
    hid = jnp.sin(f2_ref[...] * (hdot(hid, w2_ref[...]) + b2_ref[...]))
    filt = hdot(hid, w3_ref[...])
    window = jnp.exp(-t * jnp.abs(dec_ref[...])) + HY_SHIFT
    filt = jnp.where(row < l_real, filt * window, 0.0)

    @pl.when(i == 0)
    def _():
        asum_ref[...] = jnp.zeros(asum_ref.shape, F32)

    asum_ref[...] += jnp.sum(jnp.abs(filt), axis=0, keepdims=True)
    lane = lax.broadcasted_iota(jnp.int32, (1, 2 * w), 1)
    filt_ref[...] = jnp.where((row == 0) & (lane >= w), 0.0, filt)


def hyena_filter(w1, b1, f1, w2, b2, f2, w3, decay, lp, l_real, tm):
    w = GROUP_W
    nf = w1.shape[1]
    small = lambda a: pl.BlockSpec(a.shape, lambda i: (0,) * a.ndim)
    args = (w1[0:1], w1[1:1 + HY_BANDS], w1[1 + HY_BANDS:], b1.reshape(1, nf), f1.reshape(1, nf),
            w2, b2.reshape(1, nf), f2.reshape(1, nf), w3, decay.reshape(1, 2 * w))
    return pl.pallas_call(
        functools.partial(_hyena_filter_kernel, l_real=l_real),
        out_shape=(jax.ShapeDtypeStruct((lp, 2 * w), F32), jax.ShapeDtypeStruct((1, 2 * w), F32)),
        grid=(lp // tm,),
        in_specs=[small(a) for a in args],
        out_specs=(pl.BlockSpec((tm, 2 * w), lambda i: (i, 0)), pl.BlockSpec((1, 2 * w), lambda i: (0, 0))),
        compiler_params=_cparams("arbitrary"),
        name="hyena_filter",
    )(*args)


def _fft_outer_kernel(f_ref, x_ref, o_ref, pad_ref):
    k = x_ref.shape[1]
    pad_ref[...] = jnp.zeros(pad_ref.shape, F32)
    for s in range(x_ref.shape[2]):
        pad_ref[0:k, :] = x_ref[0, :, s, :]
        o_ref[0, :, s, :] = jnp.dot(f_ref[...], pad_ref[...].astype(BF16), preferred_element_type=F32)


def fft_outer(x, fmat):
    b, k, n2, c = x.shape
    rows, kp = fmat.shape
    ts = FFT_TS
    return pl.pallas_call(
        _fft_outer_kernel,
        out_shape=jax.ShapeDtypeStruct((b, rows, n2, c), F32),
        grid=(b, n2 // ts),
        in_specs=[pl.BlockSpec((rows, kp), lambda bi, j: (0, 0)),
                  pl.BlockSpec((1, k, ts, c), lambda bi, j: (bi, 0, j, 0))],
        out_specs=pl.BlockSpec((1, rows, ts, c), lambda bi, j: (bi, 0, j, 0)),
        scratch_shapes=[pltpu.VMEM((kp, c), F32)],
        compiler_params=_cparams("arbitrary", "arbitrary"),
        name="fft_outer",
    )(fmat, x)


def _filter_spectrum_kernel(gg_ref, a_ref, asum_ref, o_ref):
    w = GROUP_W
    n2 = FFT_N2
    a = jnp.concatenate([a_ref[0, 0], a_ref[1, 0]], axis=0).astype(BF16)
    x = jnp.dot(gg_ref[0], a, preferred_element_type=F32)
    s = 1.0 / (asum_ref[:, 0:w] + asum_ref[:, w:2 * w] + 1e-6)
    o_ref[0, 0] = (x[0:n2, 0:w] + x[0:n2, w:2 * w]) * s
    o_ref[0, 1] = (x[n2:2 * n2, 0:w] - x[n2:2 * n2, w:2 * w]) * s


def filter_spectrum(a, gg, asum):
    _, n1, n2, w2 = a.shape
    w = w2 // 2
    return pl.pallas_call(
        _filter_spectrum_kernel,
        out_shape=jax.ShapeDtypeStruct((n1, 2, n2, w), F32),
        grid=(n1,),
        in_specs=[pl.BlockSpec((1, 2 * n2, 2 * n2), lambda k: (k, 0, 0)),
                  pl.BlockSpec((2, 1, n2, w2), lambda k: (0, k, 0, 0)),
                  pl.BlockSpec((1, w2), lambda k: (0, 0))],
        out_specs=pl.BlockSpec((1, 2, n2, w), lambda k: (k, 0, 0, 0)),
        compiler_params=_cparams("arbitrary"),
        name="hyena_filter_spectrum",
    )(gg, a, asum)


def _fft_mid_kernel(gg_ref, ggt_ref, kf_ref, a_ref, o_ref):
    n2 = FFT_N2
    a = jnp.concatenate([a_ref[0, 0, 0], a_ref[0, 1, 0]], axis=0).astype(BF16)
    x = jnp.dot(gg_ref[0], a, preferred_element_type=F32)
    xr, xi = x[0:n2], x[n2:2 * n2]
    kr, ki = kf_ref[0, 0], kf_ref[0, 1]
    y = jnp.concatenate([xr * kr - xi * ki, xr * ki + xi * kr], axis=0).astype(BF16)
    bm = jnp.dot(ggt_ref[0], y, preferred_element_type=F32)
    o_ref[0, 0, 0] = bm[0:n2]
    o_ref[0, 1, 0] = bm[n2:2 * n2]


def fft_mid(a, gg, ggt, kf):
    b, _, n1, n2, w = a.shape
    aspec = pl.BlockSpec((1, 2, 1, n2, w), lambda k, bi: (bi, 0, k, 0, 0))
    gspec = pl.BlockSpec((1, 2 * n2, 2 * n2), lambda k, bi: (k, 0, 0))
    return pl.pallas_call(
        _fft_mid_kernel,
        out_shape=jax.ShapeDtypeStruct(a.shape, F32),
        grid=(n1, b),
        in_specs=[gspec, gspec, pl.BlockSpec((1, 2, n2, w), lambda k, bi: (k, 0, 0, 0)), aspec],
        out_specs=aspec,
        compiler_params=_cparams("arbitrary", "arbitrary"),
        name="fft_inner_conv",
    )(gg, ggt, kf, a)


def _fft_final_kernel(f_ref, b_ref, x0_ref, z_ref, d_ref, o_ref):
    k = o_ref.shape[1]
    for s in range(o_ref.shape[2]):
        y = jnp.dot(f_ref[...], b_ref[0, :, s, :], precision=HIGHEST, preferred_element_type=F32)
        o_ref[0, :, s, :] = x0_ref[0, :, s, :] * (y[0:k] + z_ref[0, :, s, :] * d_ref[...])


def fft_final(bm, finv, x0, z, d):
    b, rows, n2, c = bm.shape
    k = x0.shape[1]
    kp = finv.shape[0]
    ts = FFT_TS
    xspec = pl.BlockSpec((1, k, ts, c), lambda bi, j: (bi, 0, j, 0))
    return pl.pallas_call(
        _fft_final_kernel,
        out_shape=jax.ShapeDtypeStruct((b, k, n2, c), F32),
        grid=(b, n2 // ts),
        in_specs=[pl.BlockSpec((kp, rows), lambda bi, j: (0, 0)),
                  pl.BlockSpec((1, rows, ts, c), lambda bi, j: (bi, 0, j, 0)),
                  xspec, xspec, pl.BlockSpec((1, c), lambda bi, j: (0, 0))],
        out_specs=xspec,
        compiler_params=_cparams("arbitrary", "arbitrary"),
        name="fft_outer_inverse_gate",
    )(finv, bm, x0, z, d)


def hyena_mixer(pb, hp, nb, lp, padf, l_real, tm):
    w = GROUP_W
    n2 = FFT_N2
    k1 = lp // n2
    n1 = _fft_n1(lp, l_real)
    n_total = n1 * n2
    ffwd = _outer_dft(n1, k1, False, n_total)
    finv = _outer_dft(n1, k1, True, n_total)
    gg, ggt = hp['gg'], hp['ggt']
    filt, asum = hyena_filter(hp['w1'], hp['b1'], hp['f1'], hp['w2'], hp['b2'], hp['f2'], hp['w3'],
                              hp['decay'], lp, l_real, tm)
    fa = fft_outer(filt.reshape(1, k1, n2, 2 * w), ffwd)
    kf = filter_spectrum(fa.reshape(2, n1, n2, 2 * w), gg, asum)
    x0, z = hyena_prep(pb, hp['conv_w'], hp['conv_b'], lp, padf, tm)
    z4 = z.reshape(nb, k1, n2, w)
    a = fft_outer(z4, ffwd)
    bm = fft_mid(a.reshape(nb, 2, n1, n2, w), gg, ggt, kf)
    out = fft_final(bm.reshape(nb, 2 * n1, n2, w), finv, x0.reshape(nb, k1, n2, w), z4, hp['d'].reshape(1, w))
    return out.reshape(nb * lp, w)


TM = 640
ATTN_TQ = 1280
ATTN_TKB = 1280
ATTN_UNROLL = 4
SCAN_TB = 640
FFN_TN = 512
FFN_DOWN_TM = 320


def kernel(x_prompt, x_sample, meta, emb_ln_g, emb_ln_b, w_in, lam_q1, lam_k1, lam_q2, lam_k2, attn_norm_g, hy_conv_w, hy_conv_b, hy_w1, hy_b1, hy_f1, hy_w2, hy_b2, hy_f2, hy_w3, hy_decay, hy_d, hg_lb, hg_norm_g, gdn_conv_w, gdn_a_log, gdn_dt_bias, gdn_norm_g, w_out, ln1_g, ln1_b, w_up, ffn_conv_w, ffn_conv_b, w_down, ln2_g, ln2_b):
    n_prompt = x_prompt.shape[0]
    x = jnp.concatenate([x_prompt, x_sample], axis=0)
    nb, seq, d = x.shape
    padf = ROW_ALIGN - N_META
    lp = seq + ROW_ALIGN
    l_real = seq + N_META
    m = nb * lp
    w = GROUP_W

    h, hb = (a.reshape(m, d) for a in embed(x, meta, emb_ln_g, emb_ln_b, padf))
    tables = rope_tables(lp, padf)
    consts = _scan_constants()
    sel = _gate_selectors()
    gg, ggt = inner_dft(_fft_n1(lp, l_real))
    sm = jax.nn.softmax(hg_lb, axis=0)
    lb_all = jnp.cumsum(sm, axis=0) - sm[0]

    for l in range(DEPTH):
        wl = w_in[l].astype(BF16)
        wd = jnp.pad(wl[:, 11 * w:], ((0, 0), (0, 4 * w + GATE_LANES - (wl.shape[1] - 11 * w))))
        pa = matmul(hb, wl[:, 0:3 * w], TM, 3 * w)
        pb = matmul(hb, wl[:, 3 * w:6 * w], TM, 3 * w)
        pc = matmul(hb, wl[:, 6 * w:11 * w], TM, 5 * w // 2)
        pd = matmul(hb, wd, TM, wd.shape[1])

        qlo, qhi, ka, va = attn_prep(pa, tables, nb, lp, TM)
        lam_vecs = jnp.stack([lam_q1[l], lam_k1[l], lam_q2[l], lam_k2[l]])
        oa = diff_attention(qlo, qhi, ka, va, lam_vecs, attn_norm_g[l], nb, lp, padf, l,
                            ATTN_TQ, ATTN_TKB, ATTN_UNROLL)

        hp = dict(gg=gg, ggt=ggt, w1=hy_w1[l], b1=hy_b1[l], f1=hy_f1[l], w2=hy_w2[l], b2=hy_b2[l],
                  f2=hy_f2[l], w3=hy_w3[l], decay=hy_decay[l], d=hy_d[l], conv_w=hy_conv_w[l],
                  conv_b=hy_conv_b[l])
        ob = hyena_mixer(pb, hp, nb, lp, padf, l_real, TM)

        qc, vc, gc, sgc = hgrn_prep(pc, 0, lb_all[l], TM)
        oc = gated_norm(hgrn_scan(qc, vc, gc, consts, nb, lp, SCAN_TB), sgc, hg_norm_g[l], TM)

        qd, kd, vd, szd, gates = gdn_prep(pd, gdn_conv_w[l], gdn_a_log[l], gdn_dt_bias[l], lp, padf, TM)
        od = gated_norm(gdn_scan(qd, kd, vd, gates, consts, sel, nb, lp, SCAN_TB), szd, gdn_norm_g[l], TM)

        h, hb = proj_residual_ln([oa, ob, oc, od], w_out[l].astype(BF16), h, ln1_g[l], ln1_b[l], lp, padf, TM)
        act = ffn_up(hb, w_up[l].astype(BF16), ffn_conv_w[l], ffn_conv_b[l], TM, FFN_TN)
        h, hb = ffn_down_ln(act, w_down[l].astype(BF16), h, ln2_g[l], ln2_b[l], lp, padf, FFN_DOWN_TM)

    y = h.reshape(nb, lp, d)[:, ROW_ALIGN:]
    return (y[:n_prompt], y[n_prompt:])
```

```python
import functools
import math

import jax
import jax.numpy as jnp
import numpy as np
from jax import lax
from jax.experimental import pallas as pl
from jax.experimental.pallas import tpu as pltpu

D_MODEL = 2048
DEPTH = 4
N_META = 16
GROUP_W = 512
N_HEADS = 4
HEAD_W = 128
DA_HD = 64
ROT_DIM = 16
ROPE_THETA = 500000.0
HY_BANDS = 16
HY_SHIFT = 0.05
CHUNK = 64
D_FF = 5632
ALPHA = (2.0 * DEPTH) ** 0.25
LN_EPS = 1e-5
RMS_EPS = 1e-6
F32 = jnp.float32
BF16 = jnp.bfloat16
HIGHEST = lax.Precision.HIGHEST

ROW_ALIGN = 256
VMEM_LIMIT = 56 * 1024 * 1024


def _cparams(*sem):
    return pltpu.CompilerParams(dimension_semantics=sem, vmem_limit_bytes=VMEM_LIMIT)


def _valid_rows(tile_idx, tiles_per_seq, tm, padf):
    base = (tile_idx % tiles_per_seq) * tm
    return (base + lax.broadcasted_iota(jnp.int32, (tm, 1), 0)) >= padf


def _ln_rows(y, g, b):
    mu = jnp.mean(y, axis=-1, keepdims=True)
    d = y - mu
    var = jnp.mean(d * d, axis=-1, keepdims=True)
    return d * lax.rsqrt(var + LN_EPS) * g + b


def _embed_kernel(x_ref, meta_ref, g_ref, b_ref, o_ref, ob_ref, *, padf):
    t = pl.program_id(1)
    g = g_ref[...]
    b = b_ref[...]

    @pl.when(t == 0)
    def _():
        y = jnp.concatenate([jnp.zeros((padf, o_ref.shape[2]), F32), _ln_rows(meta_ref[...], g, b)], axis=0)
        o_ref[0] = y
        ob_ref[0] = y.astype(BF16)

    @pl.when(t > 0)
    def _():
        y = _ln_rows(x_ref[0], g, b)
        o_ref[0] = y
        ob_ref[0] = y.astype(BF16)


def embed(x, meta, g, b, padf):
    nb, s, d = x.shape
    tm = ROW_ALIGN
    assert padf + N_META == tm and s % tm == 0
    lp = s + tm
    ospec = pl.BlockSpec((1, tm, d), lambda bi, t: (bi, t, 0))
    return pl.pallas_call(
        functools.partial(_embed_kernel, padf=padf),
        out_shape=(jax.ShapeDtypeStruct((nb, lp, d), F32), jax.ShapeDtypeStruct((nb, lp, d), BF16)),
        grid=(nb, lp // tm),
        in_specs=[pl.BlockSpec((1, tm, d), lambda bi, t: (bi, jnp.maximum(t - 1, 0), 0)),
                  pl.BlockSpec((N_META, d), lambda bi, t: (0, 0)),
                  pl.BlockSpec((1, d), lambda bi, t: (0, 0)),
                  pl.BlockSpec((1, d), lambda bi, t: (0, 0))],
        out_specs=(ospec, ospec),
        compiler_params=_cparams("arbitrary", "arbitrary"),
        name="embed_ln",
    )(x, meta, g.reshape(1, d), b.reshape(1, d))


def _mm_kernel(x_ref, w_ref, o_ref):
    o_ref[...] = jnp.dot(x_ref[...], w_ref[...], preferred_element_type=F32).astype(o_ref.dtype)


def matmul(x, w, tm, tn, out_dtype=F32):
    m, k = x.shape
    n = w.shape[1]
    assert m % tm == 0 and n % tn == 0
    return pl.pallas_call(
        _mm_kernel,
        out_shape=jax.ShapeDtypeStruct((m, n), out_dtype),
        grid=(n // tn, m // tm),
        in_specs=[pl.BlockSpec((tm, k), lambda j, i: (i, 0)),
                  pl.BlockSpec((k, tn), lambda j, i: (0, j))],
        out_specs=pl.BlockSpec((tm, tn), lambda j, i: (i, j)),
        compiler_params=_cparams("arbitrary", "arbitrary"),
        name="dense_matmul",
    )(x, w)


def _proj_ln_kernel(*refs, n_in, tiles_per_seq, padf):
    o_refs = refs[:n_in]
    w_ref, h_ref, g_ref, b_ref, out_ref, outb_ref = refs[n_in:]
    tm = h_ref.shape[0]
    acc = None
    off = 0
    for r in o_refs:
        kw = r.shape[1]
        part = jnp.dot(r[...].astype(BF16), w_ref[off:off + kw, :], preferred_element_type=F32)
        acc = part if acc is None else acc + part
        off += kw
    y = _ln_rows(ALPHA * h_ref[...] + acc, g_ref[...], b_ref[...])
    valid = _valid_rows(pl.program_id(0), tiles_per_seq, tm, padf)
    y = jnp.where(valid, y, 0.0)
    out_ref[...] = y
    outb_ref[...] = y.astype(BF16)


def proj_residual_ln(parts, w, h, g, b, lp, padf, tm):
    m, d = h.shape
    assert m % tm == 0 and lp % tm == 0
    n_in = len(parts)
    in_specs = [pl.BlockSpec((tm, p.shape[1]), lambda i: (i, 0)) for p in parts]
    in_specs += [pl.BlockSpec(w.shape, lambda i: (0, 0)),
                 pl.BlockSpec((tm, d), lambda i: (i, 0)),
                 pl.BlockSpec((1, d), lambda i: (0, 0)),
                 pl.BlockSpec((1, d), lambda i: (0, 0))]
    ospec = pl.BlockSpec((tm, d), lambda i: (i, 0))
    return pl.pallas_call(
        functools.partial(_proj_ln_kernel, n_in=n_in, tiles_per_seq=lp // tm, padf=padf),
        out_shape=(jax.ShapeDtypeStruct((m, d), F32), jax.ShapeDtypeStruct((m, d), BF16)),
        grid=(m // tm,),
        in_specs=in_specs,
        out_specs=(ospec, ospec),
        compiler_params=_cparams("arbitrary"),
        name="proj_residual_ln",
    )(*parts, w, h, g.reshape(1, d), b.reshape(1, d))


HALO = 8
HALO_BF16 = 16


def _halo_specs(tm, width, m, col_block=0, halo=HALO):
    nblk = m // halo
    per = tm // halo

    def prev_map(*idx):
        i = idx[-1]
        return (jnp.maximum(i * per - 1, 0), col_block)

    def next_map(*idx):
        i = idx[-1]
        return (jnp.minimum((i + 1) * per, nblk - 1), col_block)

    return (pl.BlockSpec((halo, width), prev_map), pl.BlockSpec((halo, width), next_map))


def _conv3_rows(pm, prev_row, next_row, cw):
    tm = pm.shape[0]
    rid = lax.broadcasted_iota(jnp.int32, (tm, 1), 0)
    down = jnp.where(rid == 0, prev_row, pltpu.roll(pm, 1, 0))
    up = jnp.where(rid == tm - 1, next_row, pltpu.roll(pm, tm - 1, 0))
    return down * cw[0:1, :] + pm * cw[1:2, :] + up * cw[2:3, :]


def _ffn_up_kernel(x_ref, xp_ref, xn_ref, wg_ref, wu_ref, cwg_ref, cwu_ref, cbg_ref, cbu_ref, o_ref):
    i = pl.program_id(1)
    last = pl.num_programs(1) - 1
    x = x_ref[...]
    hr = xp_ref.shape[0]
    halo = jnp.concatenate([xp_ref[...], xn_ref[...]], axis=0)
    nmask = jnp.where(i == last, 0.0, 1.0)

    def branch(w_ref, cw_ref, cb_ref, sl):
        w = w_ref[:, sl]
        pm = jnp.dot(x, w, preferred_element_type=F32)
        ph = jnp.dot(halo, w, preferred_element_type=F32)
        return _conv3_rows(pm, ph[hr - 1:hr, :], ph[hr:hr + 1, :] * nmask, cw_ref[:, sl]) + cb_ref[:, sl]

    tn = o_ref.shape[1]
    for c0 in range(0, tn, FFN_SUB):
        sl = slice(c0, min(c0 + FFN_SUB, tn))
        g = branch(wg_ref, cwg_ref, cbg_ref, sl)
        u = branch(wu_ref, cwu_ref, cbu_ref, sl)
        o_ref[:, sl] = (g * jax.nn.sigmoid(g) * u).astype(o_ref.dtype)


FFN_SUB = 256


def ffn_up(h, w_up, conv_w, conv_b, tm, tn):
    m, k = h.shape
    f = w_up.shape[1] // 2
    assert m % tm == 0 and f % tn == 0 and h.dtype == BF16
    nj = f // tn
    xp_spec, xn_spec = _halo_specs(tm, k, m, halo=HALO_BF16)
    cb = conv_b.reshape(1, 2 * f)
    return pl.pallas_call(
        _ffn_up_kernel,
        out_shape=jax.ShapeDtypeStruct((m, f), BF16),
        grid=(nj, m // tm),
        in_specs=[pl.BlockSpec((tm, k), lambda j, i: (i, 0)), xp_spec, xn_spec,
                  pl.BlockSpec((k, tn), lambda j, i: (0, j)),
                  pl.BlockSpec((k, tn), lambda j, i: (0, j + nj)),
                  pl.BlockSpec((3, tn), lambda j, i: (0, j)),
                  pl.BlockSpec((3, tn), lambda j, i: (0, j + nj)),
                  pl.BlockSpec((1, tn), lambda j, i: (0, j)),
                  pl.BlockSpec((1, tn), lambda j, i: (0, j + nj))],
        out_specs=pl.BlockSpec((tm, tn), lambda j, i: (i, j)),
        compiler_params=_cparams("arbitrary", "arbitrary"),
        name="ffn_up_conv_gate",
    )(h, h, h, w_up, w_up, conv_w, conv_w, cb, cb)


def _ffn_down_kernel(a_ref, w_ref, h_ref, g_ref, b_ref, out_ref, outb_ref, *, tiles_per_seq, padf):
    tm = h_ref.shape[0]
    acc = jnp.dot(a_ref[...], w_ref[...], preferred_element_type=F32)
    y = _ln_rows(ALPHA * h_ref[...] + acc, g_ref[...], b_ref[...])
    valid = _valid_rows(pl.program_id(0), tiles_per_seq, tm, padf)
    y = jnp.where(valid, y, 0.0)
    out_ref[...] = y
    outb_ref[...] = y.astype(BF16)


def ffn_down_ln(a, w_down, h, g, b, lp, padf, tm):
    m, d = h.shape
    f = a.shape[1]
    assert m % tm == 0 and lp % tm == 0
    ospec = pl.BlockSpec((tm, d), lambda i: (i, 0))
    return pl.pallas_call(
        functools.partial(_ffn_down_kernel, tiles_per_seq=lp // tm, padf=padf),
        out_shape=(jax.ShapeDtypeStruct((m, d), F32), jax.ShapeDtypeStruct((m, d), BF16)),
        grid=(m // tm,),
        in_specs=[pl.BlockSpec((tm, f), lambda i: (i, 0)),
                  pl.BlockSpec((f, d), lambda i: (0, 0), pipeline_mode=pl.Buffered(1)),
                  pl.BlockSpec((tm, d), lambda i: (i, 0)),
                  pl.BlockSpec((1, d), lambda i: (0, 0)),
                  pl.BlockSpec((1, d), lambda i: (0, 0))],
        out_specs=(ospec, ospec),
        compiler_params=_cparams("arbitrary"),
        name="ffn_down_residual_ln",
    )(a, w_down, h, g.reshape(1, d), b.reshape(1, d))


def rope_tables(lp, padf):
    half = ROT_DIM // 2
    pos = (jnp.arange(lp) - padf).astype(F32)
    inv = 1.0 / (ROPE_THETA ** (jnp.arange(half, dtype=F32) / half))
    ang = pos[:, None] * inv[None]
    cos, sin = jnp.cos(ang), jnp.sin(ang)
    ones = jnp.ones((lp, DA_HD - ROT_DIM), F32)
    zeros = jnp.zeros((lp, DA_HD - ROT_DIM), F32)
    zh = jnp.zeros((lp, half), F32)
    c = jnp.concatenate([cos, cos, ones], axis=1)
    sa = jnp.concatenate([-sin, zh, zeros], axis=1)
    sb = jnp.concatenate([zh, sin, zeros], axis=1)
    return tuple(jnp.concatenate([t, t], axis=1) for t in (c, sa, sb))


def _attn_prep_kernel(p_ref, c_ref, sa_ref, sb_ref, qlo_ref, qhi_ref, k_ref, vt_ref):
    w = GROUP_W
    reps = w // c_ref.shape[1]
    c = jnp.concatenate([c_ref[...]] * reps, axis=1)
    sa = jnp.concatenate([sa_ref[...]] * reps, axis=1)
    sb = jnp.concatenate([sb_ref[...]] * reps, axis=1)
    half = ROT_DIM // 2

    def rot(x):
        return x * c + pltpu.roll(x, w - half, 1) * sa + pltpu.roll(x, half, 1) * sb

    q = rot(p_ref[:, 0:w]) * (DA_HD ** -0.5)
    lane = lax.broadcasted_iota(jnp.int32, (1, w), 1) % (2 * DA_HD)
    qlo_ref[0] = jnp.where(lane < DA_HD, q, 0.0).T.astype(BF16)
    qhi_ref[0] = jnp.where(lane >= DA_HD, q, 0.0).T.astype(BF16)
    k_ref[...] = rot(p_ref[:, w:2 * w]).astype(BF16)
    vt_ref[0] = p_ref[:, 2 * w:3 * w].T.astype(BF16)


def attn_prep(proj, tables, nb, lp, tm):
    m = proj.shape[0]
    w = GROUP_W
    tps = lp // tm
    tspec = pl.BlockSpec((tm, 2 * DA_HD), lambda i: (i % tps, 0))
    tr_shape = jax.ShapeDtypeStruct((nb, w, lp), BF16)
    tr_spec = pl.BlockSpec((1, w, tm), lambda i: (i // tps, 0, i % tps))
    return pl.pallas_call(
        _attn_prep_kernel,
        out_shape=(tr_shape, tr_shape, jax.ShapeDtypeStruct((m, w), BF16), tr_shape),
        grid=(m // tm,),
        in_specs=[pl.BlockSpec((tm, 3 * w), lambda i: (i, 0)), tspec, tspec, tspec],
        out_specs=(tr_spec, tr_spec, pl.BlockSpec((tm, w), lambda i: (i, 0)), tr_spec),
        compiler_params=_cparams("arbitrary"),
        name="attn_prep_rotary",
    )(proj, *tables)


ATTN_QC = 256
ATTN_GROUP = 5
ATTN_ONES = 8


def _attn_kernel(lam_ref, g_ref, qlo_ref, qhi_ref, k_ref, vt_ref, o_ref, qs_ref, acc_ref,
                 *, tkb, padf, lam_init, unroll):
    tq = qlo_ref.shape[2]
    nq2 = 2 * tq
    nkb = k_ref.shape[0] // tkb
    qs_ref[:, :tq] = qlo_ref[0]
    qs_ref[:, tq:] = qhi_ref[0]
    row_ok = lax.broadcasted_iota(jnp.int32, (tkb, 1), 0) >= padf
    ones = jnp.ones((ATTN_ONES, tkb), BF16)

    def scores(kb, c0, masked):
        start = kb * tkb
        if not isinstance(start, int):
            start = pl.multiple_of(start, tkb)
        s = jnp.dot(k_ref[pl.ds(start, tkb), :], qs_ref[:, c0:c0 + ATTN_QC], preferred_element_type=F32)
        if masked:
            s = jnp.where(row_ok, s, -1e30)
        return s, start

    def colmax(s):
        return jnp.max(jnp.max(s.reshape(tkb // 8, 8, ATTN_QC), axis=0), axis=0, keepdims=True)

    groups = [tuple(c0 + i * ATTN_QC for i in range(ATTN_GROUP))
              for c0 in range(0, nq2, ATTN_GROUP * ATTN_QC)]
    s_first = tuple(scores(0, c, True)[0] for c in groups[0])
    for gi, cols in enumerate(groups):

        def consume(kb, s, m):
            start = kb * tkb
            if not isinstance(start, int):
                start = pl.multiple_of(start, tkb)
            vext = jnp.concatenate([vt_ref[0, :, pl.ds(start, tkb)], ones], axis=0)
            m_out = []
            for i in range(ATTN_GROUP):
                m_new = jnp.maximum(m[i], colmax(s[i]))
                p = jnp.exp(s[i] - m_new).astype(BF16)
                sl = slice(cols[i], cols[i] + ATTN_QC)
                acc_ref[:, sl] = acc_ref[:, sl] * jnp.exp(m[i] - m_new) + jnp.dot(vext, p, preferred_element_type=F32)
                m_out.append(m_new)
            return tuple(m_out)

        for c in cols:
            acc_ref[:, c:c + ATTN_QC] = jnp.zeros((HEAD_W + ATTN_ONES, ATTN_QC), F32)

        def body(kb, carry):
            s, m = carry
            s_next = tuple(scores(kb + 1, c, False)[0] for c in cols)
            return s_next, consume(kb, s, m)

        m0 = tuple(jnp.full((1, ATTN_QC), -1e30, F32) for _ in cols)
        s_last, m_last = lax.fori_loop(0, nkb - 1, body, (s_first, m0), unroll=unroll)
        if gi + 1 < len(groups):
            s_first = tuple(scores(0, c, True)[0] for c in groups[gi + 1])
        consume(nkb - 1, s_last, m_last)

    lam = (jnp.exp(jnp.sum(lam_ref[0:1, :] * lam_ref[1:2, :], axis=-1, keepdims=True))
           - jnp.exp(jnp.sum(lam_ref[2:3, :] * lam_ref[3:4, :], axis=-1, keepdims=True)) + lam_init)
    o = acc_ref[0:HEAD_W, :] / acc_ref[HEAD_W:HEAD_W + 1, :]
    o = (o[:, :tq] - lam * o[:, tq:]).T
    o = o * lax.rsqrt(jnp.mean(o * o, axis=-1, keepdims=True) + RMS_EPS) * g_ref[...]
    o_ref[...] = (o * (1.0 - lam_init)).astype(o_ref.dtype)


def diff_attention(qlo_t, qhi_t, k, v_t, lam_vecs, norm_g, nb, lp, padf, layer, tq, tkb, unroll):
    m = k.shape[0]
    assert lp % tq == 0 and lp % tkb == 0 and padf < tkb and (2 * tq) % (ATTN_GROUP * ATTN_QC) == 0
    nq = lp // tq
    lam_init = 0.8 - 0.6 * math.exp(-0.3 * layer)
    qspec = pl.BlockSpec((1, HEAD_W, tq), lambda b, h, qi: (b, h, qi))
    return pl.pallas_call(
        functools.partial(_attn_kernel, tkb=tkb, padf=padf, lam_init=lam_init, unroll=unroll),
        out_shape=jax.ShapeDtypeStruct((m, GROUP_W), BF16),
        grid=(nb, N_HEADS, nq),
        in_specs=[pl.BlockSpec((4, DA_HD), lambda b, h, qi: (0, 0)),
                  pl.BlockSpec((1, HEAD_W), lambda b, h, qi: (0, 0)),
                  qspec, qspec,
                  pl.BlockSpec((lp, HEAD_W), lambda b, h, qi: (b, h)),
                  pl.BlockSpec((1, HEAD_W, lp), lambda b, h, qi: (b, h, 0))],
        out_specs=pl.BlockSpec((tq, HEAD_W), lambda b, h, qi: (b * nq + qi, h)),
        scratch_shapes=[pltpu.VMEM((HEAD_W, 2 * tq), BF16),
                        pltpu.VMEM((HEAD_W + ATTN_ONES, 2 * tq), F32)],
        compiler_params=_cparams("arbitrary", "arbitrary", "arbitrary"),
        name="diff_attention",
    )(lam_vecs, norm_g.reshape(1, HEAD_W), qlo_t, qhi_t, k, v_t)


N_LEVELS = 6


def _scan_constants():
    c = CHUNK
    idx = np.arange(c)
    cs = np.zeros((2, (N_LEVELS + 2) * c, c), np.float32)
    mask = np.zeros((2, N_LEVELS + 2, c, c), np.float32)
    for d in range(2):
        cum = (idx[None, :] <= idx[:, None]) if d == 0 else (idx[None, :] >= idx[:, None])
        cs[d, 0:c] = cum
        for l in range(N_LEVELS):
            s = (c // 2) >> l
            blk = idx // s
            ref = (blk | 1) * s - 1 if d == 0 else (blk | 1) * s
            cs[d, (l + 1) * c:(l + 2) * c] = cum[ref]
            if d == 0:
                mask[d, l] = ((blk[:, None] & 1) == 1) & (blk[None, :] == blk[:, None] - 1)
            else:
                mask[d, l] = ((blk[:, None] & 1) == 0) & (blk[None, :] == blk[:, None] + 1)
        cs[d, (N_LEVELS + 1) * c:] = 1.0
        mask[d, N_LEVELS] = cum
        mask[d, N_LEVELS + 1] = cum & (idx[None, :] != idx[:, None])
    return jnp.asarray(np.concatenate([cs] * 3, axis=2), BF16), jnp.asarray(mask)


def _split3(x):
    hi = x.astype(BF16)
    r = x - hi.astype(F32)
    mid = r.astype(BF16)
    return hi, mid, (r - mid.astype(F32)).astype(BF16)


def _split3_rows(x):
    return jnp.concatenate(_split3(x), axis=0)


def _split3_lanes(x):
    return jnp.concatenate(_split3(x), axis=1)


def _nt(a, b):
    return lax.dot_general(a.astype(BF16), b.astype(BF16), (((1,), (1,)), ((), ())),
                           preferred_element_type=F32)


def _tn(a, b):
    return lax.dot_general(a.astype(BF16), b.astype(BF16), (((0,), (0,)), ((), ())),
                           preferred_element_type=F32)


def _nn(a, b):
    return jnp.dot(a.astype(BF16), b.astype(BF16), preferred_element_type=F32)


def _hgrn_prep_kernel(p_ref, lb_ref, q_ref, v_ref, g_ref, sg_ref):
    w = GROUP_W
    q = p_ref[:, 0:w]
    q_ref[...] = (q * jax.nn.sigmoid(q) * (HEAD_W ** -0.5)).astype(BF16)
    for d in range(2):
        lb = lb_ref[d:d + 1, :]
        f = p_ref[:, (1 + d) * w:(2 + d) * w]
        g_ref[d] = jnp.log(lb + (1.0 - lb) * jax.nn.sigmoid(f))
    v_ref[...] = p_ref[:, 3 * w:4 * w].astype(BF16)
    gate = p_ref[:, 4 * w:5 * w]
    sg_ref[...] = (gate * jax.nn.sigmoid(gate)).astype(BF16)


def hgrn_prep(proj, col_block, lb, tm):
    m = proj.shape[0]
    w = GROUP_W
    ospec = pl.BlockSpec((tm, w), lambda i: (i, 0))
    return pl.pallas_call(
        _hgrn_prep_kernel,
        out_shape=(jax.ShapeDtypeStruct((m, w), BF16), jax.ShapeDtypeStruct((m, w), BF16),
                   jax.ShapeDtypeStruct((2, m, w), F32), jax.ShapeDtypeStruct((m, w), BF16)),
        grid=(m // tm,),
        in_specs=[pl.BlockSpec((tm, 5 * w), lambda i: (i, col_block)),
                  pl.BlockSpec((2, w), lambda i: (0, 0))],
        out_specs=(ospec, ospec, pl.BlockSpec((2, tm, w), lambda i: (0, i, 0)), ospec),
        compiler_params=_cparams("arbitrary"),
        name="hgrn_prep",
    )(proj, lb)


def _head(x, h):
    return x[:, h * HEAD_W:(h + 1) * HEAD_W]


def _chunk_rows(ci, nch, d):
    return pl.multiple_of((ci + d * (nch - 1 - 2 * ci)) * CHUNK, CHUNK)


def _hgrn_scan_kernel(cs_ref, mask_ref, q_ref, v_ref, g_ref, o_ref, s_ref, *, nch):
    c = CHUNK
    d = pl.program_id(1)
    heads = range(N_HEADS)
    chunks = range(nch)

    @pl.when(pl.program_id(2) == 0)
    def _():
        s_ref[...] = jnp.zeros(s_ref.shape, F32)

    eye = mask_ref[0, N_LEVELS] - mask_ref[0, N_LEVELS + 1]
    rows = [_chunk_rows(ci, nch, d) for ci in chunks]
    g = [g_ref[0, pl.ds(r0, c), :] for r0 in rows]
    q = [q_ref[pl.ds(r0, c), :].astype(F32) for r0 in rows]
    v = [v_ref[pl.ds(r0, c), :] for r0 in rows]
    big = [jnp.dot(cs_ref[0], _split3_rows(gi), preferred_element_type=F32) for gi in g]
    b = [x[0:c] for x in big]
    tot = [x[(N_LEVELS + 1) * c:(N_LEVELS + 2) * c] for x in big]
    k = [1.0 - jnp.exp(gi) for gi in g]
    a = [[eye * jnp.sum(_head(q[ci] * k[ci], h), axis=-1, keepdims=True) for h in heads] for ci in chunks]
    q16 = [q_ref[pl.ds(r0, c), :] for r0 in rows]
    k16 = [ki.astype(BF16) for ki in k]
    for l in range(N_LEVELS):
        e = [jnp.exp(-jnp.abs(b[ci] - big[ci][(l + 1) * c:(l + 2) * c])).astype(BF16) for ci in chunks]
        qe = [q16[ci] * e[ci] for ci in chunks]
        ke = [k16[ci] * e[ci] for ci in chunks]
        lm = mask_ref[0, l]
        a = [[a[ci][h] + lm * _nt(_head(qe[ci], h), _head(ke[ci], h)) for h in heads] for ci in chunks]
    o_intra = [[_nn(a[ci][h], _head(v[ci], h)) for h in heads] for ci in chunks]
    qb = [(q[ci] * jnp.exp(b[ci])).astype(BF16) for ci in chunks]
    kd = [(k[ci] * jnp.exp(tot[ci] - b[ci])).astype(BF16) for ci in chunks]
    dec = [jnp.exp(tot[ci][0:1, :]) for ci in chunks]
    s = [s_ref[h] for h in heads]
    for ci in chunks:
        for h in heads:
            o = o_intra[ci][h] + _nt(_head(qb[ci], h), s[h])
            o_ref[0, pl.ds(rows[ci], c), h * HEAD_W:(h + 1) * HEAD_W] = o
        s = [s[h] * _head(dec[ci], h) + _tn(_head(v[ci], h), _head(kd[ci], h)) for h in heads]
    for h in heads:
        s_ref[h] = s[h]


def _scan_row_block(nblk):
    def row(b, d, t):
        return b * nblk + t + d * (nblk - 1 - 2 * t)
    return row


def hgrn_scan(q, v, g, consts, nb, lp, tb):
    m, w = q.shape
    cs, mask = consts
    row = _scan_row_block(lp // tb)
    return pl.pallas_call(
        functools.partial(_hgrn_scan_kernel, nch=tb // CHUNK),
        out_shape=jax.ShapeDtypeStruct((2, m, w), F32),
        grid=(nb, 2, lp // tb),
        in_specs=[pl.BlockSpec((1,) + cs.shape[1:], lambda b, d, t: (d, 0, 0)),
                  pl.BlockSpec((1,) + mask.shape[1:], lambda b, d, t: (d, 0, 0, 0)),
                  pl.BlockSpec((tb, w), lambda b, d, t: (row(b, d, t), 0)),
                  pl.BlockSpec((tb, w), lambda b, d, t: (row(b, d, t), 0)),
                  pl.BlockSpec((1, tb, w), lambda b, d, t: (d, row(b, d, t), 0))],
        out_specs=pl.BlockSpec((1, tb, w), lambda b, d, t: (d, row(b, d, t), 0)),
        scratch_shapes=[pltpu.VMEM((N_HEADS, HEAD_W, HEAD_W), F32)],
        compiler_params=_cparams("arbitrary", "arbitrary", "arbitrary"),
        name="hgrn_scan",
    )(cs, mask, q, v, g)


def _gated_norm_kernel(o_ref, sg_ref, g_ref, out_ref):
    o = o_ref[0] + o_ref[1]
    gain = g_ref[...]
    for h in range(N_HEADS):
        sl = slice(h * HEAD_W, (h + 1) * HEAD_W)
        oh = o[:, sl]
        y = oh * lax.rsqrt(jnp.mean(oh * oh, axis=-1, keepdims=True) + RMS_EPS) * gain
        out_ref[:, sl] = (y * sg_ref[:, sl].astype(F32)).astype(out_ref.dtype)


def gated_norm(o2, sgate, norm_g, tm):
    m = sgate.shape[0]
    w = GROUP_W
    return pl.pallas_call(
        _gated_norm_kernel,
        out_shape=jax.ShapeDtypeStruct((m, w), BF16),
        grid=(m // tm,),
        in_specs=[pl.BlockSpec((2, tm, w), lambda i: (0, i, 0)),
                  pl.BlockSpec((tm, w), lambda i: (i, 0)),
                  pl.BlockSpec((1, HEAD_W), lambda i: (0, 0))],
        out_specs=pl.BlockSpec((tm, w), lambda i: (i, 0)),
        compiler_params=_cparams("arbitrary"),
        name="gated_rms_norm",
    )(o2, sgate, norm_g.reshape(1, HEAD_W))


GATE_LANES = 128


def _softplus(x):
    return jnp.maximum(x, 0.0) + jnp.log(1.0 + jnp.exp(-jnp.abs(x)))


def _gdn_prep_kernel(p_ref, pp_ref, pn_ref, cw_ref, acoef_ref, dtb_ref,
                     q_ref, k_ref, v_ref, sz_ref, gates_ref, *, tiles_per_seq, padf):
    w = GROUP_W
    i = pl.program_id(0)
    tm = p_ref.shape[0]
    nmask = jnp.where(i == pl.num_programs(0) - 1, 0.0, 1.0)
    x = _conv3_rows(p_ref[:, 0:3 * w], pp_ref[HALO - 1:HALO, :], pn_ref[0:1, :] * nmask, cw_ref[...])
    x = x * jax.nn.sigmoid(x)
    valid = _valid_rows(i, tiles_per_seq, tm, padf)
    for h in range(N_HEADS):
        sl = slice(h * HEAD_W, (h + 1) * HEAD_W)
        qh = x[:, sl]
        q_ref[:, sl] = (qh * lax.rsqrt(jnp.sum(qh * qh, axis=-1, keepdims=True) + 1e-6)
                        * (HEAD_W ** -0.5)).astype(BF16)
        kh = x[:, w + h * HEAD_W:w + (h + 1) * HEAD_W]
        k_ref[:, sl] = (kh * lax.rsqrt(jnp.sum(kh * kh, axis=-1, keepdims=True) + 1e-6)).astype(BF16)
    v_ref[...] = jnp.where(valid, x[:, 2 * w:3 * w], 0.0).astype(BF16)
    z = p_ref[:, 3 * w:4 * w]
    sz_ref[...] = (z * jax.nn.sigmoid(z)).astype(BF16)
    gt = p_ref[:, 4 * w:4 * w + GATE_LANES]
    lane = lax.broadcasted_iota(jnp.int32, (1, GATE_LANES), 1)
    decay = -acoef_ref[...] * _softplus(gt + dtb_ref[...])
    gates_ref[...] = jnp.where(lane < 2 * N_HEADS, jax.nn.sigmoid(gt), decay)


def gdn_prep(pd, conv_w, a_log, dt_bias, lp, padf, tm):
    m = pd.shape[0]
    w = GROUP_W
    zeros8 = jnp.zeros((2 * N_HEADS,), F32)
    padl = jnp.zeros((GATE_LANES - 4 * N_HEADS,), F32)
    acoef = jnp.concatenate([zeros8, jnp.exp(a_log).reshape(-1), padl]).reshape(1, GATE_LANES)
    dtb = jnp.concatenate([zeros8, dt_bias.reshape(-1), padl]).reshape(1, GATE_LANES)
    pspec, nspec = _halo_specs(tm, 3 * w, m)
    ospec = pl.BlockSpec((tm, w), lambda i: (i, 0))
    return pl.pallas_call(
        functools.partial(_gdn_prep_kernel, tiles_per_seq=lp // tm, padf=padf),
        out_shape=tuple(jax.ShapeDtypeStruct((m, w), BF16) for _ in range(4))
        + (jax.ShapeDtypeStruct((m, GATE_LANES), F32),),
        grid=(m // tm,),
        in_specs=[pl.BlockSpec((tm, pd.shape[1]), lambda i: (i, 0)), pspec, nspec,
                  pl.BlockSpec((3, 3 * w), lambda i: (0, 0)),
                  pl.BlockSpec((1, GATE_LANES), lambda i: (0, 0)),
                  pl.BlockSpec((1, GATE_LANES), lambda i: (0, 0))],
        out_specs=(ospec, ospec, ospec, ospec, pl.BlockSpec((tm, GATE_LANES), lambda i: (i, 0))),
        compiler_params=_cparams("arbitrary"),
        name="gdn_prep",
    )(pd, pd, pd, conv_w, acoef, dtb)


def _gate_selectors():
    sel = np.zeros((2, GATE_LANES, 2 * GROUP_W), np.float32)
    for d in range(2):
        for h in range(N_HEADS):
            sel[d, N_HEADS * d + h, h * HEAD_W:(h + 1) * HEAD_W] = 1.0
            sel[d, 2 * N_HEADS + N_HEADS * d + h, GROUP_W + h * HEAD_W:GROUP_W + (h + 1) * HEAD_W] = 1.0
    return jnp.asarray(np.concatenate([sel] * 3, axis=1), BF16)


def _gdn_scan_kernel(cs_ref, mask_ref, sel_ref, q_ref, k_ref, v_ref, gt_ref, o_ref, s_ref, *, nch):
    c = CHUNK
    w = GROUP_W
    d = pl.program_id(1)
    heads = range(N_HEADS)
    chunks = range(nch)
    items = [(ci, h) for ci in chunks for h in heads]

    @pl.when(pl.program_id(2) == 0)
    def _():
        s_ref[...] = jnp.zeros(s_ref.shape, F32)

    incl = mask_ref[0, N_LEVELS]
    strict = mask_ref[0, N_LEVELS + 1]
    eye = incl - strict
    cum_tot = jnp.concatenate([cs_ref[0, 0:c, :], cs_ref[0, (N_LEVELS + 1) * c:(N_LEVELS + 2) * c, :]], axis=0)
    ones3 = cs_ref[0, (N_LEVELS + 1) * c:(N_LEVELS + 2) * c, :]
    rows = [_chunk_rows(ci, nch, d) for ci in chunks]
    q = [q_ref[pl.ds(r0, c), :].astype(F32) for r0 in rows]
    k = [k_ref[pl.ds(r0, c), :].astype(F32) for r0 in rows]
    v = [v_ref[pl.ds(r0, c), :].astype(F32) for r0 in rows]
    bg = [jnp.dot(_split3_lanes(gt_ref[pl.ds(r0, c), :]), sel_ref[0], preferred_element_type=F32)
          for r0 in rows]
    beta = [x[:, 0:w] for x in bg]
    cb = [jnp.dot(cum_tot, _split3_rows(x[:, w:2 * w]), preferred_element_type=F32) for x in bg]
    bb = [x[0:c] for x in cb]
    tot = [x[c:2 * c] for x in cb]
    bc = {(ci, h): bb[ci][:, h * HEAD_W:h * HEAD_W + c] for ci, h in items}
    br = {it: jnp.dot(ones3, _split3_rows(eye * bc[it]), preferred_element_type=F32) for it in items}
    dec = {it: incl * jnp.exp(jnp.minimum(bc[it] - br[it], 0.0)) for it in items}
    kb = [k[ci] * beta[ci] for ci in chunks]
    n = {(ci, h): strict * _nt(_head(kb[ci], h), _head(k[ci], h)) * dec[(ci, h)] for ci, h in items}
    t = {it: eye - n[it] for it in items}
    pw = {it: _nn(n[it], n[it]) for it in items}
    for _ in range(N_LEVELS - 2):
        t = {it: t[it] + _nn(t[it], pw[it]) for it in items}
        pw = {it: _nn(pw[it], pw[it]) for it in items}
    t = {it: t[it] + _nn(t[it], pw[it]) for it in items}
    eb = [jnp.exp(x) for x in bb]
    rhs_u = [v[ci] * beta[ci] for ci in chunks]
    rhs_w = [kb[ci] * eb[ci] for ci in chunks]
    uw = {(ci, h): _nn(t[(ci, h)], jnp.concatenate([_head(rhs_u[ci], h), _head(rhs_w[ci], h)], axis=1))
          for ci, h in items}
    qk = {(ci, h): _nt(_head(q[ci], h), _head(k[ci], h)) * dec[(ci, h)] for ci, h in items}
    qe = [(q[ci] * eb[ci]).astype(BF16) for ci in chunks]
    kd = [(k[ci] * jnp.exp(tot[ci] - bb[ci])).astype(BF16) for ci in chunks]
    sdec = [jnp.exp(tot[ci][0:1, :]) for ci in chunks]
    s = [s_ref[h] for h in heads]
    for ci in chunks:
        wq = {h: _nn(jnp.concatenate([uw[(ci, h)][:, HEAD_W:].astype(BF16), _head(qe[ci], h)], axis=0), s[h])
              for h in heads}
        vnew = {h: uw[(ci, h)][:, 0:HEAD_W] - wq[h][0:c] for h in heads}
        for h in heads:
            o_ref[0, pl.ds(rows[ci], c), h * HEAD_W:(h + 1) * HEAD_W] = wq[h][c:2 * c] + _nn(qk[(ci, h)], vnew[h])
        s = [s[h] * _head(sdec[ci], h) + _tn(_head(kd[ci], h), vnew[h]) for h in heads]
    for h in heads:
        s_ref[h] = s[h]


def gdn_scan(q, k, v, gates, consts, sel, nb, lp, tb):
    m, w = q.shape
    cs, mask = consts
    row = _scan_row_block(lp // tb)
    rspec = pl.BlockSpec((tb, w), lambda b, d, t: (row(b, d, t), 0))
    return pl.pallas_call(
        functools.partial(_gdn_scan_kernel, nch=tb // CHUNK),
        out_shape=jax.ShapeDtypeStruct((2, m, w), F32),
        grid=(nb, 2, lp // tb),
        in_specs=[pl.BlockSpec((1,) + cs.shape[1:], lambda b, d, t: (d, 0, 0)),
                  pl.BlockSpec((1,) + mask.shape[1:], lambda b, d, t: (d, 0, 0, 0)),
                  pl.BlockSpec((1,) + sel.shape[1:], lambda b, d, t: (d, 0, 0)),
                  rspec, rspec, rspec,
                  pl.BlockSpec((tb, GATE_LANES), lambda b, d, t: (row(b, d, t), 0))],
        out_specs=pl.BlockSpec((1, tb, w), lambda b, d, t: (d, row(b, d, t), 0)),
        scratch_shapes=[pltpu.VMEM((N_HEADS, HEAD_W, HEAD_W), F32)],
        compiler_params=_cparams("arbitrary", "arbitrary", "arbitrary"),
        name="gdn_scan",
    )(cs, mask, sel, q, k, v, gates)


FFT_N2 = 256
FFT_TS = 8


def _fft_n1(lp, l_real):
    need = -(-(2 * l_real - 1) // FFT_N2)
    return max(-(-need // 8) * 8, -(-(lp // FFT_N2) // 8) * 8)


def _outer_dft(n1, k_rows, inverse, n_total):
    kp = -(-k_rows // 16) * 16
    a = np.arange(n1)[:, None] * np.arange(kp)[None, :]
    ang = 2.0 * np.pi * (a % n1) / n1
    c, s = np.cos(ang), np.sin(ang)
    live = (np.arange(kp) < k_rows)[None, :]
    c, s = c * live, s * live
    if not inverse:
        return jnp.asarray(np.concatenate([c, -s], axis=0), BF16)
    return jnp.asarray(np.concatenate([c.T, -s.T], axis=1) / n_total, F32)


def inner_dft(n1):
    n2 = FFT_N2
    n = n1 * n2
    k = jnp.arange(n1, dtype=jnp.int32)[:, None, None] + n1 * jnp.arange(n2, dtype=jnp.int32)[None, :, None]
    r = (k * jnp.arange(n2, dtype=jnp.int32)[None, None, :]) % n
    ang = r.astype(F32) * (2.0 * math.pi / n)
    gr, gi = jnp.cos(ang), -jnp.sin(ang)
    gg = jnp.concatenate([jnp.concatenate([gr, -gi], axis=2), jnp.concatenate([gi, gr], axis=2)], axis=1)
    return gg.astype(BF16), jnp.swapaxes(gg, 1, 2).astype(BF16)


def _hyena_prep_kernel(p_ref, pp_ref, pn_ref, cw_ref, cb_ref, x0_ref, z_ref, *, tiles_per_seq, padf):
    w = GROUP_W
    i = pl.program_id(0)
    tm = p_ref.shape[0]
    nmask = jnp.where(i == pl.num_programs(0) - 1, 0.0, 1.0)
    u = _conv3_rows(p_ref[...], pp_ref[HALO - 1:HALO, :], pn_ref[0:1, :] * nmask, cw_ref[...]) + cb_ref[...]
    valid = _valid_rows(i, tiles_per_seq, tm, padf)
    x0_ref[...] = u[:, 0:w]
    z_ref[...] = jnp.where(valid, u[:, w:2 * w] * u[:, 2 * w:3 * w], 0.0)


def hyena_prep(pb, conv_w, conv_b, lp, padf, tm):
    m, w3 = pb.shape
    w = GROUP_W
    pspec, nspec = _halo_specs(tm, w3, m)
    ospec = pl.BlockSpec((tm, w), lambda i: (i, 0))
    return pl.pallas_call(
        functools.partial(_hyena_prep_kernel, tiles_per_seq=lp // tm, padf=padf),
        out_shape=(jax.ShapeDtypeStruct((m, w), F32), jax.ShapeDtypeStruct((m, w), F32)),
        grid=(m // tm,),
        in_specs=[pl.BlockSpec((tm, w3), lambda i: (i, 0)), pspec, nspec,
                  pl.BlockSpec((3, w3), lambda i: (0, 0)), pl.BlockSpec((1, w3), lambda i: (0, 0))],
        out_specs=(ospec, ospec),
        compiler_params=_cparams("arbitrary"),
        name="hyena_prep",
    )(pb, pb, pb, conv_w, conv_b.reshape(1, w3))


def _hyena_filter_kernel(w1t_ref, w1c_ref, w1s_ref, b1_ref, f1_ref, w2_ref, b2_ref, f2_ref, w3_ref, dec_ref,
                         filt_ref, asum_ref, *, l_real):
    i = pl.program_id(0)
    tm = filt_ref.shape[0]
    w = GROUP_W
    row = i * tm + lax.broadcasted_iota(jnp.int32, (tm, 1), 0)
    rf = row.astype(F32)
    t = rf * (1.0 / (l_real - 1))
    band = lax.broadcasted_iota(jnp.int32, (1, HY_BANDS), 1).astype(F32)
    bands = 1e-4 + band * ((HY_BANDS - 1 - 1e-4) / (HY_BANDS - 1))
    ang = ((2.0 * math.pi / l_real) * rf) * bands

    def hdot(a, b):
        return jnp.dot(a, b, precision=HIGHEST, preferred_element_type=F32)

    pre = t * w1t_ref[...] + hdot(jnp.cos(ang), w1c_ref[...]) - hdot(jnp.sin(ang), w1s_ref[...])
    hid = jnp.sin(f1_ref[...] * (pre + b1_ref[...]))
    hid = jnp.sin(f2_ref[...] * (hdot(hid, w2_ref[...]) + b2_ref[...]))
    filt = hdot(hid, w3_ref[...])
    window = jnp.exp(-t * jnp.abs(dec_ref[...])) + HY_SHIFT
    filt = jnp.where(row < l_real, filt * window, 0.0)

    @pl.when(i == 0)
    def _():
        asum_ref[...] = jnp.zeros(asum_ref.shape, F32)

    asum_ref[...] += jnp.sum(jnp.abs(filt), axis=0, keepdims=True)
    lane = lax.broadcasted_iota(jnp.int32, (1, 2 * w), 1)
    filt_ref[...] = jnp.where((row == 0) & (lane >= w), 0.0, filt)


def hyena_filter(w1, b1, f1, w2, b2, f2, w3, decay, lp, l_real, tm):
    w = GROUP_W
    nf = w1.shape[1]
    small = lambda a: pl.BlockSpec(a.shape, lambda i: (0,) * a.ndim)
    args = (w1[0:1], w1[1:1 + HY_BANDS], w1[1 + HY_BANDS:], b1.reshape(1, nf), f1.reshape(1, nf),
            w2, b2.reshape(1, nf), f2.reshape(1, nf), w3, decay.reshape(1, 2 * w))
    return pl.pallas_call(
        functools.partial(_hyena_filter_kernel, l_real=l_real),
        out_shape=(jax.ShapeDtypeStruct((lp, 2 * w), F32), jax.ShapeDtypeStruct((1, 2 * w), F32)),
        grid=(lp // tm,),
        in_specs=[small(a) for a in args],
        out_specs=(pl.BlockSpec((tm, 2 * w), lambda i: (i, 0)), pl.BlockSpec((1, 2 * w), lambda i: (0, 0))),
        compiler_params=_cparams("arbitrary"),
        name="hyena_filter",
    )(*args)


def _fft_outer_kernel(f_ref, x_ref, o_ref, pad_ref):
    k = x_ref.shape[1]
    pad_ref[...] = jnp.zeros(pad_ref.shape, F32)
    for s in range(x_ref.shape[2]):
        pad_ref[0:k, :] = x_ref[0, :, s, :]
        o_ref[0, :, s, :] = jnp.dot(f_ref[...], pad_ref[...].astype(BF16), preferred_element_type=F32)


def fft_outer(x, fmat):
    b, k, n2, c = x.shape
    rows, kp = fmat.shape
    ts = FFT_TS
    return pl.pallas_call(
        _fft_outer_kernel,
        out_shape=jax.ShapeDtypeStruct((b, rows, n2, c), F32),
        grid=(b, n2 // ts),
        in_specs=[pl.BlockSpec((rows, kp), lambda bi, j: (0, 0)),
                  pl.BlockSpec((1, k, ts, c), lambda bi, j: (bi, 0, j, 0))],
        out_specs=pl.BlockSpec((1, rows, ts, c), lambda bi, j: (bi, 0, j, 0)),
        scratch_shapes=[pltpu.VMEM((kp, c), F32)],
        compiler_params=_cparams("arbitrary", "arbitrary"),
        name="fft_outer",
    )(fmat, x)


def _filter_spectrum_kernel(gg_ref, a_ref, asum_ref, o_ref):
    w = GROUP_W
    n2 = FFT_N2
    a = jnp.concatenate([a_ref[0, 0], a_ref[1, 0]], axis=0).astype(BF16)
    x = jnp.dot(gg_ref[0], a, preferred_element_type=F32)
    s = 1.0 / (asum_ref[:, 0:w] + asum_ref[:, w:2 * w] + 1e-6)
    o_ref[0, 0] = (x[0:n2, 0:w] + x[0:n2, w:2 * w]) * s
    o_ref[0, 1] = (x[n2:2 * n2, 0:w] - x[n2:2 * n2, w:2 * w]) * s


def filter_spectrum(a, gg, asum):
    _, n1, n2, w2 = a.shape
    w = w2 // 2
    return pl.pallas_call(
        _filter_spectrum_kernel,
        out_shape=jax.ShapeDtypeStruct((n1, 2, n2, w), F32),
        grid=(n1,),
        in_specs=[pl.BlockSpec((1, 2 * n2, 2 * n2), lambda k: (k, 0, 0)),
                  pl.BlockSpec((2, 1, n2, w2), lambda k: (0, k, 0, 0)),
                  pl.BlockSpec((1, w2), lambda k: (0, 0))],
        out_specs=pl.BlockSpec((1, 2, n2, w), lambda k: (k, 0, 0, 0)),
        compiler_params=_cparams("arbitrary"),
        name="hyena_filter_spectrum",
    )(gg, a, asum)


def _fft_mid_kernel(gg_ref, ggt_ref, kf_ref, a_ref, o_ref):
    n2 = FFT_N2
    a = jnp.concatenate([a_ref[0, 0, 0], a_ref[0, 1, 0]], axis=0).astype(BF16)
    x = jnp.dot(gg_ref[0], a, preferred_element_type=F32)
    xr, xi = x[0:n2], x[n2:2 * n2]
    kr, ki = kf_ref[0, 0], kf_ref[0, 1]
    y = jnp.concatenate([xr * kr - xi * ki, xr * ki + xi * kr], axis=0).astype(BF16)
    bm = jnp.dot(ggt_ref[0], y, preferred_element_type=F32)
    o_ref[0, 0, 0] = bm[0:n2]
    o_ref[0, 1, 0] = bm[n2:2 * n2]


def fft_mid(a, gg, ggt, kf):
    b, _, n1, n2, w = a.shape
    aspec = pl.BlockSpec((1, 2, 1, n2, w), lambda k, bi: (bi, 0, k, 0, 0))
    gspec = pl.BlockSpec((1, 2 * n2, 2 * n2), lambda k, bi: (k, 0, 0))
    return pl.pallas_call(
        _fft_mid_kernel,
        out_shape=jax.ShapeDtypeStruct(a.shape, F32),
        grid=(n1, b),
        in_specs=[gspec, gspec, pl.BlockSpec((1, 2, n2, w), lambda k, bi: (k, 0, 0, 0)), aspec],
        out_specs=aspec,
        compiler_params=_cparams("arbitrary", "arbitrary"),
        name="fft_inner_conv",
    )(gg, ggt, kf, a)


def _fft_final_kernel(f_ref, b_ref, x0_ref, z_ref, d_ref, o_ref):
    k = o_ref.shape[1]
    for s in range(o_ref.shape[2]):
        y = jnp.dot(f_ref[...], b_ref[0, :, s, :], precision=HIGHEST, preferred_element_type=F32)
        o_ref[0, :, s, :] = x0_ref[0, :, s, :] * (y[0:k] + z_ref[0, :, s, :] * d_ref[...])


def fft_final(bm, finv, x0, z, d):
    b, rows, n2, c = bm.shape
    k = x0.shape[1]
    kp = finv.shape[0]
    ts = FFT_TS
    xspec = pl.BlockSpec((1, k, ts, c), lambda bi, j: (bi, 0, j, 0))
    return pl.pallas_call(
        _fft_final_kernel,
        out_shape=jax.ShapeDtypeStruct((b, k, n2, c), F32),
        grid=(b, n2 // ts),
        in_specs=[pl.BlockSpec((kp, rows), lambda bi, j: (0, 0)),
                  pl.BlockSpec((1, rows, ts, c), lambda bi, j: (bi, 0, j, 0)),
                  xspec, xspec, pl.BlockSpec((1, c), lambda bi, j: (0, 0))],
        out_specs=xspec,
        compiler_params=_cparams("arbitrary", "arbitrary"),
        name="fft_outer_inverse_gate",
    )(finv, bm, x0, z, d)


def hyena_mixer(pb, hp, nb, lp, padf, l_real, tm):
    w = GROUP_W
    n2 = FFT_N2
    k1 = lp // n2
    n1 = _fft_n1(lp, l_real)
    n_total = n1 * n2
    ffwd = _outer_dft(n1, k1, False, n_total)
    finv = _outer_dft(n1, k1, True, n_total)
    gg, ggt = hp['gg'], hp['ggt']
    filt, asum = hyena_filter(hp['w1'], hp['b1'], hp['f1'], hp['w2'], hp['b2'], hp['f2'], hp['w3'],
                              hp['decay'], lp, l_real, tm)
    fa = fft_outer(filt.reshape(1, k1, n2, 2 * w), ffwd)
    kf = filter_spectrum(fa.reshape(2, n1, n2, 2 * w), gg, asum)
    x0, z = hyena_prep(pb, hp['conv_w'], hp['conv_b'], lp, padf, tm)
    z4 = z.reshape(nb, k1, n2, w)
    a = fft_outer(z4, ffwd)
    bm = fft_mid(a.reshape(nb, 2, n1, n2, w), gg, ggt, kf)
    out = fft_final(bm.reshape(nb, 2 * n1, n2, w), finv, x0.reshape(nb, k1, n2, w), z4, hp['d'].reshape(1, w))
    return out.reshape(nb * lp, w)


TM = 640
ATTN_TQ = 1280
ATTN_TKB = 1280
ATTN_UNROLL = 4
SCAN_TB = 640
FFN_TN = 512
FFN_DOWN_TM = 320


def kernel(x_prompt, x_sample, meta, emb_ln_g, emb_ln_b, w_in, lam_q1, lam_k1, lam_q2, lam_k2, attn_norm_g, hy_conv_w, hy_conv_b, hy_w1, hy_b1, hy_f1, hy_w2, hy_b2, hy_f2, hy_w3, hy_decay, hy_d, hg_lb, hg_norm_g, gdn_conv_w, gdn_a_log, gdn_dt_bias, gdn_norm_g, w_out, ln1_g, ln1_b, w_up, ffn_conv_w, ffn_conv_b, w_down, ln2_g, ln2_b):
    n_prompt = x_prompt.shape[0]
    x = jnp.concatenate([x_prompt, x_sample], axis=0)
    nb, seq, d = x.shape
    padf = ROW_ALIGN - N_META
    lp = seq + ROW_ALIGN
    l_real = seq + N_META
    m = nb * lp
    w = GROUP_W

    h, hb = (a.reshape(m, d) for a in embed(x, meta, emb_ln_g, emb_ln_b, padf))
    tables = rope_tables(lp, padf)
    consts = _scan_constants()
    sel = _gate_selectors()
    gg, ggt = inner_dft(_fft_n1(lp, l_real))
    sm = jax.nn.softmax(hg_lb, axis=0)
    lb_all = jnp.cumsum(sm, axis=0) - sm[0]

    for l in range(DEPTH):
        wl = w_in[l].astype(BF16)
        wd = jnp.pad(wl[:, 11 * w:], ((0, 0), (0, 4 * w + GATE_LANES - (wl.shape[1] - 11 * w))))
        pa = matmul(hb, wl[:, 0:3 * w], TM, 3 * w)
        pb = matmul(hb, wl[:, 3 * w:6 * w], TM, 3 * w)
        pc = matmul(hb, wl[:, 6 * w:11 * w], TM, 5 * w // 2)
        pd = matmul(hb, wd, TM, wd.shape[1])

        qlo, qhi, ka, va = attn_prep(pa, tables, nb, lp, TM)
        lam_vecs = jnp.stack([lam_q1[l], lam_k1[l], lam_q2[l], lam_k2[l]])
        oa = diff_attention(qlo, qhi, ka, va, lam_vecs, attn_norm_g[l], nb, lp, padf, l,
                            ATTN_TQ, ATTN_TKB, ATTN_UNROLL)

        hp = dict(gg=gg, ggt=ggt, w1=hy_w1[l], b1=hy_b1[l], f1=hy_f1[l], w2=hy_w2[l], b2=hy_b2[l],
                  f2=hy_f2[l], w3=hy_w3[l], decay=hy_decay[l], d=hy_d[l], conv_w=hy_conv_w[l],
                  conv_b=hy_conv_b[l])
        ob = hyena_mixer(pb, hp, nb, lp, padf, l_real, TM)

        qc, vc, gc, sgc = hgrn_prep(pc, 0, lb_all[l], TM)
        oc = gated_norm(hgrn_scan(qc, vc, gc, consts, nb, lp, SCAN_TB), sgc, hg_norm_g[l], TM)

        qd, kd, vd, szd, gates = gdn_prep(pd, gdn_conv_w[l], gdn_a_log[l], gdn_dt_bias[l], lp, padf, TM)
        od = gated_norm(gdn_scan(qd, kd, vd, gates, consts, sel, nb, lp, SCAN_TB), szd, gdn_norm_g[l], TM)

        h, hb = proj_residual_ln([oa, ob, oc, od], w_out[l].astype(BF16), h, ln1_g[l], ln1_b[l], lp, padf, TM)
        act = ffn_up(hb, w_up[l].astype(BF16), ffn_conv_w[l], ffn_conv_b[l], TM, FFN_TN)
        h, hb = ffn_down_ln(act, w_down[l].astype(BF16), h, ln2_g[l], ln2_b[l], lp, padf, FFN_DOWN_TM)

    y = h.reshape(nb, lp, d)[:, ROW_ALIGN:]
    return (y[:n_prompt], y[n_prompt:])
```

```python
import functools
import math

import jax
import jax.numpy as jnp
import numpy as np
from jax import lax
from jax.experimental import pallas as pl
from jax.experimental.pallas import tpu as pltpu

D_MODEL = 2048
DEPTH = 4
N_META = 16
GROUP_W = 512
N_HEADS = 4
HEAD_W = 128
DA_HD = 64
ROT_DIM = 16
ROPE_THETA = 500000.0
HY_BANDS = 16
HY_SHIFT = 0.05
CHUNK = 64
D_FF = 5632
ALPHA = (2.0 * DEPTH) ** 0.25
LN_EPS = 1e-5
RMS_EPS = 1e-6
F32 = jnp.float32
BF16 = jnp.bfloat16
HIGHEST = lax.Precision.HIGHEST

ROW_ALIGN = 256
VMEM_LIMIT = 56 * 1024 * 1024


def _cparams(*sem):
    return pltpu.CompilerParams(dimension_semantics=sem, vmem_limit_bytes=VMEM_LIMIT)


def _valid_rows(tile_idx, tiles_per_seq, tm, padf):
    base = (tile_idx % tiles_per_seq) * tm
    return (base + lax.broadcasted_iota(jnp.int32, (tm, 1), 0)) >= padf


def _ln_rows(y, g, b):
    mu = jnp.mean(y, axis=-1, keepdims=True)
    d = y - mu
    var = jnp.mean(d * d, axis=-1, keepdims=True)
    return d * lax.rsqrt(var + LN_EPS) * g + b


def _embed_kernel(x_ref, meta_ref, g_ref, b_ref, o_ref, ob_ref, *, padf):
    t = pl.program_id(1)
    g = g_ref[...]
    b = b_ref[...]

    @pl.when(t == 0)
    def _():
        y = jnp.concatenate([jnp.zeros((padf, o_ref.shape[2]), F32), _ln_rows(meta_ref[...], g, b)], axis=0)
        o_ref[0] = y
        ob_ref[0] = y.astype(BF16)

    @pl.when(t > 0)
    def _():
        y = _ln_rows(x_ref[0], g, b)
        o_ref[0] = y
        ob_ref[0] = y.astype(BF16)


def embed(x, meta, g, b, padf):
    nb, s, d = x.shape
    tm = ROW_ALIGN
    assert padf + N_META == tm and s % tm == 0
    lp = s + tm
    ospec = pl.BlockSpec((1, tm, d), lambda bi, t: (bi, t, 0))
    return pl.pallas_call(
        functools.partial(_embed_kernel, padf=padf),
        out_shape=(jax.ShapeDtypeStruct((nb, lp, d), F32), jax.ShapeDtypeStruct((nb, lp, d), BF16)),
        grid=(nb, lp // tm),
        in_specs=[pl.BlockSpec((1, tm, d), lambda bi, t: (bi, jnp.maximum(t - 1, 0), 0)),
                  pl.BlockSpec((N_META, d), lambda bi, t: (0, 0)),
                  pl.BlockSpec((1, d), lambda bi, t: (0, 0)),
                  pl.BlockSpec((1, d), lambda bi, t: (0, 0))],
        out_specs=(ospec, ospec),
        compiler_params=_cparams("arbitrary", "arbitrary"),
        name="embed_ln",
    )(x, meta, g.reshape(1, d), b.reshape(1, d))


def _mm_kernel(x_ref, w_ref, o_ref):
    o_ref[...] = jnp.dot(x_ref[...], w_ref[...], preferred_element_type=F32).astype(o_ref.dtype)


def matmul(x, w, tm, tn, out_dtype=F32):
    m, k = x.shape
    n = w.shape[1]
    assert m % tm == 0 and n % tn == 0
    return pl.pallas_call(
        _mm_kernel,
        out_shape=jax.ShapeDtypeStruct((m, n), out_dtype),
        grid=(n // tn, m // tm),
        in_specs=[pl.BlockSpec((tm, k), lambda j, i: (i, 0)),
                  pl.BlockSpec((k, tn), lambda j, i: (0, j))],
        out_specs=pl.BlockSpec((tm, tn), lambda j, i: (i, j)),
        compiler_params=_cparams("arbitrary", "arbitrary"),
        name="dense_matmul",
    )(x, w)


def _proj_ln_kernel(*refs, n_in, tiles_per_seq, padf):
    o_refs = refs[:n_in]
    w_ref, h_ref, g_ref, b_ref, out_ref, outb_ref = refs[n_in:]
    tm = h_ref.shape[0]
    acc = None
    off = 0
    for r in o_refs:
        kw = r.shape[1]
        part = jnp.dot(r[...].astype(BF16), w_ref[off:off + kw, :], preferred_element_type=F32)
        acc = part if acc is None else acc + part
        off += kw
    y = _ln_rows(ALPHA * h_ref[...] + acc, g_ref[...], b_ref[...])
    valid = _valid_rows(pl.program_id(0), tiles_per_seq, tm, padf)
    y = jnp.where(valid, y, 0.0)
    out_ref[...] = y
    outb_ref[...] = y.astype(BF16)


def proj_residual_ln(parts, w, h, g, b, lp, padf, tm):
    m, d = h.shape
    assert m % tm == 0 and lp % tm == 0
    n_in = len(parts)
    in_specs = [pl.BlockSpec((tm, p.shape[1]), lambda i: (i, 0)) for p in parts]
    in_specs += [pl.BlockSpec(w.shape, lambda i: (0, 0)),
                 pl.BlockSpec((tm, d), lambda i: (i, 0)),
                 pl.BlockSpec((1, d), lambda i: (0, 0)),
                 pl.BlockSpec((1, d), lambda i: (0, 0))]
    ospec = pl.BlockSpec((tm, d), lambda i: (i, 0))
    return pl.pallas_call(
        functools.partial(_proj_ln_kernel, n_in=n_in, tiles_per_seq=lp // tm, padf=padf),
        out_shape=(jax.ShapeDtypeStruct((m, d), F32), jax.ShapeDtypeStruct((m, d), BF16)),
        grid=(m // tm,),
        in_specs=in_specs,
        out_specs=(ospec, ospec),
        compiler_params=_cparams("arbitrary"),
        name="proj_residual_ln",
    )(*parts, w, h, g.reshape(1, d), b.reshape(1, d))


HALO = 8
HALO_BF16 = 16


def _halo_specs(tm, width, m, col_block=0, halo=HALO):
    nblk = m // halo
    per = tm // halo

    def prev_map(*idx):
        i = idx[-1]
        return (jnp.maximum(i * per - 1, 0), col_block)

    def next_map(*idx):
        i = idx[-1]
        return (jnp.minimum((i + 1) * per, nblk - 1), col_block)

    return (pl.BlockSpec((halo, width), prev_map), pl.BlockSpec((halo, width), next_map))


def _conv3_rows(pm, prev_row, next_row, cw):
    tm = pm.shape[0]
    rid = lax.broadcasted_iota(jnp.int32, (tm, 1), 0)
    down = jnp.where(rid == 0, prev_row, pltpu.roll(pm, 1, 0))
    up = jnp.where(rid == tm - 1, next_row, pltpu.roll(pm, tm - 1, 0))
    return down * cw[0:1, :] + pm * cw[1:2, :] + up * cw[2:3, :]


def _ffn_up_kernel(x_ref, xp_ref, xn_ref, wg_ref, wu_ref, cwg_ref, cwu_ref, cbg_ref, cbu_ref, o_ref):
    i = pl.program_id(1)
    last = pl.num_programs(1) - 1
    x = x_ref[...]
    hr = xp_ref.shape[0]
    halo = jnp.concatenate([xp_ref[...], xn_ref[...]], axis=0)
    nmask = jnp.where(i == last, 0.0, 1.0)

    def branch(w_ref, cw_ref, cb_ref, sl):
        w = w_ref[:, sl]
        pm = jnp.dot(x, w, preferred_element_type=F32)
        ph = jnp.dot(halo, w, preferred_element_type=F32)
        return _conv3_rows(pm, ph[hr - 1:hr, :], ph[hr:hr + 1, :] * nmask, cw_ref[:, sl]) + cb_ref[:, sl]

    tn = o_ref.shape[1]
    for c0 in range(0, tn, FFN_SUB):
        sl = slice(c0, min(c0 + FFN_SUB, tn))
        g = branch(wg_ref, cwg_ref, cbg_ref, sl)
        u = branch(wu_ref, cwu_ref, cbu_ref, sl)
        o_ref[:, sl] = (g * jax.nn.sigmoid(g) * u).astype(o_ref.dtype)


FFN_SUB = 256


def ffn_up(h, w_up, conv_w, conv_b, tm, tn):
    m, k = h.shape
    f = w_up.shape[1] // 2
    assert m % tm == 0 and f % tn == 0 and h.dtype == BF16
    nj = f // tn
    xp_spec, xn_spec = _halo_specs(tm, k, m, halo=HALO_BF16)
    cb = conv_b.reshape(1, 2 * f)
    return pl.pallas_call(
        _ffn_up_kernel,
        out_shape=jax.ShapeDtypeStruct((m, f), BF16),
        grid=(nj, m // tm),
        in_specs=[pl.BlockSpec((tm, k), lambda j, i: (i, 0)), xp_spec, xn_spec,
                  pl.BlockSpec((k, tn), lambda j, i: (0, j)),
                  pl.BlockSpec((k, tn), lambda j, i: (0, j + nj)),
                  pl.BlockSpec((3, tn), lambda j, i: (0, j)),
                  pl.BlockSpec((3, tn), lambda j, i: (0, j + nj)),
                  pl.BlockSpec((1, tn), lambda j, i: (0, j)),
                  pl.BlockSpec((1, tn), lambda j, i: (0, j + nj))],
        out_specs=pl.BlockSpec((tm, tn), lambda j, i: (i, j)),
        compiler_params=_cparams("arbitrary", "arbitrary"),
        name="ffn_up_conv_gate",
    )(h, h, h, w_up, w_up, conv_w, conv_w, cb, cb)


def _ffn_down_kernel(a_ref, w_ref, h_ref, g_ref, b_ref, out_ref, outb_ref, *, tiles_per_seq, padf):
    tm = h_ref.shape[0]
    acc = jnp.dot(a_ref[...], w_ref[...], preferred_element_type=F32)
    y = _ln_rows(ALPHA * h_ref[...] + acc, g_ref[...], b_ref[...])
    valid = _valid_rows(pl.program_id(0), tiles_per_seq, tm, padf)
    y = jnp.where(valid, y, 0.0)
    out_ref[...] = y
    outb_ref[...] = y.astype(BF16)


def ffn_down_ln(a, w_down, h, g, b, lp, padf, tm):
    m, d = h.shape
    f = a.shape[1]
    assert m % tm == 0 and lp % tm == 0
    ospec = pl.BlockSpec((tm, d), lambda i: (i, 0))
    return pl.pallas_call(
        functools.partial(_ffn_down_kernel, tiles_per_seq=lp // tm, padf=padf),
        out_shape=(jax.ShapeDtypeStruct((m, d), F32), jax.ShapeDtypeStruct((m, d), BF16)),
        grid=(m // tm,),
        in_specs=[pl.BlockSpec((tm, f), lambda i: (i, 0)),
                  pl.BlockSpec((f, d), lambda i: (0, 0), pipeline_mode=pl.Buffered(1)),
                  pl.BlockSpec((tm, d), lambda i: (i, 0)),
                  pl.BlockSpec((1, d), lambda i: (0, 0)),
                  pl.BlockSpec((1, d), lambda i: (0, 0))],
        out_specs=(ospec, ospec),
        compiler_params=_cparams("arbitrary"),
        name="ffn_down_residual_ln",
    )(a, w_down, h, g.reshape(1, d), b.reshape(1, d))


def rope_tables(lp, padf):
    half = ROT_DIM // 2
    pos = (jnp.arange(lp) - padf).astype(F32)
    inv = 1.0 / (ROPE_THETA ** (jnp.arange(half, dtype=F32) / half))
    ang = pos[:, None] * inv[None]
    cos, sin = jnp.cos(ang), jnp.sin(ang)
    ones = jnp.ones((lp, DA_HD - ROT_DIM), F32)
    zeros = jnp.zeros((lp, DA_HD - ROT_DIM), F32)
    zh = jnp.zeros((lp, half), F32)
    c = jnp.concatenate([cos, cos, ones], axis=1)
    sa = jnp.concatenate([-sin, zh, zeros], axis=1)
    sb = jnp.concatenate([zh, sin, zeros], axis=1)
    return tuple(jnp.concatenate([t, t], axis=1) for t in (c, sa, sb))


def _attn_prep_kernel(p_ref, c_ref, sa_ref, sb_ref, qlo_ref, qhi_ref, k_ref, vt_ref):
    w = GROUP_W
    reps = w // c_ref.shape[1]
    c = jnp.concatenate([c_ref[...]] * reps, axis=1)
    sa = jnp.concatenate([sa_ref[...]] * reps, axis=1)
    sb = jnp.concatenate([sb_ref[...]] * reps, axis=1)
    half = ROT_DIM // 2

    def rot(x):
        return x * c + pltpu.roll(x, w - half, 1) * sa + pltpu.roll(x, half, 1) * sb

    q = rot(p_ref[:, 0:w]) * (DA_HD ** -0.5)
    lane = lax.broadcasted_iota(jnp.int32, (1, w), 1) % (2 * DA_HD)
    qlo_ref[0] = jnp.where(lane < DA_HD, q, 0.0).T.astype(BF16)
    qhi_ref[0] = jnp.where(lane >= DA_HD, q, 0.0).T.astype(BF16)
    k_ref[...] = rot(p_ref[:, w:2 * w]).astype(BF16)
    vt_ref[0] = p_ref[:, 2 * w:3 * w].T.astype(BF16)


def attn_prep(proj, tables, nb, lp, tm):
    m = proj.shape[0]
    w = GROUP_W
    tps = lp // tm
    tspec = pl.BlockSpec((tm, 2 * DA_HD), lambda i: (i % tps, 0))
    tr_shape = jax.ShapeDtypeStruct((nb, w, lp), BF16)
    tr_spec = pl.BlockSpec((1, w, tm), lambda i: (i // tps, 0, i % tps))
    return pl.pallas_call(
        _attn_prep_kernel,
        out_shape=(tr_shape, tr_shape, jax.ShapeDtypeStruct((m, w), BF16), tr_shape),
        grid=(m // tm,),
        in_specs=[pl.BlockSpec((tm, 3 * w), lambda i: (i, 0)), tspec, tspec, tspec],
        out_specs=(tr_spec, tr_spec, pl.BlockSpec((tm, w), lambda i: (i, 0)), tr_spec),
        compiler_params=_cparams("arbitrary"),
        name="attn_prep_rotary",
    )(proj, *tables)


ATTN_QC = 256
ATTN_GROUP = 5
ATTN_ONES = 8


def _attn_kernel(lam_ref, g_ref, qlo_ref, qhi_ref, k_ref, vt_ref, o_ref, qs_ref, acc_ref,
                 *, tkb, padf, lam_init, unroll):
    tq = qlo_ref.shape[2]
    nq2 = 2 * tq
    nkb = k_ref.shape[0] // tkb
    qs_ref[:, :tq] = qlo_ref[0]
    qs_ref[:, tq:] = qhi_ref[0]
    row_ok = lax.broadcasted_iota(jnp.int32, (tkb, 1), 0) >= padf
    ones = jnp.ones((ATTN_ONES, tkb), BF16)

    def scores(kb, c0, masked):
        start = kb * tkb
        if not isinstance(start, int):
            start = pl.multiple_of(start, tkb)
        s = jnp.dot(k_ref[pl.ds(start, tkb), :], qs_ref[:, c0:c0 + ATTN_QC], preferred_element_type=F32)
        if masked:
            s = jnp.where(row_ok, s, -1e30)
        return s, start

    def colmax(s):
        return jnp.max(jnp.max(s.reshape(tkb // 8, 8, ATTN_QC), axis=0), axis=0, keepdims=True)

    for c0 in range(0, nq2, ATTN_GROUP * ATTN_QC):
        cols = tuple(c0 + i * ATTN_QC for i in range(ATTN_GROUP))

        def consume(kb, s, m):
            start = kb * tkb
            if not isinstance(start, int):
                start = pl.multiple_of(start, tkb)
            vext = jnp.concatenate([vt_ref[0, :, pl.ds(start, tkb)], ones], axis=0)
            m_out = []
            for i in range(ATTN_GROUP):
                m_new = jnp.maximum(m[i], colmax(s[i]))
                p = jnp.exp(s[i] - m_new).astype(BF16)
                sl = slice(cols[i], cols[i] + ATTN_QC)
                acc_ref[:, sl] = acc_ref[:, sl] * jnp.exp(m[i] - m_new) + jnp.dot(vext, p, preferred_element_type=F32)
                m_out.append(m_new)
            return tuple(m_out)

        for c in cols:
            acc_ref[:, c:c + ATTN_QC] = jnp.zeros((HEAD_W + ATTN_ONES, ATTN_QC), F32)

        def body(kb, carry):
            s, m = carry
            s_next = tuple(scores(kb + 1, c, False)[0] for c in cols)
            return s_next, consume(kb, s, m)

        m0 = tuple(jnp.full((1, ATTN_QC), -1e30, F32) for _ in cols)
        s_last, m_last = lax.fori_loop(0, nkb - 1, body, (tuple(scores(0, c, True)[0] for c in cols), m0),
                                       unroll=unroll)
        consume(nkb - 1, s_last, m_last)

    lam = (jnp.exp(jnp.sum(lam_ref[0:1, :] * lam_ref[1:2, :], axis=-1, keepdims=True))
           - jnp.exp(jnp.sum(lam_ref[2:3, :] * lam_ref[3:4, :], axis=-1, keepdims=True)) + lam_init)
    o = acc_ref[0:HEAD_W, :] / acc_ref[HEAD_W:HEAD_W + 1, :]
    o = (o[:, :tq] - lam * o[:, tq:]).T
    o = o * lax.rsqrt(jnp.mean(o * o, axis=-1, keepdims=True) + RMS_EPS) * g_ref[...]
    o_ref[...] = (o * (1.0 - lam_init)).astype(o_ref.dtype)


def diff_attention(qlo_t, qhi_t, k, v_t, lam_vecs, norm_g, nb, lp, padf, layer, tq, tkb, unroll):
    m = k.shape[0]
    assert lp % tq == 0 and lp % tkb == 0 and padf < tkb and (2 * tq) % (ATTN_GROUP * ATTN_QC) == 0
    nq = lp // tq
    lam_init = 0.8 - 0.6 * math.exp(-0.3 * layer)
    qspec = pl.BlockSpec((1, HEAD_W, tq), lambda b, h, qi: (b, h, qi))
    return pl.pallas_call(
        functools.partial(_attn_kernel, tkb=tkb, padf=padf, lam_init=lam_init, unroll=unroll),
        out_shape=jax.ShapeDtypeStruct((m, GROUP_W), BF16),
        grid=(nb, N_HEADS, nq),
        in_specs=[pl.BlockSpec((4, DA_HD), lambda b, h, qi: (0, 0)),
                  pl.BlockSpec((1, HEAD_W), lambda b, h, qi: (0, 0)),
                  qspec, qspec,
                  pl.BlockSpec((lp, HEAD_W), lambda b, h, qi: (b, h)),
                  pl.BlockSpec((1, HEAD_W, lp), lambda b, h, qi: (b, h, 0))],
        out_specs=pl.BlockSpec((tq, HEAD_W), lambda b, h, qi: (b * nq + qi, h)),
        scratch_shapes=[pltpu.VMEM((HEAD_W, 2 * tq), BF16),
                        pltpu.VMEM((HEAD_W + ATTN_ONES, 2 * tq), F32)],
        compiler_params=_cparams("arbitrary", "arbitrary", "arbitrary"),
        name="diff_attention",
    )(lam_vecs, norm_g.reshape(1, HEAD_W), qlo_t, qhi_t, k, v_t)


N_LEVELS = 6


def _scan_constants():
    c = CHUNK
    idx = np.arange(c)
    cs = np.zeros((2, (N_LEVELS + 2) * c, c), np.float32)
    mask = np.zeros((2, N_LEVELS + 2, c, c), np.float32)
    for d in range(2):
        cum = (idx[None, :] <= idx[:, None]) if d == 0 else (idx[None, :] >= idx[:, None])
        cs[d, 0:c] = cum
        for l in range(N_LEVELS):
            s = (c // 2) >> l
            blk = idx // s
            ref = (blk | 1) * s - 1 if d == 0 else (blk | 1) * s
            cs[d, (l + 1) * c:(l + 2) * c] = cum[ref]
            if d == 0:
                mask[d, l] = ((blk[:, None] & 1) == 1) & (blk[None, :] == blk[:, None] - 1)
            else:
                mask[d, l] = ((blk[:, None] & 1) == 0) & (blk[None, :] == blk[:, None] + 1)
        cs[d, (N_LEVELS + 1) * c:] = 1.0
        mask[d, N_LEVELS] = cum
        mask[d, N_LEVELS + 1] = cum & (idx[None, :] != idx[:, None])
    return jnp.asarray(np.concatenate([cs] * 3, axis=2), BF16), jnp.asarray(mask)


def _split3(x):
    hi = x.astype(BF16)
    r = x - hi.astype(F32)
    mid = r.astype(BF16)
    return hi, mid, (r - mid.astype(F32)).astype(BF16)


def _split3_rows(x):
    return jnp.concatenate(_split3(x), axis=0)


def _split3_lanes(x):
    return jnp.concatenate(_split3(x), axis=1)


def _nt(a, b):
    return lax.dot_general(a.astype(BF16), b.astype(BF16), (((1,), (1,)), ((), ())),
                           preferred_element_type=F32)


def _tn(a, b):
    return lax.dot_general(a.astype(BF16), b.astype(BF16), (((0,), (0,)), ((), ())),
                           preferred_element_type=F32)


def _nn(a, b):
    return jnp.dot(a.astype(BF16), b.astype(BF16), preferred_element_type=F32)


def _hgrn_prep_kernel(p_ref, lb_ref, q_ref, v_ref, g_ref, sg_ref):
    w = GROUP_W
    q = p_ref[:, 0:w]
    q_ref[...] = (q * jax.nn.sigmoid(q) * (HEAD_W ** -0.5)).astype(BF16)
    for d in range(2):
        lb = lb_ref[d:d + 1, :]
        f = p_ref[:, (1 + d) * w:(2 + d) * w]
        g_ref[d] = jnp.log(lb + (1.0 - lb) * jax.nn.sigmoid(f))
    v_ref[...] = p_ref[:, 3 * w:4 * w].astype(BF16)
    gate = p_ref[:, 4 * w:5 * w]
    sg_ref[...] = (gate * jax.nn.sigmoid(gate)).astype(BF16)


def hgrn_prep(proj, col_block, lb, tm):
    m = proj.shape[0]
    w = GROUP_W
    ospec = pl.BlockSpec((tm, w), lambda i: (i, 0))
    return pl.pallas_call(
        _hgrn_prep_kernel,
        out_shape=(jax.ShapeDtypeStruct((m, w), BF16), jax.ShapeDtypeStruct((m, w), BF16),
                   jax.ShapeDtypeStruct((2, m, w), F32), jax.ShapeDtypeStruct((m, w), BF16)),
        grid=(m // tm,),
        in_specs=[pl.BlockSpec((tm, 5 * w), lambda i: (i, col_block)),
                  pl.BlockSpec((2, w), lambda i: (0, 0))],
        out_specs=(ospec, ospec, pl.BlockSpec((2, tm, w), lambda i: (0, i, 0)), ospec),
        compiler_params=_cparams("arbitrary"),
        name="hgrn_prep",
    )(proj, lb)


def _head(x, h):
    return x[:, h * HEAD_W:(h + 1) * HEAD_W]


def _chunk_rows(ci, nch, d):
    return pl.multiple_of((ci + d * (nch - 1 - 2 * ci)) * CHUNK, CHUNK)


def _hgrn_scan_kernel(cs_ref, mask_ref, q_ref, v_ref, g_ref, o_ref, s_ref, *, nch):
    c = CHUNK
    d = pl.program_id(1)
    heads = range(N_HEADS)
    chunks = range(nch)

    @pl.when(pl.program_id(2) == 0)
    def _():
        s_ref[...] = jnp.zeros(s_ref.shape, F32)

    eye = mask_ref[0, N_LEVELS] - mask_ref[0, N_LEVELS + 1]
    rows = [_chunk_rows(ci, nch, d) for ci in chunks]
    g = [g_ref[0, pl.ds(r0, c), :] for r0 in rows]
    q = [q_ref[pl.ds(r0, c), :].astype(F32) for r0 in rows]
    v = [v_ref[pl.ds(r0, c), :] for r0 in rows]
    big = [jnp.dot(cs_ref[0], _split3_rows(gi), preferred_element_type=F32) for gi in g]
    b = [x[0:c] for x in big]
    tot = [x[(N_LEVELS + 1) * c:(N_LEVELS + 2) * c] for x in big]
    k = [1.0 - jnp.exp(gi) for gi in g]
    a = [[eye * jnp.sum(_head(q[ci] * k[ci], h), axis=-1, keepdims=True) for h in heads] for ci in chunks]
    q16 = [q_ref[pl.ds(r0, c), :] for r0 in rows]
    k16 = [ki.astype(BF16) for ki in k]
    for l in range(N_LEVELS):
        e = [jnp.exp(-jnp.abs(b[ci] - big[ci][(l + 1) * c:(l + 2) * c])).astype(BF16) for ci in chunks]
        qe = [q16[ci] * e[ci] for ci in chunks]
        ke = [k16[ci] * e[ci] for ci in chunks]
        lm = mask_ref[0, l]
        a = [[a[ci][h] + lm * _nt(_head(qe[ci], h), _head(ke[ci], h)) for h in heads] for ci in chunks]
    o_intra = [[_nn(a[ci][h], _head(v[ci], h)) for h in heads] for ci in chunks]
    qb = [(q[ci] * jnp.exp(b[ci])).astype(BF16) for ci in chunks]
    kd = [(k[ci] * jnp.exp(tot[ci] - b[ci])).astype(BF16) for ci in chunks]
    dec = [jnp.exp(tot[ci][0:1, :]) for ci in chunks]
    s = [s_ref[h] for h in heads]
    for ci in chunks:
        for h in heads:
            o = o_intra[ci][h] + _nt(_head(qb[ci], h), s[h])
            o_ref[0, pl.ds(rows[ci], c), h * HEAD_W:(h + 1) * HEAD_W] = o
        s = [s[h] * _head(dec[ci], h) + _tn(_head(v[ci], h), _head(kd[ci], h)) for h in heads]
    for h in heads:
        s_ref[h] = s[h]


def _scan_row_block(nblk):
    def row(b, d, t):
        return b * nblk + t + d * (nblk - 1 - 2 * t)
    return row


def hgrn_scan(q, v, g, consts, nb, lp, tb):
    m, w = q.shape
    cs, mask = consts
    row = _scan_row_block(lp // tb)
    return pl.pallas_call(
        functools.partial(_hgrn_scan_kernel, nch=tb // CHUNK),
        out_shape=jax.ShapeDtypeStruct((2, m, w), F32),
        grid=(nb, 2, lp // tb),
        in_specs=[pl.BlockSpec((1,) + cs.shape[1:], lambda b, d, t: (d, 0, 0)),
                  pl.BlockSpec((1,) + mask.shape[1:], lambda b, d, t: (d, 0, 0, 0)),
                  pl.BlockSpec((tb, w), lambda b, d, t: (row(b, d, t), 0)),
                  pl.BlockSpec((tb, w), lambda b, d, t: (row(b, d, t), 0)),
                  pl.BlockSpec((1, tb, w), lambda b, d, t: (d, row(b, d, t), 0))],
        out_specs=pl.BlockSpec((1, tb, w), lambda b, d, t: (d, row(b, d, t), 0)),
        scratch_shapes=[pltpu.VMEM((N_HEADS, HEAD_W, HEAD_W), F32)],
        compiler_params=_cparams("arbitrary", "arbitrary", "arbitrary"),
        name="hgrn_scan",
    )(cs, mask, q, v, g)


def _gated_norm_kernel(o_ref, sg_ref, g_ref, out_ref):
    o = o_ref[0] + o_ref[1]
    gain = g_ref[...]
    for h in range(N_HEADS):
        sl = slice(h * HEAD_W, (h + 1) * HEAD_W)
        oh = o[:, sl]
        y = oh * lax.rsqrt(jnp.mean(oh * oh, axis=-1, keepdims=True) + RMS_EPS) * gain
        out_ref[:, sl] = (y * sg_ref[:, sl].astype(F32)).astype(out_ref.dtype)


def gated_norm(o2, sgate, norm_g, tm):
    m = sgate.shape[0]
    w = GROUP_W
    return pl.pallas_call(
        _gated_norm_kernel,
        out_shape=jax.ShapeDtypeStruct((m, w), BF16),
        grid=(m // tm,),
        in_specs=[pl.BlockSpec((2, tm, w), lambda i: (0, i, 0)),
                  pl.BlockSpec((tm, w), lambda i: (i, 0)),
                  pl.BlockSpec((1, HEAD_W), lambda i: (0, 0))],
        out_specs=pl.BlockSpec((tm, w), lambda i: (i, 0)),
        compiler_params=_cparams("arbitrary"),
        name="gated_rms_norm",
    )(o2, sgate, norm_g.reshape(1, HEAD_W))


GATE_LANES = 128


def _softplus(x):
    return jnp.maximum(x, 0.0) + jnp.log(1.0 + jnp.exp(-jnp.abs(x)))


def _gdn_prep_kernel(p_ref, pp_ref, pn_ref, cw_ref, acoef_ref, dtb_ref,
                     q_ref, k_ref, v_ref, sz_ref, gates_ref, *, tiles_per_seq, padf):
    w = GROUP_W
    i = pl.program_id(0)
    tm = p_ref.shape[0]
    nmask = jnp.where(i == pl.num_programs(0) - 1, 0.0, 1.0)
    x = _conv3_rows(p_ref[:, 0:3 * w], pp_ref[HALO - 1:HALO, :], pn_ref[0:1, :] * nmask, cw_ref[...])
    x = x * jax.nn.sigmoid(x)
    valid = _valid_rows(i, tiles_per_seq, tm, padf)
    for h in range(N_HEADS):
        sl = slice(h * HEAD_W, (h + 1) * HEAD_W)
        qh = x[:, sl]
        q_ref[:, sl] = (qh * lax.rsqrt(jnp.sum(qh * qh, axis=-1, keepdims=True) + 1e-6)
                        * (HEAD_W ** -0.5)).astype(BF16)
        kh = x[:, w + h * HEAD_W:w + (h + 1) * HEAD_W]
        k_ref[:, sl] = (kh * lax.rsqrt(jnp.sum(kh * kh, axis=-1, keepdims=True) + 1e-6)).astype(BF16)
    v_ref[...] = jnp.where(valid, x[:, 2 * w:3 * w], 0.0).astype(BF16)
    z = p_ref[:, 3 * w:4 * w]
    sz_ref[...] = (z * jax.nn.sigmoid(z)).astype(BF16)
    gt = p_ref[:, 4 * w:4 * w + GATE_LANES]
    lane = lax.broadcasted_iota(jnp.int32, (1, GATE_LANES), 1)
    decay = -acoef_ref[...] * _softplus(gt + dtb_ref[...])
    gates_ref[...] = jnp.where(lane < 2 * N_HEADS, jax.nn.sigmoid(gt), decay)


def gdn_prep(pd, conv_w, a_log, dt_bias, lp, padf, tm):
    m = pd.shape[0]
    w = GROUP_W
    zeros8 = jnp.zeros((2 * N_HEADS,), F32)
    padl = jnp.zeros((GATE_LANES - 4 * N_HEADS,), F32)
    acoef = jnp.concatenate([zeros8, jnp.exp(a_log).reshape(-1), padl]).reshape(1, GATE_LANES)
    dtb = jnp.concatenate([zeros8, dt_bias.reshape(-1), padl]).reshape(1, GATE_LANES)
    pspec, nspec = _halo_specs(tm, 3 * w, m)
    ospec = pl.BlockSpec((tm, w), lambda i: (i, 0))
    return pl.pallas_call(
        functools.partial(_gdn_prep_kernel, tiles_per_seq=lp // tm, padf=padf),
        out_shape=tuple(jax.ShapeDtypeStruct((m, w), BF16) for _ in range(4))
        + (jax.ShapeDtypeStruct((m, GATE_LANES), F32),),
        grid=(m // tm,),
        in_specs=[pl.BlockSpec((tm, pd.shape[1]), lambda i: (i, 0)), pspec, nspec,
                  pl.BlockSpec((3, 3 * w), lambda i: (0, 0)),
                  pl.BlockSpec((1, GATE_LANES), lambda i: (0, 0)),
                  pl.BlockSpec((1, GATE_LANES), lambda i: (0, 0))],
        out_specs=(ospec, ospec, ospec, ospec, pl.BlockSpec((tm, GATE_LANES), lambda i: (i, 0))),
        compiler_params=_cparams("arbitrary"),
        name="gdn_prep",
    )(pd, pd, pd, conv_w, acoef, dtb)


def _gate_selectors():
    sel = np.zeros((2, GATE_LANES, 2 * GROUP_W), np.float32)
    for d in range(2):
        for h in range(N_HEADS):
            sel[d, N_HEADS * d + h, h * HEAD_W:(h + 1) * HEAD_W] = 1.0
            sel[d, 2 * N_HEADS + N_HEADS * d + h, GROUP_W + h * HEAD_W:GROUP_W + (h + 1) * HEAD_W] = 1.0
    return jnp.asarray(np.concatenate([sel] * 3, axis=1), BF16)


def _gdn_scan_kernel(cs_ref, mask_ref, sel_ref, q_ref, k_ref, v_ref, gt_ref, o_ref, s_ref, *, nch):
    c = CHUNK
    w = GROUP_W
    d = pl.program_id(1)
    heads = range(N_HEADS)
    chunks = range(nch)
    items = [(ci, h) for ci in chunks for h in heads]

    @pl.when(pl.program_id(2) == 0)
    def _():
        s_ref[...] = jnp.zeros(s_ref.shape, F32)

    incl = mask_ref[0, N_LEVELS]
    strict = mask_ref[0, N_LEVELS + 1]
    eye = incl - strict
    cum_tot = jnp.concatenate([cs_ref[0, 0:c, :], cs_ref[0, (N_LEVELS + 1) * c:(N_LEVELS + 2) * c, :]], axis=0)
    ones3 = cs_ref[0, (N_LEVELS + 1) * c:(N_LEVELS + 2) * c, :]
    rows = [_chunk_rows(ci, nch, d) for ci in chunks]
    q = [q_ref[pl.ds(r0, c), :].astype(F32) for r0 in rows]
    k = [k_ref[pl.ds(r0, c), :].astype(F32) for r0 in rows]
    v = [v_ref[pl.ds(r0, c), :].astype(F32) for r0 in rows]
    bg = [jnp.dot(_split3_lanes(gt_ref[pl.ds(r0, c), :]), sel_ref[0], preferred_element_type=F32)
          for r0 in rows]
    beta = [x[:, 0:w] for x in bg]
    cb = [jnp.dot(cum_tot, _split3_rows(x[:, w:2 * w]), preferred_element_type=F32) for x in bg]
    bb = [x[0:c] for x in cb]
    tot = [x[c:2 * c] for x in cb]
    bc = {(ci, h): bb[ci][:, h * HEAD_W:h * HEAD_W + c] for ci, h in items}
    br = {it: jnp.dot(ones3, _split3_rows(eye * bc[it]), preferred_element_type=F32) for it in items}
    dec = {it: incl * jnp.exp(jnp.minimum(bc[it] - br[it], 0.0)) for it in items}
    kb = [k[ci] * beta[ci] for ci in chunks]
    n = {(ci, h): strict * _nt(_head(kb[ci], h), _head(k[ci], h)) * dec[(ci, h)] for ci, h in items}
    t = {it: eye - n[it] for it in items}
    pw = {it: _nn(n[it], n[it]) for it in items}
    for _ in range(N_LEVELS - 2):
        t = {it: t[it] + _nn(t[it], pw[it]) for it in items}
        pw = {it: _nn(pw[it], pw[it]) for it in items}
    t = {it: t[it] + _nn(t[it], pw[it]) for it in items}
    eb = [jnp.exp(x) for x in bb]
    rhs_u = [v[ci] * beta[ci] for ci in chunks]
    rhs_w = [kb[ci] * eb[ci] for ci in chunks]
    uw = {(ci, h): _nn(t[(ci, h)], jnp.concatenate([_head(rhs_u[ci], h), _head(rhs_w[ci], h)], axis=1))
          for ci, h in items}
    qk = {(ci, h): _nt(_head(q[ci], h), _head(k[ci], h)) * dec[(ci, h)] for ci, h in items}
    qe = [(q[ci] * eb[ci]).astype(BF16) for ci in chunks]
    kd = [(k[ci] * jnp.exp(tot[ci] - bb[ci])).astype(BF16) for ci in chunks]
    sdec = [jnp.exp(tot[ci][0:1, :]) for ci in chunks]
    s = [s_ref[h] for h in heads]
    for ci in chunks:
        wq = {h: _nn(jnp.concatenate([uw[(ci, h)][:, HEAD_W:].astype(BF16), _head(qe[ci], h)], axis=0), s[h])
              for h in heads}
        vnew = {h: uw[(ci, h)][:, 0:HEAD_W] - wq[h][0:c] for h in heads}
        for h in heads:
            o_ref[0, pl.ds(rows[ci], c), h * HEAD_W:(h + 1) * HEAD_W] = wq[h][c:2 * c] + _nn(qk[(ci, h)], vnew[h])
        s = [s[h] * _head(sdec[ci], h) + _tn(_head(kd[ci], h), vnew[h]) for h in heads]
    for h in heads:
        s_ref[h] = s[h]


def gdn_scan(q, k, v, gates, consts, sel, nb, lp, tb):
    m, w = q.shape
    cs, mask = consts
    row = _scan_row_block(lp // tb)
    rspec = pl.BlockSpec((tb, w), lambda b, d, t: (row(b, d, t), 0))
    return pl.pallas_call(
        functools.partial(_gdn_scan_kernel, nch=tb // CHUNK),
        out_shape=jax.ShapeDtypeStruct((2, m, w), F32),
        grid=(nb, 2, lp // tb),
        in_specs=[pl.BlockSpec((1,) + cs.shape[1:], lambda b, d, t: (d, 0, 0)),
                  pl.BlockSpec((1,) + mask.shape[1:], lambda b, d, t: (d, 0, 0, 0)),
                  pl.BlockSpec((1,) + sel.shape[1:], lambda b, d, t: (d, 0, 0)),
                  rspec, rspec, rspec,
                  pl.BlockSpec((tb, GATE_LANES), lambda b, d, t: (row(b, d, t), 0))],
        out_specs=pl.BlockSpec((1, tb, w), lambda b, d, t: (d, row(b, d, t), 0)),
        scratch_shapes=[pltpu.VMEM((N_HEADS, HEAD_W, HEAD_W), F32)],
        compiler_params=_cparams("arbitrary", "arbitrary", "arbitrary"),
        name="gdn_scan",
    )(cs, mask, sel, q, k, v, gates)


FFT_N2 = 256
FFT_TS = 16


def _fft_n1(lp, l_real):
    need = -(-(2 * l_real - 1) // FFT_N2)
    return max(-(-need // 8) * 8, -(-(lp // FFT_N2) // 8) * 8)


def _outer_dft(n1, k_rows, inverse, n_total):
    kp = -(-k_rows // 16) * 16
    a = np.arange(n1)[:, None] * np.arange(kp)[None, :]
    ang = 2.0 * np.pi * (a % n1) / n1
    c, s = np.cos(ang), np.sin(ang)
    live = (np.arange(kp) < k_rows)[None, :]
    c, s = c * live, s * live
    if not inverse:
        return jnp.asarray(np.concatenate([c, -s], axis=0), BF16)
    return jnp.asarray(np.concatenate([c.T, -s.T], axis=1) / n_total, F32)


def inner_dft(n1):
    n2 = FFT_N2
    n = n1 * n2
    k = jnp.arange(n1, dtype=jnp.int32)[:, None, None] + n1 * jnp.arange(n2, dtype=jnp.int32)[None, :, None]
    r = (k * jnp.arange(n2, dtype=jnp.int32)[None, None, :]) % n
    ang = r.astype(F32) * (2.0 * math.pi / n)
    gr, gi = jnp.cos(ang), -jnp.sin(ang)
    gg = jnp.concatenate([jnp.concatenate([gr, -gi], axis=2), jnp.concatenate([gi, gr], axis=2)], axis=1)
    return gg.astype(BF16), jnp.swapaxes(gg, 1, 2).astype(BF16)


def _hyena_prep_kernel(p_ref, pp_ref, pn_ref, cw_ref, cb_ref, x0_ref, z_ref, *, tiles_per_seq, padf):
    w = GROUP_W
    i = pl.program_id(0)
    tm = p_ref.shape[0]
    nmask = jnp.where(i == pl.num_programs(0) - 1, 0.0, 1.0)
    u = _conv3_rows(p_ref[...], pp_ref[HALO - 1:HALO, :], pn_ref[0:1, :] * nmask, cw_ref[...]) + cb_ref[...]
    valid = _valid_rows(i, tiles_per_seq, tm, padf)
    x0_ref[...] = u[:, 0:w]
    z_ref[...] = jnp.where(valid, u[:, w:2 * w] * u[:, 2 * w:3 * w], 0.0)


def hyena_prep(pb, conv_w, conv_b, lp, padf, tm):
    m, w3 = pb.shape
    w = GROUP_W
    pspec, nspec = _halo_specs(tm, w3, m)
    ospec = pl.BlockSpec((tm, w), lambda i: (i, 0))
    return pl.pallas_call(
        functools.partial(_hyena_prep_kernel, tiles_per_seq=lp // tm, padf=padf),
        out_shape=(jax.ShapeDtypeStruct((m, w), F32), jax.ShapeDtypeStruct((m, w), F32)),
        grid=(m // tm,),
        in_specs=[pl.BlockSpec((tm, w3), lambda i: (i, 0)), pspec, nspec,
                  pl.BlockSpec((3, w3), lambda i: (0, 0)), pl.BlockSpec((1, w3), lambda i: (0, 0))],
        out_specs=(ospec, ospec),
        compiler_params=_cparams("arbitrary"),
        name="hyena_prep",
    )(pb, pb, pb, conv_w, conv_b.reshape(1, w3))


def _hyena_filter_kernel(w1t_ref, w1c_ref, w1s_ref, b1_ref, f1_ref, w2_ref, b2_ref, f2_ref, w3_ref, dec_ref,
                         filt_ref, asum_ref, *, l_real):
    i = pl.program_id(0)
    tm = filt_ref.shape[0]
    w = GROUP_W
    row = i * tm + lax.broadcasted_iota(jnp.int32, (tm, 1), 0)
    rf = row.astype(F32)
    t = rf * (1.0 / (l_real - 1))
    band = lax.broadcasted_iota(jnp.int32, (1, HY_BANDS), 1).astype(F32)
    bands = 1e-4 + band * ((HY_BANDS - 1 - 1e-4) / (HY_BANDS - 1))
    ang = ((2.0 * math.pi / l_real) * rf) * bands

    def hdot(a, b):
        return jnp.dot(a, b, precision=HIGHEST, preferred_element_type=F32)

    pre = t * w1t_ref[...] + hdot(jnp.cos(ang), w1c_ref[...]) - hdot(jnp.sin(ang), w1s_ref[...])
    hid = jnp.sin(f1_ref[...] * (pre + b1_ref[...]))
    hid = jnp.sin(f2_ref[...] * (hdot(hid, w2_ref[...]) + b2_ref[...]))
    filt = hdot(hid, w3_ref[...])
    window = jnp.exp(-t * jnp.abs(dec_ref[...])) + HY_SHIFT
    filt = jnp.where(row < l_real, filt * window, 0.0)

    @pl.when(i == 0)
    def _():
        asum_ref[...] = jnp.zeros(asum_ref.shape, F32)

    asum_ref[...] += jnp.sum(jnp.abs(filt), axis=0, keepdims=True)
    lane = lax.broadcasted_iota(jnp.int32, (1, 2 * w), 1)
    filt_ref[...] = jnp.where((row == 0) & (lane >= w), 0.0, filt)


def hyena_filter(w1, b1, f1, w2, b2, f2, w3, decay, lp, l_real, tm):
    w = GROUP_W
    nf = w1.shape[1]
    small = lambda a: pl.BlockSpec(a.shape, lambda i: (0,) * a.ndim)
    args = (w1[0:1], w1[1:1 + HY_BANDS], w1[1 + HY_BANDS:], b1.reshape(1, nf), f1.reshape(1, nf),
            w2, b2.reshape(1, nf), f2.reshape(1, nf), w3, decay.reshape(1, 2 * w))
    return pl.pallas_call(
        functools.partial(_hyena_filter_kernel, l_real=l_real),
        out_shape=(jax.ShapeDtypeStruct((lp, 2 * w), F32), jax.ShapeDtypeStruct((1, 2 * w), F32)),
        grid=(lp // tm,),
        in_specs=[small(a) for a in args],
        out_specs=(pl.BlockSpec((tm, 2 * w), lambda i: (i, 0)), pl.BlockSpec((1, 2 * w), lambda i: (0, 0))),
        compiler_params=_cparams("arbitrary"),
        name="hyena_filter",
    )(*args)


def _fft_outer_kernel(f_ref, x_ref, o_ref, pad_ref):
    k = x_ref.shape[1]
    pad_ref[...] = jnp.zeros(pad_ref.shape, F32)
    for s in range(x_ref.shape[2]):
        pad_ref[0:k, :] = x_ref[0, :, s, :]
        o_ref[0, :, s, :] = jnp.dot(f_ref[...], pad_ref[...].astype(BF16), preferred_element_type=F32)


def fft_outer(x, fmat):
    b, k, n2, c = x.shape
    rows, kp = fmat.shape
    ts = FFT_TS
    return pl.pallas_call(
        _fft_outer_kernel,
        out_shape=jax.ShapeDtypeStruct((b, rows, n2, c), F32),
        grid=(b, n2 // ts),
        in_specs=[pl.BlockSpec((rows, kp), lambda bi, j: (0, 0)),
                  pl.BlockSpec((1, k, ts, c), lambda bi, j: (bi, 0, j, 0))],
        out_specs=pl.BlockSpec((1, rows, ts, c), lambda bi, j: (bi, 0, j, 0)),
        scratch_shapes=[pltpu.VMEM((kp, c), F32)],
        compiler_params=_cparams("arbitrary", "arbitrary"),
        name="fft_outer",
    )(fmat, x)


def _filter_spectrum_kernel(gg_ref, a_ref, asum_ref, o_ref):
    w = GROUP_W
    n2 = FFT_N2
    a = jnp.concatenate([a_ref[0, 0], a_ref[1, 0]], axis=0).astype(BF16)
    x = jnp.dot(gg_ref[0], a, preferred_element_type=F32)
    s = 1.0 / (asum_ref[:, 0:w] + asum_ref[:, w:2 * w] + 1e-6)
    o_ref[0, 0] = (x[0:n2, 0:w] + x[0:n2, w:2 * w]) * s
    o_ref[0, 1] = (x[n2:2 * n2, 0:w] - x[n2:2 * n2, w:2 * w]) * s


def filter_spectrum(a, gg, asum):
    _, n1, n2, w2 = a.shape
    w = w2 // 2
    return pl.pallas_call(
        _filter_spectrum_kernel,
        out_shape=jax.ShapeDtypeStruct((n1, 2, n2, w), F32),
        grid=(n1,),
        in_specs=[pl.BlockSpec((1, 2 * n2, 2 * n2), lambda k: (k, 0, 0)),
                  pl.BlockSpec((2, 1, n2, w2), lambda k: (0, k, 0, 0)),
                  pl.BlockSpec((1, w2), lambda k: (0, 0))],
        out_specs=pl.BlockSpec((1, 2, n2, w), lambda k: (k, 0, 0, 0)),
        compiler_params=_cparams("arbitrary"),
        name="hyena_filter_spectrum",
    )(gg, a, asum)


def _fft_mid_kernel(gg_ref, ggt_ref, kf_ref, a_ref, o_ref):
    n2 = FFT_N2
    a = jnp.concatenate([a_ref[0, 0, 0], a_ref[0, 1, 0]], axis=0).astype(BF16)
    x = jnp.dot(gg_ref[0], a, preferred_element_type=F32)
    xr, xi = x[0:n2], x[n2:2 * n2]
    kr, ki = kf_ref[0, 0], kf_ref[0, 1]
    y = jnp.concatenate([xr * kr - xi * ki, xr * ki + xi * kr], axis=0).astype(BF16)
    bm = jnp.dot(ggt_ref[0], y, preferred_element_type=F32)
    o_ref[0, 0, 0] = bm[0:n2]
    o_ref[0, 1, 0] = bm[n2:2 * n2]


def fft_mid(a, gg, ggt, kf):
    b, _, n1, n2, w = a.shape
    aspec = pl.BlockSpec((1, 2, 1, n2, w), lambda k, bi: (bi, 0, k, 0, 0))
    gspec = pl.BlockSpec((1, 2 * n2, 2 * n2), lambda k, bi: (k, 0, 0))
    return pl.pallas_call(
        _fft_mid_kernel,
        out_shape=jax.ShapeDtypeStruct(a.shape, F32),
        grid=(n1, b),
        in_specs=[gspec, gspec, pl.BlockSpec((1, 2, n2, w), lambda k, bi: (k, 0, 0, 0)), aspec],
        out_specs=aspec,
        compiler_params=_cparams("arbitrary", "arbitrary"),
        name="fft_inner_conv",
    )(gg, ggt, kf, a)


def _fft_final_kernel(f_ref, b_ref, x0_ref, z_ref, d_ref, o_ref):
    k = o_ref.shape[1]
    for s in range(o_ref.shape[2]):
        y = jnp.dot(f_ref[...], b_ref[0, :, s, :], precision=HIGHEST, preferred_element_type=F32)
        o_ref[0, :, s, :] = x0_ref[0, :, s, :] * (y[0:k] + z_ref[0, :, s, :] * d_ref[...])


def fft_final(bm, finv, x0, z, d):
    b, rows, n2, c = bm.shape
    k = x0.shape[1]
    kp = finv.shape[0]
    ts = FFT_TS
    xspec = pl.BlockSpec((1, k, ts, c), lambda bi, j: (bi, 0, j, 0))
    return pl.pallas_call(
        _fft_final_kernel,
        out_shape=jax.ShapeDtypeStruct((b, k, n2, c), F32),
        grid=(b, n2 // ts),
        in_specs=[pl.BlockSpec((kp, rows), lambda bi, j: (0, 0)),
                  pl.BlockSpec((1, rows, ts, c), lambda bi, j: (bi, 0, j, 0)),
                  xspec, xspec, pl.BlockSpec((1, c), lambda bi, j: (0, 0))],
        out_specs=xspec,
        compiler_params=_cparams("arbitrary", "arbitrary"),
        name="fft_outer_inverse_gate",
    )(finv, bm, x0, z, d)


def hyena_mixer(pb, hp, nb, lp, padf, l_real, tm):
    w = GROUP_W
    n2 = FFT_N2
    k1 = lp // n2
    n1 = _fft_n1(lp, l_real)
    n_total = n1 * n2
    ffwd = _outer_dft(n1, k1, False, n_total)
    finv = _outer_dft(n1, k1, True, n_total)
    gg, ggt = hp['gg'], hp['ggt']
    filt, asum = hyena_filter(hp['w1'], hp['b1'], hp['f1'], hp['w2'], hp['b2'], hp['f2'], hp['w3'],
                              hp['decay'], lp, l_real, tm)
    fa = fft_outer(filt.reshape(1, k1, n2, 2 * w), ffwd)
    kf = filter_spectrum(fa.reshape(2, n1, n2, 2 * w), gg, asum)
    x0, z = hyena_prep(pb, hp['conv_w'], hp['conv_b'], lp, padf, tm)
    z4 = z.reshape(nb, k1, n2, w)
    a = fft_outer(z4, ffwd)
    bm = fft_mid(a.reshape(nb, 2, n1, n2, w), gg, ggt, kf)
    out = fft_final(bm.reshape(nb, 2 * n1, n2, w), finv, x0.reshape(nb, k1, n2, w), z4, hp['d'].reshape(1, w))
    return out.reshape(nb * lp, w)


TM = 640
ATTN_TQ = 1280
ATTN_TKB = 1280
ATTN_UNROLL = 4
SCAN_TB = 640
FFN_TN = 512
FFN_DOWN_TM = 320


def kernel(x_prompt, x_sample, meta, emb_ln_g, emb_ln_b, w_in, lam_q1, lam_k1, lam_q2, lam_k2, attn_norm_g, hy_conv_w, hy_conv_b, hy_w1, hy_b1, hy_f1, hy_w2, hy_b2, hy_f2, hy_w3, hy_decay, hy_d, hg_lb, hg_norm_g, gdn_conv_w, gdn_a_log, gdn_dt_bias, gdn_norm_g, w_out, ln1_g, ln1_b, w_up, ffn_conv_w, ffn_conv_b, w_down, ln2_g, ln2_b):
    n_prompt = x_prompt.shape[0]
    x = jnp.concatenate([x_prompt, x_sample], axis=0)
    nb, seq, d = x.shape
    padf = ROW_ALIGN - N_META
    lp = seq + ROW_ALIGN
    l_real = seq + N_META
    m = nb * lp
    w = GROUP_W

    h, hb = (a.reshape(m, d) for a in embed(x, meta, emb_ln_g, emb_ln_b, padf))
    tables = rope_tables(lp, padf)
    consts = _scan_constants()
    sel = _gate_selectors()
    gg, ggt = inner_dft(_fft_n1(lp, l_real))
    sm = jax.nn.softmax(hg_lb, axis=0)
    lb_all = jnp.cumsum(sm, axis=0) - sm[0]

    for l in range(DEPTH):
        wl = w_in[l].astype(BF16)
        wd = jnp.pad(wl[:, 11 * w:], ((0, 0), (0, 4 * w + GATE_LANES - (wl.shape[1] - 11 * w))))
        pa = matmul(hb, wl[:, 0:3 * w], TM, 3 * w)
        pb = matmul(hb, wl[:, 3 * w:6 * w], TM, 3 * w)
        pc = matmul(hb, wl[:, 6 * w:11 * w], TM, 5 * w // 2)
        pd = matmul(hb, wd, TM, wd.shape[1])

        qlo, qhi, ka, va = attn_prep(pa, tables, nb, lp, TM)
        lam_vecs = jnp.stack([lam_q1[l], lam_k1[l], lam_q2[l], lam_k2[l]])
        oa = diff_attention(qlo, qhi, ka, va, lam_vecs, attn_norm_g[l], nb, lp, padf, l,
                            ATTN_TQ, ATTN_TKB, ATTN_UNROLL)

        hp = dict(gg=gg, ggt=ggt, w1=hy_w1[l], b1=hy_b1[l], f1=hy_f1[l], w2=hy_w2[l], b2=hy_b2[l],
                  f2=hy_f2[l], w3=hy_w3[l], decay=hy_decay[l], d=hy_d[l], conv_w=hy_conv_w[l],
                  conv_b=hy_conv_b[l])
        ob = hyena_mixer(pb, hp, nb, lp, padf, l_real, TM)

        qc, vc, gc, sgc = hgrn_prep(pc, 0, lb_all[l], TM)
        oc = gated_norm(hgrn_scan(qc, vc, gc, consts, nb, lp, SCAN_TB), sgc, hg_norm_g[l], TM)

        qd, kd, vd, szd, gates = gdn_prep(pd, gdn_conv_w[l], gdn_a_log[l], gdn_dt_bias[l], lp, padf, TM)
        od = gated_norm(gdn_scan(qd, kd, vd, gates, consts, sel, nb, lp, SCAN_TB), szd, gdn_norm_g[l], TM)

        h, hb = proj_residual_ln([oa, ob, oc, od], w_out[l].astype(BF16), h, ln1_g[l], ln1_b[l], lp, padf, TM)
        act = ffn_up(hb, w_up[l].astype(BF16), ffn_conv_w[l], ffn_conv_b[l], TM, FFN_TN)
        h, hb = ffn_down_ln(act, w_down[l].astype(BF16), h, ln2_g[l], ln2_b[l], lp, padf, FFN_DOWN_TM)

    y = h.reshape(nb, lp, d)[:, ROW_ALIGN:]
    return (y[:n_prompt], y[n_prompt:])
```
